```python
import jax, jax.numpy as jnp
from jax import lax
import numpy as np

D_MODEL = 1024
BATCH = 16
SEQ = 256
DEPTH = 2
DEC_BATCH = 8
DEC_SEQ = 1024
PAST_LEN = 256

GRID_W = 64
MIX_WIDTH = D_MODEL
HEAD_DIM = 64
ATT_WIDTH = MIX_WIDTH // 2
N_ATT_HEADS = ATT_WIDTH // HEAD_DIM
N_KV_HEADS = N_ATT_HEADS // 4
ML_WIDTH = MIX_WIDTH - ATT_WIDTH
N_ML_HEADS = 4
ML_HEAD_DIM = ML_WIDTH // N_ML_HEADS
MLSTM_CHUNK = 64
Q_BLOCK = 128
ROPE_BASE = 10000.0
N_EXPERTS = 16
N_EXPERT_GROUPS = 4
EXPERTS_PER_GROUP = N_EXPERTS // N_EXPERT_GROUPS
TOP_K = 2
D_EXPERT = 512
EPS = 1e-6
IN_WIDTHS = (ATT_WIDTH, N_KV_HEADS * HEAD_DIM, N_KV_HEADS * HEAD_DIM,
             ML_WIDTH, ML_WIDTH, ML_WIDTH, ML_WIDTH, 2 * N_ML_HEADS, 2 * N_ML_HEADS)
D_IN = sum(IN_WIDTHS)

kernel_name = 'hybrid_mlstm_gqa_grouped_moe_diffusion_step'


def _rmsnorm(x, g):
    x32 = x.astype(jnp.float32)
    y = x32 * lax.rsqrt(jnp.mean(x32 * x32, axis=-1, keepdims=True) + EPS)
    return (y * g.astype(jnp.float32)).astype(x.dtype)


def _modulation(cond, w_mod, b_mod):
    m = (jax.nn.silu(cond) @ w_mod + b_mod)[:, None, :]
    return jnp.split(m, 6, axis=-1)


def _project(h, w_in, b_igate, b_fgate, g_q, g_k):
    B, S, _ = h.shape
    split_points = np.cumsum(IN_WIDTHS)[:-1].tolist()
    aq, ak, av, mq, mk, mv, mo, mi, mf = jnp.split(h @ w_in, split_points, axis=-1)
    aq = _rmsnorm(aq.reshape(B, S, N_ATT_HEADS, HEAD_DIM), g_q)
    ak = _rmsnorm(ak.reshape(B, S, N_KV_HEADS, HEAD_DIM), g_k)
    av = av.reshape(B, S, N_KV_HEADS, HEAD_DIM)
    heads = lambda a: a.reshape(B, S, N_ML_HEADS, ML_HEAD_DIM).transpose(0, 2, 1, 3).astype(jnp.float32)
    mq, mk, mv = heads(mq), heads(mk) * (ML_HEAD_DIM ** -0.5), heads(mv)
    gate = lambda a, b: (a.reshape(B, S, 2, N_ML_HEADS).astype(jnp.float32)
                         + b.astype(jnp.float32)).transpose(2, 0, 3, 1)
    ig = gate(mi, b_igate)
    lf = jax.nn.log_sigmoid(gate(mf, b_fgate))
    og = jax.nn.sigmoid(mo)
    return (aq, ak, av), (mq, mk, mv, ig, lf), og


def _axial_rope(x, row, col):
    half = HEAD_DIM // 2
    freqs = ROPE_BASE ** (-jnp.arange(0, half, 2, dtype=jnp.float32) / half)
    x32 = x.astype(jnp.float32)

    def rot(xh, pos):
        ang = pos.astype(jnp.float32)[:, None] * freqs[None, :]
        cos = jnp.cos(ang)[None, :, None, :]
        sin = jnp.sin(ang)[None, :, None, :]
        x1, x2 = jnp.split(xh, 2, axis=-1)
        return jnp.concatenate([x1 * cos - x2 * sin, x1 * sin + x2 * cos], axis=-1)

    out = jnp.concatenate([rot(x32[..., :half], row), rot(x32[..., half:], col)], axis=-1)
    return out.astype(x.dtype)


def _block_attention(q, k, v):
    B, Sq, H, dh = q.shape
    G = k.shape[2]
    R = H // G
    nb = Sq // Q_BLOCK
    qb = q.reshape(B, nb, Q_BLOCK, G, R, dh).transpose(1, 0, 2, 3, 4, 5)
    scale = dh ** -0.5

    def one_block(qblk):
        s = jnp.einsum('bqgrd,bkgd->bgrqk', qblk, k).astype(jnp.float32) * scale
        p = jax.nn.softmax(s, axis=-1).astype(v.dtype)
        return jnp.einsum('bgrqk,bkgd->bqgrd', p, v)

    o = lax.map(one_block, qb)
    return o.transpose(1, 0, 2, 3, 4, 5).reshape(B, Sq, H * dh)


def _mlstm_chunkwise(q, k, v, ig, lf, C0, n0, m0):
    B, H, S, DK = q.shape
    L = MLSTM_CHUNK
    nc = S // L
    chunks = lambda a: jnp.moveaxis(a.reshape(B, H, nc, L, *a.shape[3:]), 2, 0)
    causal = jnp.tril(jnp.ones((L, L), dtype=bool))

    def step(carry, inp):
        C, n, m = carry
        qb, kb, vb, ib, fb = inp
        b = jnp.cumsum(fb, axis=-1)
        d = b[..., :, None] - b[..., None, :] + ib[..., None, :]
        d = jnp.where(causal, d, -jnp.inf)
        inter = b + m[..., None]
        m_t = jnp.maximum(inter, jnp.max(d, axis=-1))
        w_intra = jnp.exp(d - m_t[..., None])
        w_inter = jnp.exp(inter - m_t)
        s = jnp.einsum('bhtd,bhsd->bhts', qb, kb) * w_intra
        num = (w_inter[..., None] * jnp.einsum('bhtd,bhde->bhte', qb, C)
               + jnp.einsum('bhts,bhse->bhte', s, vb))
        den = w_inter * jnp.einsum('bhtd,bhd->bht', qb, n) + jnp.sum(s, axis=-1)
        h = num / jnp.maximum(jnp.abs(den), jnp.exp(-m_t))[..., None]
        b_last = b[..., -1]
        m_new = m_t[..., -1]
        decay = jnp.exp(b_last + m - m_new)
        w_s = jnp.exp(b_last[..., None] - b + ib - m_new[..., None])
        kw = kb * w_s[..., None]
        C_new = decay[..., None, None] * C + jnp.einsum('bhsd,bhse->bhde', kw, vb)
        n_new = decay[..., None] * n + jnp.sum(kw, axis=2)
        return (C_new, n_new, m_new), h

    fin, hs = lax.scan(step, (C0, n0, m0), (chunks(q), chunks(k), chunks(v), chunks(ig), chunks(lf)))
    h = jnp.moveaxis(hs, 0, 2).reshape(B, H, S, v.shape[-1])
    return h, fin


def _mlstm_bidir(mq, mk, mv, ig, lf, init_f, init_b):
    h_f, fin_f = _mlstm_chunkwise(mq, mk, mv, ig[0], lf[0], *init_f)
    rev = lambda a: jnp.flip(a, axis=2)
    h_b, fin_b = _mlstm_chunkwise(rev(mq), rev(mk), rev(mv), rev(ig[1]), rev(lf[1]), *init_b)
    return h_f + rev(h_b), fin_f, fin_b


def _mixer_out(att, h_ml, og, g_mh, w_out):
    B, S, _ = og.shape
    h = _rmsnorm(h_ml.transpose(0, 2, 1, 3), g_mh.reshape(N_ML_HEADS, ML_HEAD_DIM))
    ml = (og * h.reshape(B, S, ML_WIDTH)).astype(att.dtype)
    return jnp.concatenate([att, ml], axis=-1) @ w_out


def _moe(h, w_router, b_router, w_gate, w_up, w_down):
    B, S, D = h.shape
    xt = h.reshape(B * S, D)
    scores = jax.nn.softmax((xt @ w_router).astype(jnp.float32), axis=-1)
    sel = scores + b_router.astype(jnp.float32)
    grp_score = jnp.sum(lax.top_k(sel.reshape(-1, N_EXPERT_GROUPS, EXPERTS_PER_GROUP), TOP_K)[0], axis=-1)
    grp = jnp.argmax(grp_score, axis=-1)
    in_grp = grp[:, None] == (jnp.arange(N_EXPERTS) // EXPERTS_PER_GROUP)[None, :]
    _, idx = lax.top_k(jnp.where(in_grp, sel, -jnp.inf), TOP_K)
    w = jnp.take_along_axis(scores, idx, axis=-1)
    w = w / jnp.sum(w, axis=-1, keepdims=True)
    gates = jnp.einsum('nk,nke->ne', w, jax.nn.one_hot(idx, N_EXPERTS, dtype=jnp.float32))
    hid = jax.nn.silu(jnp.einsum('nd,edf->nef', xt, w_gate)) * jnp.einsum('nd,edf->nef', xt, w_up)
    hid = hid * gates.astype(hid.dtype)[..., None]
    y = jnp.einsum('nef,efd->nd', hid, w_down)
    return y.reshape(B, S, D)


def setup_inputs(seed: int = 0) -> dict:
    key = jax.random.key(seed)
    ks = jax.random.split(key, 26)
    nrm = lambda k, shape, s=1.0: jax.random.normal(k, shape, jnp.float32) * s
    L = DEPTH
    return {
        'x_prompt': nrm(ks[0], (BATCH, SEQ, D_MODEL)),
        'x_sample': nrm(ks[1], (DEC_BATCH, DEC_SEQ, D_MODEL)),
        'c': nrm(ks[2], (DEC_BATCH, D_MODEL)),
        'cache_k': nrm(ks[3], (DEC_BATCH, L, PAST_LEN, N_KV_HEADS, HEAD_DIM)),
        'cache_v': nrm(ks[4], (DEC_BATCH, L, PAST_LEN, N_KV_HEADS, HEAD_DIM)),
        'state_C': nrm(ks[5], (DEC_BATCH, L, 2, N_ML_HEADS, ML_HEAD_DIM, ML_HEAD_DIM), 0.3),
        'state_n': nrm(ks[6], (DEC_BATCH, L, 2, N_ML_HEADS, ML_HEAD_DIM), 0.3),
        'state_m': nrm(ks[7], (DEC_BATCH, L, 2, N_ML_HEADS), 0.5),
        'c_ctx': nrm(ks[8], (D_MODEL,)),
        'w_mod': nrm(ks[9], (L, D_MODEL, 6 * D_MODEL), 0.5 * D_MODEL ** -0.5),
        'b_mod': nrm(ks[10], (L, 6 * D_MODEL), 0.02),
        'g_mix': 1.0 + nrm(ks[11], (L, D_MODEL), 0.1),
        'g_ffn': 1.0 + nrm(ks[12], (L, D_MODEL), 0.1),
        'w_in': nrm(ks[13], (L, D_MODEL, D_IN), D_MODEL ** -0.5),
        'b_igate': nrm(ks[14], (L, 2, N_ML_HEADS), 0.1),
        'b_fgate': jnp.linspace(3.0, 6.0, N_ML_HEADS, dtype=jnp.float32) + nrm(ks[15], (L, 2, N_ML_HEADS), 0.1),
        'g_q': 1.0 + nrm(ks[16], (L, HEAD_DIM), 0.1),
        'g_k': 1.0 + nrm(ks[17], (L, HEAD_DIM), 0.1),
        'g_mh': 1.0 + nrm(ks[18], (L, ML_WIDTH), 0.1),
        'w_out': nrm(ks[19], (L, MIX_WIDTH, D_MODEL), MIX_WIDTH ** -0.5),
        'w_router': nrm(ks[20], (D_MODEL, N_EXPERTS), D_MODEL ** -0.5),
        'b_router': nrm(ks[21], (N_EXPERTS,), 0.01),
        'w_e_gate': nrm(ks[22], (L, N_EXPERTS, D_MODEL, D_EXPERT), D_MODEL ** -0.5),
        'w_e_up': nrm(ks[23], (L, N_EXPERTS, D_MODEL, D_EXPERT), D_MODEL ** -0.5),
        'w_e_down': nrm(ks[24], (L, N_EXPERTS, D_EXPERT, D_MODEL), D_EXPERT ** -0.5),
    }


def reference(x_prompt, x_sample, c, cache_k, cache_v, state_C, state_n, state_m, c_ctx,
              w_mod, b_mod, g_mix, g_ffn, w_in, b_igate, b_fgate, g_q, g_k, g_mh, w_out,
              w_router, b_router, w_e_gate, w_e_up, w_e_down):
    f32 = jnp.float32

    x = x_prompt
    B = x.shape[0]
    zero = (jnp.zeros((B, N_ML_HEADS, ML_HEAD_DIM, ML_HEAD_DIM), f32),
            jnp.zeros((B, N_ML_HEADS, ML_HEAD_DIM), f32),
            jnp.zeros((B, N_ML_HEADS), f32))
    ks, vs, Cs, ns, ms = [], [], [], [], []
    for l in range(DEPTH):
        sh1, sc1, gt1, sh2, sc2, gt2 = _modulation(c_ctx[None, :], w_mod[l], b_mod[l])
        h = _rmsnorm(x, g_mix[l]) * (1 + sc1) + sh1
        (aq, ak, av), (mq, mk, mv, ig, lf), og = _project(h, w_in[l], b_igate[l], b_fgate[l], g_q[l], g_k[l])
        att = _block_attention(aq, ak, av)
        h_ml, fin_f, fin_b = _mlstm_bidir(mq, mk, mv, ig, lf, zero, zero)
        x = (x + gt1 * _mixer_out(att, h_ml, og, g_mh[l], w_out[l])).astype(x_prompt.dtype)
        h = _rmsnorm(x, g_ffn[l]) * (1 + sc2) + sh2
        x = (x + gt2 * _moe(h, w_router, b_router, w_e_gate[l], w_e_up[l], w_e_down[l])).astype(x_prompt.dtype)
        ks.append(ak)
        vs.append(av)
        Cs.append(jnp.stack([fin_f[0], fin_b[0]], axis=1))
        ns.append(jnp.stack([fin_f[1], fin_b[1]], axis=1))
        ms.append(jnp.stack([fin_f[2], fin_b[2]], axis=1))
    y_prompt = x
    new_cache_k = jnp.stack(ks, axis=1)
    new_cache_v = jnp.stack(vs, axis=1)
    new_state_C = jnp.stack(Cs, axis=1)
    new_state_n = jnp.stack(ns, axis=1)
    new_state_m = jnp.stack(ms, axis=1)

    x = x_sample
    S = x.shape[1]
    n_rows = S // GRID_W
    row = jnp.repeat(jnp.arange(n_rows), GRID_W)
    col = jnp.tile(jnp.arange(GRID_W), n_rows)
    for l in range(DEPTH):
        sh1, sc1, gt1, sh2, sc2, gt2 = _modulation(c, w_mod[l], b_mod[l])
        h = _rmsnorm(x, g_mix[l]) * (1 + sc1) + sh1
        (aq, ak, av), (mq, mk, mv, ig, lf), og = _project(h, w_in[l], b_igate[l], b_fgate[l], g_q[l], g_k[l])
        aq = _axial_rope(aq, row, col)
        ak = _axial_rope(ak, row, col)
        k_all = jnp.concatenate([cache_k[:, l].astype(ak.dtype), ak], axis=1)
        v_all = jnp.concatenate([cache_v[:, l].astype(av.dtype), av], axis=1)
        att = _block_attention(aq, k_all, v_all)
        init_f = (state_C[:, l, 0].astype(f32), state_n[:, l, 0].astype(f32), state_m[:, l, 0].astype(f32))
        init_b = (state_C[:, l, 1].astype(f32), state_n[:, l, 1].astype(f32), state_m[:, l, 1].astype(f32))
        h_ml, _, _ = _mlstm_bidir(mq, mk, mv, ig, lf, init_f, init_b)
        x = (x + gt1 * _mixer_out(att, h_ml, og, g_mh[l], w_out[l])).astype(x_sample.dtype)
        h = _rmsnorm(x, g_ffn[l]) * (1 + sc2) + sh2
        x = (x + gt2 * _moe(h, w_router, b_router, w_e_gate[l], w_e_up[l], w_e_down[l])).astype(x_sample.dtype)
    y_sample = x

    return (y_prompt, y_sample, new_cache_k, new_cache_v, new_state_C, new_state_n, new_state_m)
```

```python
import functools

import numpy as np
import jax
import jax.numpy as jnp
from jax import lax
from jax.experimental import pallas as pl
from jax.experimental.pallas import tpu as pltpu

F32 = jnp.float32
BF16 = jnp.bfloat16

D_MODEL = 1024
HEAD_DIM = 64
ATT_WIDTH = 512
N_KV_HEADS = 2
ML_WIDTH = 512
N_ML_HEADS = 4
ML_HEAD_DIM = 128
GRID_W = 64
ROPE_BASE = 10000.0
N_EXPERTS = 16
N_GROUPS = 4
GROUP_SIZE = 4
D_EXPERT = 512
EPS = 1e-6
MAIN_WIDTH = 2816
N_GATE_COLS = 16
LANES = 128
TOKEN_TILE = 256
SEQ_BLOCK = 256
MOE_TILE = 1024
VMEM_LIMIT = 56 * 1024 * 1024
NEG_INF = float("-inf")
HIGHEST = lax.Precision.HIGHEST
NT_DIMS = (((1,), (1,)), ((), ()))
TN_DIMS = (((0,), (0,)), ((), ()))


def _params(semantics):
    return pltpu.CompilerParams(dimension_semantics=semantics, vmem_limit_bytes=VMEM_LIMIT)


def _log_sigmoid(z):
    return jnp.minimum(z, 0.0) - jnp.log1p(jnp.exp(-jnp.abs(z)))


def _split3(x):
    h1 = x.astype(BF16)
    r1 = x - h1.astype(F32)
    h2 = r1.astype(BF16)
    h3 = (r1 - h2.astype(F32)).astype(BF16)
    return h1, h2, h3


def _mod_kernel(cond_ref, w_ref, b_ref, o_ref):
    c = cond_ref[...]
    s = c * jax.nn.sigmoid(c)
    o_ref[...] = jnp.dot(s.astype(BF16), w_ref[...].astype(BF16),
                         preferred_element_type=F32) + b_ref[...]


def _modulation(cond, w_mod, b_mod):
    n_layers = w_mod.shape[0]
    n_chunks = w_mod.shape[2] // D_MODEL
    return pl.pallas_call(
        _mod_kernel,
        grid=(n_layers, n_chunks),
        in_specs=[
            pl.BlockSpec((16, D_MODEL), lambda l, j: (0, 0)),
            pl.BlockSpec((None, D_MODEL, D_MODEL), lambda l, j: (l, 0, j)),
            pl.BlockSpec((None, 1, D_MODEL), lambda l, j: (l, 0, j)),
        ],
        out_specs=pl.BlockSpec((None, 16, D_MODEL), lambda l, j: (l, 0, j)),
        out_shape=jax.ShapeDtypeStruct((n_layers, 16, w_mod.shape[2]), F32),
        compiler_params=_params(("parallel", "parallel")),
        name="modulation",
    )(cond, w_mod, b_mod.reshape(n_layers, 1, -1))


def _inproj_kernel(x_ref, sh_ref, sc_ref, g_ref, w_ref, wg_ref, wgt_ref, bcol_ref, brow_ref,
                   gq_ref, gk_ref, cos_ref, sa_ref, sb_ref, gsum_ref,
                   q_ref, k_ref, v_ref, mq_ref, mk_ref, mv_ref, og_ref, gc_ref, gr_ref):
    x = x_ref[...]
    ms = jnp.mean(x * x, axis=-1, keepdims=True)
    h = x * lax.rsqrt(ms + EPS) * g_ref[...]
    h = h * (1.0 + sc_ref[...]) + sh_ref[...]
    hb = h.astype(BF16)
    cos = cos_ref[...]
    sa = sa_ref[...]
    sb = sb_ref[...]
    gsum = gsum_ref[...]

    def proj(c0, width):
        return jnp.dot(hb, w_ref[:, c0:c0 + width], preferred_element_type=F32)

    def headnorm_rope(z, gain):
        sq = z * z
        hi = sq.astype(BF16)
        lo = (sq - hi.astype(F32)).astype(BF16)
        ss = (jnp.dot(hi, gsum, preferred_element_type=F32)
              + jnp.dot(lo, gsum, preferred_element_type=F32))
        zn = z * lax.rsqrt(ss * (1.0 / HEAD_DIM) + EPS) * gain
        return zn * cos + pltpu.roll(zn, LANES - 16, 1) * sa + pltpu.roll(zn, 16, 1) * sb

    for c in range(ATT_WIDTH // LANES):
        z = proj(c * LANES, LANES)
        q_ref[:, c * LANES:(c + 1) * LANES] = (headnorm_rope(z, gq_ref[...]) * 0.125).astype(BF16)
    k_ref[...] = headnorm_rope(proj(512, LANES), gk_ref[...])
    v_ref[...] = proj(640, LANES)
    mq_ref[...] = proj(768, ML_WIDTH).astype(BF16)
    mk_ref[...] = (proj(1280, ML_WIDTH) * (ML_HEAD_DIM ** -0.5)).astype(BF16)
    mv_ref[...] = proj(1792, ML_WIDTH).astype(BF16)
    og_ref[...] = jax.nn.sigmoid(proj(2304, ML_WIDTH))

    zc = jnp.dot(h, wg_ref[...], precision=HIGHEST, preferred_element_type=F32) + bcol_ref[...]
    lane = lax.broadcasted_iota(jnp.int32, zc.shape, 1)
    gc_ref[...] = jnp.where(lane < 8, zc, _log_sigmoid(zc))
    zr = lax.dot_general(wgt_ref[...], h, NT_DIMS, precision=HIGHEST,
                         preferred_element_type=F32) + brow_ref[...]
    sub = lax.broadcasted_iota(jnp.int32, zr.shape, 0)
    gr_ref[...] = jnp.where(sub < 8, zr, _log_sigmoid(zr))


def _inproj(x, mod3, n_ctx_tiles, tiles_per_lat_seq, ctx_row, g_mix, w_main, w_gate, b_gate,
            g_q, g_k, rope):
    n = x.shape[0]
    n_tiles = n // TOKEN_TILE

    def mod_row(i):
        return jnp.where(i < n_ctx_tiles, ctx_row, (i - n_ctx_tiles) // tiles_per_lat_seq)

    def rope_blk(i):
        return jnp.where(i < n_ctx_tiles, 0, 1 + (i - n_ctx_tiles) % tiles_per_lat_seq)

    cos_t, sa_t, sb_t = rope
    lane = np.arange(LANES)
    gsum = jnp.asarray((lane[:, None] // HEAD_DIM) == (lane[None, :] // HEAD_DIM), BF16)
    tok = lambda w: pl.BlockSpec((TOKEN_TILE, w), lambda i: (i, 0))
    full = lambda a: pl.BlockSpec(a.shape, lambda i: (0,) * a.ndim)
    modspec = lambda j: pl.BlockSpec((None, 1, D_MODEL), lambda i: (mod_row(i), 0, j))
    ropespec = pl.BlockSpec((TOKEN_TILE, LANES), lambda i: (rope_blk(i), 0))
    args = (x, mod3, mod3, g_mix.reshape(1, -1), w_main, w_gate, w_gate.T,
            b_gate.reshape(1, -1), b_gate.reshape(-1, 1),
            jnp.tile(g_q, 2).reshape(1, -1), jnp.tile(g_k, 2).reshape(1, -1),
            cos_t, sa_t, sb_t, gsum)
    in_specs = [tok(D_MODEL), modspec(0), modspec(1)] + [full(a) for a in args[3:11]] \
        + [ropespec, ropespec, ropespec, full(gsum)]
    out_shape = (
        jax.ShapeDtypeStruct((n, ATT_WIDTH), BF16),
        jax.ShapeDtypeStruct((n, LANES), F32),
        jax.ShapeDtypeStruct((n, LANES), F32),
        jax.ShapeDtypeStruct((n, ML_WIDTH), BF16),
        jax.ShapeDtypeStruct((n, ML_WIDTH), BF16),
        jax.ShapeDtypeStruct((n, ML_WIDTH), BF16),
        jax.ShapeDtypeStruct((n, ML_WIDTH), F32),
        jax.ShapeDtypeStruct((n, N_GATE_COLS), F32),
        jax.ShapeDtypeStruct((N_GATE_COLS, n), F32),
    )
    out_specs = (tok(ATT_WIDTH), tok(LANES), tok(LANES), tok(ML_WIDTH), tok(ML_WIDTH),
                 tok(ML_WIDTH), tok(ML_WIDTH), tok(N_GATE_COLS),
                 pl.BlockSpec((N_GATE_COLS, TOKEN_TILE), lambda i: (0, i)))
    return pl.pallas_call(
        _inproj_kernel, grid=(n_tiles,), in_specs=in_specs, out_specs=out_specs,
        out_shape=out_shape, compiler_params=_params(("parallel",)), name="inproj",
    )(*args)


def _attn_kernel(*refs, n_kv):
    q_ref = refs[0]
    kv_refs = refs[1:1 + 2 * n_kv]
    o_ref = refs[-1]
    tq = q_ref.shape[0]
    lo_q = lax.broadcasted_iota(jnp.int32, (tq, LANES), 1) < HEAD_DIM

    def dup_half(ref, g):
        a = ref[...]
        r = pltpu.roll(a, HEAD_DIM, 1)
        lo = lax.broadcasted_iota(jnp.int32, a.shape, 1) < HEAD_DIM
        d = jnp.where(lo, a, r) if g == 0 else jnp.where(lo, r, a)
        return d.astype(BF16)

    for g in range(N_KV_HEADS):
        ks = [dup_half(kv_refs[2 * p], g) for p in range(n_kv)]
        vs = [dup_half(kv_refs[2 * p + 1], g) for p in range(n_kv)]
        for hb in range(2):
            c0 = (2 * g + hb) * LANES
            qb = q_ref[:, c0:c0 + LANES]
            outs = []
            for half in range(2):
                keep = lo_q if half == 0 else jnp.logical_not(lo_q)
                qm = jnp.where(keep, qb, jnp.zeros_like(qb))
                ss = [lax.dot_general(qm, kd, NT_DIMS, preferred_element_type=F32) for kd in ks]
                m = functools.reduce(jnp.maximum, [jnp.max(s, axis=1, keepdims=True) for s in ss])
                ps = [jnp.exp(s - m) for s in ss]
                den = functools.reduce(jnp.add, [jnp.sum(p, axis=1, keepdims=True) for p in ps])
                o = functools.reduce(jnp.add, [
                    jnp.dot(p.astype(BF16), vd, preferred_element_type=F32)
                    for p, vd in zip(ps, vs)])
                outs.append(o / den)
            o_ref[:, c0:c0 + LANES] = jnp.where(lo_q, outs[0], outs[1]).astype(BF16)


def _attention(q, k, v, row0, n_seq, seq_len, cache=None):
    nq = seq_len // SEQ_BLOCK
    qb0 = row0 // SEQ_BLOCK
    sb0 = row0 // seq_len
    in_specs = [
        pl.BlockSpec((SEQ_BLOCK, ATT_WIDTH), lambda b, i: (qb0 + b * nq + i, 0)),
        pl.BlockSpec((seq_len, LANES), lambda b, i: (sb0 + b, 0)),
        pl.BlockSpec((seq_len, LANES), lambda b, i: (sb0 + b, 0)),
    ]
    args = [q, k, v]
    n_kv = 1
    if cache is not None:
        ck, cv = cache
        past = ck.shape[0] // n_seq
        in_specs += [pl.BlockSpec((past, LANES), lambda b, i: (b, 0))] * 2
        args += [ck, cv]
        n_kv = 2
    return pl.pallas_call(
        functools.partial(_attn_kernel, n_kv=n_kv),
        grid=(n_seq, nq), in_specs=in_specs,
        out_specs=pl.BlockSpec((SEQ_BLOCK, ATT_WIDTH), lambda b, i: (b * nq + i, 0)),
        out_shape=jax.ShapeDtypeStruct((n_seq * seq_len, ATT_WIDTH), BF16),
        compiler_params=_params(("parallel", "parallel")), name="attention",
    )(*args)


def _mlstm_kernel(*refs, seq_len, has_init, emit_state):
    it = iter(refs)
    q_ref, k_ref, v_ref, og_ref, gmh_ref, gcol_ref, grow_ref, u_ref, l_ref = [next(it) for _ in range(9)]
    if has_init:
        c0_ref, nm0_ref = next(it), next(it)
    ml_ref = next(it)
    if emit_state:
        cf_ref, nmf_ref = next(it), next(it)

    bq = SEQ_BLOCK
    nb = seq_len // bq
    blk = lambda j: slice(j * bq, (j + 1) * bq)
    upper_incl = u_ref[...]
    lower_incl = l_ref[...]

    def tri_dot(x, tri):
        return functools.reduce(jnp.add, [jnp.dot(p, tri, preferred_element_type=F32)
                                          for p in _split3(x)])

    ig_row = [[None] * nb for _ in range(2)]
    lf_row = [[None] * nb for _ in range(2)]
    within = [[None] * nb for _ in range(2)]
    bsum = [[None] * nb for _ in range(2)]
    for j in range(nb):
        g8 = grow_ref[:, blk(j)]
        cum_f = tri_dot(g8, upper_incl)
        cum_b = tri_dot(g8, lower_incl)
        for d in range(2):
            ig_row[d][j] = g8[d:d + 1, :]
            lf_row[d][j] = g8[2 + d:3 + d, :]
            within[d][j] = (cum_f if d == 0 else cum_b)[2 + d:3 + d, :]
            bsum[d][j] = jnp.sum(lf_row[d][j], axis=1, keepdims=True)
    zero11 = jnp.zeros((1, 1), F32)
    offset = [[None] * nb for _ in range(2)]
    acc = zero11
    for j in range(nb):
        offset[0][j] = acc
        acc = acc + bsum[0][j]
    total_f = acc
    acc = zero11
    for j in reversed(range(nb)):
        offset[1][j] = acc
        acc = acc + bsum[1][j]
    total = [total_f, acc]
    a_row = [[ig_row[d][j] - (within[d][j] + offset[d][j]) for j in range(nb)] for d in range(2)]
    blkmax = [[jnp.max(a_row[d][j], axis=1, keepdims=True) for j in range(nb)] for d in range(2)]
    if has_init:
        m0 = [nm0_ref[2:3, 0:1], nm0_ref[3:4, 0:1]]
    else:
        m0 = [zero11, zero11]

    r_i = lax.broadcasted_iota(jnp.int32, (bq, bq), 0)
    c_i = lax.broadcasted_iota(jnp.int32, (bq, bq), 1)
    causal = [c_i <= r_i, c_i >= r_i]
    before = [lambda i: range(0, i), lambda i: range(i + 1, nb)]

    gmh = gmh_ref[...]
    b_col = [[None] * nb for _ in range(2)]
    for i in range(nb):
        q_i = q_ref[blk(i), :]
        a_blocks = {}

        def scores(j):
            if j not in a_blocks:
                a_blocks[j] = lax.dot_general(q_i, k_ref[blk(j), :], NT_DIMS,
                                              preferred_element_type=F32)
            return a_blocks[j]

        h = None
        for d in range(2):
            m_prev = functools.reduce(jnp.maximum, [blkmax[d][j] for j in before[d](i)], m0[d])
            b_col[d][i] = offset[d][i] + jnp.sum(jnp.where(causal[d], lf_row[d][i], 0.0),
                                                 axis=1, keepdims=True)
            m_col = jnp.maximum(m_prev, jnp.max(jnp.where(causal[d], a_row[d][i], NEG_INF),
                                                axis=1, keepdims=True))
            num = jnp.zeros((bq, ML_HEAD_DIM), F32)
            den = jnp.zeros((bq, 1), F32)
            for j in list(before[d](i)) + [i]:
                arg = a_row[d][j] - m_col
                if j == i:
                    arg = jnp.where(causal[d], arg, NEG_INF)
                p = jnp.exp(arg) * scores(j)
                den = den + jnp.sum(p, axis=1, keepdims=True)
                num = num + jnp.dot(p.astype(BF16), v_ref[blk(j), :], preferred_element_type=F32)
            if has_init:
                w_inter = jnp.exp(m0[d] - m_col)
                qc = jnp.dot(q_i, c0_ref[d].astype(BF16), preferred_element_type=F32)
                qn = jnp.sum(q_i.astype(F32) * nm0_ref[d:d + 1, :], axis=1, keepdims=True)
                num = num + w_inter * qc
                den = den + w_inter * qn
            nrm = jnp.maximum(jnp.abs(den), jnp.exp(-(b_col[d][i] + m_col)))
            h = num / nrm if h is None else h + num / nrm
        hn = h * lax.rsqrt(jnp.mean(h * h, axis=-1, keepdims=True) + EPS) * gmh
        ml_ref[blk(i), :] = (og_ref[blk(i), :] * hn).astype(BF16)

    if emit_state:
        nmf_ref[...] = jnp.zeros_like(nmf_ref)
        for d in range(2):
            m_last = functools.reduce(jnp.maximum, blkmax[d], m0[d])
            c_fin = jnp.zeros((ML_HEAD_DIM, ML_HEAD_DIM), F32)
            n_fin = jnp.zeros((1, ML_HEAD_DIM), F32)
            for j in range(nb):
                a_col = gcol_ref[blk(j), d:d + 1] - b_col[d][j]
                kw = k_ref[blk(j), :].astype(F32) * jnp.exp(a_col - m_last)
                c_fin = c_fin + lax.dot_general(kw.astype(BF16), v_ref[blk(j), :], TN_DIMS,
                                                preferred_element_type=F32)
                n_fin = n_fin + jnp.sum(kw, axis=0, keepdims=True)
            if has_init:
                decay = jnp.exp(m0[d] - m_last)
                c_fin = c_fin + decay * c0_ref[d]
                n_fin = n_fin + decay * nm0_ref[d:d + 1, :]
            cf_ref[d] = c_fin
            nmf_ref[d:d + 1, :] = n_fin
            nmf_ref[2 + d:3 + d, :] = jnp.broadcast_to(total[d] + m_last, (1, ML_HEAD_DIM))


def _mlstm(mq, mk, mv, og, g_mh, gcol, grow, row0, n_seq, seq_len, init=None, emit_state=False):
    sb0 = row0 // seq_len
    tri = np.arange(SEQ_BLOCK)
    upper_incl = jnp.asarray(tri[:, None] <= tri[None, :], BF16)
    lower_incl = jnp.asarray(tri[:, None] >= tri[None, :], BF16)
    headblk = lambda: pl.BlockSpec((seq_len, ML_HEAD_DIM), lambda b, h: (sb0 + b, h))
    const = lambda a: pl.BlockSpec(a.shape, lambda b, h: (0,) * a.ndim)
    in_specs = [headblk(), headblk(), headblk(), headblk(),
                pl.BlockSpec((1, ML_HEAD_DIM), lambda b, h: (0, h)),
                pl.BlockSpec((None, seq_len, 4), lambda b, h: (h, sb0 + b, 0)),
                pl.BlockSpec((None, 8, seq_len), lambda b, h: (h, 0, sb0 + b)),
                const(upper_incl), const(lower_incl)]
    args = [mq, mk, mv, og, g_mh.reshape(1, -1), gcol, grow, upper_incl, lower_incl]
    if init is not None:
        c0, nm0, layer = init
        in_specs += [
            pl.BlockSpec((None, None, 2, None, ML_HEAD_DIM, ML_HEAD_DIM),
                         lambda b, h: (b, layer, 0, h, 0, 0)),
            pl.BlockSpec((None, None, 8, ML_HEAD_DIM), lambda b, h: (b, h, 0, 0))]
        args += [c0, nm0]
    out_shape = [jax.ShapeDtypeStruct((n_seq * seq_len, ML_WIDTH), BF16)]
    out_specs = [pl.BlockSpec((seq_len, ML_HEAD_DIM), lambda b, h: (b, h))]
    if emit_state:
        out_shape += [jax.ShapeDtypeStruct((n_seq, 2, N_ML_HEADS, ML_HEAD_DIM, ML_HEAD_DIM), F32),
                      jax.ShapeDtypeStruct((n_seq, N_ML_HEADS, 8, ML_HEAD_DIM), F32)]
        out_specs += [pl.BlockSpec((None, 2, None, ML_HEAD_DIM, ML_HEAD_DIM),
                                   lambda b, h: (b, 0, h, 0, 0)),
                      pl.BlockSpec((None, None, 8, ML_HEAD_DIM), lambda b, h: (b, h, 0, 0))]
    return pl.pallas_call(
        functools.partial(_mlstm_kernel, seq_len=seq_len, has_init=init is not None,
                          emit_state=emit_state),
        grid=(n_seq, N_ML_HEADS), in_specs=in_specs, out_specs=tuple(out_specs),
        out_shape=tuple(out_shape),
        compiler_params=_params(("parallel", "parallel")), name="mlstm",
    )(*args)


def _outproj_kernel(att_ref, ml_ref, x_ref, w_ref, gt_ref, sh_ref, sc_ref, g_ref, wrt_ref, br_ref,
                    x1_ref, h2_ref, gates_ref):
    y = (jnp.dot(att_ref[...], w_ref[:ATT_WIDTH, :], preferred_element_type=F32)
         + jnp.dot(ml_ref[...], w_ref[ATT_WIDTH:, :], preferred_element_type=F32))
    x1 = x_ref[...] + gt_ref[...] * y
    x1_ref[...] = x1
    ms = jnp.mean(x1 * x1, axis=-1, keepdims=True)
    h2 = x1 * lax.rsqrt(ms + EPS) * g_ref[...]
    h2 = h2 * (1.0 + sc_ref[...]) + sh_ref[...]
    h2_ref[...] = h2.astype(BF16)

    logits = lax.dot_general(wrt_ref[...], h2, NT_DIMS, precision=HIGHEST,
                             preferred_element_type=F32)
    ex = jnp.exp(logits - jnp.max(logits, axis=0, keepdims=True))
    scores = ex / jnp.sum(ex, axis=0, keepdims=True)
    sel = scores + br_ref[...]
    row = lambda a, e: a[e:e + 1, :]
    grp_score = []
    for g in range(N_GROUPS):
        xs = [row(sel, g * GROUP_SIZE + j) for j in range(GROUP_SIZE)]
        pairs = [xs[a] + xs[b] for a in range(GROUP_SIZE) for b in range(a + 1, GROUP_SIZE)]
        grp_score.append(functools.reduce(jnp.maximum, pairs))
    best = grp_score[0]
    grp = jnp.zeros_like(best, dtype=jnp.int32)
    for g in range(1, N_GROUPS):
        better = grp_score[g] > best
        grp = jnp.where(better, g, grp)
        best = jnp.where(better, grp_score[g], best)
    pick = lambda a, j: functools.reduce(
        lambda acc, g: jnp.where(grp == g, row(a, g * GROUP_SIZE + j), acc),
        range(1, N_GROUPS), row(a, j))
    xs = [pick(sel, j) for j in range(GROUP_SIZE)]
    ws = [pick(scores, j) for j in range(GROUP_SIZE)]

    def argmax4(vals):
        bv, bi = vals[0], jnp.zeros_like(grp)
        for j in range(1, GROUP_SIZE):
            better = vals[j] > bv
            bi = jnp.where(better, j, bi)
            bv = jnp.where(better, vals[j], bv)
        return bi

    i1 = argmax4(xs)
    i2 = argmax4([jnp.where(i1 == j, NEG_INF, xs[j]) for j in range(GROUP_SIZE)])
    take = lambda vals, idx: functools.reduce(
        lambda acc, j: jnp.where(idx == j, vals[j], acc), range(1, GROUP_SIZE), vals[0])
    w1, w2 = take(ws, i1), take(ws, i2)
    wsum = w1 + w2
    w1, w2 = w1 / wsum, w2 / wsum
    e1 = grp * GROUP_SIZE + i1
    e2 = grp * GROUP_SIZE + i2
    eid = lax.broadcasted_iota(jnp.int32, logits.shape, 0)
    gates_ref[...] = jnp.where(eid == e1, w1, 0.0) + jnp.where(eid == e2, w2, 0.0)


def _outproj(att, ml, x, mod3, n_ctx_tiles, tiles_per_lat_seq, ctx_row, w_out, g_ffn, w_router,
             b_router):
    n = x.shape[0]

    def mod_row(i):
        return jnp.where(i < n_ctx_tiles, ctx_row, (i - n_ctx_tiles) // tiles_per_lat_seq)

    tok = lambda w: pl.BlockSpec((TOKEN_TILE, w), lambda i: (i, 0))
    full = lambda a: pl.BlockSpec(a.shape, lambda i: (0,) * a.ndim)
    modspec = lambda j: pl.BlockSpec((None, 1, D_MODEL), lambda i: (mod_row(i), 0, j))
    args = (att, ml, x, w_out, mod3, mod3, mod3, g_ffn.reshape(1, -1), w_router.T,
            b_router.reshape(-1, 1))
    in_specs = [tok(ATT_WIDTH), tok(ML_WIDTH), tok(D_MODEL), full(w_out),
                modspec(2), modspec(3), modspec(4), full(args[7]), full(args[8]), full(args[9])]
    return pl.pallas_call(
        _outproj_kernel, grid=(n // TOKEN_TILE,), in_specs=in_specs,
        out_specs=(tok(D_MODEL), tok(D_MODEL),
                   pl.BlockSpec((N_EXPERTS, TOKEN_TILE), lambda i: (0, i))),
        out_shape=(jax.ShapeDtypeStruct((n, D_MODEL), F32),
                   jax.ShapeDtypeStruct((n, D_MODEL), BF16),
                   jax.ShapeDtypeStruct((N_EXPERTS, n), F32)),
        compiler_params=_params(("parallel",)), name="outproj_router",
    )(*args)


def _moe_kernel(h_ref, gates_ref, wg_ref, wu_ref, wd_ref, x1_ref, gt_ref, o_ref):
    e = pl.program_id(1)

    @pl.when(e == 0)
    def _():
        o_ref[...] = jnp.zeros_like(o_ref)

    h = h_ref[...]
    a = jnp.dot(h, wg_ref[...].astype(BF16), preferred_element_type=F32)
    b = jnp.dot(h, wu_ref[...].astype(BF16), preferred_element_type=F32)
    gates = gates_ref[...]
    lane = lax.broadcasted_iota(jnp.int32, gates.shape, 1)
    gcol = jnp.sum(jnp.where(lane == e, gates, 0.0), axis=1, keepdims=True)
    hid = (a * jax.nn.sigmoid(a)) * b * gcol
    o_ref[...] += jnp.dot(hid.astype(BF16), wd_ref[...].astype(BF16), preferred_element_type=F32)

    @pl.when(e == N_EXPERTS - 1)
    def _():
        o_ref[...] = x1_ref[...] + gt_ref[...] * o_ref[...]


def _moe(h2, gates, x1, mod3, layer, n_ctx_rows, lat_seq_len, ctx_row, w_e_gate, w_e_up, w_e_down):
    n = h2.shape[0]
    n_ctx_tiles = n_ctx_rows // MOE_TILE
    per_seq = lat_seq_len // MOE_TILE

    def mod_row(t):
        return jnp.where(t < n_ctx_tiles, ctx_row, (t - n_ctx_tiles) // per_seq)

    wspec = lambda r, c: pl.BlockSpec((None, None, r, c), lambda t, e: (layer, e, 0, 0))
    tok = lambda w: pl.BlockSpec((MOE_TILE, w), lambda t, e: (t, 0))
    return pl.pallas_call(
        _moe_kernel, grid=(n // MOE_TILE, N_EXPERTS),
        in_specs=[tok(D_MODEL), tok(N_EXPERTS), wspec(D_MODEL, D_EXPERT), wspec(D_MODEL, D_EXPERT),
                  wspec(D_EXPERT, D_MODEL), tok(D_MODEL),
                  pl.BlockSpec((None, 1, D_MODEL), lambda t, e: (mod_row(t), 0, 5))],
        out_specs=tok(D_MODEL),
        out_shape=jax.ShapeDtypeStruct((n, D_MODEL), F32),
        compiler_params=_params(("parallel", "arbitrary")), name="experts",
    )(h2, gates, w_e_gate, w_e_up, w_e_down, x1, mod3)


def _rope_tables(seq_len):
    half = HEAD_DIM // 2
    freqs = ROPE_BASE ** (-np.arange(0, half, 2, dtype=np.float64) / half)
    pos = np.arange(seq_len)
    row, col = pos // GRID_W, pos % GRID_W
    d = np.arange(HEAD_DIM)
    position = np.where(d[None, :] < half, row[:, None], col[:, None]).astype(np.float64)
    ang = (position.astype(np.float32) * freqs.astype(np.float32)[d % (half // 2)][None, :]).astype(np.float32)
    cos, sin = np.cos(ang), np.sin(ang)
    first = (d % half) < half // 2
    sa = np.where(first[None, :], -sin, 0.0)
    sb = np.where(first[None, :], 0.0, sin)
    ident = lambda v: np.full((SEQ_BLOCK, HEAD_DIM), v, np.float32)
    stack = lambda ctx, lat: jnp.asarray(
        np.tile(np.concatenate([ctx, lat.astype(np.float32)], axis=0), (1, 2)), F32)
    return stack(ident(1.0), cos), stack(ident(0.0), sa), stack(ident(0.0), sb)


def kernel(x_prompt, x_sample, c, cache_k, cache_v, state_C, state_n, state_m, c_ctx, w_mod, b_mod,
           g_mix, g_ffn, w_in, b_igate, b_fgate, g_q, g_k, g_mh, w_out, w_router, b_router,
           w_e_gate, w_e_up, w_e_down):
    n_ctx_seq, ctx_len, _ = x_prompt.shape
    n_lat_seq, lat_len, _ = x_sample.shape
    n_layers = w_mod.shape[0]
    n_ctx = n_ctx_seq * ctx_len
    assert ctx_len == SEQ_BLOCK and lat_len % MOE_TILE == 0 and n_ctx % MOE_TILE == 0
    assert n_lat_seq < 16 and n_ctx % lat_len == 0
    n_ctx_tiles = n_ctx // TOKEN_TILE
    tiles_per_lat_seq = lat_len // TOKEN_TILE
    ctx_row = n_lat_seq

    x = jnp.concatenate([x_prompt.reshape(n_ctx, D_MODEL), x_sample.reshape(-1, D_MODEL)], axis=0)
    cond = jnp.zeros((16, D_MODEL), F32).at[:n_lat_seq].set(c).at[ctx_row].set(c_ctx)
    mod = _modulation(cond, w_mod, b_mod)
    rope = _rope_tables(lat_len)

    ks, vs, cs, ns, ms = [], [], [], [], []
    for l in range(n_layers):
        mod3 = mod[l].reshape(16, 1, -1)
        w_main = w_in[l, :, :MAIN_WIDTH].astype(BF16)
        w_gate = w_in[l, :, MAIN_WIDTH:]
        b_gate = jnp.concatenate([b_igate[l].reshape(-1), b_fgate[l].reshape(-1)])
        q, k, v, mq, mk, mv, og, gc, gr = _inproj(
            x, mod3, n_ctx_tiles, tiles_per_lat_seq, ctx_row, g_mix[l], w_main, w_gate, b_gate,
            g_q[l], g_k[l], rope)
        n = x.shape[0]
        gcol = gc.reshape(n, 4, N_ML_HEADS).transpose(2, 0, 1)
        grow = gr.reshape(4, N_ML_HEADS, n).transpose(1, 0, 2)
        grow = jnp.concatenate([grow, jnp.zeros_like(grow)], axis=1)

        att_ctx = _attention(q, k, v, 0, n_ctx_seq, ctx_len)
        past = cache_k.shape[2]
        ck = cache_k[:, l].reshape(n_lat_seq * past, LANES)
        cv = cache_v[:, l].reshape(n_lat_seq * past, LANES)
        att_lat = _attention(q, k, v, n_ctx, n_lat_seq, lat_len, cache=(ck, cv))

        ml_ctx, c_fin, nm_fin = _mlstm(mq, mk, mv, og, g_mh[l], gcol, grow, 0, n_ctx_seq, ctx_len,
                                       emit_state=True)
        n0 = state_n[:, l].transpose(0, 2, 1, 3)
        m0 = jnp.broadcast_to(state_m[:, l].transpose(0, 2, 1)[..., None], n0.shape)
        nm0 = jnp.concatenate([n0, m0, jnp.zeros_like(n0), jnp.zeros_like(n0)], axis=2)
        (ml_lat,) = _mlstm(mq, mk, mv, og, g_mh[l], gcol, grow, n_ctx, n_lat_seq, lat_len,
                           init=(state_C, nm0, l))

        att = jnp.concatenate([att_ctx, att_lat], axis=0)
        ml = jnp.concatenate([ml_ctx, ml_lat], axis=0)
        x1, h2, gates_t = _outproj(att, ml, x, mod3, n_ctx_tiles, tiles_per_lat_seq, ctx_row,
                                   w_out[l].astype(BF16), g_ffn[l], w_router, b_router)
        x = _moe(h2, gates_t.T, x1, mod3, l, n_ctx, lat_len, ctx_row, w_e_gate, w_e_up, w_e_down)

        ks.append(k[:n_ctx].reshape(n_ctx_seq, ctx_len, N_KV_HEADS, HEAD_DIM))
        vs.append(v[:n_ctx].reshape(n_ctx_seq, ctx_len, N_KV_HEADS, HEAD_DIM))
        cs.append(c_fin)
        ns.append(nm_fin[:, :, 0:2, :].transpose(0, 2, 1, 3))
        ms.append(nm_fin[:, :, 2:4, 0].transpose(0, 2, 1))

    y_prompt = x[:n_ctx].reshape(x_prompt.shape)
    y_sample = x[n_ctx:].reshape(x_sample.shape)
    return (y_prompt, y_sample, jnp.stack(ks, axis=1), jnp.stack(vs, axis=1),
            jnp.stack(cs, axis=1), jnp.stack(ns, axis=1), jnp.stack(ms, axis=1))
```

```python
import functools

import numpy as np
import jax
import jax.numpy as jnp
from jax import lax
from jax.experimental import pallas as pl
from jax.experimental.pallas import tpu as pltpu

F32 = jnp.float32
BF16 = jnp.bfloat16

D_MODEL = 1024
HEAD_DIM = 64
ATT_WIDTH = 512
N_KV_HEADS = 2
ML_WIDTH = 512
N_ML_HEADS = 4
ML_HEAD_DIM = 128
GRID_W = 64
ROPE_BASE = 10000.0
N_EXPERTS = 16
N_GROUPS = 4
GROUP_SIZE = 4
D_EXPERT = 512
EPS = 1e-6
MAIN_WIDTH = 2816
N_GATE_COLS = 16
LANES = 128
TOKEN_TILE = 256
SEQ_BLOCK = 256
MOE_TILE = 2048
MOE_CHUNK = 320
MOE_SLOTS = 2 * MOE_TILE + 8 * N_EXPERTS + MOE_CHUNK
VMEM_LIMIT = 56 * 1024 * 1024
NEG_INF = float("-inf")
HIGHEST = lax.Precision.HIGHEST
NT_DIMS = (((1,), (1,)), ((), ()))
TN_DIMS = (((0,), (0,)), ((), ()))


def _params(semantics):
    return pltpu.CompilerParams(dimension_semantics=semantics, vmem_limit_bytes=VMEM_LIMIT)


def _log_sigmoid(z):
    return jnp.minimum(z, 0.0) - jnp.log1p(jnp.exp(-jnp.abs(z)))


def _split3(x):
    h1 = x.astype(BF16)
    r1 = x - h1.astype(F32)
    h2 = r1.astype(BF16)
    h3 = (r1 - h2.astype(F32)).astype(BF16)
    return h1, h2, h3


def _mod_kernel(cond_ref, w_ref, b_ref, o_ref):
    c = cond_ref[...]
    s = c * jax.nn.sigmoid(c)
    o_ref[...] = jnp.dot(s.astype(BF16), w_ref[...].astype(BF16),
                         preferred_element_type=F32) + b_ref[...]


def _modulation(cond, w_mod, b_mod):
    n_layers = w_mod.shape[0]
    n_chunks = w_mod.shape[2] // D_MODEL
    return pl.pallas_call(
        _mod_kernel,
        grid=(n_layers, n_chunks),
        in_specs=[
            pl.BlockSpec((16, D_MODEL), lambda l, j: (0, 0)),
            pl.BlockSpec((None, D_MODEL, D_MODEL), lambda l, j: (l, 0, j)),
            pl.BlockSpec((None, 1, D_MODEL), lambda l, j: (l, 0, j)),
        ],
        out_specs=pl.BlockSpec((None, 16, D_MODEL), lambda l, j: (l, 0, j)),
        out_shape=jax.ShapeDtypeStruct((n_layers, 16, w_mod.shape[2]), F32),
        compiler_params=_params(("parallel", "parallel")),
        name="modulation",
    )(cond, w_mod, b_mod.reshape(n_layers, 1, -1))


def _inproj_kernel(x_ref, sh_ref, sc_ref, g_ref, w_ref, wg_ref, wgt_ref, bcol_ref, brow_ref,
                   gq_ref, gk_ref, cos_ref, sa_ref, sb_ref, gsum_ref,
                   q_ref, k_ref, v_ref, mq_ref, mk_ref, mv_ref, og_ref, gc_ref, gr_ref):
    x = x_ref[...]
    ms = jnp.mean(x * x, axis=-1, keepdims=True)
    h = x * lax.rsqrt(ms + EPS) * g_ref[...]
    h = h * (1.0 + sc_ref[...]) + sh_ref[...]
    hb = h.astype(BF16)
    cos = cos_ref[...]
    sa = sa_ref[...]
    sb = sb_ref[...]
    gsum = gsum_ref[...]

    def proj(c0, width):
        return jnp.dot(hb, w_ref[:, c0:c0 + width], preferred_element_type=F32)

    def headnorm_rope(z, gain):
        sq = z * z
        hi = sq.astype(BF16)
        lo = (sq - hi.astype(F32)).astype(BF16)
        ss = (jnp.dot(hi, gsum, preferred_element_type=F32)
              + jnp.dot(lo, gsum, preferred_element_type=F32))
        zn = z * lax.rsqrt(ss * (1.0 / HEAD_DIM) + EPS) * gain
        return zn * cos + pltpu.roll(zn, LANES - 16, 1) * sa + pltpu.roll(zn, 16, 1) * sb

    for c in range(ATT_WIDTH // LANES):
        z = proj(c * LANES, LANES)
        q_ref[:, c * LANES:(c + 1) * LANES] = (headnorm_rope(z, gq_ref[...]) * 0.125).astype(BF16)
    k_ref[...] = headnorm_rope(proj(512, LANES), gk_ref[...])
    v_ref[...] = proj(640, LANES)
    mq_ref[...] = proj(768, ML_WIDTH).astype(BF16)
    mk_ref[...] = (proj(1280, ML_WIDTH) * (ML_HEAD_DIM ** -0.5)).astype(BF16)
    mv_ref[...] = proj(1792, ML_WIDTH).astype(BF16)
    og_ref[...] = jax.nn.sigmoid(proj(2304, ML_WIDTH))

    zc = jnp.dot(h, wg_ref[...], precision=HIGHEST, preferred_element_type=F32) + bcol_ref[...]
    lane = lax.broadcasted_iota(jnp.int32, zc.shape, 1)
    gc_ref[...] = jnp.where(lane < 8, zc, _log_sigmoid(zc))
    zr = lax.dot_general(wgt_ref[...], h, NT_DIMS, precision=HIGHEST,
                         preferred_element_type=F32) + brow_ref[...]
    sub = lax.broadcasted_iota(jnp.int32, zr.shape, 0)
    gr_ref[...] = jnp.where(sub < 8, zr, _log_sigmoid(zr))


def _inproj(x, mod3, n_ctx_tiles, tiles_per_lat_seq, ctx_row, g_mix, w_main, w_gate, b_gate,
            g_q, g_k, rope):
    n = x.shape[0]
    n_tiles = n // TOKEN_TILE

    def mod_row(i):
        return jnp.where(i < n_ctx_tiles, ctx_row, (i - n_ctx_tiles) // tiles_per_lat_seq)

    def rope_blk(i):
        return jnp.where(i < n_ctx_tiles, 0, 1 + (i - n_ctx_tiles) % tiles_per_lat_seq)

    cos_t, sa_t, sb_t = rope
    lane = np.arange(LANES)
    gsum = jnp.asarray((lane[:, None] // HEAD_DIM) == (lane[None, :] // HEAD_DIM), BF16)
    tok = lambda w: pl.BlockSpec((TOKEN_TILE, w), lambda i: (i, 0))
    full = lambda a: pl.BlockSpec(a.shape, lambda i: (0,) * a.ndim)
    modspec = lambda j: pl.BlockSpec((None, 1, D_MODEL), lambda i: (mod_row(i), 0, j))
    ropespec = pl.BlockSpec((TOKEN_TILE, LANES), lambda i: (rope_blk(i), 0))
    args = (x, mod3, mod3, g_mix.reshape(1, -1), w_main, w_gate, w_gate.T,
            b_gate.reshape(1, -1), b_gate.reshape(-1, 1),
            jnp.tile(g_q, 2).reshape(1, -1), jnp.tile(g_k, 2).reshape(1, -1),
            cos_t, sa_t, sb_t, gsum)
    in_specs = [tok(D_MODEL), modspec(0), modspec(1)] + [full(a) for a in args[3:11]] \
        + [ropespec, ropespec, ropespec, full(gsum)]
    out_shape = (
        jax.ShapeDtypeStruct((n, ATT_WIDTH), BF16),
        jax.ShapeDtypeStruct((n, LANES), F32),
        jax.ShapeDtypeStruct((n, LANES), F32),
        jax.ShapeDtypeStruct((n, ML_WIDTH), BF16),
        jax.ShapeDtypeStruct((n, ML_WIDTH), BF16),
        jax.ShapeDtypeStruct((n, ML_WIDTH), BF16),
        jax.ShapeDtypeStruct((n, ML_WIDTH), F32),
        jax.ShapeDtypeStruct((n, N_GATE_COLS), F32),
        jax.ShapeDtypeStruct((N_GATE_COLS, n), F32),
    )
    out_specs = (tok(ATT_WIDTH), tok(LANES), tok(LANES), tok(ML_WIDTH), tok(ML_WIDTH),
                 tok(ML_WIDTH), tok(ML_WIDTH), tok(N_GATE_COLS),
                 pl.BlockSpec((N_GATE_COLS, TOKEN_TILE), lambda i: (0, i)))
    return pl.pallas_call(
        _inproj_kernel, grid=(n_tiles,), in_specs=in_specs, out_specs=out_specs,
        out_shape=out_shape, compiler_params=_params(("parallel",)), name="inproj",
    )(*args)


def _attn_kernel(*refs, n_kv):
    q_ref = refs[0]
    kv_refs = refs[1:1 + 2 * n_kv]
    o_ref = refs[-1]
    tq = q_ref.shape[0]
    lo_q = lax.broadcasted_iota(jnp.int32, (tq, LANES), 1) < HEAD_DIM

    def dup_half(ref, g):
        a = ref[...]
        r = pltpu.roll(a, HEAD_DIM, 1)
        lo = lax.broadcasted_iota(jnp.int32, a.shape, 1) < HEAD_DIM
        d = jnp.where(lo, a, r) if g == 0 else jnp.where(lo, r, a)
        return d.astype(BF16)

    for g in range(N_KV_HEADS):
        ks = [dup_half(kv_refs[2 * p], g) for p in range(n_kv)]
        vs = [dup_half(kv_refs[2 * p + 1], g) for p in range(n_kv)]
        for hb in range(2):
            c0 = (2 * g + hb) * LANES
            qb = q_ref[:, c0:c0 + LANES]
            outs = []
            for half in range(2):
                keep = lo_q if half == 0 else jnp.logical_not(lo_q)
                qm = jnp.where(keep, qb, jnp.zeros_like(qb))
                ss = [lax.dot_general(qm, kd, NT_DIMS, preferred_element_type=F32) for kd in ks]
                m = functools.reduce(jnp.maximum, [jnp.max(s, axis=1, keepdims=True) for s in ss])
                ps = [jnp.exp(s - m) for s in ss]
                den = functools.reduce(jnp.add, [jnp.sum(p, axis=1, keepdims=True) for p in ps])
                o = functools.reduce(jnp.add, [
                    jnp.dot(p.astype(BF16), vd, preferred_element_type=F32)
                    for p, vd in zip(ps, vs)])
                outs.append(o / den)
            o_ref[:, c0:c0 + LANES] = jnp.where(lo_q, outs[0], outs[1]).astype(BF16)


def _attention(q, k, v, row0, n_seq, seq_len, cache=None):
    nq = seq_len // SEQ_BLOCK
    qb0 = row0 // SEQ_BLOCK
    sb0 = row0 // seq_len
    in_specs = [
        pl.BlockSpec((SEQ_BLOCK, ATT_WIDTH), lambda b, i: (qb0 + b * nq + i, 0)),
        pl.BlockSpec((seq_len, LANES), lambda b, i: (sb0 + b, 0)),
        pl.BlockSpec((seq_len, LANES), lambda b, i: (sb0 + b, 0)),
    ]
    args = [q, k, v]
    n_kv = 1
    if cache is not None:
        ck, cv = cache
        past = ck.shape[0] // n_seq
        in_specs += [pl.BlockSpec((past, LANES), lambda b, i: (b, 0))] * 2
        args += [ck, cv]
        n_kv = 2
    return pl.pallas_call(
        functools.partial(_attn_kernel, n_kv=n_kv),
        grid=(n_seq, nq), in_specs=in_specs,
        out_specs=pl.BlockSpec((SEQ_BLOCK, ATT_WIDTH), lambda b, i: (b * nq + i, 0)),
        out_shape=jax.ShapeDtypeStruct((n_seq * seq_len, ATT_WIDTH), BF16),
        compiler_params=_params(("parallel", "parallel")), name="attention",
    )(*args)


def _mlstm_kernel(*refs, seq_len, has_init, emit_state):
    it = iter(refs)
    q_ref, k_ref, v_ref, og_ref, gmh_ref, gcol_ref, grow_ref, u_ref, l_ref = [next(it) for _ in range(9)]
    if has_init:
        c0_ref, nm0_ref = next(it), next(it)
    ml_ref = next(it)
    if emit_state:
        cf_ref, nmf_ref = next(it), next(it)

    bq = SEQ_BLOCK
    nb = seq_len // bq
    blk = lambda j: slice(j * bq, (j + 1) * bq)
    upper_incl = u_ref[...]
    lower_incl = l_ref[...]

    def tri_dot(x, tri):
        return functools.reduce(jnp.add, [jnp.dot(p, tri, preferred_element_type=F32)
                                          for p in _split3(x)])

    ig_row = [[None] * nb for _ in range(2)]
    lf_row = [[None] * nb for _ in range(2)]
    within = [[None] * nb for _ in range(2)]
    bsum = [[None] * nb for _ in range(2)]
    for j in range(nb):
        g8 = grow_ref[:, blk(j)]
        cum_f = tri_dot(g8, upper_incl)
        cum_b = tri_dot(g8, lower_incl)
        for d in range(2):
            ig_row[d][j] = g8[d:d + 1, :]
            lf_row[d][j] = g8[2 + d:3 + d, :]
            within[d][j] = (cum_f if d == 0 else cum_b)[2 + d:3 + d, :]
            bsum[d][j] = jnp.sum(lf_row[d][j], axis=1, keepdims=True)
    zero11 = jnp.zeros((1, 1), F32)
    offset = [[None] * nb for _ in range(2)]
    acc = zero11
    for j in range(nb):
        offset[0][j] = acc
        acc = acc + bsum[0][j]
    total_f = acc
    acc = zero11
    for j in reversed(range(nb)):
        offset[1][j] = acc
        acc = acc + bsum[1][j]
    total = [total_f, acc]
    a_row = [[ig_row[d][j] - (within[d][j] + offset[d][j]) for j in range(nb)] for d in range(2)]
    blkmax = [[jnp.max(a_row[d][j], axis=1, keepdims=True) for j in range(nb)] for d in range(2)]
    if has_init:
        m0 = [nm0_ref[2:3, 0:1], nm0_ref[3:4, 0:1]]
    else:
        m0 = [zero11, zero11]

    r_i = lax.broadcasted_iota(jnp.int32, (bq, bq), 0)
    c_i = lax.broadcasted_iota(jnp.int32, (bq, bq), 1)
    causal = [c_i <= r_i, c_i >= r_i]
    before = [lambda i: range(0, i), lambda i: range(i + 1, nb)]

    gmh = gmh_ref[...]
    b_col = [[None] * nb for _ in range(2)]
    for i in range(nb):
        q_i = q_ref[blk(i), :]
        a_blocks = {}

        def scores(j):
            if j not in a_blocks:
                a_blocks[j] = lax.dot_general(q_i, k_ref[blk(j), :], NT_DIMS,
                                              preferred_element_type=F32)
            return a_blocks[j]

        h = None
        for d in range(2):
            m_prev = functools.reduce(jnp.maximum, [blkmax[d][j] for j in before[d](i)], m0[d])
            b_col[d][i] = offset[d][i] + jnp.sum(jnp.where(causal[d], lf_row[d][i], 0.0),
                                                 axis=1, keepdims=True)
            m_col = jnp.maximum(m_prev, jnp.max(jnp.where(causal[d], a_row[d][i], NEG_INF),
                                                axis=1, keepdims=True))
            num = jnp.zeros((bq, ML_HEAD_DIM), F32)
            den = jnp.zeros((bq, 1), F32)
            for j in list(before[d](i)) + [i]:
                arg = a_row[d][j] - m_col
                if j == i:
                    arg = jnp.where(causal[d], arg, NEG_INF)
                p = jnp.exp(arg) * scores(j)
                den = den + jnp.sum(p, axis=1, keepdims=True)
                num = num + jnp.dot(p.astype(BF16), v_ref[blk(j), :], preferred_element_type=F32)
            if has_init:
                w_inter = jnp.exp(m0[d] - m_col)
                qc = jnp.dot(q_i, c0_ref[d].astype(BF16), preferred_element_type=F32)
                qn = jnp.sum(q_i.astype(F32) * nm0_ref[d:d + 1, :], axis=1, keepdims=True)
                num = num + w_inter * qc
                den = den + w_inter * qn
            nrm = jnp.maximum(jnp.abs(den), jnp.exp(-(b_col[d][i] + m_col)))
            h = num / nrm if h is None else h + num / nrm
        hn = h * lax.rsqrt(jnp.mean(h * h, axis=-1, keepdims=True) + EPS) * gmh
        ml_ref[blk(i), :] = (og_ref[blk(i), :] * hn).astype(BF16)

    if emit_state:
        nmf_ref[...] = jnp.zeros_like(nmf_ref)
        for d in range(2):
            m_last = functools.reduce(jnp.maximum, blkmax[d], m0[d])
            c_fin = jnp.zeros((ML_HEAD_DIM, ML_HEAD_DIM), F32)
            n_fin = jnp.zeros((1, ML_HEAD_DIM), F32)
            for j in range(nb):
                a_col = gcol_ref[blk(j), d:d + 1] - b_col[d][j]
                kw = k_ref[blk(j), :].astype(F32) * jnp.exp(a_col - m_last)
                c_fin = c_fin + lax.dot_general(kw.astype(BF16), v_ref[blk(j), :], TN_DIMS,
                                                preferred_element_type=F32)
                n_fin = n_fin + jnp.sum(kw, axis=0, keepdims=True)
            if has_init:
                decay = jnp.exp(m0[d] - m_last)
                c_fin = c_fin + decay * c0_ref[d]
                n_fin = n_fin + decay * nm0_ref[d:d + 1, :]
            cf_ref[d] = c_fin
            nmf_ref[d:d + 1, :] = n_fin
            nmf_ref[2 + d:3 + d, :] = jnp.broadcast_to(total[d] + m_last, (1, ML_HEAD_DIM))


def _mlstm(mq, mk, mv, og, g_mh, gcol, grow, row0, n_seq, seq_len, init=None, emit_state=False):
    sb0 = row0 // seq_len
    tri = np.arange(SEQ_BLOCK)
    upper_incl = jnp.asarray(tri[:, None] <= tri[None, :], BF16)
    lower_incl = jnp.asarray(tri[:, None] >= tri[None, :], BF16)
    headblk = lambda: pl.BlockSpec((seq_len, ML_HEAD_DIM), lambda b, h: (sb0 + b, h))
    const = lambda a: pl.BlockSpec(a.shape, lambda b, h: (0,) * a.ndim)
    in_specs = [headblk(), headblk(), headblk(), headblk(),
                pl.BlockSpec((1, ML_HEAD_DIM), lambda b, h: (0, h)),
                pl.BlockSpec((None, seq_len, 4), lambda b, h: (h, sb0 + b, 0)),
                pl.BlockSpec((None, 8, seq_len), lambda b, h: (h, 0, sb0 + b)),
                const(upper_incl), const(lower_incl)]
    args = [mq, mk, mv, og, g_mh.reshape(1, -1), gcol, grow, upper_incl, lower_incl]
    if init is not None:
        c0, nm0, layer = init
        in_specs += [
            pl.BlockSpec((None, None, 2, None, ML_HEAD_DIM, ML_HEAD_DIM),
                         lambda b, h: (b, layer, 0, h, 0, 0)),
            pl.BlockSpec((None, None, 8, ML_HEAD_DIM), lambda b, h: (b, h, 0, 0))]
        args += [c0, nm0]
    out_shape = [jax.ShapeDtypeStruct((n_seq * seq_len, ML_WIDTH), BF16)]
    out_specs = [pl.BlockSpec((seq_len, ML_HEAD_DIM), lambda b, h: (b, h))]
    if emit_state:
        out_shape += [jax.ShapeDtypeStruct((n_seq, 2, N_ML_HEADS, ML_HEAD_DIM, ML_HEAD_DIM), F32),
                      jax.ShapeDtypeStruct((n_seq, N_ML_HEADS, 8, ML_HEAD_DIM), F32)]
        out_specs += [pl.BlockSpec((None, 2, None, ML_HEAD_DIM, ML_HEAD_DIM),
                                   lambda b, h: (b, 0, h, 0, 0)),
                      pl.BlockSpec((None, None, 8, ML_HEAD_DIM), lambda b, h: (b, h, 0, 0))]
    return pl.pallas_call(
        functools.partial(_mlstm_kernel, seq_len=seq_len, has_init=init is not None,
                          emit_state=emit_state),
        grid=(n_seq, N_ML_HEADS), in_specs=in_specs, out_specs=tuple(out_specs),
        out_shape=tuple(out_shape),
        compiler_params=_params(("parallel", "parallel")), name="mlstm",
    )(*args)


def _outproj_kernel(att_ref, ml_ref, x_ref, w_ref, gt_ref, sh_ref, sc_ref, g_ref, wrt_ref, br_ref,
                    x1_ref, hp_ref, route_ref):
    y = (jnp.dot(att_ref[...], w_ref[:ATT_WIDTH, :], preferred_element_type=F32)
         + jnp.dot(ml_ref[...], w_ref[ATT_WIDTH:, :], preferred_element_type=F32))
    x1 = x_ref[...] + gt_ref[...] * y
    x1_ref[...] = x1
    ms = jnp.mean(x1 * x1, axis=-1, keepdims=True)
    h2 = x1 * lax.rsqrt(ms + EPS) * g_ref[...]
    h2 = h2 * (1.0 + sc_ref[...]) + sh_ref[...]
    hp_ref[...] = h2

    logits = lax.dot_general(wrt_ref[...], h2, NT_DIMS, precision=HIGHEST,
                             preferred_element_type=F32)
    ex = jnp.exp(logits - jnp.max(logits, axis=0, keepdims=True))
    scores = ex / jnp.sum(ex, axis=0, keepdims=True)
    sel = scores + br_ref[...]
    row = lambda a, e: a[e:e + 1, :]
    grp_score = []
    for g in range(N_GROUPS):
        xs = [row(sel, g * GROUP_SIZE + j) for j in range(GROUP_SIZE)]
        pairs = [xs[a] + xs[b] for a in range(GROUP_SIZE) for b in range(a + 1, GROUP_SIZE)]
        grp_score.append(functools.reduce(jnp.maximum, pairs))
    best = grp_score[0]
    grp = jnp.zeros_like(best, dtype=jnp.int32)
    for g in range(1, N_GROUPS):
        better = grp_score[g] > best
        grp = jnp.where(better, g, grp)
        best = jnp.where(better, grp_score[g], best)
    pick = lambda a, j: functools.reduce(
        lambda acc, g: jnp.where(grp == g, row(a, g * GROUP_SIZE + j), acc),
        range(1, N_GROUPS), row(a, j))
    xs = [pick(sel, j) for j in range(GROUP_SIZE)]
    ws = [pick(scores, j) for j in range(GROUP_SIZE)]

    def argmax4(vals):
        bv, bi = vals[0], jnp.zeros_like(grp)
        for j in range(1, GROUP_SIZE):
            better = vals[j] > bv
            bi = jnp.where(better, j, bi)
            bv = jnp.where(better, vals[j], bv)
        return bi

    i1 = argmax4(xs)
    i2 = argmax4([jnp.where(i1 == j, NEG_INF, xs[j]) for j in range(GROUP_SIZE)])
    take = lambda vals, idx: functools.reduce(
        lambda acc, j: jnp.where(idx == j, vals[j], acc), range(1, GROUP_SIZE), vals[0])
    w1, w2 = take(ws, i1), take(ws, i2)
    wsum = w1 + w2
    w1, w2 = w1 / wsum, w2 / wsum
    e1 = grp * GROUP_SIZE + i1
    e2 = grp * GROUP_SIZE + i2
    route_ref[...] = jnp.zeros_like(route_ref)
    route_ref[0:1, :] = e1.astype(F32)
    route_ref[1:2, :] = e2.astype(F32)
    route_ref[2:3, :] = w1
    route_ref[3:4, :] = w2


def _outproj(att, ml, x, mod3, n_ctx_tiles, tiles_per_lat_seq, ctx_row, w_out, g_ffn, w_router,
             b_router):
    n = x.shape[0]

    def mod_row(i):
        return jnp.where(i < n_ctx_tiles, ctx_row, (i - n_ctx_tiles) // tiles_per_lat_seq)

    tok = lambda w: pl.BlockSpec((TOKEN_TILE, w), lambda i: (i, 0))
    full = lambda a: pl.BlockSpec(a.shape, lambda i: (0,) * a.ndim)
    modspec = lambda j: pl.BlockSpec((None, 1, D_MODEL), lambda i: (mod_row(i), 0, j))
    args = (att, ml, x, w_out, mod3, mod3, mod3, g_ffn.reshape(1, -1), w_router.T,
            b_router.reshape(-1, 1))
    in_specs = [tok(ATT_WIDTH), tok(ML_WIDTH), tok(D_MODEL), full(w_out),
                modspec(2), modspec(3), modspec(4), full(args[7]), full(args[8]), full(args[9])]
    return pl.pallas_call(
        _outproj_kernel, grid=(n // TOKEN_TILE,), in_specs=in_specs,
        out_specs=(tok(D_MODEL), tok(D_MODEL),
                   pl.BlockSpec((8, TOKEN_TILE), lambda i: (0, i))),
        out_shape=(jax.ShapeDtypeStruct((n, D_MODEL), F32),
                   jax.ShapeDtypeStruct((n, D_MODEL), F32),
                   jax.ShapeDtypeStruct((8, n), F32)),
        compiler_params=_params(("parallel",)), name="outproj_router",
    )(*args)


def _moe_kernel(order_ref, pos0_ref, pos1_ref, off_ref, cnt_ref,
                h_ref, gw_ref, wg_ref, wu_ref, wd_ref, x1_ref, gt_ref, y_ref,
                o_scr, xs_scr, comb_scr):
    t = pl.program_id(0)
    s = pl.program_id(1)

    @pl.when(s == 0)
    def _init():
        xs_scr[...] = jnp.zeros_like(xs_scr)

    @pl.when(s < N_EXPERTS)
    def _expert():
        seg = t * N_EXPERTS + s
        off = off_ref[seg]
        cnt = cnt_ref[seg]

        def chunk(c, carry):
            base = pl.multiple_of(off + c * MOE_CHUNK, 8)
            rows = jnp.minimum(cnt - c * MOE_CHUNK, MOE_CHUNK)

            def gather8(i, carry2):
                slot = t * MOE_SLOTS + base + i * 8
                for k in range(8):
                    src = order_ref[slot + k]
                    xs_scr[i, pl.ds(k, 1), :] = h_ref[pl.ds(src, 1), :]
                return carry2

            lax.fori_loop(0, (rows + 7) // 8, gather8, 0)
            xs = xs_scr[...].reshape(MOE_CHUNK, D_MODEL).astype(BF16)
            a = jnp.dot(xs, wg_ref[...], preferred_element_type=F32)
            b = jnp.dot(xs, wu_ref[...], preferred_element_type=F32)
            hid = (a * jax.nn.sigmoid(a)) * b * gw_ref[pl.ds(base, MOE_CHUNK), :]
            o_scr[pl.ds(base, MOE_CHUNK), :] = jnp.dot(hid.astype(BF16), wd_ref[...],
                                                       preferred_element_type=F32)
            return carry

        lax.fori_loop(0, (cnt + MOE_CHUNK - 1) // MOE_CHUNK, chunk, 0)

    @pl.when(s >= N_EXPERTS)
    def _combine():
        tok0 = t * MOE_TILE + (s - N_EXPERTS) * TOKEN_TILE

        def body8(i, carry):
            for k in range(8):
                p0 = pos0_ref[tok0 + i * 8 + k]
                p1 = pos1_ref[tok0 + i * 8 + k]
                comb_scr[i, pl.ds(k, 1), :] = o_scr[pl.ds(p0, 1), :] + o_scr[pl.ds(p1, 1), :]
            return carry

        lax.fori_loop(0, TOKEN_TILE // 8, body8, 0)
        comb = comb_scr[...].reshape(TOKEN_TILE, D_MODEL)
        y_ref[...] = x1_ref[...] + gt_ref[...] * comb


def _route_tables(route):
    n = route.shape[1]
    nt = n // MOE_TILE
    e = route[0:2].astype(jnp.int32).T.reshape(nt, MOE_TILE, 2)
    w = route[2:4].T.reshape(nt, MOE_TILE, 2)
    onehot = (e[..., None] == jnp.arange(N_EXPERTS, dtype=jnp.int32)).astype(jnp.int32).sum(2)
    count = onehot.sum(1)
    rank = jnp.cumsum(onehot, axis=1) - onehot
    seg = (count + 7) // 8 * 8
    off = jnp.cumsum(seg, axis=1) - seg
    pos = jnp.take_along_axis(off[:, None, :] + rank, e, axis=2)
    tile_ix = jnp.arange(nt, dtype=jnp.int32)[:, None, None]
    tok_ix = jnp.broadcast_to(jnp.arange(MOE_TILE, dtype=jnp.int32)[None, :, None], pos.shape)
    order = jnp.zeros((nt, MOE_SLOTS), jnp.int32).at[tile_ix, pos].set(tok_ix)
    gw = jnp.zeros((nt, MOE_SLOTS), F32).at[tile_ix, pos].set(w)
    return (order.reshape(-1), pos[..., 0].reshape(-1), pos[..., 1].reshape(-1),
            off.reshape(-1), count.reshape(-1), gw[..., None])


def _moe(hp, route, x1, mod3, layer, n_ctx_tiles, tiles_per_lat_seq, ctx_row, wg, wu, wd):
    n = hp.shape[0]
    order, pos0, pos1, off, count, gw = _route_tables(route)
    chunks_per_tile = MOE_TILE // TOKEN_TILE
    n_steps = N_EXPERTS + chunks_per_tile

    def chunk_ix(t, s):
        return t * chunks_per_tile + jnp.maximum(s - N_EXPERTS, 0)

    def mod_row(g):
        return jnp.where(g < n_ctx_tiles, ctx_row, (g - n_ctx_tiles) // tiles_per_lat_seq)

    wspec = lambda r, c: pl.BlockSpec(
        (None, None, r, c), lambda t, s, *_: (layer, jnp.minimum(s, N_EXPERTS - 1), 0, 0))
    chunk_spec = pl.BlockSpec((TOKEN_TILE, D_MODEL), lambda t, s, *_: (chunk_ix(t, s), 0))
    grid_spec = pltpu.PrefetchScalarGridSpec(
        num_scalar_prefetch=5,
        grid=(n // MOE_TILE, n_steps),
        in_specs=[
            pl.BlockSpec((MOE_TILE, D_MODEL), lambda t, s, *_: (t, 0),
                         pipeline_mode=pl.Buffered(1)),
            pl.BlockSpec((None, MOE_SLOTS, 1), lambda t, s, *_: (t, 0, 0)),
            wspec(D_MODEL, D_EXPERT), wspec(D_MODEL, D_EXPERT), wspec(D_EXPERT, D_MODEL),
            chunk_spec,
            pl.BlockSpec((None, 1, D_MODEL), lambda t, s, *_: (mod_row(chunk_ix(t, s)), 0, 5)),
        ],
        out_specs=chunk_spec,
        scratch_shapes=[pltpu.VMEM((MOE_SLOTS, D_MODEL), F32),
                        pltpu.VMEM((MOE_CHUNK // 8, 8, D_MODEL), F32),
                        pltpu.VMEM((TOKEN_TILE // 8, 8, D_MODEL), F32)],
    )
    return pl.pallas_call(
        _moe_kernel, grid_spec=grid_spec,
        out_shape=jax.ShapeDtypeStruct((n, D_MODEL), F32),
        compiler_params=_params(("parallel", "arbitrary")), name="experts",
    )(order, pos0, pos1, off, count, hp, gw, wg, wu, wd, x1, mod3)


def _rope_tables(seq_len):
    half = HEAD_DIM // 2
    freqs = ROPE_BASE ** (-np.arange(0, half, 2, dtype=np.float64) / half)
    pos = np.arange(seq_len)
    row, col = pos // GRID_W, pos % GRID_W
    d = np.arange(HEAD_DIM)
    position = np.where(d[None, :] < half, row[:, None], col[:, None]).astype(np.float64)
    ang = (position.astype(np.float32) * freqs.astype(np.float32)[d % (half // 2)][None, :]).astype(np.float32)
    cos, sin = np.cos(ang), np.sin(ang)
    first = (d % half) < half // 2
    sa = np.where(first[None, :], -sin, 0.0)
    sb = np.where(first[None, :], 0.0, sin)
    ident = lambda v: np.full((SEQ_BLOCK, HEAD_DIM), v, np.float32)
    stack = lambda ctx, lat: jnp.asarray(
        np.tile(np.concatenate([ctx, lat.astype(np.float32)], axis=0), (1, 2)), F32)
    return stack(ident(1.0), cos), stack(ident(0.0), sa), stack(ident(0.0), sb)


def kernel(x_prompt, x_sample, c, cache_k, cache_v, state_C, state_n, state_m, c_ctx, w_mod, b_mod,
           g_mix, g_ffn, w_in, b_igate, b_fgate, g_q, g_k, g_mh, w_out, w_router, b_router,
           w_e_gate, w_e_up, w_e_down):
    n_ctx_seq, ctx_len, _ = x_prompt.shape
    n_lat_seq, lat_len, _ = x_sample.shape
    n_layers = w_mod.shape[0]
    n_ctx = n_ctx_seq * ctx_len
    assert ctx_len == SEQ_BLOCK and lat_len % TOKEN_TILE == 0
    assert n_ctx % MOE_TILE == 0 and (n_lat_seq * lat_len) % MOE_TILE == 0
    assert n_lat_seq < 16 and n_ctx % lat_len == 0
    n_ctx_tiles = n_ctx // TOKEN_TILE
    tiles_per_lat_seq = lat_len // TOKEN_TILE
    ctx_row = n_lat_seq

    x = jnp.concatenate([x_prompt.reshape(n_ctx, D_MODEL), x_sample.reshape(-1, D_MODEL)], axis=0)
    cond = jnp.zeros((16, D_MODEL), F32).at[:n_lat_seq].set(c).at[ctx_row].set(c_ctx)
    mod = _modulation(cond, w_mod, b_mod)
    rope = _rope_tables(lat_len)

    wg_b, wu_b, wd_b = w_e_gate.astype(BF16), w_e_up.astype(BF16), w_e_down.astype(BF16)

    ks, vs, cs, ns, ms = [], [], [], [], []
    for l in range(n_layers):
        mod3 = mod[l].reshape(16, 1, -1)
        w_main = w_in[l, :, :MAIN_WIDTH].astype(BF16)
        w_gate = w_in[l, :, MAIN_WIDTH:]
        b_gate = jnp.concatenate([b_igate[l].reshape(-1), b_fgate[l].reshape(-1)])
        q, k, v, mq, mk, mv, og, gc, gr = _inproj(
            x, mod3, n_ctx_tiles, tiles_per_lat_seq, ctx_row, g_mix[l], w_main, w_gate, b_gate,
            g_q[l], g_k[l], rope)
        n = x.shape[0]
        gcol = gc.reshape(n, 4, N_ML_HEADS).transpose(2, 0, 1)
        grow = gr.reshape(4, N_ML_HEADS, n).transpose(1, 0, 2)
        grow = jnp.concatenate([grow, jnp.zeros_like(grow)], axis=1)

        att_ctx = _attention(q, k, v, 0, n_ctx_seq, ctx_len)
        past = cache_k.shape[2]
        ck = cache_k[:, l].reshape(n_lat_seq * past, LANES)
        cv = cache_v[:, l].reshape(n_lat_seq * past, LANES)
        att_lat = _attention(q, k, v, n_ctx, n_lat_seq, lat_len, cache=(ck, cv))

        ml_ctx, c_fin, nm_fin = _mlstm(mq, mk, mv, og, g_mh[l], gcol, grow, 0, n_ctx_seq, ctx_len,
                                       emit_state=True)
        n0 = state_n[:, l].transpose(0, 2, 1, 3)
        m0 = jnp.broadcast_to(state_m[:, l].transpose(0, 2, 1)[..., None], n0.shape)
        nm0 = jnp.concatenate([n0, m0, jnp.zeros_like(n0), jnp.zeros_like(n0)], axis=2)
        (ml_lat,) = _mlstm(mq, mk, mv, og, g_mh[l], gcol, grow, n_ctx, n_lat_seq, lat_len,
                           init=(state_C, nm0, l))

        att = jnp.concatenate([att_ctx, att_lat], axis=0)
        ml = jnp.concatenate([ml_ctx, ml_lat], axis=0)
        x1, hp, route = _outproj(att, ml, x, mod3, n_ctx_tiles, tiles_per_lat_seq, ctx_row,
                                 w_out[l].astype(BF16), g_ffn[l], w_router, b_router)
        x = _moe(hp, route, x1, mod3, l, n_ctx_tiles, tiles_per_lat_seq, ctx_row, wg_b, wu_b, wd_b)

        ks.append(k[:n_ctx].reshape(n_ctx_seq, ctx_len, N_KV_HEADS, HEAD_DIM))
        vs.append(v[:n_ctx].reshape(n_ctx_seq, ctx_len, N_KV_HEADS, HEAD_DIM))
        cs.append(c_fin)
        ns.append(nm_fin[:, :, 0:2, :].transpose(0, 2, 1, 3))
        ms.append(nm_fin[:, :, 2:4, 0].transpose(0, 2, 1))

    y_prompt = x[:n_ctx].reshape(x_prompt.shape)
    y_sample = x[n_ctx:].reshape(x_sample.shape)
    return (y_prompt, y_sample, jnp.stack(ks, axis=1), jnp.stack(vs, axis=1),
            jnp.stack(cs, axis=1), jnp.stack(ns, axis=1), jnp.stack(ms, axis=1))
```

```python
import functools

import numpy as np
import jax
import jax.numpy as jnp
from jax import lax
from jax.experimental import pallas as pl
from jax.experimental.pallas import tpu as pltpu

F32 = jnp.float32
BF16 = jnp.bfloat16

D_MODEL = 1024
HEAD_DIM = 64
ATT_WIDTH = 512
N_KV_HEADS = 2
ML_WIDTH = 512
N_ML_HEADS = 4
ML_HEAD_DIM = 128
GRID_W = 64
ROPE_BASE = 10000.0
N_EXPERTS = 16
N_GROUPS = 4
GROUP_SIZE = 4
D_EXPERT = 512
EPS = 1e-6
MAIN_WIDTH = 2816
N_GATE_COLS = 16
LANES = 128
TOKEN_TILE = 256
SEQ_BLOCK = 256
MOE_TILE = 2048
MOE_CHUNK = 320
ROUTE_SLOT_BLOCKS = 9
MOE_SLOTS = -(-(2 * MOE_TILE + 8 * N_EXPERTS + MOE_CHUNK) // (512 * ROUTE_SLOT_BLOCKS)) * 512 * ROUTE_SLOT_BLOCKS
VMEM_LIMIT = 56 * 1024 * 1024
NEG_INF = float("-inf")
HIGHEST = lax.Precision.HIGHEST
NT_DIMS = (((1,), (1,)), ((), ()))
TN_DIMS = (((0,), (0,)), ((), ()))


def _params(semantics):
    return pltpu.CompilerParams(dimension_semantics=semantics, vmem_limit_bytes=VMEM_LIMIT)


def _log_sigmoid(z):
    return jnp.minimum(z, 0.0) - jnp.log1p(jnp.exp(-jnp.abs(z)))


def _split3(x):
    h1 = x.astype(BF16)
    r1 = x - h1.astype(F32)
    h2 = r1.astype(BF16)
    h3 = (r1 - h2.astype(F32)).astype(BF16)
    return h1, h2, h3


def _mod_kernel(cond_ref, w_ref, b_ref, o_ref):
    c = cond_ref[...]
    s = c * jax.nn.sigmoid(c)
    o_ref[...] = jnp.dot(s.astype(BF16), w_ref[...].astype(BF16),
                         preferred_element_type=F32) + b_ref[...]


def _modulation(cond, w_mod, b_mod):
    n_layers = w_mod.shape[0]
    n_chunks = w_mod.shape[2] // D_MODEL
    return pl.pallas_call(
        _mod_kernel,
        grid=(n_layers, n_chunks),
        in_specs=[
            pl.BlockSpec((16, D_MODEL), lambda l, j: (0, 0)),
            pl.BlockSpec((None, D_MODEL, D_MODEL), lambda l, j: (l, 0, j)),
            pl.BlockSpec((None, 1, D_MODEL), lambda l, j: (l, 0, j)),
        ],
        out_specs=pl.BlockSpec((None, 16, D_MODEL), lambda l, j: (l, 0, j)),
        out_shape=jax.ShapeDtypeStruct((n_layers, 16, w_mod.shape[2]), F32),
        compiler_params=_params(("parallel", "parallel")),
        name="modulation",
    )(cond, w_mod, b_mod.reshape(n_layers, 1, -1))


def _two_part(i, n_first, a_ref, b_ref):
    return jnp.where(i < n_first, a_ref[...], b_ref[...])


def _two_part_specs(width, n_first, tile=TOKEN_TILE):
    return [pl.BlockSpec((tile, width), lambda i: (jnp.minimum(i, n_first - 1), 0)),
            pl.BlockSpec((tile, width), lambda i: (jnp.maximum(i - n_first, 0), 0))]


def _inproj_kernel(xa_ref, xb_ref, sh_ref, sc_ref, g_ref, w_ref, wg_ref, wgt_ref, bcol_ref,
                   brow_ref, gq_ref, gk_ref, cos_ref, sa_ref, sb_ref, gsum_ref,
                   q_ref, k_ref, v_ref, mq_ref, mk_ref, mv_ref, og_ref, gc_ref, gr_ref,
                   *, n_ctx_tiles):
    x = _two_part(pl.program_id(0), n_ctx_tiles, xa_ref, xb_ref)
    ms = jnp.mean(x * x, axis=-1, keepdims=True)
    h = x * lax.rsqrt(ms + EPS) * g_ref[...]
    h = h * (1.0 + sc_ref[...]) + sh_ref[...]
    hb = h.astype(BF16)
    cos = cos_ref[...]
    sa = sa_ref[...]
    sb = sb_ref[...]
    gsum = gsum_ref[...]

    def proj(c0, width):
        return jnp.dot(hb, w_ref[:, c0:c0 + width], preferred_element_type=F32)

    def headnorm_rope(z, gain):
        sq = z * z
        hi = sq.astype(BF16)
        lo = (sq - hi.astype(F32)).astype(BF16)
        ss = (jnp.dot(hi, gsum, preferred_element_type=F32)
              + jnp.dot(lo, gsum, preferred_element_type=F32))
        zn = z * lax.rsqrt(ss * (1.0 / HEAD_DIM) + EPS) * gain
        return zn * cos + pltpu.roll(zn, LANES - 16, 1) * sa + pltpu.roll(zn, 16, 1) * sb

    for c in range(ATT_WIDTH // LANES):
        z = proj(c * LANES, LANES)
        q_ref[:, c * LANES:(c + 1) * LANES] = (headnorm_rope(z, gq_ref[...]) * 0.125).astype(BF16)
    k_ref[...] = headnorm_rope(proj(512, LANES), gk_ref[...])
    v_ref[...] = proj(640, LANES)
    mq_ref[...] = proj(768, ML_WIDTH).astype(BF16)
    mk_ref[...] = (proj(1280, ML_WIDTH) * (ML_HEAD_DIM ** -0.5)).astype(BF16)
    mv_ref[...] = proj(1792, ML_WIDTH).astype(BF16)
    og_ref[...] = jax.nn.sigmoid(proj(2304, ML_WIDTH))

    zc = jnp.dot(h, wg_ref[...], precision=HIGHEST, preferred_element_type=F32) + bcol_ref[...]
    lane = lax.broadcasted_iota(jnp.int32, zc.shape, 1)
    gc = jnp.where(lane % 4 < 2, zc, _log_sigmoid(zc))
    zr = lax.dot_general(wgt_ref[...], h, NT_DIMS, precision=HIGHEST,
                         preferred_element_type=F32) + brow_ref[...]
    sub = lax.broadcasted_iota(jnp.int32, zr.shape, 0)
    gr = jnp.where(sub % 4 < 2, zr, _log_sigmoid(zr))
    gr_ref[...] = jnp.zeros_like(gr_ref)
    for hd in range(N_ML_HEADS):
        gc_ref[hd] = gc[:, 4 * hd:4 * hd + 4]
        gr_ref[hd, 0:4, :] = gr[4 * hd:4 * hd + 4, :]


def _inproj(xa, xb, mod3, n_ctx_tiles, tiles_per_lat_seq, ctx_row, g_mix, w_main, w_gate, b_gate,
            g_q, g_k, rope):
    n = xa.shape[0] + xb.shape[0]
    n_tiles = n // TOKEN_TILE

    def mod_row(i):
        return jnp.where(i < n_ctx_tiles, ctx_row, (i - n_ctx_tiles) // tiles_per_lat_seq)

    def rope_blk(i):
        return jnp.where(i < n_ctx_tiles, 0, 1 + (i - n_ctx_tiles) % tiles_per_lat_seq)

    cos_t, sa_t, sb_t = rope
    lane = np.arange(LANES)
    gsum = jnp.asarray((lane[:, None] // HEAD_DIM) == (lane[None, :] // HEAD_DIM), BF16)
    tok = lambda w: pl.BlockSpec((TOKEN_TILE, w), lambda i: (i, 0))
    full = lambda a: pl.BlockSpec(a.shape, lambda i: (0,) * a.ndim)
    modspec = lambda j: pl.BlockSpec((None, 1, D_MODEL), lambda i: (mod_row(i), 0, j))
    ropespec = pl.BlockSpec((TOKEN_TILE, LANES), lambda i: (rope_blk(i), 0))
    consts = (g_mix.reshape(1, -1), w_main, w_gate, w_gate.T,
              b_gate.reshape(1, -1), b_gate.reshape(-1, 1),
              jnp.tile(g_q, 2).reshape(1, -1), jnp.tile(g_k, 2).reshape(1, -1))
    args = (xa, xb, mod3, mod3) + consts + (cos_t, sa_t, sb_t, gsum)
    in_specs = _two_part_specs(D_MODEL, n_ctx_tiles) + [modspec(0), modspec(1)] \
        + [full(a) for a in consts] + [ropespec, ropespec, ropespec, full(gsum)]
    out_shape = (
        jax.ShapeDtypeStruct((n, ATT_WIDTH), BF16),
        jax.ShapeDtypeStruct((n, LANES), F32),
        jax.ShapeDtypeStruct((n, LANES), F32),
        jax.ShapeDtypeStruct((n, ML_WIDTH), BF16),
        jax.ShapeDtypeStruct((n, ML_WIDTH), BF16),
        jax.ShapeDtypeStruct((n, ML_WIDTH), BF16),
        jax.ShapeDtypeStruct((n, ML_WIDTH), F32),
        jax.ShapeDtypeStruct((N_ML_HEADS, n, 4), F32),
        jax.ShapeDtypeStruct((N_ML_HEADS, 8, n), F32),
    )
    out_specs = (tok(ATT_WIDTH), tok(LANES), tok(LANES), tok(ML_WIDTH), tok(ML_WIDTH),
                 tok(ML_WIDTH), tok(ML_WIDTH),
                 pl.BlockSpec((N_ML_HEADS, TOKEN_TILE, 4), lambda i: (0, i, 0)),
                 pl.BlockSpec((N_ML_HEADS, 8, TOKEN_TILE), lambda i: (0, 0, i)))
    return pl.pallas_call(
        functools.partial(_inproj_kernel, n_ctx_tiles=n_ctx_tiles),
        grid=(n_tiles,), in_specs=in_specs, out_specs=out_specs,
        out_shape=out_shape, compiler_params=_params(("parallel",)), name="inproj",
    )(*args)


def _attn_kernel(*refs, n_kv):
    q_ref = refs[0]
    kv_refs = refs[1:1 + 2 * n_kv]
    o_ref = refs[-1]
    tq = q_ref.shape[0]
    lo_q = lax.broadcasted_iota(jnp.int32, (tq, LANES), 1) < HEAD_DIM

    def dup_half(ref, g):
        a = ref[...]
        r = pltpu.roll(a, HEAD_DIM, 1)
        lo = lax.broadcasted_iota(jnp.int32, a.shape, 1) < HEAD_DIM
        d = jnp.where(lo, a, r) if g == 0 else jnp.where(lo, r, a)
        return d.astype(BF16)

    for g in range(N_KV_HEADS):
        ks = [dup_half(kv_refs[2 * p], g) for p in range(n_kv)]
        vs = [dup_half(kv_refs[2 * p + 1], g) for p in range(n_kv)]
        for hb in range(2):
            c0 = (2 * g + hb) * LANES
            qb = q_ref[:, c0:c0 + LANES]
            outs = []
            for half in range(2):
                keep = lo_q if half == 0 else jnp.logical_not(lo_q)
                qm = jnp.where(keep, qb, jnp.zeros_like(qb))
                ss = [lax.dot_general(qm, kd, NT_DIMS, preferred_element_type=F32) for kd in ks]
                m = functools.reduce(jnp.maximum, [jnp.max(s, axis=1, keepdims=True) for s in ss])
                ps = [jnp.exp(s - m) for s in ss]
                den = functools.reduce(jnp.add, [jnp.sum(p, axis=1, keepdims=True) for p in ps])
                o = functools.reduce(jnp.add, [
                    jnp.dot(p.astype(BF16), vd, preferred_element_type=F32)
                    for p, vd in zip(ps, vs)])
                outs.append(o / den)
            o_ref[:, c0:c0 + LANES] = jnp.where(lo_q, outs[0], outs[1]).astype(BF16)


def _attention(q, k, v, row0, n_seq, seq_len, cache=None):
    nq = seq_len // SEQ_BLOCK
    qb0 = row0 // SEQ_BLOCK
    sb0 = row0 // seq_len
    in_specs = [
        pl.BlockSpec((SEQ_BLOCK, ATT_WIDTH), lambda b, i: (qb0 + b * nq + i, 0)),
        pl.BlockSpec((seq_len, LANES), lambda b, i: (sb0 + b, 0)),
        pl.BlockSpec((seq_len, LANES), lambda b, i: (sb0 + b, 0)),
    ]
    args = [q, k, v]
    n_kv = 1
    if cache is not None:
        ck, cv = cache
        past = ck.shape[0] // n_seq
        in_specs += [pl.BlockSpec((past, LANES), lambda b, i: (b, 0))] * 2
        args += [ck, cv]
        n_kv = 2
    return pl.pallas_call(
        functools.partial(_attn_kernel, n_kv=n_kv),
        grid=(n_seq, nq), in_specs=in_specs,
        out_specs=pl.BlockSpec((SEQ_BLOCK, ATT_WIDTH), lambda b, i: (b * nq + i, 0)),
        out_shape=jax.ShapeDtypeStruct((n_seq * seq_len, ATT_WIDTH), BF16),
        compiler_params=_params(("parallel", "parallel")), name="attention",
    )(*args)


def _mlstm_kernel(*refs, seq_len, has_init, emit_state):
    it = iter(refs)
    q_ref, k_ref, v_ref, og_ref, gmh_ref, gcol_ref, grow_ref, u_ref, l_ref = [next(it) for _ in range(9)]
    if has_init:
        c0_ref, nm0_ref = next(it), next(it)
    ml_ref = next(it)
    if emit_state:
        cf_ref, nmf_ref = next(it), next(it)

    bq = SEQ_BLOCK
    nb = seq_len // bq
    blk = lambda j: slice(j * bq, (j + 1) * bq)
    upper_incl = u_ref[...]
    lower_incl = l_ref[...]

    def tri_dot(x, tri):
        return functools.reduce(jnp.add, [jnp.dot(p, tri, preferred_element_type=F32)
                                          for p in _split3(x)])

    ig_row = [[None] * nb for _ in range(2)]
    lf_row = [[None] * nb for _ in range(2)]
    within = [[None] * nb for _ in range(2)]
    bsum = [[None] * nb for _ in range(2)]
    for j in range(nb):
        g8 = grow_ref[:, blk(j)]
        cum_f = tri_dot(g8, upper_incl)
        cum_b = tri_dot(g8, lower_incl)
        for d in range(2):
            ig_row[d][j] = g8[d:d + 1, :]
            lf_row[d][j] = g8[2 + d:3 + d, :]
            within[d][j] = (cum_f if d == 0 else cum_b)[2 + d:3 + d, :]
            bsum[d][j] = jnp.sum(lf_row[d][j], axis=1, keepdims=True)
    zero11 = jnp.zeros((1, 1), F32)
    offset = [[None] * nb for _ in range(2)]
    acc = zero11
    for j in range(nb):
        offset[0][j] = acc
        acc = acc + bsum[0][j]
    total_f = acc
    acc = zero11
    for j in reversed(range(nb)):
        offset[1][j] = acc
        acc = acc + bsum[1][j]
    total = [total_f, acc]
    a_row = [[ig_row[d][j] - (within[d][j] + offset[d][j]) for j in range(nb)] for d in range(2)]
    blkmax = [[jnp.max(a_row[d][j], axis=1, keepdims=True) for j in range(nb)] for d in range(2)]
    if has_init:
        m0 = [nm0_ref[2:3, 0:1], nm0_ref[3:4, 0:1]]
    else:
        m0 = [zero11, zero11]

    r_i = lax.broadcasted_iota(jnp.int32, (bq, bq), 0)
    c_i = lax.broadcasted_iota(jnp.int32, (bq, bq), 1)
    causal = [c_i <= r_i, c_i >= r_i]
    before = [lambda i: range(0, i), lambda i: range(i + 1, nb)]

    gmh = gmh_ref[...]
    b_col = [[None] * nb for _ in range(2)]
    for i in range(nb):
        q_i = q_ref[blk(i), :]
        a_blocks = {}

        def scores(j):
            if j not in a_blocks:
                a_blocks[j] = lax.dot_general(q_i, k_ref[blk(j), :], NT_DIMS,
                                              preferred_element_type=F32)
            return a_blocks[j]

        h = None
        for d in range(2):
            m_prev = functools.reduce(jnp.maximum, [blkmax[d][j] for j in before[d](i)], m0[d])
            b_col[d][i] = offset[d][i] + jnp.sum(jnp.where(causal[d], lf_row[d][i], 0.0),
                                                 axis=1, keepdims=True)
            m_col = jnp.maximum(m_prev, jnp.max(jnp.where(causal[d], a_row[d][i], NEG_INF),
                                                axis=1, keepdims=True))
            num = jnp.zeros((bq, ML_HEAD_DIM), F32)
            den = jnp.zeros((bq, 1), F32)
            for j in list(before[d](i)) + [i]:
                arg = a_row[d][j] - m_col
                if j == i:
                    arg = jnp.where(causal[d], arg, NEG_INF)
                p = jnp.exp(arg) * scores(j)
                den = den + jnp.sum(p, axis=1, keepdims=True)
                num = num + jnp.dot(p.astype(BF16), v_ref[blk(j), :], preferred_element_type=F32)
            if has_init:
                w_inter = jnp.exp(m0[d] - m_col)
                qc = jnp.dot(q_i, c0_ref[d].astype(BF16), preferred_element_type=F32)
                qn = jnp.sum(q_i.astype(F32) * nm0_ref[d:d + 1, :], axis=1, keepdims=True)
                num = num + w_inter * qc
                den = den + w_inter * qn
            nrm = jnp.maximum(jnp.abs(den), jnp.exp(-(b_col[d][i] + m_col)))
            h = num / nrm if h is None else h + num / nrm
        hn = h * lax.rsqrt(jnp.mean(h * h, axis=-1, keepdims=True) + EPS) * gmh
        ml_ref[blk(i), :] = (og_ref[blk(i), :] * hn).astype(BF16)

    if emit_state:
        nmf_ref[...] = jnp.zeros_like(nmf_ref)
        for d in range(2):
            m_last = functools.reduce(jnp.maximum, blkmax[d], m0[d])
            c_fin = jnp.zeros((ML_HEAD_DIM, ML_HEAD_DIM), F32)
            n_fin = jnp.zeros((1, ML_HEAD_DIM), F32)
            for j in range(nb):
                a_col = gcol_ref[blk(j), d:d + 1] - b_col[d][j]
                kw = k_ref[blk(j), :].astype(F32) * jnp.exp(a_col - m_last)
                c_fin = c_fin + lax.dot_general(kw.astype(BF16), v_ref[blk(j), :], TN_DIMS,
                                                preferred_element_type=F32)
                n_fin = n_fin + jnp.sum(kw, axis=0, keepdims=True)
            if has_init:
                decay = jnp.exp(m0[d] - m_last)
                c_fin = c_fin + decay * c0_ref[d]
                n_fin = n_fin + decay * nm0_ref[d:d + 1, :]
            cf_ref[d] = c_fin
            nmf_ref[d:d + 1, :] = n_fin
            nmf_ref[2 + d:3 + d, :] = jnp.broadcast_to(total[d] + m_last, (1, ML_HEAD_DIM))


def _mlstm(mq, mk, mv, og, g_mh, gcol, grow, row0, n_seq, seq_len, init=None, emit_state=False):
    sb0 = row0 // seq_len
    tri = np.arange(SEQ_BLOCK)
    upper_incl = jnp.asarray(tri[:, None] <= tri[None, :], BF16)
    lower_incl = jnp.asarray(tri[:, None] >= tri[None, :], BF16)
    headblk = lambda: pl.BlockSpec((seq_len, ML_HEAD_DIM), lambda b, h: (sb0 + b, h))
    const = lambda a: pl.BlockSpec(a.shape, lambda b, h: (0,) * a.ndim)
    in_specs = [headblk(), headblk(), headblk(), headblk(),
                pl.BlockSpec((1, ML_HEAD_DIM), lambda b, h: (0, h)),
                pl.BlockSpec((None, seq_len, 4), lambda b, h: (h, sb0 + b, 0)),
                pl.BlockSpec((None, 8, seq_len), lambda b, h: (h, 0, sb0 + b)),
                const(upper_incl), const(lower_incl)]
    args = [mq, mk, mv, og, g_mh.reshape(1, -1), gcol, grow, upper_incl, lower_incl]
    if init is not None:
        c0, nm0, layer = init
        in_specs += [
            pl.BlockSpec((None, None, 2, None, ML_HEAD_DIM, ML_HEAD_DIM),
                         lambda b, h: (b, layer, 0, h, 0, 0)),
            pl.BlockSpec((None, None, 8, ML_HEAD_DIM), lambda b, h: (b, h, 0, 0))]
        args += [c0, nm0]
    out_shape = [jax.ShapeDtypeStruct((n_seq * seq_len, ML_WIDTH), BF16)]
    out_specs = [pl.BlockSpec((seq_len, ML_HEAD_DIM), lambda b, h: (b, h))]
    if emit_state:
        out_shape += [jax.ShapeDtypeStruct((n_seq, 2, N_ML_HEADS, ML_HEAD_DIM, ML_HEAD_DIM), F32),
                      jax.ShapeDtypeStruct((n_seq, N_ML_HEADS, 8, ML_HEAD_DIM), F32)]
        out_specs += [pl.BlockSpec((None, 2, None, ML_HEAD_DIM, ML_HEAD_DIM),
                                   lambda b, h: (b, 0, h, 0, 0)),
                      pl.BlockSpec((None, None, 8, ML_HEAD_DIM), lambda b, h: (b, h, 0, 0))]
    return pl.pallas_call(
        functools.partial(_mlstm_kernel, seq_len=seq_len, has_init=init is not None,
                          emit_state=emit_state),
        grid=(n_seq, N_ML_HEADS), in_specs=in_specs, out_specs=tuple(out_specs),
        out_shape=tuple(out_shape),
        compiler_params=_params(("parallel", "parallel")), name="mlstm",
    )(*args)


def _outproj_kernel(atta_ref, attb_ref, mla_ref, mlb_ref, xa_ref, xb_ref, w_ref, gt_ref, sh_ref,
                    sc_ref, g_ref, wrt_ref, br_ref, eye_ref,
                    x1_ref, hp_ref, route_ref, wcol_ref, *, n_ctx_tiles):
    i = pl.program_id(0)
    att = _two_part(i, n_ctx_tiles, atta_ref, attb_ref)
    ml = _two_part(i, n_ctx_tiles, mla_ref, mlb_ref)
    y = (jnp.dot(att, w_ref[:ATT_WIDTH, :], preferred_element_type=F32)
         + jnp.dot(ml, w_ref[ATT_WIDTH:, :], preferred_element_type=F32))
    x1 = _two_part(i, n_ctx_tiles, xa_ref, xb_ref) + gt_ref[...] * y
    x1_ref[...] = x1
    ms = jnp.mean(x1 * x1, axis=-1, keepdims=True)
    h2 = x1 * lax.rsqrt(ms + EPS) * g_ref[...]
    h2 = h2 * (1.0 + sc_ref[...]) + sh_ref[...]
    hp_ref[...] = h2

    logits = lax.dot_general(wrt_ref[...], h2, NT_DIMS, precision=HIGHEST,
                             preferred_element_type=F32)
    ex = jnp.exp(logits - jnp.max(logits, axis=0, keepdims=True))
    scores = ex / jnp.sum(ex, axis=0, keepdims=True)
    sel = scores + br_ref[...]
    row = lambda a, e: a[e:e + 1, :]
    grp_score = []
    for g in range(N_GROUPS):
        xs = [row(sel, g * GROUP_SIZE + j) for j in range(GROUP_SIZE)]
        pairs = [xs[a] + xs[b] for a in range(GROUP_SIZE) for b in range(a + 1, GROUP_SIZE)]
        grp_score.append(functools.reduce(jnp.maximum, pairs))
    best = grp_score[0]
    grp = jnp.zeros_like(best, dtype=jnp.int32)
    for g in range(1, N_GROUPS):
        better = grp_score[g] > best
        grp = jnp.where(better, g, grp)
        best = jnp.where(better, grp_score[g], best)
    pick = lambda a, j: functools.reduce(
        lambda acc, g: jnp.where(grp == g, row(a, g * GROUP_SIZE + j), acc),
        range(1, N_GROUPS), row(a, j))
    xs = [pick(sel, j) for j in range(GROUP_SIZE)]
    ws = [pick(scores, j) for j in range(GROUP_SIZE)]

    def argmax4(vals):
        bv, bi = vals[0], jnp.zeros_like(grp)
        for j in range(1, GROUP_SIZE):
            better = vals[j] > bv
            bi = jnp.where(better, j, bi)
            bv = jnp.where(better, vals[j], bv)
        return bi

    i1 = argmax4(xs)
    i2 = argmax4([jnp.where(i1 == j, NEG_INF, xs[j]) for j in range(GROUP_SIZE)])
    take = lambda vals, idx: functools.reduce(
        lambda acc, j: jnp.where(idx == j, vals[j], acc), range(1, GROUP_SIZE), vals[0])
    w1, w2 = take(ws, i1), take(ws, i2)
    wsum = w1 + w2
    w1, w2 = w1 / wsum, w2 / wsum
    e1 = grp * GROUP_SIZE + i1
    e2 = grp * GROUP_SIZE + i2
    route_ref[...] = jnp.zeros_like(route_ref)
    route_ref[0:1, :] = e1.astype(F32)
    route_ref[1:2, :] = e2.astype(F32)
    route_ref[2:3, :] = w1
    route_ref[3:4, :] = w2
    eye = eye_ref[...]
    wcol_ref[...] = functools.reduce(jnp.add, [
        lax.dot_general(eye, p, NT_DIMS, preferred_element_type=F32)
        for p in _split3(route_ref[...])])


def _outproj(att, ml, x, mod3, n_ctx_tiles, tiles_per_lat_seq, ctx_row, w_out, g_ffn, w_router,
             b_router):
    n = x[0].shape[0] + x[1].shape[0]

    def mod_row(i):
        return jnp.where(i < n_ctx_tiles, ctx_row, (i - n_ctx_tiles) // tiles_per_lat_seq)

    tok = lambda w: pl.BlockSpec((TOKEN_TILE, w), lambda i: (i, 0))
    full = lambda a: pl.BlockSpec(a.shape, lambda i: (0,) * a.ndim)
    modspec = lambda j: pl.BlockSpec((None, 1, D_MODEL), lambda i: (mod_row(i), 0, j))
    consts = (g_ffn.reshape(1, -1), w_router.T, b_router.reshape(-1, 1),
              jnp.asarray(np.eye(TOKEN_TILE), BF16))
    args = (*att, *ml, *x, w_out, mod3, mod3, mod3) + consts
    in_specs = (_two_part_specs(ATT_WIDTH, n_ctx_tiles) + _two_part_specs(ML_WIDTH, n_ctx_tiles)
                + _two_part_specs(D_MODEL, n_ctx_tiles)
                + [full(w_out), modspec(2), modspec(3), modspec(4)] + [full(a) for a in consts])
    return pl.pallas_call(
        functools.partial(_outproj_kernel, n_ctx_tiles=n_ctx_tiles),
        grid=(n // TOKEN_TILE,), in_specs=in_specs,
        out_specs=(tok(D_MODEL), tok(D_MODEL),
                   pl.BlockSpec((8, TOKEN_TILE), lambda i: (0, i)), tok(8)),
        out_shape=(jax.ShapeDtypeStruct((n, D_MODEL), F32),
                   jax.ShapeDtypeStruct((n, D_MODEL), F32),
                   jax.ShapeDtypeStruct((8, n), F32),
                   jax.ShapeDtypeStruct((n, 8), F32)),
        compiler_params=_params(("parallel",)), name="outproj_router",
    )(*args)


def _moe_kernel(order_ref, pos0_ref, pos1_ref, off_ref, cnt_ref,
                h_ref, wg_ref, wu_ref, wd_ref, x1_ref, gt_ref, wcol_ref, ya_ref, yb_ref,
                o_scr, xs_scr, comb0_scr, comb1_scr, *, n_ctx_tiles):
    t = pl.program_id(0)
    s = pl.program_id(1)

    @pl.when(s == 0)
    def _init():
        xs_scr[...] = jnp.zeros_like(xs_scr)

    @pl.when(s < N_EXPERTS)
    def _expert():
        seg = t * N_EXPERTS + s
        off = off_ref[seg]
        cnt = cnt_ref[seg]

        def chunk(c, carry):
            base = pl.multiple_of(off + c * MOE_CHUNK, 8)
            rows = jnp.minimum(cnt - c * MOE_CHUNK, MOE_CHUNK)

            def gather8(i, carry2):
                slot = t * MOE_SLOTS + base + i * 8
                for k in range(8):
                    src = order_ref[slot + k]
                    xs_scr[i, pl.ds(k, 1), :] = h_ref[pl.ds(src, 1), :]
                return carry2

            lax.fori_loop(0, (rows + 7) // 8, gather8, 0)
            xs = xs_scr[...].reshape(MOE_CHUNK, D_MODEL).astype(BF16)
            a = jnp.dot(xs, wg_ref[...], preferred_element_type=F32)
            b = jnp.dot(xs, wu_ref[...], preferred_element_type=F32)
            hid = (a * jax.nn.sigmoid(a)) * b
            o_scr[pl.ds(base, MOE_CHUNK), :] = jnp.dot(hid.astype(BF16), wd_ref[...],
                                                       preferred_element_type=F32)
            return carry

        lax.fori_loop(0, (cnt + MOE_CHUNK - 1) // MOE_CHUNK, chunk, 0)

    @pl.when(s >= N_EXPERTS)
    def _combine():
        tok0 = t * MOE_TILE + (s - N_EXPERTS) * TOKEN_TILE

        def body8(i, carry):
            for k in range(8):
                p0 = pos0_ref[tok0 + i * 8 + k]
                p1 = pos1_ref[tok0 + i * 8 + k]
                comb0_scr[i, pl.ds(k, 1), :] = o_scr[pl.ds(p0, 1), :]
                comb1_scr[i, pl.ds(k, 1), :] = o_scr[pl.ds(p1, 1), :]
            return carry

        lax.fori_loop(0, TOKEN_TILE // 8, body8, 0)
        wcol = wcol_ref[...]
        comb = (wcol[:, 2:3] * comb0_scr[...].reshape(TOKEN_TILE, D_MODEL)
                + wcol[:, 3:4] * comb1_scr[...].reshape(TOKEN_TILE, D_MODEL))
        y = x1_ref[...] + gt_ref[...] * comb
        chunk_ix = t * (MOE_TILE // TOKEN_TILE) + s - N_EXPERTS

        @pl.when(chunk_ix < n_ctx_tiles)
        def _ctx():
            ya_ref[...] = y

        @pl.when(chunk_ix >= n_ctx_tiles)
        def _lat():
            yb_ref[...] = y


def _route_tables(route):
    n = route.shape[1]
    nt = n // MOE_TILE
    tri = np.arange(SEQ_BLOCK)
    strict_upper = jnp.asarray(tri[:, None] < tri[None, :], BF16)
    tok = np.arange(MOE_TILE)
    digits = np.zeros((MOE_TILE, LANES), np.float32)
    digits[:, 0] = tok % 256
    digits[:, 1] = tok // 256
    pos, order, meta = pl.pallas_call(
        _route_kernel, grid=(nt,),
        in_specs=[pl.BlockSpec((8, MOE_TILE), lambda t: (0, t)),
                  pl.BlockSpec((SEQ_BLOCK, SEQ_BLOCK), lambda t: (0, 0)),
                  pl.BlockSpec((MOE_TILE, LANES), lambda t: (0, 0))],
        out_specs=(pl.BlockSpec((8, MOE_TILE), lambda t: (0, t)),
                   pl.BlockSpec((None, MOE_SLOTS, 1), lambda t: (t, 0, 0)),
                   pl.BlockSpec((None, N_EXPERTS, 8), lambda t: (t, 0, 0))),
        out_shape=(jax.ShapeDtypeStruct((8, n), jnp.int32),
                   jax.ShapeDtypeStruct((nt, MOE_SLOTS, 1), jnp.int32),
                   jax.ShapeDtypeStruct((nt, N_EXPERTS, 8), jnp.int32)),
        compiler_params=_params(("parallel",)), name="route_tables",
    )(route, strict_upper, jnp.asarray(digits, BF16))
    return (order.reshape(-1), pos[0], pos[1], meta[:, :, 0].reshape(-1),
            meta[:, :, 1].reshape(-1))


def _route_kernel(route_ref, su_ref, digits_ref, pos_ref, order_ref, meta_ref):
    r = route_ref[...]
    e1, e2 = r[0:1, :], r[1:2, :]
    eid = lax.broadcasted_iota(jnp.int32, (N_EXPERTS, MOE_TILE), 0).astype(F32)
    oh1, oh2 = eid == e1, eid == e2
    oh = jnp.where(oh1, 1.0, 0.0) + jnp.where(oh2, 1.0, 0.0)
    nblk = MOE_TILE // SEQ_BLOCK
    blocks = [oh[:, b * SEQ_BLOCK:(b + 1) * SEQ_BLOCK] for b in range(nblk)]
    inner = jnp.dot(jnp.concatenate(blocks, axis=0).astype(BF16), su_ref[...],
                    preferred_element_type=F32)
    run = jnp.zeros((N_EXPERTS, 1), F32)
    ranks = []
    for b in range(nblk):
        ranks.append(inner[b * N_EXPERTS:(b + 1) * N_EXPERTS, :] + run)
        run = run + jnp.sum(blocks[b], axis=1, keepdims=True)
    count = run
    seg = jnp.floor((count + 7.0) * 0.125) * 8.0
    sub = lax.broadcasted_iota(jnp.int32, (N_EXPERTS, 1), 0)
    off = jnp.zeros((N_EXPERTS, 1), F32)
    for e in range(N_EXPERTS - 1):
        off = off + jnp.where(sub > e, seg[e:e + 1, :], 0.0)
    slot = jnp.concatenate(ranks, axis=1) + off
    pos1 = jnp.sum(jnp.where(oh1, slot, 0.0), axis=0, keepdims=True).astype(jnp.int32)
    pos2 = jnp.sum(jnp.where(oh2, slot, 0.0), axis=0, keepdims=True).astype(jnp.int32)
    pos_ref[...] = jnp.zeros_like(pos_ref)
    pos_ref[0:1, :] = pos1
    pos_ref[1:2, :] = pos2
    meta_ref[...] = jnp.zeros_like(meta_ref)
    meta_ref[:, 0:1] = off.astype(jnp.int32)
    meta_ref[:, 1:2] = count.astype(jnp.int32)
    digits = digits_ref[...]
    rows = MOE_SLOTS // ROUTE_SLOT_BLOCKS
    for sb in range(ROUTE_SLOT_BLOCKS):
        j = lax.broadcasted_iota(jnp.int32, (rows, MOE_TILE), 0) + sb * rows
        hit = jnp.where(j == pos1, 1.0, 0.0) + jnp.where(j == pos2, 1.0, 0.0)
        d = jnp.dot(hit.astype(BF16), digits, preferred_element_type=F32)
        order_ref[sb * rows:(sb + 1) * rows, :] = (d[:, 0:1] + 256.0 * d[:, 1:2]).astype(jnp.int32)


def _moe(hp, route, wcol, x1, mod3, layer, n_ctx_tiles, tiles_per_lat_seq, ctx_row, wg, wu, wd):
    n = hp.shape[0]
    order, pos0, pos1, off, count = _route_tables(route)
    chunks_per_tile = MOE_TILE // TOKEN_TILE
    n_steps = N_EXPERTS + chunks_per_tile

    def chunk_ix(t, s):
        return t * chunks_per_tile + jnp.maximum(s - N_EXPERTS, 0)

    def mod_row(g):
        return jnp.where(g < n_ctx_tiles, ctx_row, (g - n_ctx_tiles) // tiles_per_lat_seq)

    wspec = lambda r, c: pl.BlockSpec(
        (None, None, r, c), lambda t, s, *_: (layer, jnp.minimum(s, N_EXPERTS - 1), 0, 0))
    chunk_spec = pl.BlockSpec((TOKEN_TILE, D_MODEL), lambda t, s, *_: (chunk_ix(t, s), 0))
    grid_spec = pltpu.PrefetchScalarGridSpec(
        num_scalar_prefetch=5,
        grid=(n // MOE_TILE, n_steps),
        in_specs=[
            pl.BlockSpec((MOE_TILE, D_MODEL), lambda t, s, *_: (t, 0),
                         pipeline_mode=pl.Buffered(1)),
            wspec(D_MODEL, D_EXPERT), wspec(D_MODEL, D_EXPERT), wspec(D_EXPERT, D_MODEL),
            chunk_spec,
            pl.BlockSpec((None, 1, D_MODEL), lambda t, s, *_: (mod_row(chunk_ix(t, s)), 0, 5)),
            pl.BlockSpec((TOKEN_TILE, 8), lambda t, s, *_: (chunk_ix(t, s), 0)),
        ],
        out_specs=(
            pl.BlockSpec((TOKEN_TILE, D_MODEL),
                         lambda t, s, *_: (jnp.minimum(chunk_ix(t, s), n_ctx_tiles - 1), 0)),
            pl.BlockSpec((TOKEN_TILE, D_MODEL),
                         lambda t, s, *_: (jnp.maximum(chunk_ix(t, s) - n_ctx_tiles, 0), 0))),
        scratch_shapes=[pltpu.VMEM((MOE_SLOTS, D_MODEL), F32),
                        pltpu.VMEM((MOE_CHUNK // 8, 8, D_MODEL), F32),
                        pltpu.VMEM((TOKEN_TILE // 8, 8, D_MODEL), F32),
                        pltpu.VMEM((TOKEN_TILE // 8, 8, D_MODEL), F32)],
    )
    n_ctx = n_ctx_tiles * TOKEN_TILE
    return pl.pallas_call(
        functools.partial(_moe_kernel, n_ctx_tiles=n_ctx_tiles), grid_spec=grid_spec,
        out_shape=(jax.ShapeDtypeStruct((n_ctx, D_MODEL), F32),
                   jax.ShapeDtypeStruct((n - n_ctx, D_MODEL), F32)),
        compiler_params=_params(("arbitrary", "arbitrary")), name="experts",
    )(order, pos0, pos1, off, count, hp, wg, wu, wd, x1, mod3, wcol)


def _rope_tables(seq_len):
    half = HEAD_DIM // 2
    freqs = ROPE_BASE ** (-np.arange(0, half, 2, dtype=np.float64) / half)
    pos = np.arange(seq_len)
    row, col = pos // GRID_W, pos % GRID_W
    d = np.arange(HEAD_DIM)
    position = np.where(d[None, :] < half, row[:, None], col[:, None]).astype(np.float64)
    ang = (position.astype(np.float32) * freqs.astype(np.float32)[d % (half // 2)][None, :]).astype(np.float32)
    cos, sin = np.cos(ang), np.sin(ang)
    first = (d % half) < half // 2
    sa = np.where(first[None, :], -sin, 0.0)
    sb = np.where(first[None, :], 0.0, sin)
    ident = lambda v: np.full((SEQ_BLOCK, HEAD_DIM), v, np.float32)
    stack = lambda ctx, lat: jnp.asarray(
        np.tile(np.concatenate([ctx, lat.astype(np.float32)], axis=0), (1, 2)), F32)
    return stack(ident(1.0), cos), stack(ident(0.0), sa), stack(ident(0.0), sb)


def kernel(x_prompt, x_sample, c, cache_k, cache_v, state_C, state_n, state_m, c_ctx, w_mod, b_mod,
           g_mix, g_ffn, w_in, b_igate, b_fgate, g_q, g_k, g_mh, w_out, w_router, b_router,
           w_e_gate, w_e_up, w_e_down):
    n_ctx_seq, ctx_len, _ = x_prompt.shape
    n_lat_seq, lat_len, _ = x_sample.shape
    n_layers = w_mod.shape[0]
    n_ctx = n_ctx_seq * ctx_len
    assert ctx_len == SEQ_BLOCK and lat_len % TOKEN_TILE == 0
    assert n_ctx % MOE_TILE == 0 and (n_lat_seq * lat_len) % MOE_TILE == 0
    assert n_lat_seq < 16 and n_ctx % lat_len == 0
    n_ctx_tiles = n_ctx // TOKEN_TILE
    tiles_per_lat_seq = lat_len // TOKEN_TILE
    ctx_row = n_lat_seq

    x = (x_prompt.reshape(n_ctx, D_MODEL), x_sample.reshape(-1, D_MODEL))
    cond = jnp.zeros((16, D_MODEL), F32).at[:n_lat_seq].set(c).at[ctx_row].set(c_ctx)
    mod = _modulation(cond, w_mod, b_mod)
    rope = _rope_tables(lat_len)

    wg_b, wu_b, wd_b = w_e_gate.astype(BF16), w_e_up.astype(BF16), w_e_down.astype(BF16)
    gate_perm = np.array([(q % 2) * N_ML_HEADS + hd + 2 * N_ML_HEADS * (q // 2)
                          for hd in range(N_ML_HEADS) for q in range(4)])

    ks, vs, cs, ns, ms = [], [], [], [], []
    for l in range(n_layers):
        mod3 = mod[l].reshape(16, 1, -1)
        w_main = w_in[l, :, :MAIN_WIDTH].astype(BF16)
        w_gate = w_in[l, :, MAIN_WIDTH:][:, gate_perm]
        b_gate = jnp.concatenate([b_igate[l].reshape(-1), b_fgate[l].reshape(-1)])[gate_perm]
        q, k, v, mq, mk, mv, og, gcol, grow = _inproj(
            *x, mod3, n_ctx_tiles, tiles_per_lat_seq, ctx_row, g_mix[l], w_main, w_gate, b_gate,
            g_q[l], g_k[l], rope)

        att_ctx = _attention(q, k, v, 0, n_ctx_seq, ctx_len)
        past = cache_k.shape[2]
        ck = cache_k[:, l].reshape(n_lat_seq * past, LANES)
        cv = cache_v[:, l].reshape(n_lat_seq * past, LANES)
        att_lat = _attention(q, k, v, n_ctx, n_lat_seq, lat_len, cache=(ck, cv))

        ml_ctx, c_fin, nm_fin = _mlstm(mq, mk, mv, og, g_mh[l], gcol, grow, 0, n_ctx_seq, ctx_len,
                                       emit_state=True)
        n0 = state_n[:, l].transpose(0, 2, 1, 3)
        m0 = jnp.broadcast_to(state_m[:, l].transpose(0, 2, 1)[..., None], n0.shape)
        nm0 = jnp.concatenate([n0, m0, jnp.zeros_like(n0), jnp.zeros_like(n0)], axis=2)
        (ml_lat,) = _mlstm(mq, mk, mv, og, g_mh[l], gcol, grow, n_ctx, n_lat_seq, lat_len,
                           init=(state_C, nm0, l))

        x1, hp, route, wcol = _outproj(
            (att_ctx, att_lat), (ml_ctx, ml_lat), x, mod3, n_ctx_tiles, tiles_per_lat_seq, ctx_row,
            w_out[l].astype(BF16), g_ffn[l], w_router, b_router)
        x = _moe(hp, route, wcol, x1, mod3, l, n_ctx_tiles, tiles_per_lat_seq, ctx_row,
                 wg_b, wu_b, wd_b)

        ks.append(k[:n_ctx].reshape(n_ctx_seq, ctx_len, N_KV_HEADS, HEAD_DIM))
        vs.append(v[:n_ctx].reshape(n_ctx_seq, ctx_len, N_KV_HEADS, HEAD_DIM))
        cs.append(c_fin)
        ns.append(nm_fin[:, :, 0:2, :].transpose(0, 2, 1, 3))
        ms.append(nm_fin[:, :, 2:4, 0].transpose(0, 2, 1))

    y_prompt = x[0].reshape(x_prompt.shape)
    y_sample = x[1].reshape(x_sample.shape)
    return (y_prompt, y_sample, jnp.stack(ks, axis=1), jnp.stack(vs, axis=1),
            jnp.stack(cs, axis=1), jnp.stack(ns, axis=1), jnp.stack(ms, axis=1))
```

```python
import functools

import numpy as np
import jax
import jax.numpy as jnp
from jax import lax
from jax.experimental import pallas as pl
from jax.experimental.pallas import tpu as pltpu

F32 = jnp.float32
BF16 = jnp.bfloat16

D_MODEL = 1024
HEAD_DIM = 64
ATT_WIDTH = 512
N_KV_HEADS = 2
ML_WIDTH = 512
N_ML_HEADS = 4
ML_HEAD_DIM = 128
GRID_W = 64
ROPE_BASE = 10000.0
N_EXPERTS = 16
N_GROUPS = 4
GROUP_SIZE = 4
D_EXPERT = 512
EPS = 1e-6
MAIN_WIDTH = 2816
N_GATE_COLS = 16
LANES = 128
TOKEN_TILE = 512
SEQ_BLOCK = 256
MOE_TILE = 2048
MOE_CHUNK = 320
ROUTE_SLOT_BLOCKS = 9
MOE_SLOTS = -(-(2 * MOE_TILE + 8 * N_EXPERTS + MOE_CHUNK) // (512 * ROUTE_SLOT_BLOCKS)) * 512 * ROUTE_SLOT_BLOCKS
VMEM_LIMIT = 56 * 1024 * 1024
NEG_INF = float("-inf")
HIGHEST = lax.Precision.HIGHEST
NT_DIMS = (((1,), (1,)), ((), ()))
TN_DIMS = (((0,), (0,)), ((), ()))


def _params(semantics):
    return pltpu.CompilerParams(dimension_semantics=semantics, vmem_limit_bytes=VMEM_LIMIT)


def _log_sigmoid(z):
    return jnp.minimum(z, 0.0) - jnp.log1p(jnp.exp(-jnp.abs(z)))


def _split3(x):
    h1 = x.astype(BF16)
    r1 = x - h1.astype(F32)
    h2 = r1.astype(BF16)
    h3 = (r1 - h2.astype(F32)).astype(BF16)
    return h1, h2, h3


def _mod_kernel(cond_ref, w_ref, b_ref, o_ref):
    c = cond_ref[...]
    s = c * jax.nn.sigmoid(c)
    o_ref[...] = jnp.dot(s.astype(BF16), w_ref[...].astype(BF16),
                         preferred_element_type=F32) + b_ref[...]


def _modulation(cond, w_mod, b_mod):
    n_layers = w_mod.shape[0]
    n_chunks = w_mod.shape[2] // D_MODEL
    return pl.pallas_call(
        _mod_kernel,
        grid=(n_layers, n_chunks),
        in_specs=[
            pl.BlockSpec((16, D_MODEL), lambda l, j: (0, 0)),
            pl.BlockSpec((None, D_MODEL, D_MODEL), lambda l, j: (l, 0, j)),
            pl.BlockSpec((None, 1, D_MODEL), lambda l, j: (l, 0, j)),
        ],
        out_specs=pl.BlockSpec((None, 16, D_MODEL), lambda l, j: (l, 0, j)),
        out_shape=jax.ShapeDtypeStruct((n_layers, 16, w_mod.shape[2]), F32),
        compiler_params=_params(("parallel", "parallel")),
        name="modulation",
    )(cond, w_mod, b_mod.reshape(n_layers, 1, -1))


def _two_part(i, n_first, a_ref, b_ref):
    return jnp.where(i < n_first, a_ref[...], b_ref[...])


def _two_part_specs(width, n_first, tile=TOKEN_TILE):
    return [pl.BlockSpec((tile, width), lambda i: (jnp.minimum(i, n_first - 1), 0)),
            pl.BlockSpec((tile, width), lambda i: (jnp.maximum(i - n_first, 0), 0))]


def _inproj_kernel(xa_ref, xb_ref, sh_ref, sc_ref, g_ref, w_ref, wgt_ref,
                   brow_ref, gq_ref, gk_ref, eye_ref, cos_ref, sa_ref, sb_ref, gsum_ref,
                   q_ref, k_ref, v_ref, mq_ref, mk_ref, mv_ref, og_ref, gc_ref, gr_ref,
                   *, n_ctx_tiles):
    x = _two_part(pl.program_id(0), n_ctx_tiles, xa_ref, xb_ref)
    ms = jnp.mean(x * x, axis=-1, keepdims=True)
    h = x * lax.rsqrt(ms + EPS) * g_ref[...]
    h = h * (1.0 + sc_ref[...]) + sh_ref[...]
    hb = h.astype(BF16)

    zr = lax.dot_general(wgt_ref[...], h, NT_DIMS, precision=HIGHEST,
                         preferred_element_type=F32) + brow_ref[...]
    sub = lax.broadcasted_iota(jnp.int32, zr.shape, 0)
    gr = jnp.where(sub % 4 < 2, zr, _log_sigmoid(zr))
    eye = eye_ref[...]
    gc = functools.reduce(jnp.add, [lax.dot_general(eye, p, NT_DIMS, preferred_element_type=F32)
                                    for p in _split3(gr)])
    gr_ref[...] = jnp.zeros_like(gr_ref)
    for hd in range(N_ML_HEADS):
        gc_ref[hd] = gc[:, 4 * hd:4 * hd + 4]
        gr_ref[hd, 0:4, :] = gr[4 * hd:4 * hd + 4, :]

    cos = cos_ref[...]
    sa = sa_ref[...]
    sb = sb_ref[...]
    gsum = gsum_ref[...]

    def proj(c0, width):
        return jnp.dot(hb, w_ref[:, c0:c0 + width], preferred_element_type=F32)

    n_qk = ATT_WIDTH // LANES + 1
    t_rows = hb.shape[0]
    zqk = proj(0, n_qk * LANES)
    zs = [zqk[:, c * LANES:(c + 1) * LANES] for c in range(n_qk)]
    sq = jnp.concatenate([z * z for z in zs], axis=0)
    hi = sq.astype(BF16)
    lo = (sq - hi.astype(F32)).astype(BF16)
    ss = jnp.dot(jnp.concatenate([hi, lo], axis=0), gsum, preferred_element_type=F32)
    ss = ss[:n_qk * t_rows] + ss[n_qk * t_rows:]

    def headnorm_rope(c, gain):
        zn = zs[c] * lax.rsqrt(ss[c * t_rows:(c + 1) * t_rows] * (1.0 / HEAD_DIM) + EPS) * gain
        return zn * cos + pltpu.roll(zn, LANES - 16, 1) * sa + pltpu.roll(zn, 16, 1) * sb

    for c in range(n_qk - 1):
        q_ref[:, c * LANES:(c + 1) * LANES] = (headnorm_rope(c, gq_ref[...]) * 0.125).astype(BF16)
    k_ref[...] = headnorm_rope(n_qk - 1, gk_ref[...])
    v_ref[...] = proj(640, LANES)
    mq_ref[...] = proj(768, ML_WIDTH).astype(BF16)
    mk_ref[...] = (proj(1280, ML_WIDTH) * (ML_HEAD_DIM ** -0.5)).astype(BF16)
    mv_ref[...] = proj(1792, ML_WIDTH).astype(BF16)
    og_ref[...] = jax.nn.sigmoid(proj(2304, ML_WIDTH))


def _inproj(xa, xb, mod3, n_ctx_tiles, tiles_per_lat_seq, ctx_row, g_mix, w_main, w_gate, b_gate,
            g_q, g_k, rope):
    n = xa.shape[0] + xb.shape[0]
    n_tiles = n // TOKEN_TILE

    def mod_row(i):
        return jnp.where(i < n_ctx_tiles, ctx_row, (i - n_ctx_tiles) // tiles_per_lat_seq)

    def rope_blk(i):
        return jnp.where(i < n_ctx_tiles, 0, 1 + (i - n_ctx_tiles) % tiles_per_lat_seq)

    cos_t, sa_t, sb_t = rope
    lane = np.arange(LANES)
    gsum = jnp.asarray((lane[:, None] // HEAD_DIM) == (lane[None, :] // HEAD_DIM), BF16)
    tok = lambda w: pl.BlockSpec((TOKEN_TILE, w), lambda i: (i, 0))
    full = lambda a: pl.BlockSpec(a.shape, lambda i: (0,) * a.ndim)
    modspec = lambda j: pl.BlockSpec((None, 1, D_MODEL), lambda i: (mod_row(i), 0, j))
    ropespec = pl.BlockSpec((TOKEN_TILE, LANES), lambda i: (rope_blk(i), 0))
    consts = (g_mix.reshape(1, -1), w_main, w_gate.T, b_gate.reshape(-1, 1),
              jnp.tile(g_q, 2).reshape(1, -1), jnp.tile(g_k, 2).reshape(1, -1),
              jnp.asarray(np.eye(TOKEN_TILE), BF16))
    args = (xa, xb, mod3, mod3) + consts + (cos_t, sa_t, sb_t, gsum)
    in_specs = _two_part_specs(D_MODEL, n_ctx_tiles) + [modspec(0), modspec(1)] \
        + [full(a) for a in consts] + [ropespec, ropespec, ropespec, full(gsum)]
    out_shape = (
        jax.ShapeDtypeStruct((n, ATT_WIDTH), BF16),
        jax.ShapeDtypeStruct((n, LANES), F32),
        jax.ShapeDtypeStruct((n, LANES), F32),
        jax.ShapeDtypeStruct((n, ML_WIDTH), BF16),
        jax.ShapeDtypeStruct((n, ML_WIDTH), BF16),
        jax.ShapeDtypeStruct((n, ML_WIDTH), BF16),
        jax.ShapeDtypeStruct((n, ML_WIDTH), F32),
        jax.ShapeDtypeStruct((N_ML_HEADS, n, 4), F32),
        jax.ShapeDtypeStruct((N_ML_HEADS, 8, n), F32),
    )
    out_specs = (tok(ATT_WIDTH), tok(LANES), tok(LANES), tok(ML_WIDTH), tok(ML_WIDTH),
                 tok(ML_WIDTH), tok(ML_WIDTH),
                 pl.BlockSpec((N_ML_HEADS, TOKEN_TILE, 4), lambda i: (0, i, 0)),
                 pl.BlockSpec((N_ML_HEADS, 8, TOKEN_TILE), lambda i: (0, 0, i)))
    return pl.pallas_call(
        functools.partial(_inproj_kernel, n_ctx_tiles=n_ctx_tiles),
        grid=(n_tiles,), in_specs=in_specs, out_specs=out_specs,
        out_shape=out_shape, compiler_params=_params(("parallel",)), name="inproj",
    )(*args)


def _attn_kernel(*refs, n_kv):
    q_ref = refs[0]
    kv_refs = refs[1:1 + 2 * n_kv]
    o_ref = refs[-1]
    tq = q_ref.shape[0]
    lo_q = lax.broadcasted_iota(jnp.int32, (tq, LANES), 1) < HEAD_DIM

    def dup_half(ref, g):
        a = ref[...]
        r = pltpu.roll(a, HEAD_DIM, 1)
        lo = lax.broadcasted_iota(jnp.int32, a.shape, 1) < HEAD_DIM
        d = jnp.where(lo, a, r) if g == 0 else jnp.where(lo, r, a)
        return d.astype(BF16)

    for g in range(N_KV_HEADS):
        ks = [dup_half(kv_refs[2 * p], g) for p in range(n_kv)]
        vs = [dup_half(kv_refs[2 * p + 1], g) for p in range(n_kv)]
        for hb in range(2):
            c0 = (2 * g + hb) * LANES
            qb = q_ref[:, c0:c0 + LANES]
            outs = []
            for half in range(2):
                keep = lo_q if half == 0 else jnp.logical_not(lo_q)
                qm = jnp.where(keep, qb, jnp.zeros_like(qb))
                ss = [lax.dot_general(qm, kd, NT_DIMS, preferred_element_type=F32) for kd in ks]
                m = functools.reduce(jnp.maximum, [jnp.max(s, axis=1, keepdims=True) for s in ss])
                ps = [jnp.exp(s - m) for s in ss]
                den = functools.reduce(jnp.add, [jnp.sum(p, axis=1, keepdims=True) for p in ps])
                o = functools.reduce(jnp.add, [
                    jnp.dot(p.astype(BF16), vd, preferred_element_type=F32)
                    for p, vd in zip(ps, vs)])
                outs.append(o / den)
            o_ref[:, c0:c0 + LANES] = jnp.where(lo_q, outs[0], outs[1]).astype(BF16)


def _attention(q, k, v, row0, n_seq, seq_len, cache=None):
    nq = seq_len // SEQ_BLOCK
    qb0 = row0 // SEQ_BLOCK
    sb0 = row0 // seq_len
    in_specs = [
        pl.BlockSpec((SEQ_BLOCK, ATT_WIDTH), lambda b, i: (qb0 + b * nq + i, 0)),
        pl.BlockSpec((seq_len, LANES), lambda b, i: (sb0 + b, 0)),
        pl.BlockSpec((seq_len, LANES), lambda b, i: (sb0 + b, 0)),
    ]
    args = [q, k, v]
    n_kv = 1
    if cache is not None:
        ck, cv = cache
        past = ck.shape[0] // n_seq
        in_specs += [pl.BlockSpec((past, LANES), lambda b, i: (b, 0))] * 2
        args += [ck, cv]
        n_kv = 2
    return pl.pallas_call(
        functools.partial(_attn_kernel, n_kv=n_kv),
        grid=(n_seq, nq), in_specs=in_specs,
        out_specs=pl.BlockSpec((SEQ_BLOCK, ATT_WIDTH), lambda b, i: (b * nq + i, 0)),
        out_shape=jax.ShapeDtypeStruct((n_seq * seq_len, ATT_WIDTH), BF16),
        compiler_params=_params(("parallel", "parallel")), name="attention",
    )(*args)


def _mlstm_kernel(*refs, seq_len, has_init, emit_state):
    it = iter(refs)
    q_ref, k_ref, v_ref, og_ref, gmh_ref, gcol_ref, grow_ref, u_ref, l_ref = [next(it) for _ in range(9)]
    if has_init:
        c0_ref, nm0_ref = next(it), next(it)
    ml_ref = next(it)
    if emit_state:
        cf_ref, nmf_ref = next(it), next(it)

    bq = SEQ_BLOCK
    nb = seq_len // bq
    blk = lambda j: slice(j * bq, (j + 1) * bq)
    upper_incl = u_ref[...]
    lower_incl = l_ref[...]

    def tri_dot(x, tri):
        return functools.reduce(jnp.add, [jnp.dot(p, tri, preferred_element_type=F32)
                                          for p in _split3(x)])

    ig_row = [[None] * nb for _ in range(2)]
    lf_row = [[None] * nb for _ in range(2)]
    within = [[None] * nb for _ in range(2)]
    bsum = [[None] * nb for _ in range(2)]
    for j in range(nb):
        g8 = grow_ref[:, blk(j)]
        cum_f = tri_dot(g8, upper_incl)
        cum_b = tri_dot(g8, lower_incl)
        for d in range(2):
            ig_row[d][j] = g8[d:d + 1, :]
            lf_row[d][j] = g8[2 + d:3 + d, :]
            within[d][j] = (cum_f if d == 0 else cum_b)[2 + d:3 + d, :]
            bsum[d][j] = jnp.sum(lf_row[d][j], axis=1, keepdims=True)
    zero11 = jnp.zeros((1, 1), F32)
    offset = [[None] * nb for _ in range(2)]
    acc = zero11
    for j in range(nb):
        offset[0][j] = acc
        acc = acc + bsum[0][j]
    total_f = acc
    acc = zero11
    for j in reversed(range(nb)):
        offset[1][j] = acc
        acc = acc + bsum[1][j]
    total = [total_f, acc]
    a_row = [[ig_row[d][j] - (within[d][j] + offset[d][j]) for j in range(nb)] for d in range(2)]
    blkmax = [[jnp.max(a_row[d][j], axis=1, keepdims=True) for j in range(nb)] for d in range(2)]
    if has_init:
        m0 = [nm0_ref[2:3, 0:1], nm0_ref[3:4, 0:1]]
    else:
        m0 = [zero11, zero11]

    r_i = lax.broadcasted_iota(jnp.int32, (bq, bq), 0)
    c_i = lax.broadcasted_iota(jnp.int32, (bq, bq), 1)
    causal = [c_i <= r_i, c_i >= r_i]
    before = [lambda i: range(0, i), lambda i: range(i + 1, nb)]

    gmh = gmh_ref[...]
    b_col = [[None] * nb for _ in range(2)]
    for i in range(nb):
        q_i = q_ref[blk(i), :]
        a_blocks = {}

        def scores(j):
            if j not in a_blocks:
                a_blocks[j] = lax.dot_general(q_i, k_ref[blk(j), :], NT_DIMS,
                                              preferred_element_type=F32)
            return a_blocks[j]

        h = None
        for d in range(2):
            m_prev = functools.reduce(jnp.maximum, [blkmax[d][j] for j in before[d](i)], m0[d])
            b_col[d][i] = offset[d][i] + jnp.sum(jnp.where(causal[d], lf_row[d][i], 0.0),
                                                 axis=1, keepdims=True)
            m_col = jnp.maximum(m_prev, jnp.max(jnp.where(causal[d], a_row[d][i], NEG_INF),
                                                axis=1, keepdims=True))
            num = jnp.zeros((bq, ML_HEAD_DIM), F32)
            den = jnp.zeros((bq, 1), F32)
            for j in list(before[d](i)) + [i]:
                arg = a_row[d][j] - m_col
                if j == i:
                    arg = jnp.where(causal[d], arg, NEG_INF)
                p = jnp.exp(arg) * scores(j)
                den = den + jnp.sum(p, axis=1, keepdims=True)
                num = num + jnp.dot(p.astype(BF16), v_ref[blk(j), :], preferred_element_type=F32)
            if has_init:
                w_inter = jnp.exp(m0[d] - m_col)
                qc = jnp.dot(q_i, c0_ref[d].astype(BF16), preferred_element_type=F32)
                qn = jnp.sum(q_i.astype(F32) * nm0_ref[d:d + 1, :], axis=1, keepdims=True)
                num = num + w_inter * qc
                den = den + w_inter * qn
            nrm = jnp.maximum(jnp.abs(den), jnp.exp(-(b_col[d][i] + m_col)))
            h = num / nrm if h is None else h + num / nrm
        hn = h * lax.rsqrt(jnp.mean(h * h, axis=-1, keepdims=True) + EPS) * gmh
        ml_ref[blk(i), :] = (og_ref[blk(i), :] * hn).astype(BF16)

    if emit_state:
        nmf_ref[...] = jnp.zeros_like(nmf_ref)
        for d in range(2):
            m_last = functools.reduce(jnp.maximum, blkmax[d], m0[d])
            c_fin = jnp.zeros((ML_HEAD_DIM, ML_HEAD_DIM), F32)
            n_fin = jnp.zeros((1, ML_HEAD_DIM), F32)
            for j in range(nb):
                a_col = gcol_ref[blk(j), d:d + 1] - b_col[d][j]
                kw = k_ref[blk(j), :].astype(F32) * jnp.exp(a_col - m_last)
                c_fin = c_fin + lax.dot_general(kw.astype(BF16), v_ref[blk(j), :], TN_DIMS,
                                                preferred_element_type=F32)
                n_fin = n_fin + jnp.sum(kw, axis=0, keepdims=True)
            if has_init:
                decay = jnp.exp(m0[d] - m_last)
                c_fin = c_fin + decay * c0_ref[d]
                n_fin = n_fin + decay * nm0_ref[d:d + 1, :]
            cf_ref[d] = c_fin
            nmf_ref[d:d + 1, :] = n_fin
            nmf_ref[2 + d:3 + d, :] = jnp.broadcast_to(total[d] + m_last, (1, ML_HEAD_DIM))


def _mlstm(mq, mk, mv, og, g_mh, gcol, grow, row0, n_seq, seq_len, init=None, emit_state=False):
    sb0 = row0 // seq_len
    tri = np.arange(SEQ_BLOCK)
    upper_incl = jnp.asarray(tri[:, None] <= tri[None, :], BF16)
    lower_incl = jnp.asarray(tri[:, None] >= tri[None, :], BF16)
    headblk = lambda: pl.BlockSpec((seq_len, ML_HEAD_DIM), lambda b, h: (sb0 + b, h))
    const = lambda a: pl.BlockSpec(a.shape, lambda b, h: (0,) * a.ndim)
    in_specs = [headblk(), headblk(), headblk(), headblk(),
                pl.BlockSpec((1, ML_HEAD_DIM), lambda b, h: (0, h)),
                pl.BlockSpec((None, seq_len, 4), lambda b, h: (h, sb0 + b, 0)),
                pl.BlockSpec((None, 8, seq_len), lambda b, h: (h, 0, sb0 + b)),
                const(upper_incl), const(lower_incl)]
    args = [mq, mk, mv, og, g_mh.reshape(1, -1), gcol, grow, upper_incl, lower_incl]
    if init is not None:
        c0, nm0, layer = init
        in_specs += [
            pl.BlockSpec((None, None, 2, None, ML_HEAD_DIM, ML_HEAD_DIM),
                         lambda b, h: (b, layer, 0, h, 0, 0)),
            pl.BlockSpec((None, None, 8, ML_HEAD_DIM), lambda b, h: (b, h, 0, 0))]
        args += [c0, nm0]
    out_shape = [jax.ShapeDtypeStruct((n_seq * seq_len, ML_WIDTH), BF16)]
    out_specs = [pl.BlockSpec((seq_len, ML_HEAD_DIM), lambda b, h: (b, h))]
    if emit_state:
        out_shape += [jax.ShapeDtypeStruct((n_seq, 2, N_ML_HEADS, ML_HEAD_DIM, ML_HEAD_DIM), F32),
                      jax.ShapeDtypeStruct((n_seq, N_ML_HEADS, 8, ML_HEAD_DIM), F32)]
        out_specs += [pl.BlockSpec((None, 2, None, ML_HEAD_DIM, ML_HEAD_DIM),
                                   lambda b, h: (b, 0, h, 0, 0)),
                      pl.BlockSpec((None, None, 8, ML_HEAD_DIM), lambda b, h: (b, h, 0, 0))]
    return pl.pallas_call(
        functools.partial(_mlstm_kernel, seq_len=seq_len, has_init=init is not None,
                          emit_state=emit_state),
        grid=(n_seq, N_ML_HEADS), in_specs=in_specs, out_specs=tuple(out_specs),
        out_shape=tuple(out_shape),
        compiler_params=_params(("parallel", "parallel")), name="mlstm",
    )(*args)


def _outproj_kernel(atta_ref, attb_ref, mla_ref, mlb_ref, xa_ref, xb_ref, w_ref, gt_ref, sh_ref,
                    sc_ref, g_ref, wrt_ref,
                    x1_ref, hp_ref, logits_ref, *, n_ctx_tiles):
    i = pl.program_id(0)
    att = _two_part(i, n_ctx_tiles, atta_ref, attb_ref)
    ml = _two_part(i, n_ctx_tiles, mla_ref, mlb_ref)
    y = (jnp.dot(att, w_ref[:ATT_WIDTH, :], preferred_element_type=F32)
         + jnp.dot(ml, w_ref[ATT_WIDTH:, :], preferred_element_type=F32))
    x1 = _two_part(i, n_ctx_tiles, xa_ref, xb_ref) + gt_ref[...] * y
    x1_ref[...] = x1
    ms = jnp.mean(x1 * x1, axis=-1, keepdims=True)
    h2 = x1 * lax.rsqrt(ms + EPS) * g_ref[...]
    h2 = h2 * (1.0 + sc_ref[...]) + sh_ref[...]
    hp_ref[...] = h2

    logits_ref[...] = lax.dot_general(wrt_ref[...], h2, NT_DIMS, precision=HIGHEST,
                                      preferred_element_type=F32)


def _select_experts(logits, b_col):
    ex = jnp.exp(logits - jnp.max(logits, axis=0, keepdims=True))
    scores = ex / jnp.sum(ex, axis=0, keepdims=True)
    sel = scores + b_col
    row = lambda a, e: a[e:e + 1, :]
    grp_score = []
    for g in range(N_GROUPS):
        xs = [row(sel, g * GROUP_SIZE + j) for j in range(GROUP_SIZE)]
        pairs = [xs[a] + xs[b] for a in range(GROUP_SIZE) for b in range(a + 1, GROUP_SIZE)]
        grp_score.append(functools.reduce(jnp.maximum, pairs))
    best = grp_score[0]
    grp = jnp.zeros_like(best, dtype=jnp.int32)
    for g in range(1, N_GROUPS):
        better = grp_score[g] > best
        grp = jnp.where(better, g, grp)
        best = jnp.where(better, grp_score[g], best)
    pick = lambda a, j: functools.reduce(
        lambda acc, g: jnp.where(grp == g, row(a, g * GROUP_SIZE + j), acc),
        range(1, N_GROUPS), row(a, j))
    xs = [pick(sel, j) for j in range(GROUP_SIZE)]
    ws = [pick(scores, j) for j in range(GROUP_SIZE)]

    def argmax4(vals):
        bv, bi = vals[0], jnp.zeros_like(grp)
        for j in range(1, GROUP_SIZE):
            better = vals[j] > bv
            bi = jnp.where(better, j, bi)
            bv = jnp.where(better, vals[j], bv)
        return bi

    i1 = argmax4(xs)
    i2 = argmax4([jnp.where(i1 == j, NEG_INF, xs[j]) for j in range(GROUP_SIZE)])
    take = lambda vals, idx: functools.reduce(
        lambda acc, j: jnp.where(idx == j, vals[j], acc), range(1, GROUP_SIZE), vals[0])
    w1, w2 = take(ws, i1), take(ws, i2)
    wsum = w1 + w2
    w1, w2 = w1 / wsum, w2 / wsum
    return grp * GROUP_SIZE + i1, grp * GROUP_SIZE + i2, w1, w2


def _outproj(att, ml, x, mod3, n_ctx_tiles, tiles_per_lat_seq, ctx_row, w_out, g_ffn, w_router):
    n = x[0].shape[0] + x[1].shape[0]

    def mod_row(i):
        return jnp.where(i < n_ctx_tiles, ctx_row, (i - n_ctx_tiles) // tiles_per_lat_seq)

    tok = lambda w: pl.BlockSpec((TOKEN_TILE, w), lambda i: (i, 0))
    full = lambda a: pl.BlockSpec(a.shape, lambda i: (0,) * a.ndim)
    modspec = lambda j: pl.BlockSpec((None, 1, D_MODEL), lambda i: (mod_row(i), 0, j))
    consts = (g_ffn.reshape(1, -1), w_router.T)
    args = (*att, *ml, *x, w_out, mod3, mod3, mod3) + consts
    in_specs = (_two_part_specs(ATT_WIDTH, n_ctx_tiles) + _two_part_specs(ML_WIDTH, n_ctx_tiles)
                + _two_part_specs(D_MODEL, n_ctx_tiles)
                + [full(w_out), modspec(2), modspec(3), modspec(4)] + [full(a) for a in consts])
    return pl.pallas_call(
        functools.partial(_outproj_kernel, n_ctx_tiles=n_ctx_tiles),
        grid=(n // TOKEN_TILE,), in_specs=in_specs,
        out_specs=(tok(D_MODEL), tok(D_MODEL),
                   pl.BlockSpec((N_EXPERTS, TOKEN_TILE), lambda i: (0, i))),
        out_shape=(jax.ShapeDtypeStruct((n, D_MODEL), F32),
                   jax.ShapeDtypeStruct((n, D_MODEL), F32),
                   jax.ShapeDtypeStruct((N_EXPERTS, n), F32)),
        compiler_params=_params(("parallel",)), name="outproj_router",
    )(*args)


def _moe_kernel(order_ref, pos0_ref, pos1_ref, off_ref, cnt_ref,
                h_ref, wg_ref, wu_ref, wd_ref, x1_ref, gt_ref, wcol_ref, ya_ref, yb_ref,
                o_scr, xs_scr, comb0_scr, comb1_scr, *, n_ctx_tiles):
    t = pl.program_id(0)
    s = pl.program_id(1)

    groups = MOE_CHUNK // 8

    def gather_rows(buf, slot0):
        for j in range(MOE_CHUNK):
            src = order_ref[slot0 + j]
            xs_scr[buf, j // 8, pl.ds(j % 8, 1), :] = h_ref[pl.ds(src, 1), :]

    def ffn(buf, base):
        xs = xs_scr[buf].reshape(MOE_CHUNK, D_MODEL).astype(BF16)
        a = jnp.dot(xs, wg_ref[...], preferred_element_type=F32)
        b = jnp.dot(xs, wu_ref[...], preferred_element_type=F32)
        hid = (a * jax.nn.sigmoid(a)) * b
        o_scr[pl.ds(base, MOE_CHUNK), :] = jnp.dot(hid.astype(BF16), wd_ref[...],
                                                   preferred_element_type=F32)

    @pl.when(s == 0)
    def _first():
        gather_rows(0, t * MOE_SLOTS + pl.multiple_of(off_ref[t * N_EXPERTS], 8))

    @pl.when(s < N_EXPERTS)
    def _expert():
        seg = t * N_EXPERTS + s
        off = pl.multiple_of(off_ref[seg], 8)
        cur = s % 2
        nxt_seg = t * N_EXPERTS + jnp.minimum(s + 1, N_EXPERTS - 1)
        ffn(cur, off)
        gather_rows(1 - cur, t * MOE_SLOTS + pl.multiple_of(off_ref[nxt_seg], 8))

        def extra(c, carry):
            base = pl.multiple_of(off + c * MOE_CHUNK, 8)

            def gather8(i, carry2):
                slot = t * MOE_SLOTS + base + i * 8
                for k in range(8):
                    src = order_ref[slot + k]
                    xs_scr[2, i, pl.ds(k, 1), :] = h_ref[pl.ds(src, 1), :]
                return carry2

            lax.fori_loop(0, groups, gather8, 0)
            ffn(2, base)
            return carry

        lax.fori_loop(1, (cnt_ref[seg] + MOE_CHUNK - 1) // MOE_CHUNK, extra, 0)

    @pl.when(s >= N_EXPERTS)
    def _combine():
        tok0 = t * MOE_TILE + (s - N_EXPERTS) * TOKEN_TILE

        def body8(i, carry):
            for k in range(8):
                p0 = pos0_ref[tok0 + i * 8 + k]
                p1 = pos1_ref[tok0 + i * 8 + k]
                comb0_scr[i, pl.ds(k, 1), :] = o_scr[pl.ds(p0, 1), :]
                comb1_scr[i, pl.ds(k, 1), :] = o_scr[pl.ds(p1, 1), :]
            return carry

        lax.fori_loop(0, TOKEN_TILE // 8, body8, 0)
        wcol = wcol_ref[...]
        comb = (wcol[:, 2:3] * comb0_scr[...].reshape(TOKEN_TILE, D_MODEL)
                + wcol[:, 3:4] * comb1_scr[...].reshape(TOKEN_TILE, D_MODEL))
        y = x1_ref[...] + gt_ref[...] * comb
        chunk_ix = t * (MOE_TILE // TOKEN_TILE) + s - N_EXPERTS

        @pl.when(chunk_ix < n_ctx_tiles)
        def _ctx():
            ya_ref[...] = y

        @pl.when(chunk_ix >= n_ctx_tiles)
        def _lat():
            yb_ref[...] = y


def _route_tables(logits, b_router):
    n = logits.shape[1]
    nt = n // MOE_TILE
    tri = np.arange(SEQ_BLOCK)
    strict_upper = jnp.asarray(tri[:, None] < tri[None, :], BF16)
    tok = np.arange(MOE_TILE)
    digits = np.zeros((MOE_TILE, LANES), np.float32)
    digits[:, 0] = tok % 256
    digits[:, 1] = tok // 256
    pos, order, meta, wcol = pl.pallas_call(
        _route_kernel, grid=(nt,),
        in_specs=[pl.BlockSpec((N_EXPERTS, MOE_TILE), lambda t: (0, t)),
                  pl.BlockSpec((N_EXPERTS, 1), lambda t: (0, 0)),
                  pl.BlockSpec((SEQ_BLOCK, SEQ_BLOCK), lambda t: (0, 0)),
                  pl.BlockSpec((SEQ_BLOCK, SEQ_BLOCK), lambda t: (0, 0)),
                  pl.BlockSpec((MOE_TILE, LANES), lambda t: (0, 0))],
        out_specs=(pl.BlockSpec((8, MOE_TILE), lambda t: (0, t)),
                   pl.BlockSpec((None, MOE_SLOTS, 1), lambda t: (t, 0, 0)),
                   pl.BlockSpec((None, N_EXPERTS, 8), lambda t: (t, 0, 0)),
                   pl.BlockSpec((MOE_TILE, 8), lambda t: (t, 0))),
        out_shape=(jax.ShapeDtypeStruct((8, n), jnp.int32),
                   jax.ShapeDtypeStruct((nt, MOE_SLOTS, 1), jnp.int32),
                   jax.ShapeDtypeStruct((nt, N_EXPERTS, 8), jnp.int32),
                   jax.ShapeDtypeStruct((n, 8), F32)),
        scratch_shapes=[pltpu.VMEM((8, MOE_TILE), F32)],
        compiler_params=_params(("parallel",)), name="route_tables",
    )(logits, b_router.reshape(-1, 1), strict_upper, jnp.asarray(np.eye(SEQ_BLOCK), BF16),
      jnp.asarray(digits, BF16))
    return (order.reshape(-1), pos[0], pos[1], meta[:, :, 0].reshape(-1),
            meta[:, :, 1].reshape(-1), wcol)


def _route_kernel(logits_ref, br_ref, su_ref, eye_ref, digits_ref,
                  pos_ref, order_ref, meta_ref, wcol_ref, wrow_scr):
    e1, e2, w1, w2 = _select_experts(logits_ref[...], br_ref[...])
    wrow_scr[...] = jnp.zeros_like(wrow_scr)
    wrow_scr[2:3, :] = w1
    wrow_scr[3:4, :] = w2
    eye = eye_ref[...]
    for b in range(MOE_TILE // SEQ_BLOCK):
        cols = slice(b * SEQ_BLOCK, (b + 1) * SEQ_BLOCK)
        wcol_ref[cols, :] = functools.reduce(jnp.add, [
            lax.dot_general(eye, p, NT_DIMS, preferred_element_type=F32)
            for p in _split3(wrow_scr[:, cols])])
    eid = lax.broadcasted_iota(jnp.int32, (N_EXPERTS, MOE_TILE), 0)
    oh1, oh2 = eid == e1, eid == e2
    oh = jnp.where(oh1, 1.0, 0.0) + jnp.where(oh2, 1.0, 0.0)
    nblk = MOE_TILE // SEQ_BLOCK
    blocks = [oh[:, b * SEQ_BLOCK:(b + 1) * SEQ_BLOCK] for b in range(nblk)]
    inner = jnp.dot(jnp.concatenate(blocks, axis=0).astype(BF16), su_ref[...],
                    preferred_element_type=F32)
    run = jnp.zeros((N_EXPERTS, 1), F32)
    ranks = []
    for b in range(nblk):
        ranks.append(inner[b * N_EXPERTS:(b + 1) * N_EXPERTS, :] + run)
        run = run + jnp.sum(blocks[b], axis=1, keepdims=True)
    count = run
    seg = jnp.floor((count + 7.0) * 0.125) * 8.0
    sub = lax.broadcasted_iota(jnp.int32, (N_EXPERTS, 1), 0)
    off = jnp.zeros((N_EXPERTS, 1), F32)
    for e in range(N_EXPERTS - 1):
        off = off + jnp.where(sub > e, seg[e:e + 1, :], 0.0)
    slot = jnp.concatenate(ranks, axis=1) + off
    pos1 = jnp.sum(jnp.where(oh1, slot, 0.0), axis=0, keepdims=True).astype(jnp.int32)
    pos2 = jnp.sum(jnp.where(oh2, slot, 0.0), axis=0, keepdims=True).astype(jnp.int32)
    pos_ref[...] = jnp.zeros_like(pos_ref)
    pos_ref[0:1, :] = pos1
    pos_ref[1:2, :] = pos2
    meta_ref[...] = jnp.zeros_like(meta_ref)
    meta_ref[:, 0:1] = off.astype(jnp.int32)
    meta_ref[:, 1:2] = count.astype(jnp.int32)
    digits = digits_ref[...]
    rows = MOE_SLOTS // ROUTE_SLOT_BLOCKS
    for sb in range(ROUTE_SLOT_BLOCKS):
        j = lax.broadcasted_iota(jnp.int32, (rows, MOE_TILE), 0) + sb * rows
        hit = jnp.where(j == pos1, 1.0, 0.0) + jnp.where(j == pos2, 1.0, 0.0)
        d = jnp.dot(hit.astype(BF16), digits, preferred_element_type=F32)
        order_ref[sb * rows:(sb + 1) * rows, :] = (d[:, 0:1] + 256.0 * d[:, 1:2]).astype(jnp.int32)


def _moe(hp, logits, b_router, x1, mod3, layer, n_ctx_tiles, tiles_per_lat_seq, ctx_row,
         wg, wu, wd):
    n = hp.shape[0]
    order, pos0, pos1, off, count, wcol = _route_tables(logits, b_router)
    chunks_per_tile = MOE_TILE // TOKEN_TILE
    n_steps = N_EXPERTS + chunks_per_tile

    def chunk_ix(t, s):
        return t * chunks_per_tile + jnp.maximum(s - N_EXPERTS, 0)

    def mod_row(g):
        return jnp.where(g < n_ctx_tiles, ctx_row, (g - n_ctx_tiles) // tiles_per_lat_seq)

    wspec = lambda r, c: pl.BlockSpec(
        (None, None, r, c), lambda t, s, *_: (layer, jnp.minimum(s, N_EXPERTS - 1), 0, 0))
    chunk_spec = pl.BlockSpec((TOKEN_TILE, D_MODEL), lambda t, s, *_: (chunk_ix(t, s), 0))
    grid_spec = pltpu.PrefetchScalarGridSpec(
        num_scalar_prefetch=5,
        grid=(n // MOE_TILE, n_steps),
        in_specs=[
            pl.BlockSpec((MOE_TILE, D_MODEL), lambda t, s, *_: (t, 0),
                         pipeline_mode=pl.Buffered(1)),
            wspec(D_MODEL, D_EXPERT), wspec(D_MODEL, D_EXPERT), wspec(D_EXPERT, D_MODEL),
            chunk_spec,
            pl.BlockSpec((None, 1, D_MODEL), lambda t, s, *_: (mod_row(chunk_ix(t, s)), 0, 5)),
            pl.BlockSpec((TOKEN_TILE, 8), lambda t, s, *_: (chunk_ix(t, s), 0)),
        ],
        out_specs=(
            pl.BlockSpec((TOKEN_TILE, D_MODEL),
                         lambda t, s, *_: (jnp.minimum(chunk_ix(t, s), n_ctx_tiles - 1), 0)),
            pl.BlockSpec((TOKEN_TILE, D_MODEL),
                         lambda t, s, *_: (jnp.maximum(chunk_ix(t, s) - n_ctx_tiles, 0), 0))),
        scratch_shapes=[pltpu.VMEM((MOE_SLOTS, D_MODEL), F32),
                        pltpu.VMEM((3, MOE_CHUNK // 8, 8, D_MODEL), F32),
                        pltpu.VMEM((TOKEN_TILE // 8, 8, D_MODEL), F32),
                        pltpu.VMEM((TOKEN_TILE // 8, 8, D_MODEL), F32)],
    )
    n_ctx = n_ctx_tiles * TOKEN_TILE
    return pl.pallas_call(
        functools.partial(_moe_kernel, n_ctx_tiles=n_ctx_tiles), grid_spec=grid_spec,
        out_shape=(jax.ShapeDtypeStruct((n_ctx, D_MODEL), F32),
                   jax.ShapeDtypeStruct((n - n_ctx, D_MODEL), F32)),
        compiler_params=_params(("arbitrary", "arbitrary")), name="experts",
    )(order, pos0, pos1, off, count, hp, wg, wu, wd, x1, mod3, wcol)


def _rope_tables(seq_len):
    half = HEAD_DIM // 2
    freqs = ROPE_BASE ** (-np.arange(0, half, 2, dtype=np.float64) / half)
    pos = np.arange(seq_len)
    row, col = pos // GRID_W, pos % GRID_W
    d = np.arange(HEAD_DIM)
    position = np.where(d[None, :] < half, row[:, None], col[:, None]).astype(np.float64)
    ang = (position.astype(np.float32) * freqs.astype(np.float32)[d % (half // 2)][None, :]).astype(np.float32)
    cos, sin = np.cos(ang), np.sin(ang)
    first = (d % half) < half // 2
    sa = np.where(first[None, :], -sin, 0.0)
    sb = np.where(first[None, :], 0.0, sin)
    ident = lambda v: np.full((TOKEN_TILE, HEAD_DIM), v, np.float32)
    stack = lambda ctx, lat: jnp.asarray(
        np.tile(np.concatenate([ctx, lat.astype(np.float32)], axis=0), (1, 2)), F32)
    return stack(ident(1.0), cos), stack(ident(0.0), sa), stack(ident(0.0), sb)


def kernel(x_prompt, x_sample, c, cache_k, cache_v, state_C, state_n, state_m, c_ctx, w_mod, b_mod,
           g_mix, g_ffn, w_in, b_igate, b_fgate, g_q, g_k, g_mh, w_out, w_router, b_router,
           w_e_gate, w_e_up, w_e_down):
    n_ctx_seq, ctx_len, _ = x_prompt.shape
    n_lat_seq, lat_len, _ = x_sample.shape
    n_layers = w_mod.shape[0]
    n_ctx = n_ctx_seq * ctx_len
    assert ctx_len == SEQ_BLOCK and lat_len % TOKEN_TILE == 0
    assert n_ctx % MOE_TILE == 0 and (n_lat_seq * lat_len) % MOE_TILE == 0
    assert n_lat_seq < 16 and n_ctx % lat_len == 0
    n_ctx_tiles = n_ctx // TOKEN_TILE
    tiles_per_lat_seq = lat_len // TOKEN_TILE
    ctx_row = n_lat_seq

    x = (x_prompt.reshape(n_ctx, D_MODEL), x_sample.reshape(-1, D_MODEL))
    cond = jnp.zeros((16, D_MODEL), F32).at[:n_lat_seq].set(c).at[ctx_row].set(c_ctx)
    mod = _modulation(cond, w_mod, b_mod)
    rope = _rope_tables(lat_len)

    wg_b, wu_b, wd_b = w_e_gate.astype(BF16), w_e_up.astype(BF16), w_e_down.astype(BF16)
    gate_perm = np.array([(q % 2) * N_ML_HEADS + hd + 2 * N_ML_HEADS * (q // 2)
                          for hd in range(N_ML_HEADS) for q in range(4)])

    ks, vs, cs, ns, ms = [], [], [], [], []
    for l in range(n_layers):
        mod3 = mod[l].reshape(16, 1, -1)
        w_main = w_in[l, :, :MAIN_WIDTH].astype(BF16)
        w_gate = w_in[l, :, MAIN_WIDTH:][:, gate_perm]
        b_gate = jnp.concatenate([b_igate[l].reshape(-1), b_fgate[l].reshape(-1)])[gate_perm]
        q, k, v, mq, mk, mv, og, gcol, grow = _inproj(
            *x, mod3, n_ctx_tiles, tiles_per_lat_seq, ctx_row, g_mix[l], w_main, w_gate, b_gate,
            g_q[l], g_k[l], rope)

        att_ctx = _attention(q, k, v, 0, n_ctx_seq, ctx_len)
        past = cache_k.shape[2]
        ck = cache_k[:, l].reshape(n_lat_seq * past, LANES)
        cv = cache_v[:, l].reshape(n_lat_seq * past, LANES)
        att_lat = _attention(q, k, v, n_ctx, n_lat_seq, lat_len, cache=(ck, cv))

        ml_ctx, c_fin, nm_fin = _mlstm(mq, mk, mv, og, g_mh[l], gcol, grow, 0, n_ctx_seq, ctx_len,
                                       emit_state=True)
        n0 = state_n[:, l].transpose(0, 2, 1, 3)
        m0 = jnp.broadcast_to(state_m[:, l].transpose(0, 2, 1)[..., None], n0.shape)
        nm0 = jnp.concatenate([n0, m0, jnp.zeros_like(n0), jnp.zeros_like(n0)], axis=2)
        (ml_lat,) = _mlstm(mq, mk, mv, og, g_mh[l], gcol, grow, n_ctx, n_lat_seq, lat_len,
                           init=(state_C, nm0, l))

        x1, hp, logits = _outproj(
            (att_ctx, att_lat), (ml_ctx, ml_lat), x, mod3, n_ctx_tiles, tiles_per_lat_seq, ctx_row,
            w_out[l].astype(BF16), g_ffn[l], w_router)
        x = _moe(hp, logits, b_router, x1, mod3, l, n_ctx_tiles, tiles_per_lat_seq, ctx_row,
                 wg_b, wu_b, wd_b)

        ks.append(k[:n_ctx].reshape(n_ctx_seq, ctx_len, N_KV_HEADS, HEAD_DIM))
        vs.append(v[:n_ctx].reshape(n_ctx_seq, ctx_len, N_KV_HEADS, HEAD_DIM))
        cs.append(c_fin)
        ns.append(nm_fin[:, :, 0:2, :].transpose(0, 2, 1, 3))
        ms.append(nm_fin[:, :, 2:4, 0].transpose(0, 2, 1))

    y_prompt = x[0].reshape(x_prompt.shape)
    y_sample = x[1].reshape(x_sample.shape)
    return (y_prompt, y_sample, jnp.stack(ks, axis=1), jnp.stack(vs, axis=1),
            jnp.stack(cs, axis=1), jnp.stack(ns, axis=1), jnp.stack(ms, axis=1))
```

```python
import functools

import numpy as np
import jax
import jax.numpy as jnp
from jax import lax
from jax.experimental import pallas as pl
from jax.experimental.pallas import tpu as pltpu

F32 = jnp.float32
BF16 = jnp.bfloat16

D_MODEL = 1024
HEAD_DIM = 64
ATT_WIDTH = 512
N_KV_HEADS = 2
ML_WIDTH = 512
N_ML_HEADS = 4
ML_HEAD_DIM = 128
GRID_W = 64
ROPE_BASE = 10000.0
N_EXPERTS = 16
N_GROUPS = 4
GROUP_SIZE = 4
D_EXPERT = 512
EPS = 1e-6
MAIN_WIDTH = 2816
N_GATE_COLS = 16
LANES = 128
TOKEN_TILE = 512
SEQ_BLOCK = 256
MOE_TILE = 2048
MOE_CHUNK = 320
ROUTE_SLOT_BLOCKS = 9
MOE_SLOTS = -(-(2 * MOE_TILE + 8 * N_EXPERTS + MOE_CHUNK) // (512 * ROUTE_SLOT_BLOCKS)) * 512 * ROUTE_SLOT_BLOCKS
VMEM_LIMIT = 56 * 1024 * 1024
NEG_INF = float("-inf")
HIGHEST = lax.Precision.HIGHEST
NT_DIMS = (((1,), (1,)), ((), ()))
TN_DIMS = (((0,), (0,)), ((), ()))


def _params(semantics):
    return pltpu.CompilerParams(dimension_semantics=semantics, vmem_limit_bytes=VMEM_LIMIT)


def _log_sigmoid(z):
    return jnp.minimum(z, 0.0) - jnp.log1p(jnp.exp(-jnp.abs(z)))


def _split3(x):
    h1 = x.astype(BF16)
    r1 = x - h1.astype(F32)
    h2 = r1.astype(BF16)
    h3 = (r1 - h2.astype(F32)).astype(BF16)
    return h1, h2, h3


def _mod_kernel(cond_ref, w_ref, b_ref, o_ref):
    c = cond_ref[...]
    s = c * jax.nn.sigmoid(c)
    o_ref[...] = _dot_x3(_hi_lo(s), _hi_lo(w_ref[...])) + b_ref[...]


def _modulation(cond, w_mod, b_mod):
    n_layers = w_mod.shape[0]
    n_chunks = w_mod.shape[2] // D_MODEL
    return pl.pallas_call(
        _mod_kernel,
        grid=(n_layers, n_chunks),
        in_specs=[
            pl.BlockSpec((16, D_MODEL), lambda l, j: (0, 0)),
            pl.BlockSpec((None, D_MODEL, D_MODEL), lambda l, j: (l, 0, j)),
            pl.BlockSpec((None, 1, D_MODEL), lambda l, j: (l, 0, j)),
        ],
        out_specs=pl.BlockSpec((None, 16, D_MODEL), lambda l, j: (l, 0, j)),
        out_shape=jax.ShapeDtypeStruct((n_layers, 16, w_mod.shape[2]), F32),
        compiler_params=_params(("parallel", "parallel")),
        name="modulation",
    )(cond, w_mod, b_mod.reshape(n_layers, 1, -1))


def _two_part(i, n_first, a_ref, b_ref):
    return jnp.where(i < n_first, a_ref[...], b_ref[...])


def _hi_lo(x):
    hi = x.astype(BF16)
    return hi, (x - hi.astype(F32)).astype(BF16)


def _dot_x3(a, b, dims=None):
    (ah, al), (bh, bl) = a, b
    if dims is None:
        d = lambda x, y: jnp.dot(x, y, preferred_element_type=F32)
    else:
        d = lambda x, y: lax.dot_general(x, y, dims, preferred_element_type=F32)
    return d(ah, bh) + (d(al, bh) + d(ah, bl))


def _two_part_out_specs(width, n_first, tile=TOKEN_TILE):
    return _two_part_specs(width, n_first, tile)


def _two_part_specs(width, n_first, tile=TOKEN_TILE):
    return [pl.BlockSpec((tile, width), lambda i: (jnp.minimum(i, n_first - 1), 0)),
            pl.BlockSpec((tile, width), lambda i: (jnp.maximum(i - n_first, 0), 0))]


def _inproj_kernel(*refs, n_ctx_tiles, precise_ctx):
    it = iter(refs)
    xa_ref, xb_ref, sh_ref, sc_ref, g_ref, w_ref = [next(it) for _ in range(6)]
    wl_ref = next(it) if precise_ctx else None
    (wgt_ref, brow_ref, gq_ref, gk_ref, eye_ref, cos_ref, sa_ref, sb_ref, gsum_ref,
     qa_ref, qb_ref, k_ref, v_ref, mqa_ref, mqb_ref, mka_ref, mkb_ref, mva_ref, mvb_ref,
     og_ref, gc_ref, gr_ref) = it
    tile = pl.program_id(0)
    x = _two_part(tile, n_ctx_tiles, xa_ref, xb_ref)
    ms = jnp.mean(x * x, axis=-1, keepdims=True)
    h = x * lax.rsqrt(ms + EPS) * g_ref[...]
    h = h * (1.0 + sc_ref[...]) + sh_ref[...]
    hb = h.astype(BF16)

    zr = lax.dot_general(wgt_ref[...], h, NT_DIMS, precision=HIGHEST,
                         preferred_element_type=F32) + brow_ref[...]
    sub = lax.broadcasted_iota(jnp.int32, zr.shape, 0)
    gr = jnp.where(sub % 4 < 2, zr, _log_sigmoid(zr))
    eye = eye_ref[...]
    gc = functools.reduce(jnp.add, [lax.dot_general(eye, p, NT_DIMS, preferred_element_type=F32)
                                    for p in _split3(gr)])
    gr_ref[...] = jnp.zeros_like(gr_ref)
    for hd in range(N_ML_HEADS):
        gc_ref[hd] = gc[:, 4 * hd:4 * hd + 4]
        gr_ref[hd, 0:4, :] = gr[4 * hd:4 * hd + 4, :]

    cos = cos_ref[...]
    sa = sa_ref[...]
    sb = sb_ref[...]
    gsum = gsum_ref[...]

    def project(precise, q_ref, mq_ref, mk_ref, mv_ref):
        if precise:
            h_pair = _hi_lo(h)
            proj = lambda c0, width: _dot_x3(
                h_pair, (w_ref[:, c0:c0 + width], wl_ref[:, c0:c0 + width]))
        else:
            proj = lambda c0, width: jnp.dot(hb, w_ref[:, c0:c0 + width],
                                             preferred_element_type=F32)
        act = F32 if precise else BF16

        n_qk = ATT_WIDTH // LANES + 1
        t_rows = hb.shape[0]
        zqk = proj(0, n_qk * LANES)
        zs = [zqk[:, c * LANES:(c + 1) * LANES] for c in range(n_qk)]
        sq = jnp.concatenate([z * z for z in zs], axis=0)
        ss = jnp.dot(jnp.concatenate(_hi_lo(sq), axis=0), gsum, preferred_element_type=F32)
        ss = ss[:n_qk * t_rows] + ss[n_qk * t_rows:]

        def headnorm_rope(c, gain):
            zn = zs[c] * lax.rsqrt(ss[c * t_rows:(c + 1) * t_rows] * (1.0 / HEAD_DIM) + EPS) * gain
            return zn * cos + pltpu.roll(zn, LANES - 16, 1) * sa + pltpu.roll(zn, 16, 1) * sb

        for c in range(n_qk - 1):
            q_ref[:, c * LANES:(c + 1) * LANES] = (headnorm_rope(c, gq_ref[...]) * 0.125).astype(act)
        k_ref[...] = headnorm_rope(n_qk - 1, gk_ref[...])
        v_ref[...] = proj(640, LANES)
        mq_ref[...] = proj(768, ML_WIDTH).astype(act)
        mk_ref[...] = (proj(1280, ML_WIDTH) * (ML_HEAD_DIM ** -0.5)).astype(act)
        mv_ref[...] = proj(1792, ML_WIDTH).astype(act)
        og_ref[...] = jax.nn.sigmoid(proj(2304, ML_WIDTH))

    @pl.when(tile < n_ctx_tiles)
    def _ctx():
        project(precise_ctx, qa_ref, mqa_ref, mka_ref, mva_ref)

    @pl.when(tile >= n_ctx_tiles)
    def _lat():
        project(False, qb_ref, mqb_ref, mkb_ref, mvb_ref)


def _inproj(xa, xb, mod3, n_ctx_tiles, tiles_per_lat_seq, ctx_row, g_mix, w_main, w_gate, b_gate,
            g_q, g_k, rope):
    w_hi, w_lo = w_main
    precise_ctx = w_lo is not None
    n_ctx, n_lat = xa.shape[0], xb.shape[0]
    n = n_ctx + n_lat
    n_tiles = n // TOKEN_TILE

    def mod_row(i):
        return jnp.where(i < n_ctx_tiles, ctx_row, (i - n_ctx_tiles) // tiles_per_lat_seq)

    def rope_blk(i):
        return jnp.where(i < n_ctx_tiles, 0, 1 + (i - n_ctx_tiles) % tiles_per_lat_seq)

    cos_t, sa_t, sb_t = rope
    lane = np.arange(LANES)
    gsum = jnp.asarray((lane[:, None] // HEAD_DIM) == (lane[None, :] // HEAD_DIM), BF16)
    tok = lambda w: pl.BlockSpec((TOKEN_TILE, w), lambda i: (i, 0))
    full = lambda a: pl.BlockSpec(a.shape, lambda i: (0,) * a.ndim, pipeline_mode=pl.Buffered(1))
    modspec = lambda j: pl.BlockSpec((None, 1, D_MODEL), lambda i: (mod_row(i), 0, j))
    ropespec = pl.BlockSpec((TOKEN_TILE, LANES), lambda i: (rope_blk(i), 0))
    consts = (g_mix.reshape(1, -1), w_hi) + ((w_lo,) if precise_ctx else ()) + (
        w_gate.T, b_gate.reshape(-1, 1),
        jnp.tile(g_q, 2).reshape(1, -1), jnp.tile(g_k, 2).reshape(1, -1),
        jnp.asarray(np.eye(TOKEN_TILE), BF16))
    args = (xa, xb, mod3, mod3) + consts + (cos_t, sa_t, sb_t, gsum)
    in_specs = _two_part_specs(D_MODEL, n_ctx_tiles) + [modspec(0), modspec(1)] \
        + [full(a) for a in consts] + [ropespec, ropespec, ropespec, full(gsum)]
    ctx_act = F32 if precise_ctx else BF16
    pair_shape = lambda w: [jax.ShapeDtypeStruct((n_ctx, w), ctx_act),
                            jax.ShapeDtypeStruct((n_lat, w), BF16)]
    pair_spec = lambda w: _two_part_out_specs(w, n_ctx_tiles)
    out_shape = (
        pair_shape(ATT_WIDTH)
        + [jax.ShapeDtypeStruct((n, LANES), F32),
           jax.ShapeDtypeStruct((n, LANES), F32)]
        + pair_shape(ML_WIDTH) + pair_shape(ML_WIDTH) + pair_shape(ML_WIDTH)
        + [jax.ShapeDtypeStruct((n, ML_WIDTH), F32),
           jax.ShapeDtypeStruct((N_ML_HEADS, n, 4), F32),
           jax.ShapeDtypeStruct((N_ML_HEADS, 8, n), F32)])
    out_specs = (pair_spec(ATT_WIDTH) + [tok(LANES), tok(LANES)]
                 + pair_spec(ML_WIDTH) + pair_spec(ML_WIDTH) + pair_spec(ML_WIDTH)
                 + [tok(ML_WIDTH),
                    pl.BlockSpec((N_ML_HEADS, TOKEN_TILE, 4), lambda i: (0, i, 0)),
                    pl.BlockSpec((N_ML_HEADS, 8, TOKEN_TILE), lambda i: (0, 0, i))])
    outs = pl.pallas_call(
        functools.partial(_inproj_kernel, n_ctx_tiles=n_ctx_tiles, precise_ctx=precise_ctx),
        grid=(n_tiles,), in_specs=in_specs, out_specs=tuple(out_specs),
        out_shape=tuple(out_shape), compiler_params=_params(("arbitrary",)), name="inproj",
    )(*args)
    qa, qb, k, v, mqa, mqb, mka, mkb, mva, mvb, og, gcol, grow = outs
    return (qa, qb), k, v, (mqa, mqb), (mka, mkb), (mva, mvb), og, gcol, grow


def _attn_kernel(*refs, n_kv, precise):
    q_ref = refs[0]
    kv_refs = refs[1:1 + 2 * n_kv]
    o_ref = refs[-1]
    tq = q_ref.shape[0]
    lo_q = lax.broadcasted_iota(jnp.int32, (tq, LANES), 1) < HEAD_DIM
    operand = _hi_lo if precise else (lambda a: a.astype(BF16))
    if precise:
        qk = lambda a, b: _dot_x3(a, b, NT_DIMS)
        pv = _dot_x3
    else:
        qk = lambda a, b: lax.dot_general(a, b, NT_DIMS, preferred_element_type=F32)
        pv = lambda a, b: jnp.dot(a, b, preferred_element_type=F32)

    def dup_half(ref, g):
        a = ref[...]
        r = pltpu.roll(a, HEAD_DIM, 1)
        lo = lax.broadcasted_iota(jnp.int32, a.shape, 1) < HEAD_DIM
        return operand(jnp.where(lo, a, r) if g == 0 else jnp.where(lo, r, a))

    for g in range(N_KV_HEADS):
        ks = [dup_half(kv_refs[2 * p], g) for p in range(n_kv)]
        vs = [dup_half(kv_refs[2 * p + 1], g) for p in range(n_kv)]
        for hb in range(2):
            c0 = (2 * g + hb) * LANES
            qb = q_ref[:, c0:c0 + LANES]
            outs = []
            for half in range(2):
                keep = lo_q if half == 0 else jnp.logical_not(lo_q)
                qm = operand(jnp.where(keep, qb, jnp.zeros_like(qb)))
                ss = [qk(qm, kd) for kd in ks]
                m = functools.reduce(jnp.maximum, [jnp.max(s, axis=1, keepdims=True) for s in ss])
                ps = [jnp.exp(s - m) for s in ss]
                den = functools.reduce(jnp.add, [jnp.sum(p, axis=1, keepdims=True) for p in ps])
                o = functools.reduce(jnp.add, [pv(operand(p), vd) for p, vd in zip(ps, vs)])
                outs.append(o / den)
            o_ref[:, c0:c0 + LANES] = jnp.where(lo_q, outs[0], outs[1]).astype(o_ref.dtype)


def _attention(q, k, v, kv_row0, n_seq, seq_len, cache=None):
    precise = q.dtype == F32
    nq = seq_len // SEQ_BLOCK
    sb0 = kv_row0 // seq_len
    in_specs = [
        pl.BlockSpec((SEQ_BLOCK, ATT_WIDTH), lambda b, i: (b * nq + i, 0)),
        pl.BlockSpec((seq_len, LANES), lambda b, i: (sb0 + b, 0)),
        pl.BlockSpec((seq_len, LANES), lambda b, i: (sb0 + b, 0)),
    ]
    args = [q, k, v]
    n_kv = 1
    if cache is not None:
        ck, cv = cache
        past = ck.shape[0] // n_seq
        in_specs += [pl.BlockSpec((past, LANES), lambda b, i: (b, 0))] * 2
        args += [ck, cv]
        n_kv = 2
    return pl.pallas_call(
        functools.partial(_attn_kernel, n_kv=n_kv, precise=precise),
        grid=(n_seq, nq), in_specs=in_specs,
        out_specs=pl.BlockSpec((SEQ_BLOCK, ATT_WIDTH), lambda b, i: (b * nq + i, 0)),
        out_shape=jax.ShapeDtypeStruct((n_seq * seq_len, ATT_WIDTH), q.dtype),
        compiler_params=_params(("parallel", "parallel")), name="attention",
    )(*args)


def _mlstm_kernel(*refs, seq_len, has_init, emit_state, precise):
    it = iter(refs)
    q_ref, k_ref, v_ref, og_ref, gmh_ref, gcol_ref, grow_ref, u_ref, l_ref = [next(it) for _ in range(9)]
    if has_init:
        c0_ref, nm0_ref = next(it), next(it)
    ml_ref = next(it)
    if emit_state:
        cf_ref, nmf_ref = next(it), next(it)

    operand = _hi_lo if precise else (lambda a: a.astype(BF16))
    if precise:
        qk = lambda a, b: _dot_x3(a, b, NT_DIMS)
        pv = _dot_x3
    else:
        qk = lambda a, b: lax.dot_general(a, b, NT_DIMS, preferred_element_type=F32)
        pv = lambda a, b: jnp.dot(a, b, preferred_element_type=F32)

    bq = SEQ_BLOCK
    nb = seq_len // bq
    blk = lambda j: slice(j * bq, (j + 1) * bq)
    upper_incl = u_ref[...]
    lower_incl = l_ref[...]

    def tri_dot(x, tri):
        return functools.reduce(jnp.add, [jnp.dot(p, tri, preferred_element_type=F32)
                                          for p in _split3(x)])

    ig_row = [[None] * nb for _ in range(2)]
    lf_row = [[None] * nb for _ in range(2)]
    within = [[None] * nb for _ in range(2)]
    bsum = [[None] * nb for _ in range(2)]
    for j in range(nb):
        g8 = grow_ref[:, blk(j)]
        cum_f = tri_dot(g8, upper_incl)
        cum_b = tri_dot(g8, lower_incl)
        for d in range(2):
            ig_row[d][j] = g8[d:d + 1, :]
            lf_row[d][j] = g8[2 + d:3 + d, :]
            within[d][j] = (cum_f if d == 0 else cum_b)[2 + d:3 + d, :]
            bsum[d][j] = jnp.sum(lf_row[d][j], axis=1, keepdims=True)
    zero11 = jnp.zeros((1, 1), F32)
    offset = [[None] * nb for _ in range(2)]
    acc = zero11
    for j in range(nb):
        offset[0][j] = acc
        acc = acc + bsum[0][j]
    total_f = acc
    acc = zero11
    for j in reversed(range(nb)):
        offset[1][j] = acc
        acc = acc + bsum[1][j]
    total = [total_f, acc]
    a_row = [[ig_row[d][j] - (within[d][j] + offset[d][j]) for j in range(nb)] for d in range(2)]
    blkmax = [[jnp.max(a_row[d][j], axis=1, keepdims=True) for j in range(nb)] for d in range(2)]
    if has_init:
        m0 = [nm0_ref[2:3, 0:1], nm0_ref[3:4, 0:1]]
    else:
        m0 = [zero11, zero11]

    r_i = lax.broadcasted_iota(jnp.int32, (bq, bq), 0)
    c_i = lax.broadcasted_iota(jnp.int32, (bq, bq), 1)
    causal = [c_i <= r_i, c_i >= r_i]
    before = [lambda i: range(0, i), lambda i: range(i + 1, nb)]

    gmh = gmh_ref[...]
    b_col = [[None] * nb for _ in range(2)]
    for i in range(nb):
        q_i = q_ref[blk(i), :]
        q_op = operand(q_i)
        a_blocks = {}

        def scores(j):
            if j not in a_blocks:
                a_blocks[j] = qk(q_op, operand(k_ref[blk(j), :]))
            return a_blocks[j]

        h = None
        for d in range(2):
            m_prev = functools.reduce(jnp.maximum, [blkmax[d][j] for j in before[d](i)], m0[d])
            b_col[d][i] = offset[d][i] + jnp.sum(jnp.where(causal[d], lf_row[d][i], 0.0),
                                                 axis=1, keepdims=True)
            m_col = jnp.maximum(m_prev, jnp.max(jnp.where(causal[d], a_row[d][i], NEG_INF),
                                                axis=1, keepdims=True))
            num = jnp.zeros((bq, ML_HEAD_DIM), F32)
            den = jnp.zeros((bq, 1), F32)
            for j in list(before[d](i)) + [i]:
                arg = a_row[d][j] - m_col
                if j == i:
                    arg = jnp.where(causal[d], arg, NEG_INF)
                p = jnp.exp(arg) * scores(j)
                den = den + jnp.sum(p, axis=1, keepdims=True)
                num = num + pv(operand(p), operand(v_ref[blk(j), :]))
            if has_init:
                w_inter = jnp.exp(m0[d] - m_col)
                qc = jnp.dot(q_i.astype(BF16), c0_ref[d].astype(BF16),
                             preferred_element_type=F32)
                qn = jnp.sum(q_i.astype(F32) * nm0_ref[d:d + 1, :], axis=1, keepdims=True)
                num = num + w_inter * qc
                den = den + w_inter * qn
            nrm = jnp.maximum(jnp.abs(den), jnp.exp(-(b_col[d][i] + m_col)))
            h = num / nrm if h is None else h + num / nrm
        hn = h * lax.rsqrt(jnp.mean(h * h, axis=-1, keepdims=True) + EPS) * gmh
        ml_ref[blk(i), :] = (og_ref[blk(i), :] * hn).astype(ml_ref.dtype)

    if emit_state:
        nmf_ref[...] = jnp.zeros_like(nmf_ref)
        for d in range(2):
            m_last = functools.reduce(jnp.maximum, blkmax[d], m0[d])
            c_fin = jnp.zeros((ML_HEAD_DIM, ML_HEAD_DIM), F32)
            n_fin = jnp.zeros((1, ML_HEAD_DIM), F32)
            for j in range(nb):
                a_col = gcol_ref[blk(j), d:d + 1] - b_col[d][j]
                kw = k_ref[blk(j), :].astype(F32) * jnp.exp(a_col - m_last)
                c_fin = c_fin + lax.dot_general(kw.astype(BF16), v_ref[blk(j), :].astype(BF16),
                                                TN_DIMS, preferred_element_type=F32)
                n_fin = n_fin + jnp.sum(kw, axis=0, keepdims=True)
            if has_init:
                decay = jnp.exp(m0[d] - m_last)
                c_fin = c_fin + decay * c0_ref[d]
                n_fin = n_fin + decay * nm0_ref[d:d + 1, :]
            cf_ref[d] = c_fin
            nmf_ref[d:d + 1, :] = n_fin
            nmf_ref[2 + d:3 + d, :] = jnp.broadcast_to(total[d] + m_last, (1, ML_HEAD_DIM))


def _mlstm(mq, mk, mv, og, g_mh, gcol, grow, row0, n_seq, seq_len, init=None, emit_state=False):
    precise = mq.dtype == F32
    assert not (precise and init is not None)
    sb0 = row0 // seq_len
    tri = np.arange(SEQ_BLOCK)
    upper_incl = jnp.asarray(tri[:, None] <= tri[None, :], BF16)
    lower_incl = jnp.asarray(tri[:, None] >= tri[None, :], BF16)
    ownblk = lambda: pl.BlockSpec((seq_len, ML_HEAD_DIM), lambda b, h: (b, h))
    headblk = lambda: pl.BlockSpec((seq_len, ML_HEAD_DIM), lambda b, h: (sb0 + b, h))
    const = lambda a: pl.BlockSpec(a.shape, lambda b, h: (0,) * a.ndim)
    in_specs = [ownblk(), ownblk(), ownblk(), headblk(),
                pl.BlockSpec((1, ML_HEAD_DIM), lambda b, h: (0, h)),
                pl.BlockSpec((None, seq_len, 4), lambda b, h: (h, sb0 + b, 0)),
                pl.BlockSpec((None, 8, seq_len), lambda b, h: (h, 0, sb0 + b)),
                const(upper_incl), const(lower_incl)]
    args = [mq, mk, mv, og, g_mh.reshape(1, -1), gcol, grow, upper_incl, lower_incl]
    if init is not None:
        c0, nm0, layer = init
        in_specs += [
            pl.BlockSpec((None, None, 2, None, ML_HEAD_DIM, ML_HEAD_DIM),
                         lambda b, h: (b, layer, 0, h, 0, 0)),
            pl.BlockSpec((None, None, 8, ML_HEAD_DIM), lambda b, h: (b, h, 0, 0))]
        args += [c0, nm0]
    out_shape = [jax.ShapeDtypeStruct((n_seq * seq_len, ML_WIDTH), mq.dtype)]
    out_specs = [pl.BlockSpec((seq_len, ML_HEAD_DIM), lambda b, h: (b, h))]
    if emit_state:
        out_shape += [jax.ShapeDtypeStruct((n_seq, 2, N_ML_HEADS, ML_HEAD_DIM, ML_HEAD_DIM), F32),
                      jax.ShapeDtypeStruct((n_seq, N_ML_HEADS, 8, ML_HEAD_DIM), F32)]
        out_specs += [pl.BlockSpec((None, 2, None, ML_HEAD_DIM, ML_HEAD_DIM),
                                   lambda b, h: (b, 0, h, 0, 0)),
                      pl.BlockSpec((None, None, 8, ML_HEAD_DIM), lambda b, h: (b, h, 0, 0))]
    return pl.pallas_call(
        functools.partial(_mlstm_kernel, seq_len=seq_len, has_init=init is not None,
                          emit_state=emit_state, precise=precise),
        grid=(n_seq, N_ML_HEADS), in_specs=in_specs, out_specs=tuple(out_specs),
        out_shape=tuple(out_shape),
        compiler_params=_params(("parallel", "parallel")), name="mlstm",
    )(*args)


def _outproj_kernel(*refs, n_ctx_tiles, precise_ctx):
    it = iter(refs)
    atta_ref, attb_ref, mla_ref, mlb_ref, xa_ref, xb_ref, w_ref = [next(it) for _ in range(7)]
    wl_ref = next(it) if precise_ctx else None
    gt_ref, sh_ref, sc_ref, g_ref, wrt_ref, x1_ref, hp_ref, logits_ref, y_scr = it
    i = pl.program_id(0)

    def mix(att, ml):
        return (jnp.dot(att, w_ref[:ATT_WIDTH, :], preferred_element_type=F32)
                + jnp.dot(ml, w_ref[ATT_WIDTH:, :], preferred_element_type=F32))

    @pl.when(i < n_ctx_tiles)
    def _ctx():
        if precise_ctx:
            y_scr[...] = (
                _dot_x3(_hi_lo(atta_ref[...]), (w_ref[:ATT_WIDTH, :], wl_ref[:ATT_WIDTH, :]))
                + _dot_x3(_hi_lo(mla_ref[...]), (w_ref[ATT_WIDTH:, :], wl_ref[ATT_WIDTH:, :])))
        else:
            y_scr[...] = mix(atta_ref[...], mla_ref[...])

    @pl.when(i >= n_ctx_tiles)
    def _lat():
        y_scr[...] = mix(attb_ref[...], mlb_ref[...])

    x1 = _two_part(i, n_ctx_tiles, xa_ref, xb_ref) + gt_ref[...] * y_scr[...]
    x1_ref[...] = x1
    ms = jnp.mean(x1 * x1, axis=-1, keepdims=True)
    h2 = x1 * lax.rsqrt(ms + EPS) * g_ref[...]
    h2 = h2 * (1.0 + sc_ref[...]) + sh_ref[...]
    hp_ref[...] = h2

    logits_ref[...] = lax.dot_general(wrt_ref[...], h2, NT_DIMS, precision=HIGHEST,
                                      preferred_element_type=F32)


def _select_experts(logits, b_col):
    ex = jnp.exp(logits - jnp.max(logits, axis=0, keepdims=True))
    scores = ex / jnp.sum(ex, axis=0, keepdims=True)
    sel = scores + b_col
    row = lambda a, e: a[e:e + 1, :]
    grp_score = []
    for g in range(N_GROUPS):
        xs = [row(sel, g * GROUP_SIZE + j) for j in range(GROUP_SIZE)]
        pairs = [xs[a] + xs[b] for a in range(GROUP_SIZE) for b in range(a + 1, GROUP_SIZE)]
        grp_score.append(functools.reduce(jnp.maximum, pairs))
    best = grp_score[0]
    grp = jnp.zeros_like(best, dtype=jnp.int32)
    for g in range(1, N_GROUPS):
        better = grp_score[g] > best
        grp = jnp.where(better, g, grp)
        best = jnp.where(better, grp_score[g], best)
    pick = lambda a, j: functools.reduce(
        lambda acc, g: jnp.where(grp == g, row(a, g * GROUP_SIZE + j), acc),
        range(1, N_GROUPS), row(a, j))
    xs = [pick(sel, j) for j in range(GROUP_SIZE)]
    ws = [pick(scores, j) for j in range(GROUP_SIZE)]

    def argmax4(vals):
        bv, bi = vals[0], jnp.zeros_like(grp)
        for j in range(1, GROUP_SIZE):
            better = vals[j] > bv
            bi = jnp.where(better, j, bi)
            bv = jnp.where(better, vals[j], bv)
        return bi

    i1 = argmax4(xs)
    i2 = argmax4([jnp.where(i1 == j, NEG_INF, xs[j]) for j in range(GROUP_SIZE)])
    take = lambda vals, idx: functools.reduce(
        lambda acc, j: jnp.where(idx == j, vals[j], acc), range(1, GROUP_SIZE), vals[0])
    w1, w2 = take(ws, i1), take(ws, i2)
    wsum = w1 + w2
    w1, w2 = w1 / wsum, w2 / wsum
    return grp * GROUP_SIZE + i1, grp * GROUP_SIZE + i2, w1, w2


def _outproj(att, ml, x, mod3, n_ctx_tiles, tiles_per_lat_seq, ctx_row, w_out, g_ffn, w_router):
    w_hi, w_lo = w_out
    precise_ctx = w_lo is not None
    n = x[0].shape[0] + x[1].shape[0]

    def mod_row(i):
        return jnp.where(i < n_ctx_tiles, ctx_row, (i - n_ctx_tiles) // tiles_per_lat_seq)

    tok = lambda w: pl.BlockSpec((TOKEN_TILE, w), lambda i: (i, 0))
    full = lambda a: pl.BlockSpec(a.shape, lambda i: (0,) * a.ndim, pipeline_mode=pl.Buffered(1))
    modspec = lambda j: pl.BlockSpec((None, 1, D_MODEL), lambda i: (mod_row(i), 0, j))
    weights = (w_hi, w_lo) if precise_ctx else (w_hi,)
    consts = (g_ffn.reshape(1, -1), w_router.T)
    args = (*att, *ml, *x, *weights, mod3, mod3, mod3) + consts
    in_specs = (_two_part_specs(ATT_WIDTH, n_ctx_tiles) + _two_part_specs(ML_WIDTH, n_ctx_tiles)
                + _two_part_specs(D_MODEL, n_ctx_tiles) + [full(w) for w in weights]
                + [modspec(2), modspec(3), modspec(4)] + [full(a) for a in consts])
    return pl.pallas_call(
        functools.partial(_outproj_kernel, n_ctx_tiles=n_ctx_tiles, precise_ctx=precise_ctx),
        grid=(n // TOKEN_TILE,), in_specs=in_specs,
        scratch_shapes=[pltpu.VMEM((TOKEN_TILE, D_MODEL), F32)],
        out_specs=(tok(D_MODEL), tok(D_MODEL),
                   pl.BlockSpec((N_EXPERTS, TOKEN_TILE), lambda i: (0, i))),
        out_shape=(jax.ShapeDtypeStruct((n, D_MODEL), F32),
                   jax.ShapeDtypeStruct((n, D_MODEL), F32),
                   jax.ShapeDtypeStruct((N_EXPERTS, n), F32)),
        compiler_params=_params(("parallel",)), name="outproj_router",
    )(*args)


def _moe_kernel(order_ref, pos0_ref, pos1_ref, off_ref, cnt_ref,
                h_ref, wg_ref, wu_ref, wd_ref, x1_ref, gt_ref, wcol_ref, ya_ref, yb_ref,
                o_scr, xs_scr, comb0_scr, comb1_scr, *, n_ctx_tiles):
    t = pl.program_id(0)
    s = pl.program_id(1)

    groups = MOE_CHUNK // 8

    def gather_rows(buf, slot0):
        for j in range(MOE_CHUNK):
            src = order_ref[slot0 + j]
            xs_scr[buf, j // 8, pl.ds(j % 8, 1), :] = h_ref[pl.ds(src, 1), :]

    def ffn(buf, base):
        xs = xs_scr[buf].reshape(MOE_CHUNK, D_MODEL).astype(BF16)
        a = jnp.dot(xs, wg_ref[...], preferred_element_type=F32)
        b = jnp.dot(xs, wu_ref[...], preferred_element_type=F32)
        hid = (a * jax.nn.sigmoid(a)) * b
        o_scr[pl.ds(base, MOE_CHUNK), :] = jnp.dot(hid.astype(BF16), wd_ref[...],
                                                   preferred_element_type=F32)

    @pl.when(s == 0)
    def _first():
        gather_rows(0, t * MOE_SLOTS + pl.multiple_of(off_ref[t * N_EXPERTS], 8))

    @pl.when(s < N_EXPERTS)
    def _expert():
        seg = t * N_EXPERTS + s
        off = pl.multiple_of(off_ref[seg], 8)
        cur = s % 2
        nxt_seg = t * N_EXPERTS + jnp.minimum(s + 1, N_EXPERTS - 1)
        ffn(cur, off)
        gather_rows(1 - cur, t * MOE_SLOTS + pl.multiple_of(off_ref[nxt_seg], 8))

        def extra(c, carry):
            base = pl.multiple_of(off + c * MOE_CHUNK, 8)

            def gather8(i, carry2):
                slot = t * MOE_SLOTS + base + i * 8
                for k in range(8):
                    src = order_ref[slot + k]
                    xs_scr[2, i, pl.ds(k, 1), :] = h_ref[pl.ds(src, 1), :]
                return carry2

            lax.fori_loop(0, groups, gather8, 0)
            ffn(2, base)
            return carry

        lax.fori_loop(1, (cnt_ref[seg] + MOE_CHUNK - 1) // MOE_CHUNK, extra, 0)

    @pl.when(s >= N_EXPERTS)
    def _combine():
        tok0 = t * MOE_TILE + (s - N_EXPERTS) * TOKEN_TILE

        def body8(i, carry):
            for k in range(8):
                p0 = pos0_ref[tok0 + i * 8 + k]
                p1 = pos1_ref[tok0 + i * 8 + k]
                comb0_scr[i, pl.ds(k, 1), :] = o_scr[pl.ds(p0, 1), :]
                comb1_scr[i, pl.ds(k, 1), :] = o_scr[pl.ds(p1, 1), :]
            return carry

        lax.fori_loop(0, TOKEN_TILE // 8, body8, 0)
        wcol = wcol_ref[...]
        comb = (wcol[:, 2:3] * comb0_scr[...].reshape(TOKEN_TILE, D_MODEL)
                + wcol[:, 3:4] * comb1_scr[...].reshape(TOKEN_TILE, D_MODEL))
        y = x1_ref[...] + gt_ref[...] * comb
        chunk_ix = t * (MOE_TILE // TOKEN_TILE) + s - N_EXPERTS

        @pl.when(chunk_ix < n_ctx_tiles)
        def _ctx():
            ya_ref[...] = y

        @pl.when(chunk_ix >= n_ctx_tiles)
        def _lat():
            yb_ref[...] = y


def _route_tables(logits, b_router):
    n = logits.shape[1]
    nt = n // MOE_TILE
    tri = np.arange(SEQ_BLOCK)
    strict_upper = jnp.asarray(tri[:, None] < tri[None, :], BF16)
    tok = np.arange(MOE_TILE)
    digits = np.zeros((MOE_TILE, LANES), np.float32)
    digits[:, 0] = tok % 256
    digits[:, 1] = tok // 256
    pos, order, meta, wcol = pl.pallas_call(
        _route_kernel, grid=(nt,),
        in_specs=[pl.BlockSpec((N_EXPERTS, MOE_TILE), lambda t: (0, t)),
                  pl.BlockSpec((N_EXPERTS, 1), lambda t: (0, 0)),
                  pl.BlockSpec((SEQ_BLOCK, SEQ_BLOCK), lambda t: (0, 0)),
                  pl.BlockSpec((SEQ_BLOCK, SEQ_BLOCK), lambda t: (0, 0)),
                  pl.BlockSpec((MOE_TILE, LANES), lambda t: (0, 0))],
        out_specs=(pl.BlockSpec((8, MOE_TILE), lambda t: (0, t)),
                   pl.BlockSpec((None, MOE_SLOTS, 1), lambda t: (t, 0, 0)),
                   pl.BlockSpec((None, N_EXPERTS, 8), lambda t: (t, 0, 0)),
                   pl.BlockSpec((MOE_TILE, 8), lambda t: (t, 0))),
        out_shape=(jax.ShapeDtypeStruct((8, n), jnp.int32),
                   jax.ShapeDtypeStruct((nt, MOE_SLOTS, 1), jnp.int32),
                   jax.ShapeDtypeStruct((nt, N_EXPERTS, 8), jnp.int32),
                   jax.ShapeDtypeStruct((n, 8), F32)),
        scratch_shapes=[pltpu.VMEM((8, MOE_TILE), F32)],
        compiler_params=_params(("parallel",)), name="route_tables",
    )(logits, b_router.reshape(-1, 1), strict_upper, jnp.asarray(np.eye(SEQ_BLOCK), BF16),
      jnp.asarray(digits, BF16))
    return (order.reshape(-1), pos[0], pos[1], meta[:, :, 0].reshape(-1),
            meta[:, :, 1].reshape(-1), wcol)


def _route_kernel(logits_ref, br_ref, su_ref, eye_ref, digits_ref,
                  pos_ref, order_ref, meta_ref, wcol_ref, wrow_scr):
    e1, e2, w1, w2 = _select_experts(logits_ref[...], br_ref[...])
    wrow_scr[...] = jnp.zeros_like(wrow_scr)
    wrow_scr[2:3, :] = w1
    wrow_scr[3:4, :] = w2
    eye = eye_ref[...]
    for b in range(MOE_TILE // SEQ_BLOCK):
        cols = slice(b * SEQ_BLOCK, (b + 1) * SEQ_BLOCK)
        wcol_ref[cols, :] = functools.reduce(jnp.add, [
            lax.dot_general(eye, p, NT_DIMS, preferred_element_type=F32)
            for p in _split3(wrow_scr[:, cols])])
    eid = lax.broadcasted_iota(jnp.int32, (N_EXPERTS, MOE_TILE), 0)
    oh1, oh2 = eid == e1, eid == e2
    oh = jnp.where(oh1, 1.0, 0.0) + jnp.where(oh2, 1.0, 0.0)
    nblk = MOE_TILE // SEQ_BLOCK
    blocks = [oh[:, b * SEQ_BLOCK:(b + 1) * SEQ_BLOCK] for b in range(nblk)]
    inner = jnp.dot(jnp.concatenate(blocks, axis=0).astype(BF16), su_ref[...],
                    preferred_element_type=F32)
    run = jnp.zeros((N_EXPERTS, 1), F32)
    ranks = []
    for b in range(nblk):
        ranks.append(inner[b * N_EXPERTS:(b + 1) * N_EXPERTS, :] + run)
        run = run + jnp.sum(blocks[b], axis=1, keepdims=True)
    count = run
    seg = jnp.floor((count + 7.0) * 0.125) * 8.0
    sub = lax.broadcasted_iota(jnp.int32, (N_EXPERTS, 1), 0)
    off = jnp.zeros((N_EXPERTS, 1), F32)
    for e in range(N_EXPERTS - 1):
        off = off + jnp.where(sub > e, seg[e:e + 1, :], 0.0)
    slot = jnp.concatenate(ranks, axis=1) + off
    pos1 = jnp.sum(jnp.where(oh1, slot, 0.0), axis=0, keepdims=True).astype(jnp.int32)
    pos2 = jnp.sum(jnp.where(oh2, slot, 0.0), axis=0, keepdims=True).astype(jnp.int32)
    pos_ref[...] = jnp.zeros_like(pos_ref)
    pos_ref[0:1, :] = pos1
    pos_ref[1:2, :] = pos2
    meta_ref[...] = jnp.zeros_like(meta_ref)
    meta_ref[:, 0:1] = off.astype(jnp.int32)
    meta_ref[:, 1:2] = count.astype(jnp.int32)
    digits = digits_ref[...]
    rows = MOE_SLOTS // ROUTE_SLOT_BLOCKS
    for sb in range(ROUTE_SLOT_BLOCKS):
        j = lax.broadcasted_iota(jnp.int32, (rows, MOE_TILE), 0) + sb * rows
        hit = jnp.where(j == pos1, 1.0, 0.0) + jnp.where(j == pos2, 1.0, 0.0)
        d = jnp.dot(hit.astype(BF16), digits, preferred_element_type=F32)
        order_ref[sb * rows:(sb + 1) * rows, :] = (d[:, 0:1] + 256.0 * d[:, 1:2]).astype(jnp.int32)


def _moe(hp, logits, b_router, x1, mod3, layer, n_ctx_tiles, tiles_per_lat_seq, ctx_row,
         wg, wu, wd):
    n = hp.shape[0]
    order, pos0, pos1, off, count, wcol = _route_tables(logits, b_router)
    chunks_per_tile = MOE_TILE // TOKEN_TILE
    n_steps = N_EXPERTS + chunks_per_tile

    def chunk_ix(t, s):
        return t * chunks_per_tile + jnp.maximum(s - N_EXPERTS, 0)

    def mod_row(g):
        return jnp.where(g < n_ctx_tiles, ctx_row, (g - n_ctx_tiles) // tiles_per_lat_seq)

    wspec = lambda r, c: pl.BlockSpec(
        (None, None, r, c), lambda t, s, *_: (layer, jnp.minimum(s, N_EXPERTS - 1), 0, 0))
    chunk_spec = pl.BlockSpec((TOKEN_TILE, D_MODEL), lambda t, s, *_: (chunk_ix(t, s), 0))
    grid_spec = pltpu.PrefetchScalarGridSpec(
        num_scalar_prefetch=5,
        grid=(n // MOE_TILE, n_steps),
        in_specs=[
            pl.BlockSpec((MOE_TILE, D_MODEL), lambda t, s, *_: (t, 0),
                         pipeline_mode=pl.Buffered(1)),
            wspec(D_MODEL, D_EXPERT), wspec(D_MODEL, D_EXPERT), wspec(D_EXPERT, D_MODEL),
            chunk_spec,
            pl.BlockSpec((None, 1, D_MODEL), lambda t, s, *_: (mod_row(chunk_ix(t, s)), 0, 5)),
            pl.BlockSpec((TOKEN_TILE, 8), lambda t, s, *_: (chunk_ix(t, s), 0)),
        ],
        out_specs=(
            pl.BlockSpec((TOKEN_TILE, D_MODEL),
                         lambda t, s, *_: (jnp.minimum(chunk_ix(t, s), n_ctx_tiles - 1), 0)),
            pl.BlockSpec((TOKEN_TILE, D_MODEL),
                         lambda t, s, *_: (jnp.maximum(chunk_ix(t, s) - n_ctx_tiles, 0), 0))),
        scratch_shapes=[pltpu.VMEM((MOE_SLOTS, D_MODEL), F32),
                        pltpu.VMEM((3, MOE_CHUNK // 8, 8, D_MODEL), F32),
                        pltpu.VMEM((TOKEN_TILE // 8, 8, D_MODEL), F32),
                        pltpu.VMEM((TOKEN_TILE // 8, 8, D_MODEL), F32)],
    )
    n_ctx = n_ctx_tiles * TOKEN_TILE
    return pl.pallas_call(
        functools.partial(_moe_kernel, n_ctx_tiles=n_ctx_tiles), grid_spec=grid_spec,
        out_shape=(jax.ShapeDtypeStruct((n_ctx, D_MODEL), F32),
                   jax.ShapeDtypeStruct((n - n_ctx, D_MODEL), F32)),
        compiler_params=_params(("arbitrary", "arbitrary")), name="experts",
    )(order, pos0, pos1, off, count, hp, wg, wu, wd, x1, mod3, wcol)


def _rope_tables(seq_len):
    half = HEAD_DIM // 2
    freqs = ROPE_BASE ** (-np.arange(0, half, 2, dtype=np.float64) / half)
    pos = np.arange(seq_len)
    row, col = pos // GRID_W, pos % GRID_W
    d = np.arange(HEAD_DIM)
    position = np.where(d[None, :] < half, row[:, None], col[:, None]).astype(np.float64)
    ang = (position.astype(np.float32) * freqs.astype(np.float32)[d % (half // 2)][None, :]).astype(np.float32)
    cos, sin = np.cos(ang), np.sin(ang)
    first = (d % half) < half // 2
    sa = np.where(first[None, :], -sin, 0.0)
    sb = np.where(first[None, :], 0.0, sin)
    ident = lambda v: np.full((TOKEN_TILE, HEAD_DIM), v, np.float32)
    stack = lambda ctx, lat: jnp.asarray(
        np.tile(np.concatenate([ctx, lat.astype(np.float32)], axis=0), (1, 2)), F32)
    return stack(ident(1.0), cos), stack(ident(0.0), sa), stack(ident(0.0), sb)


def kernel(x_prompt, x_sample, c, cache_k, cache_v, state_C, state_n, state_m, c_ctx, w_mod, b_mod,
           g_mix, g_ffn, w_in, b_igate, b_fgate, g_q, g_k, g_mh, w_out, w_router, b_router,
           w_e_gate, w_e_up, w_e_down):
    n_ctx_seq, ctx_len, _ = x_prompt.shape
    n_lat_seq, lat_len, _ = x_sample.shape
    n_layers = w_mod.shape[0]
    n_ctx = n_ctx_seq * ctx_len
    assert ctx_len == SEQ_BLOCK and lat_len % TOKEN_TILE == 0
    assert n_ctx % MOE_TILE == 0 and (n_lat_seq * lat_len) % MOE_TILE == 0
    assert n_lat_seq < 16 and n_ctx % lat_len == 0
    n_ctx_tiles = n_ctx // TOKEN_TILE
    tiles_per_lat_seq = lat_len // TOKEN_TILE
    ctx_row = n_lat_seq

    x = (x_prompt.reshape(n_ctx, D_MODEL), x_sample.reshape(-1, D_MODEL))
    cond = jnp.zeros((16, D_MODEL), F32).at[:n_lat_seq].set(c).at[ctx_row].set(c_ctx)
    mod = _modulation(cond, w_mod, b_mod)
    rope = _rope_tables(lat_len)

    wg_b, wu_b, wd_b = w_e_gate.astype(BF16), w_e_up.astype(BF16), w_e_down.astype(BF16)
    gate_perm = np.array([(q % 2) * N_ML_HEADS + hd + 2 * N_ML_HEADS * (q // 2)
                          for hd in range(N_ML_HEADS) for q in range(4)])

    ks, vs, cs, ns, ms = [], [], [], [], []
    for l in range(n_layers):
        mod3 = mod[l].reshape(16, 1, -1)
        precise_ctx = l < n_layers - 1

        def weight_pair(w):
            hi = w.astype(BF16)
            return hi, ((w - hi.astype(F32)).astype(BF16) if precise_ctx else None)

        w_main = weight_pair(w_in[l, :, :MAIN_WIDTH])
        w_gate = w_in[l, :, MAIN_WIDTH:][:, gate_perm]
        b_gate = jnp.concatenate([b_igate[l].reshape(-1), b_fgate[l].reshape(-1)])[gate_perm]
        q, k, v, mq, mk, mv, og, gcol, grow = _inproj(
            *x, mod3, n_ctx_tiles, tiles_per_lat_seq, ctx_row, g_mix[l], w_main, w_gate, b_gate,
            g_q[l], g_k[l], rope)

        att_ctx = _attention(q[0], k, v, 0, n_ctx_seq, ctx_len)
        past = cache_k.shape[2]
        ck = cache_k[:, l].reshape(n_lat_seq * past, LANES)
        cv = cache_v[:, l].reshape(n_lat_seq * past, LANES)
        att_lat = _attention(q[1], k, v, n_ctx, n_lat_seq, lat_len, cache=(ck, cv))

        ml_ctx, c_fin, nm_fin = _mlstm(mq[0], mk[0], mv[0], og, g_mh[l], gcol, grow, 0,
                                       n_ctx_seq, ctx_len, emit_state=True)
        n0 = state_n[:, l].transpose(0, 2, 1, 3)
        m0 = jnp.broadcast_to(state_m[:, l].transpose(0, 2, 1)[..., None], n0.shape)
        nm0 = jnp.concatenate([n0, m0, jnp.zeros_like(n0), jnp.zeros_like(n0)], axis=2)
        (ml_lat,) = _mlstm(mq[1], mk[1], mv[1], og, g_mh[l], gcol, grow, n_ctx, n_lat_seq,
                           lat_len, init=(state_C, nm0, l))

        x1, hp, logits = _outproj(
            (att_ctx, att_lat), (ml_ctx, ml_lat), x, mod3, n_ctx_tiles, tiles_per_lat_seq, ctx_row,
            weight_pair(w_out[l]), g_ffn[l], w_router)
        x = _moe(hp, logits, b_router, x1, mod3, l, n_ctx_tiles, tiles_per_lat_seq, ctx_row,
                 wg_b, wu_b, wd_b)

        ks.append(k[:n_ctx].reshape(n_ctx_seq, ctx_len, N_KV_HEADS, HEAD_DIM))
        vs.append(v[:n_ctx].reshape(n_ctx_seq, ctx_len, N_KV_HEADS, HEAD_DIM))
        cs.append(c_fin)
        ns.append(nm_fin[:, :, 0:2, :].transpose(0, 2, 1, 3))
        ms.append(nm_fin[:, :, 2:4, 0].transpose(0, 2, 1))

    y_prompt = x[0].reshape(x_prompt.shape)
    y_sample = x[1].reshape(x_sample.shape)
    return (y_prompt, y_sample, jnp.stack(ks, axis=1), jnp.stack(vs, axis=1),
            jnp.stack(cs, axis=1), jnp.stack(ns, axis=1), jnp.stack(ms, axis=1))
```

```python
import functools

import numpy as np
import jax
import jax.numpy as jnp
from jax import lax
from jax.experimental import pallas as pl
from jax.experimental.pallas import tpu as pltpu

F32 = jnp.float32
BF16 = jnp.bfloat16

D_MODEL = 1024
HEAD_DIM = 64
ATT_WIDTH = 512
N_KV_HEADS = 2
ML_WIDTH = 512
N_ML_HEADS = 4
ML_HEAD_DIM = 128
GRID_W = 64
ROPE_BASE = 10000.0
N_EXPERTS = 16
N_GROUPS = 4
GROUP_SIZE = 4
D_EXPERT = 512
EPS = 1e-6
MAIN_WIDTH = 2816
N_GATE_COLS = 16
LANES = 128
TOKEN_TILE = 512
SEQ_BLOCK = 256
MOE_TILE = 2048
MOE_CHUNK = 320
MOE_OUT_TILE = 256
ROUTE_SLOT_BLOCKS = 9
MOE_SLOTS = -(-(2 * MOE_TILE + 8 * N_EXPERTS + MOE_CHUNK) // (512 * ROUTE_SLOT_BLOCKS)) * 512 * ROUTE_SLOT_BLOCKS
VMEM_LIMIT = 56 * 1024 * 1024
NEG_INF = float("-inf")
HIGHEST = lax.Precision.HIGHEST
NT_DIMS = (((1,), (1,)), ((), ()))
TN_DIMS = (((0,), (0,)), ((), ()))


def _params(semantics):
    return pltpu.CompilerParams(dimension_semantics=semantics, vmem_limit_bytes=VMEM_LIMIT)


def _log_sigmoid(z):
    return jnp.minimum(z, 0.0) - jnp.log1p(jnp.exp(-jnp.abs(z)))


def _split3(x):
    h1 = x.astype(BF16)
    r1 = x - h1.astype(F32)
    h2 = r1.astype(BF16)
    h3 = (r1 - h2.astype(F32)).astype(BF16)
    return h1, h2, h3


def _mod_kernel(cond_ref, w_ref, b_ref, o_ref):
    c = cond_ref[...]
    s = c * jax.nn.sigmoid(c)
    o_ref[...] = _dot_x3(_hi_lo(s), _hi_lo(w_ref[...])) + b_ref[...]


def _modulation(cond, w_mod, b_mod):
    n_layers = w_mod.shape[0]
    n_chunks = w_mod.shape[2] // D_MODEL
    return pl.pallas_call(
        _mod_kernel,
        grid=(n_layers, n_chunks),
        in_specs=[
            pl.BlockSpec((16, D_MODEL), lambda l, j: (0, 0)),
            pl.BlockSpec((None, D_MODEL, D_MODEL), lambda l, j: (l, 0, j)),
            pl.BlockSpec((None, 1, D_MODEL), lambda l, j: (l, 0, j)),
        ],
        out_specs=pl.BlockSpec((None, 16, D_MODEL), lambda l, j: (l, 0, j)),
        out_shape=jax.ShapeDtypeStruct((n_layers, 16, w_mod.shape[2]), F32),
        compiler_params=_params(("parallel", "parallel")),
        name="modulation",
    )(cond, w_mod, b_mod.reshape(n_layers, 1, -1))


def _two_part(i, n_first, a_ref, b_ref):
    return jnp.where(i < n_first, a_ref[...], b_ref[...])


def _hi_lo(x):
    hi = x.astype(BF16)
    return hi, (x - hi.astype(F32)).astype(BF16)


def _dot_x3(a, b, dims=None):
    (ah, al), (bh, bl) = a, b
    if dims is None:
        d = lambda x, y: jnp.dot(x, y, preferred_element_type=F32)
    else:
        d = lambda x, y: lax.dot_general(x, y, dims, preferred_element_type=F32)
    return d(ah, bh) + (d(al, bh) + d(ah, bl))


def _two_part_out_specs(width, n_first, tile=TOKEN_TILE):
    return _two_part_specs(width, n_first, tile)


def _two_part_specs(width, n_first, tile=TOKEN_TILE):
    return [pl.BlockSpec((tile, width), lambda i: (jnp.minimum(i, n_first - 1), 0)),
            pl.BlockSpec((tile, width), lambda i: (jnp.maximum(i - n_first, 0), 0))]


def _inproj_kernel(*refs, n_ctx_tiles, precise_ctx):
    it = iter(refs)
    xa_ref, xb_ref, sh_ref, sc_ref, g_ref, w_ref = [next(it) for _ in range(6)]
    wl_ref = next(it) if precise_ctx else None
    (wgt_ref, brow_ref, gq_ref, gk_ref, eye_ref, cos_ref, sa_ref, sb_ref, gsum_ref,
     qa_ref, qb_ref, k_ref, v_ref, mqa_ref, mqb_ref, mka_ref, mkb_ref, mva_ref, mvb_ref,
     og_ref, gc_ref, gr_ref) = it
    tile = pl.program_id(0)
    x = _two_part(tile, n_ctx_tiles, xa_ref, xb_ref)
    ms = jnp.mean(x * x, axis=-1, keepdims=True)
    h = x * lax.rsqrt(ms + EPS) * g_ref[...]
    h = h * (1.0 + sc_ref[...]) + sh_ref[...]
    hb = h.astype(BF16)

    zr = lax.dot_general(wgt_ref[...], h, NT_DIMS, precision=HIGHEST,
                         preferred_element_type=F32) + brow_ref[...]
    sub = lax.broadcasted_iota(jnp.int32, zr.shape, 0)
    gr = jnp.where(sub % 4 < 2, zr, _log_sigmoid(zr))
    eye = eye_ref[...]
    gc = functools.reduce(jnp.add, [lax.dot_general(eye, p, NT_DIMS, preferred_element_type=F32)
                                    for p in _split3(gr)])
    gr_ref[...] = jnp.zeros_like(gr_ref)
    for hd in range(N_ML_HEADS):
        gc_ref[hd] = gc[:, 4 * hd:4 * hd + 4]
        gr_ref[hd, 0:4, :] = gr[4 * hd:4 * hd + 4, :]

    cos = cos_ref[...]
    sa = sa_ref[...]
    sb = sb_ref[...]
    gsum = gsum_ref[...]

    def project(precise, q_ref, mq_ref, mk_ref, mv_ref):
        if precise:
            h_pair = _hi_lo(h)
            proj = lambda c0, width: _dot_x3(
                h_pair, (w_ref[:, c0:c0 + width], wl_ref[:, c0:c0 + width]))
        else:
            proj = lambda c0, width: jnp.dot(hb, w_ref[:, c0:c0 + width],
                                             preferred_element_type=F32)
        act = F32 if precise else BF16

        n_qk = ATT_WIDTH // LANES + 1
        t_rows = hb.shape[0]
        zqk = proj(0, n_qk * LANES)
        zs = [zqk[:, c * LANES:(c + 1) * LANES] for c in range(n_qk)]
        sq = jnp.concatenate([z * z for z in zs], axis=0)
        ss = jnp.dot(jnp.concatenate(_hi_lo(sq), axis=0), gsum, preferred_element_type=F32)
        ss = ss[:n_qk * t_rows] + ss[n_qk * t_rows:]

        def headnorm_rope(c, gain):
            zn = zs[c] * lax.rsqrt(ss[c * t_rows:(c + 1) * t_rows] * (1.0 / HEAD_DIM) + EPS) * gain
            return zn * cos + pltpu.roll(zn, LANES - 16, 1) * sa + pltpu.roll(zn, 16, 1) * sb

        for c in range(n_qk - 1):
            q_ref[:, c * LANES:(c + 1) * LANES] = (headnorm_rope(c, gq_ref[...]) * 0.125).astype(act)
        k_ref[...] = headnorm_rope(n_qk - 1, gk_ref[...])
        v_ref[...] = proj(640, LANES)
        mq_ref[...] = proj(768, ML_WIDTH).astype(act)
        mk_ref[...] = (proj(1280, ML_WIDTH) * (ML_HEAD_DIM ** -0.5)).astype(act)
        mv_ref[...] = proj(1792, ML_WIDTH).astype(act)
        og_ref[...] = jax.nn.sigmoid(proj(2304, ML_WIDTH))

    @pl.when(tile < n_ctx_tiles)
    def _ctx():
        project(precise_ctx, qa_ref, mqa_ref, mka_ref, mva_ref)

    @pl.when(tile >= n_ctx_tiles)
    def _lat():
        project(False, qb_ref, mqb_ref, mkb_ref, mvb_ref)


def _inproj(xa, xb, mod3, n_ctx_tiles, tiles_per_lat_seq, ctx_row, g_mix, w_main, w_gate, b_gate,
            g_q, g_k, rope):
    w_hi, w_lo = w_main
    precise_ctx = w_lo is not None
    n_ctx, n_lat = xa.shape[0], xb.shape[0]
    n = n_ctx + n_lat
    n_tiles = n // TOKEN_TILE

    def mod_row(i):
        return jnp.where(i < n_ctx_tiles, ctx_row, (i - n_ctx_tiles) // tiles_per_lat_seq)

    def rope_blk(i):
        return jnp.where(i < n_ctx_tiles, 0, 1 + (i - n_ctx_tiles) % tiles_per_lat_seq)

    cos_t, sa_t, sb_t = rope
    lane = np.arange(LANES)
    gsum = jnp.asarray((lane[:, None] // HEAD_DIM) == (lane[None, :] // HEAD_DIM), BF16)
    tok = lambda w: pl.BlockSpec((TOKEN_TILE, w), lambda i: (i, 0))
    full = lambda a: pl.BlockSpec(a.shape, lambda i: (0,) * a.ndim)
    modspec = lambda j: pl.BlockSpec((None, 1, D_MODEL), lambda i: (mod_row(i), 0, j))
    ropespec = pl.BlockSpec((TOKEN_TILE, LANES), lambda i: (rope_blk(i), 0))
    consts = (g_mix.reshape(1, -1), w_hi) + ((w_lo,) if precise_ctx else ()) + (
        w_gate.T, b_gate.reshape(-1, 1),
        jnp.tile(g_q, 2).reshape(1, -1), jnp.tile(g_k, 2).reshape(1, -1),
        jnp.asarray(np.eye(TOKEN_TILE), BF16))
    args = (xa, xb, mod3, mod3) + consts + (cos_t, sa_t, sb_t, gsum)
    in_specs = _two_part_specs(D_MODEL, n_ctx_tiles) + [modspec(0), modspec(1)] \
        + [full(a) for a in consts] + [ropespec, ropespec, ropespec, full(gsum)]
    ctx_act = F32 if precise_ctx else BF16
    pair_shape = lambda w: [jax.ShapeDtypeStruct((n_ctx, w), ctx_act),
                            jax.ShapeDtypeStruct((n_lat, w), BF16)]
    pair_spec = lambda w: _two_part_out_specs(w, n_ctx_tiles)
    out_shape = (
        pair_shape(ATT_WIDTH)
        + [jax.ShapeDtypeStruct((n, LANES), F32),
           jax.ShapeDtypeStruct((n, LANES), F32)]
        + pair_shape(ML_WIDTH) + pair_shape(ML_WIDTH) + pair_shape(ML_WIDTH)
        + [jax.ShapeDtypeStruct((n, ML_WIDTH), F32),
           jax.ShapeDtypeStruct((N_ML_HEADS, n, 4), F32),
           jax.ShapeDtypeStruct((N_ML_HEADS, 8, n), F32)])
    out_specs = (pair_spec(ATT_WIDTH) + [tok(LANES), tok(LANES)]
                 + pair_spec(ML_WIDTH) + pair_spec(ML_WIDTH) + pair_spec(ML_WIDTH)
                 + [tok(ML_WIDTH),
                    pl.BlockSpec((N_ML_HEADS, TOKEN_TILE, 4), lambda i: (0, i, 0)),
                    pl.BlockSpec((N_ML_HEADS, 8, TOKEN_TILE), lambda i: (0, 0, i))])
    outs = pl.pallas_call(
        functools.partial(_inproj_kernel, n_ctx_tiles=n_ctx_tiles, precise_ctx=precise_ctx),
        grid=(n_tiles,), in_specs=in_specs, out_specs=tuple(out_specs),
        out_shape=tuple(out_shape), compiler_params=_params(("arbitrary",)), name="inproj",
    )(*args)
    qa, qb, k, v, mqa, mqb, mka, mkb, mva, mvb, og, gcol, grow = outs
    return (qa, qb), k, v, (mqa, mqb), (mka, mkb), (mva, mvb), og, gcol, grow


def _attn_kernel(*refs, n_kv, precise):
    q_ref = refs[0]
    kv_refs = refs[1:1 + 2 * n_kv]
    o_ref = refs[-1]
    tq = q_ref.shape[0]
    lo_q = lax.broadcasted_iota(jnp.int32, (tq, LANES), 1) < HEAD_DIM
    operand = _hi_lo if precise else (lambda a: a.astype(BF16))
    if precise:
        qk = lambda a, b: _dot_x3(a, b, NT_DIMS)
        pv = _dot_x3
    else:
        qk = lambda a, b: lax.dot_general(a, b, NT_DIMS, preferred_element_type=F32)
        pv = lambda a, b: jnp.dot(a, b, preferred_element_type=F32)

    def dup_half(ref, g):
        a = ref[...]
        r = pltpu.roll(a, HEAD_DIM, 1)
        lo = lax.broadcasted_iota(jnp.int32, a.shape, 1) < HEAD_DIM
        return operand(jnp.where(lo, a, r) if g == 0 else jnp.where(lo, r, a))

    for g in range(N_KV_HEADS):
        ks = [dup_half(kv_refs[2 * p], g) for p in range(n_kv)]
        vs = [dup_half(kv_refs[2 * p + 1], g) for p in range(n_kv)]
        for hb in range(2):
            c0 = (2 * g + hb) * LANES
            qb = q_ref[:, c0:c0 + LANES]
            outs = []
            for half in range(2):
                keep = lo_q if half == 0 else jnp.logical_not(lo_q)
                qm = operand(jnp.where(keep, qb, jnp.zeros_like(qb)))
                ss = [qk(qm, kd) for kd in ks]
                m = functools.reduce(jnp.maximum, [jnp.max(s, axis=1, keepdims=True) for s in ss])
                ps = [jnp.exp(s - m) for s in ss]
                den = functools.reduce(jnp.add, [jnp.sum(p, axis=1, keepdims=True) for p in ps])
                o = functools.reduce(jnp.add, [pv(operand(p), vd) for p, vd in zip(ps, vs)])
                outs.append(o / den)
            o_ref[:, c0:c0 + LANES] = jnp.where(lo_q, outs[0], outs[1]).astype(o_ref.dtype)


def _attention(q, k, v, kv_row0, n_seq, seq_len, cache=None):
    precise = q.dtype == F32
    nq = seq_len // SEQ_BLOCK
    sb0 = kv_row0 // seq_len
    in_specs = [
        pl.BlockSpec((SEQ_BLOCK, ATT_WIDTH), lambda b, i: (b * nq + i, 0)),
        pl.BlockSpec((seq_len, LANES), lambda b, i: (sb0 + b, 0)),
        pl.BlockSpec((seq_len, LANES), lambda b, i: (sb0 + b, 0)),
    ]
    args = [q, k, v]
    n_kv = 1
    if cache is not None:
        ck, cv = cache
        past = ck.shape[0] // n_seq
        in_specs += [pl.BlockSpec((past, LANES), lambda b, i: (b, 0))] * 2
        args += [ck, cv]
        n_kv = 2
    return pl.pallas_call(
        functools.partial(_attn_kernel, n_kv=n_kv, precise=precise),
        grid=(n_seq, nq), in_specs=in_specs,
        out_specs=pl.BlockSpec((SEQ_BLOCK, ATT_WIDTH), lambda b, i: (b * nq + i, 0)),
        out_shape=jax.ShapeDtypeStruct((n_seq * seq_len, ATT_WIDTH), q.dtype),
        compiler_params=_params(("parallel", "parallel")), name="attention",
    )(*args)


def _mlstm_kernel(*refs, seq_len, has_init, emit_state, precise):
    it = iter(refs)
    q_ref, k_ref, v_ref, og_ref, gmh_ref, gcol_ref, grow_ref, u_ref, l_ref = [next(it) for _ in range(9)]
    if has_init:
        c0_ref, nm0_ref = next(it), next(it)
    ml_ref = next(it)
    if emit_state:
        cf_ref, nmf_ref = next(it), next(it)

    operand = _hi_lo if precise else (lambda a: a.astype(BF16))
    if precise:
        qk = lambda a, b: _dot_x3(a, b, NT_DIMS)
        pv = _dot_x3
    else:
        qk = lambda a, b: lax.dot_general(a, b, NT_DIMS, preferred_element_type=F32)
        pv = lambda a, b: jnp.dot(a, b, preferred_element_type=F32)

    bq = SEQ_BLOCK
    nb = seq_len // bq
    blk = lambda j: slice(j * bq, (j + 1) * bq)
    upper_incl = u_ref[...]
    lower_incl = l_ref[...]

    def tri_dot(x, tri):
        return functools.reduce(jnp.add, [jnp.dot(p, tri, preferred_element_type=F32)
                                          for p in _split3(x)])

    ig_row = [[None] * nb for _ in range(2)]
    lf_row = [[None] * nb for _ in range(2)]
    within = [[None] * nb for _ in range(2)]
    bsum = [[None] * nb for _ in range(2)]
    for j in range(nb):
        g8 = grow_ref[:, blk(j)]
        cum_f = tri_dot(g8, upper_incl)
        cum_b = tri_dot(g8, lower_incl)
        for d in range(2):
            ig_row[d][j] = g8[d:d + 1, :]
            lf_row[d][j] = g8[2 + d:3 + d, :]
            within[d][j] = (cum_f if d == 0 else cum_b)[2 + d:3 + d, :]
            bsum[d][j] = jnp.sum(lf_row[d][j], axis=1, keepdims=True)
    zero11 = jnp.zeros((1, 1), F32)
    offset = [[None] * nb for _ in range(2)]
    acc = zero11
    for j in range(nb):
        offset[0][j] = acc
        acc = acc + bsum[0][j]
    total_f = acc
    acc = zero11
    for j in reversed(range(nb)):
        offset[1][j] = acc
        acc = acc + bsum[1][j]
    total = [total_f, acc]
    a_row = [[ig_row[d][j] - (within[d][j] + offset[d][j]) for j in range(nb)] for d in range(2)]
    blkmax = [[jnp.max(a_row[d][j], axis=1, keepdims=True) for j in range(nb)] for d in range(2)]
    if has_init:
        m0 = [nm0_ref[2:3, 0:1], nm0_ref[3:4, 0:1]]
    else:
        m0 = [zero11, zero11]

    r_i = lax.broadcasted_iota(jnp.int32, (bq, bq), 0)
    c_i = lax.broadcasted_iota(jnp.int32, (bq, bq), 1)
    causal = [c_i <= r_i, c_i >= r_i]
    before = [lambda i: range(0, i), lambda i: range(i + 1, nb)]

    gmh = gmh_ref[...]
    b_col = [[None] * nb for _ in range(2)]
    for i in range(nb):
        q_i = q_ref[blk(i), :]
        q_op = operand(q_i)
        a_blocks = {}

        def scores(j):
            if j not in a_blocks:
                a_blocks[j] = qk(q_op, operand(k_ref[blk(j), :]))
            return a_blocks[j]

        h = None
        for d in range(2):
            m_prev = functools.reduce(jnp.maximum, [blkmax[d][j] for j in before[d](i)], m0[d])
            b_col[d][i] = offset[d][i] + jnp.sum(jnp.where(causal[d], lf_row[d][i], 0.0),
                                                 axis=1, keepdims=True)
            m_col = jnp.maximum(m_prev, jnp.max(jnp.where(causal[d], a_row[d][i], NEG_INF),
                                                axis=1, keepdims=True))
            num = jnp.zeros((bq, ML_HEAD_DIM), F32)
            den = jnp.zeros((bq, 1), F32)
            for j in list(before[d](i)) + [i]:
                arg = a_row[d][j] - m_col
                if j == i:
                    arg = jnp.where(causal[d], arg, NEG_INF)
                p = jnp.exp(arg) * scores(j)
                den = den + jnp.sum(p, axis=1, keepdims=True)
                num = num + pv(operand(p), operand(v_ref[blk(j), :]))
            if has_init:
                w_inter = jnp.exp(m0[d] - m_col)
                qc = jnp.dot(q_i.astype(BF16), c0_ref[d].astype(BF16),
                             preferred_element_type=F32)
                qn = jnp.sum(q_i.astype(F32) * nm0_ref[d:d + 1, :], axis=1, keepdims=True)
                num = num + w_inter * qc
                den = den + w_inter * qn
            nrm = jnp.maximum(jnp.abs(den), jnp.exp(-(b_col[d][i] + m_col)))
            h = num / nrm if h is None else h + num / nrm
        hn = h * lax.rsqrt(jnp.mean(h * h, axis=-1, keepdims=True) + EPS) * gmh
        ml_ref[blk(i), :] = (og_ref[blk(i), :] * hn).astype(ml_ref.dtype)

    if emit_state:
        nmf_ref[...] = jnp.zeros_like(nmf_ref)
        for d in range(2):
            m_last = functools.reduce(jnp.maximum, blkmax[d], m0[d])
            c_fin = jnp.zeros((ML_HEAD_DIM, ML_HEAD_DIM), F32)
            n_fin = jnp.zeros((1, ML_HEAD_DIM), F32)
            for j in range(nb):
                a_col = gcol_ref[blk(j), d:d + 1] - b_col[d][j]
                kw = k_ref[blk(j), :].astype(F32) * jnp.exp(a_col - m_last)
                c_fin = c_fin + lax.dot_general(kw.astype(BF16), v_ref[blk(j), :].astype(BF16),
                                                TN_DIMS, preferred_element_type=F32)
                n_fin = n_fin + jnp.sum(kw, axis=0, keepdims=True)
            if has_init:
                decay = jnp.exp(m0[d] - m_last)
                c_fin = c_fin + decay * c0_ref[d]
                n_fin = n_fin + decay * nm0_ref[d:d + 1, :]
            cf_ref[d] = c_fin
            nmf_ref[d:d + 1, :] = n_fin
            nmf_ref[2 + d:3 + d, :] = jnp.broadcast_to(total[d] + m_last, (1, ML_HEAD_DIM))


def _mlstm(mq, mk, mv, og, g_mh, gcol, grow, row0, n_seq, seq_len, init=None, emit_state=False):
    precise = mq.dtype == F32
    assert not (precise and init is not None)
    sb0 = row0 // seq_len
    tri = np.arange(SEQ_BLOCK)
    upper_incl = jnp.asarray(tri[:, None] <= tri[None, :], BF16)
    lower_incl = jnp.asarray(tri[:, None] >= tri[None, :], BF16)
    ownblk = lambda: pl.BlockSpec((seq_len, ML_HEAD_DIM), lambda b, h: (b, h))
    headblk = lambda: pl.BlockSpec((seq_len, ML_HEAD_DIM), lambda b, h: (sb0 + b, h))
    const = lambda a: pl.BlockSpec(a.shape, lambda b, h: (0,) * a.ndim)
    in_specs = [ownblk(), ownblk(), ownblk(), headblk(),
                pl.BlockSpec((1, ML_HEAD_DIM), lambda b, h: (0, h)),
                pl.BlockSpec((None, seq_len, 4), lambda b, h: (h, sb0 + b, 0)),
                pl.BlockSpec((None, 8, seq_len), lambda b, h: (h, 0, sb0 + b)),
                const(upper_incl), const(lower_incl)]
    args = [mq, mk, mv, og, g_mh.reshape(1, -1), gcol, grow, upper_incl, lower_incl]
    if init is not None:
        c0, nm0, layer = init
        in_specs += [
            pl.BlockSpec((None, None, 2, None, ML_HEAD_DIM, ML_HEAD_DIM),
                         lambda b, h: (b, layer, 0, h, 0, 0)),
            pl.BlockSpec((None, None, 8, ML_HEAD_DIM), lambda b, h: (b, h, 0, 0))]
        args += [c0, nm0]
    out_shape = [jax.ShapeDtypeStruct((n_seq * seq_len, ML_WIDTH), mq.dtype)]
    out_specs = [pl.BlockSpec((seq_len, ML_HEAD_DIM), lambda b, h: (b, h))]
    if emit_state:
        out_shape += [jax.ShapeDtypeStruct((n_seq, 2, N_ML_HEADS, ML_HEAD_DIM, ML_HEAD_DIM), F32),
                      jax.ShapeDtypeStruct((n_seq, N_ML_HEADS, 8, ML_HEAD_DIM), F32)]
        out_specs += [pl.BlockSpec((None, 2, None, ML_HEAD_DIM, ML_HEAD_DIM),
                                   lambda b, h: (b, 0, h, 0, 0)),
                      pl.BlockSpec((None, None, 8, ML_HEAD_DIM), lambda b, h: (b, h, 0, 0))]
    return pl.pallas_call(
        functools.partial(_mlstm_kernel, seq_len=seq_len, has_init=init is not None,
                          emit_state=emit_state, precise=precise),
        grid=(n_seq, N_ML_HEADS), in_specs=in_specs, out_specs=tuple(out_specs),
        out_shape=tuple(out_shape),
        compiler_params=_params(("parallel", "parallel")), name="mlstm",
    )(*args)


def _outproj_kernel(*refs, n_ctx_tiles, precise_ctx):
    it = iter(refs)
    atta_ref, attb_ref, mla_ref, mlb_ref, xa_ref, xb_ref, w_ref = [next(it) for _ in range(7)]
    wl_ref = next(it) if precise_ctx else None
    gt_ref, sh_ref, sc_ref, g_ref, wrt_ref, x1_ref, hp_ref, logits_ref, y_scr = it
    i = pl.program_id(0)

    def mix(att, ml):
        return (jnp.dot(att, w_ref[:ATT_WIDTH, :], preferred_element_type=F32)
                + jnp.dot(ml, w_ref[ATT_WIDTH:, :], preferred_element_type=F32))

    @pl.when(i < n_ctx_tiles)
    def _ctx():
        if precise_ctx:
            y_scr[...] = (
                _dot_x3(_hi_lo(atta_ref[...]), (w_ref[:ATT_WIDTH, :], wl_ref[:ATT_WIDTH, :]))
                + _dot_x3(_hi_lo(mla_ref[...]), (w_ref[ATT_WIDTH:, :], wl_ref[ATT_WIDTH:, :])))
        else:
            y_scr[...] = mix(atta_ref[...], mla_ref[...])

    @pl.when(i >= n_ctx_tiles)
    def _lat():
        y_scr[...] = mix(attb_ref[...], mlb_ref[...])

    x1 = _two_part(i, n_ctx_tiles, xa_ref, xb_ref) + gt_ref[...] * y_scr[...]
    x1_ref[...] = x1
    ms = jnp.mean(x1 * x1, axis=-1, keepdims=True)
    h2 = x1 * lax.rsqrt(ms + EPS) * g_ref[...]
    h2 = h2 * (1.0 + sc_ref[...]) + sh_ref[...]
    hp_ref[...] = h2

    logits_ref[...] = lax.dot_general(wrt_ref[...], h2, NT_DIMS, precision=HIGHEST,
                                      preferred_element_type=F32)


def _select_experts(logits, b_col):
    ex = jnp.exp(logits - jnp.max(logits, axis=0, keepdims=True))
    scores = ex / jnp.sum(ex, axis=0, keepdims=True)
    sel = scores + b_col
    row = lambda a, e: a[e:e + 1, :]
    grp_score = []
    for g in range(N_GROUPS):
        xs = [row(sel, g * GROUP_SIZE + j) for j in range(GROUP_SIZE)]
        pairs = [xs[a] + xs[b] for a in range(GROUP_SIZE) for b in range(a + 1, GROUP_SIZE)]
        grp_score.append(functools.reduce(jnp.maximum, pairs))
    best = grp_score[0]
    grp = jnp.zeros_like(best, dtype=jnp.int32)
    for g in range(1, N_GROUPS):
        better = grp_score[g] > best
        grp = jnp.where(better, g, grp)
        best = jnp.where(better, grp_score[g], best)
    pick = lambda a, j: functools.reduce(
        lambda acc, g: jnp.where(grp == g, row(a, g * GROUP_SIZE + j), acc),
        range(1, N_GROUPS), row(a, j))
    xs = [pick(sel, j) for j in range(GROUP_SIZE)]
    ws = [pick(scores, j) for j in range(GROUP_SIZE)]

    def argmax4(vals):
        bv, bi = vals[0], jnp.zeros_like(grp)
        for j in range(1, GROUP_SIZE):
            better = vals[j] > bv
            bi = jnp.where(better, j, bi)
            bv = jnp.where(better, vals[j], bv)
        return bi

    i1 = argmax4(xs)
    i2 = argmax4([jnp.where(i1 == j, NEG_INF, xs[j]) for j in range(GROUP_SIZE)])
    take = lambda vals, idx: functools.reduce(
        lambda acc, j: jnp.where(idx == j, vals[j], acc), range(1, GROUP_SIZE), vals[0])
    w1, w2 = take(ws, i1), take(ws, i2)
    wsum = w1 + w2
    w1, w2 = w1 / wsum, w2 / wsum
    return grp * GROUP_SIZE + i1, grp * GROUP_SIZE + i2, w1, w2


def _outproj(att, ml, x, mod3, n_ctx_tiles, tiles_per_lat_seq, ctx_row, w_out, g_ffn, w_router):
    w_hi, w_lo = w_out
    precise_ctx = w_lo is not None
    n = x[0].shape[0] + x[1].shape[0]

    def mod_row(i):
        return jnp.where(i < n_ctx_tiles, ctx_row, (i - n_ctx_tiles) // tiles_per_lat_seq)

    tok = lambda w: pl.BlockSpec((TOKEN_TILE, w), lambda i: (i, 0))
    full = lambda a: pl.BlockSpec(a.shape, lambda i: (0,) * a.ndim)
    modspec = lambda j: pl.BlockSpec((None, 1, D_MODEL), lambda i: (mod_row(i), 0, j))
    weights = (w_hi, w_lo) if precise_ctx else (w_hi,)
    consts = (g_ffn.reshape(1, -1), w_router.T)
    args = (*att, *ml, *x, *weights, mod3, mod3, mod3) + consts
    in_specs = (_two_part_specs(ATT_WIDTH, n_ctx_tiles) + _two_part_specs(ML_WIDTH, n_ctx_tiles)
                + _two_part_specs(D_MODEL, n_ctx_tiles) + [full(w) for w in weights]
                + [modspec(2), modspec(3), modspec(4)] + [full(a) for a in consts])
    return pl.pallas_call(
        functools.partial(_outproj_kernel, n_ctx_tiles=n_ctx_tiles, precise_ctx=precise_ctx),
        grid=(n // TOKEN_TILE,), in_specs=in_specs,
        scratch_shapes=[pltpu.VMEM((TOKEN_TILE, D_MODEL), F32)],
        out_specs=(tok(D_MODEL), tok(D_MODEL),
                   pl.BlockSpec((N_EXPERTS, TOKEN_TILE), lambda i: (0, i))),
        out_shape=(jax.ShapeDtypeStruct((n, D_MODEL), F32),
                   jax.ShapeDtypeStruct((n, D_MODEL), F32),
                   jax.ShapeDtypeStruct((N_EXPERTS, n), F32)),
        compiler_params=_params(("parallel",)), name="outproj_router",
    )(*args)


def _moe_kernel(order_ref, pos0_ref, pos1_ref, off_ref, cnt_ref,
                h_ref, wg_ref, wu_ref, wd_ref, x1_ref, gt_ref, wcol_ref, ya_ref, yb_ref,
                o_scr, xs_scr, comb0_scr, comb1_scr, *, n_ctx_tiles):
    t = pl.program_id(0)
    s = pl.program_id(1)

    groups = MOE_CHUNK // 8

    def gather_rows(buf, slot0):
        for j in range(MOE_CHUNK):
            src = order_ref[slot0 + j]
            xs_scr[buf, j // 8, pl.ds(j % 8, 1), :] = h_ref[pl.ds(src, 1), :]

    def ffn(buf, base):
        xs = xs_scr[buf].reshape(MOE_CHUNK, D_MODEL).astype(BF16)
        a = jnp.dot(xs, wg_ref[...], preferred_element_type=F32)
        b = jnp.dot(xs, wu_ref[...], preferred_element_type=F32)
        hid = (a * jax.nn.sigmoid(a)) * b
        o_scr[pl.ds(base, MOE_CHUNK), :] = jnp.dot(hid.astype(BF16), wd_ref[...],
                                                   preferred_element_type=F32)

    @pl.when(s == 0)
    def _first():
        gather_rows(0, t * MOE_SLOTS + pl.multiple_of(off_ref[t * N_EXPERTS], 8))

    @pl.when(s < N_EXPERTS)
    def _expert():
        seg = t * N_EXPERTS + s
        off = pl.multiple_of(off_ref[seg], 8)
        cur = s % 2
        nxt_seg = t * N_EXPERTS + jnp.minimum(s + 1, N_EXPERTS - 1)
        ffn(cur, off)
        gather_rows(1 - cur, t * MOE_SLOTS + pl.multiple_of(off_ref[nxt_seg], 8))

        def extra(c, carry):
            base = pl.multiple_of(off + c * MOE_CHUNK, 8)

            def gather8(i, carry2):
                slot = t * MOE_SLOTS + base + i * 8
                for k in range(8):
                    src = order_ref[slot + k]
                    xs_scr[2, i, pl.ds(k, 1), :] = h_ref[pl.ds(src, 1), :]
                return carry2

            lax.fori_loop(0, groups, gather8, 0)
            ffn(2, base)
            return carry

        lax.fori_loop(1, (cnt_ref[seg] + MOE_CHUNK - 1) // MOE_CHUNK, extra, 0)

    @pl.when(s >= N_EXPERTS)
    def _combine():
        tok0 = t * MOE_TILE + (s - N_EXPERTS) * MOE_OUT_TILE

        def body8(i, carry):
            for k in range(8):
                p0 = pos0_ref[tok0 + i * 8 + k]
                p1 = pos1_ref[tok0 + i * 8 + k]
                comb0_scr[i, pl.ds(k, 1), :] = o_scr[pl.ds(p0, 1), :]
                comb1_scr[i, pl.ds(k, 1), :] = o_scr[pl.ds(p1, 1), :]
            return carry

        lax.fori_loop(0, MOE_OUT_TILE // 8, body8, 0)
        wcol = wcol_ref[...]
        comb = (wcol[:, 2:3] * comb0_scr[...].reshape(MOE_OUT_TILE, D_MODEL)
                + wcol[:, 3:4] * comb1_scr[...].reshape(MOE_OUT_TILE, D_MODEL))
        y = x1_ref[...] + gt_ref[...] * comb
        chunk_ix = t * (MOE_TILE // MOE_OUT_TILE) + s - N_EXPERTS

        @pl.when(chunk_ix < n_ctx_tiles)
        def _ctx():
            ya_ref[...] = y

        @pl.when(chunk_ix >= n_ctx_tiles)
        def _lat():
            yb_ref[...] = y


def _route_tables(logits, b_router):
    n = logits.shape[1]
    nt = n // MOE_TILE
    tri = np.arange(SEQ_BLOCK)
    strict_upper = jnp.asarray(tri[:, None] < tri[None, :], BF16)
    tok = np.arange(MOE_TILE)
    digits = np.zeros((MOE_TILE, LANES), np.float32)
    digits[:, 0] = tok % 256
    digits[:, 1] = tok // 256
    pos, order, meta, wcol = pl.pallas_call(
        _route_kernel, grid=(nt,),
        in_specs=[pl.BlockSpec((N_EXPERTS, MOE_TILE), lambda t: (0, t)),
                  pl.BlockSpec((N_EXPERTS, 1), lambda t: (0, 0)),
                  pl.BlockSpec((SEQ_BLOCK, SEQ_BLOCK), lambda t: (0, 0)),
                  pl.BlockSpec((SEQ_BLOCK, SEQ_BLOCK), lambda t: (0, 0)),
                  pl.BlockSpec((MOE_TILE, LANES), lambda t: (0, 0))],
        out_specs=(pl.BlockSpec((8, MOE_TILE), lambda t: (0, t)),
                   pl.BlockSpec((None, MOE_SLOTS, 1), lambda t: (t, 0, 0)),
                   pl.BlockSpec((None, N_EXPERTS, 8), lambda t: (t, 0, 0)),
                   pl.BlockSpec((MOE_TILE, 8), lambda t: (t, 0))),
        out_shape=(jax.ShapeDtypeStruct((8, n), jnp.int32),
                   jax.ShapeDtypeStruct((nt, MOE_SLOTS, 1), jnp.int32),
                   jax.ShapeDtypeStruct((nt, N_EXPERTS, 8), jnp.int32),
                   jax.ShapeDtypeStruct((n, 8), F32)),
        scratch_shapes=[pltpu.VMEM((8, MOE_TILE), F32)],
        compiler_params=_params(("parallel",)), name="route_tables",
    )(logits, b_router.reshape(-1, 1), strict_upper, jnp.asarray(np.eye(SEQ_BLOCK), BF16),
      jnp.asarray(digits, BF16))
    return (order.reshape(-1), pos[0], pos[1], meta[:, :, 0].reshape(-1),
            meta[:, :, 1].reshape(-1), wcol)


def _route_kernel(logits_ref, br_ref, su_ref, eye_ref, digits_ref,
                  pos_ref, order_ref, meta_ref, wcol_ref, wrow_scr):
    e1, e2, w1, w2 = _select_experts(logits_ref[...], br_ref[...])
    wrow_scr[...] = jnp.zeros_like(wrow_scr)
    wrow_scr[2:3, :] = w1
    wrow_scr[3:4, :] = w2
    eye = eye_ref[...]
    for b in range(MOE_TILE // SEQ_BLOCK):
        cols = slice(b * SEQ_BLOCK, (b + 1) * SEQ_BLOCK)
        wcol_ref[cols, :] = functools.reduce(jnp.add, [
            lax.dot_general(eye, p, NT_DIMS, preferred_element_type=F32)
            for p in _split3(wrow_scr[:, cols])])
    eid = lax.broadcasted_iota(jnp.int32, (N_EXPERTS, MOE_TILE), 0)
    oh1, oh2 = eid == e1, eid == e2
    oh = jnp.where(oh1, 1.0, 0.0) + jnp.where(oh2, 1.0, 0.0)
    nblk = MOE_TILE // SEQ_BLOCK
    blocks = [oh[:, b * SEQ_BLOCK:(b + 1) * SEQ_BLOCK] for b in range(nblk)]
    inner = jnp.dot(jnp.concatenate(blocks, axis=0).astype(BF16), su_ref[...],
                    preferred_element_type=F32)
    run = jnp.zeros((N_EXPERTS, 1), F32)
    ranks = []
    for b in range(nblk):
        ranks.append(inner[b * N_EXPERTS:(b + 1) * N_EXPERTS, :] + run)
        run = run + jnp.sum(blocks[b], axis=1, keepdims=True)
    count = run
    seg = jnp.floor((count + 7.0) * 0.125) * 8.0
    sub = lax.broadcasted_iota(jnp.int32, (N_EXPERTS, 1), 0)
    off = jnp.zeros((N_EXPERTS, 1), F32)
    for e in range(N_EXPERTS - 1):
        off = off + jnp.where(sub > e, seg[e:e + 1, :], 0.0)
    slot = jnp.concatenate(ranks, axis=1) + off
    pos1 = jnp.sum(jnp.where(oh1, slot, 0.0), axis=0, keepdims=True).astype(jnp.int32)
    pos2 = jnp.sum(jnp.where(oh2, slot, 0.0), axis=0, keepdims=True).astype(jnp.int32)
    pos_ref[...] = jnp.zeros_like(pos_ref)
    pos_ref[0:1, :] = pos1
    pos_ref[1:2, :] = pos2
    meta_ref[...] = jnp.zeros_like(meta_ref)
    meta_ref[:, 0:1] = off.astype(jnp.int32)
    meta_ref[:, 1:2] = count.astype(jnp.int32)
    digits = digits_ref[...]
    rows = MOE_SLOTS // ROUTE_SLOT_BLOCKS
    for sb in range(ROUTE_SLOT_BLOCKS):
        j = lax.broadcasted_iota(jnp.int32, (rows, MOE_TILE), 0) + sb * rows
        hit = jnp.where(j == pos1, 1.0, 0.0) + jnp.where(j == pos2, 1.0, 0.0)
        d = jnp.dot(hit.astype(BF16), digits, preferred_element_type=F32)
        order_ref[sb * rows:(sb + 1) * rows, :] = (d[:, 0:1] + 256.0 * d[:, 1:2]).astype(jnp.int32)


def _moe(hp, logits, b_router, x1, mod3, layer, n_ctx_tiles, tiles_per_lat_seq, ctx_row,
         wg, wu, wd):
    n = hp.shape[0]
    order, pos0, pos1, off, count, wcol = _route_tables(logits, b_router)
    chunks_per_tile = MOE_TILE // MOE_OUT_TILE
    n_steps = N_EXPERTS + chunks_per_tile

    def chunk_ix(t, s):
        return t * chunks_per_tile + jnp.maximum(s - N_EXPERTS, 0)

    def mod_row(g):
        return jnp.where(g < n_ctx_tiles, ctx_row, (g - n_ctx_tiles) // tiles_per_lat_seq)

    wspec = lambda r, c: pl.BlockSpec(
        (None, None, r, c), lambda t, s, *_: (layer, jnp.minimum(s, N_EXPERTS - 1), 0, 0))
    chunk_spec = pl.BlockSpec((MOE_OUT_TILE, D_MODEL), lambda t, s, *_: (chunk_ix(t, s), 0))
    grid_spec = pltpu.PrefetchScalarGridSpec(
        num_scalar_prefetch=5,
        grid=(n // MOE_TILE, n_steps),
        in_specs=[
            pl.BlockSpec((MOE_TILE, D_MODEL), lambda t, s, *_: (t, 0)),
            wspec(D_MODEL, D_EXPERT), wspec(D_MODEL, D_EXPERT), wspec(D_EXPERT, D_MODEL),
            chunk_spec,
            pl.BlockSpec((None, 1, D_MODEL), lambda t, s, *_: (mod_row(chunk_ix(t, s)), 0, 5)),
            pl.BlockSpec((MOE_OUT_TILE, 8), lambda t, s, *_: (chunk_ix(t, s), 0)),
        ],
        out_specs=(
            pl.BlockSpec((MOE_OUT_TILE, D_MODEL),
                         lambda t, s, *_: (jnp.minimum(chunk_ix(t, s), n_ctx_tiles - 1), 0)),
            pl.BlockSpec((MOE_OUT_TILE, D_MODEL),
                         lambda t, s, *_: (jnp.maximum(chunk_ix(t, s) - n_ctx_tiles, 0), 0))),
        scratch_shapes=[pltpu.VMEM((MOE_SLOTS, D_MODEL), F32),
                        pltpu.VMEM((3, MOE_CHUNK // 8, 8, D_MODEL), F32),
                        pltpu.VMEM((MOE_OUT_TILE // 8, 8, D_MODEL), F32),
                        pltpu.VMEM((MOE_OUT_TILE // 8, 8, D_MODEL), F32)],
    )
    n_ctx = n_ctx_tiles * MOE_OUT_TILE
    return pl.pallas_call(
        functools.partial(_moe_kernel, n_ctx_tiles=n_ctx_tiles), grid_spec=grid_spec,
        out_shape=(jax.ShapeDtypeStruct((n_ctx, D_MODEL), F32),
                   jax.ShapeDtypeStruct((n - n_ctx, D_MODEL), F32)),
        compiler_params=_params(("arbitrary", "arbitrary")), name="experts",
    )(order, pos0, pos1, off, count, hp, wg, wu, wd, x1, mod3, wcol)


def _rope_tables(seq_len):
    half = HEAD_DIM // 2
    freqs = ROPE_BASE ** (-np.arange(0, half, 2, dtype=np.float64) / half)
    pos = np.arange(seq_len)
    row, col = pos // GRID_W, pos % GRID_W
    d = np.arange(HEAD_DIM)
    position = np.where(d[None, :] < half, row[:, None], col[:, None]).astype(np.float64)
    ang = (position.astype(np.float32) * freqs.astype(np.float32)[d % (half // 2)][None, :]).astype(np.float32)
    cos, sin = np.cos(ang), np.sin(ang)
    first = (d % half) < half // 2
    sa = np.where(first[None, :], -sin, 0.0)
    sb = np.where(first[None, :], 0.0, sin)
    ident = lambda v: np.full((TOKEN_TILE, HEAD_DIM), v, np.float32)
    stack = lambda ctx, lat: jnp.asarray(
        np.tile(np.concatenate([ctx, lat.astype(np.float32)], axis=0), (1, 2)), F32)
    return stack(ident(1.0), cos), stack(ident(0.0), sa), stack(ident(0.0), sb)


def kernel(x_prompt, x_sample, c, cache_k, cache_v, state_C, state_n, state_m, c_ctx, w_mod, b_mod,
           g_mix, g_ffn, w_in, b_igate, b_fgate, g_q, g_k, g_mh, w_out, w_router, b_router,
           w_e_gate, w_e_up, w_e_down):
    n_ctx_seq, ctx_len, _ = x_prompt.shape
    n_lat_seq, lat_len, _ = x_sample.shape
    n_layers = w_mod.shape[0]
    n_ctx = n_ctx_seq * ctx_len
    assert ctx_len == SEQ_BLOCK and lat_len % TOKEN_TILE == 0
    assert n_ctx % MOE_TILE == 0 and (n_lat_seq * lat_len) % MOE_TILE == 0
    assert n_lat_seq < 16 and n_ctx % lat_len == 0
    n_ctx_tiles = n_ctx // TOKEN_TILE
    tiles_per_lat_seq = lat_len // TOKEN_TILE
    ctx_row = n_lat_seq

    x = (x_prompt.reshape(n_ctx, D_MODEL), x_sample.reshape(-1, D_MODEL))
    cond = jnp.zeros((16, D_MODEL), F32).at[:n_lat_seq].set(c).at[ctx_row].set(c_ctx)
    mod = _modulation(cond, w_mod, b_mod)
    rope = _rope_tables(lat_len)

    wg_b, wu_b, wd_b = w_e_gate.astype(BF16), w_e_up.astype(BF16), w_e_down.astype(BF16)
    gate_perm = np.array([(q % 2) * N_ML_HEADS + hd + 2 * N_ML_HEADS * (q // 2)
                          for hd in range(N_ML_HEADS) for q in range(4)])

    ks, vs, cs, ns, ms = [], [], [], [], []
    for l in range(n_layers):
        mod3 = mod[l].reshape(16, 1, -1)
        precise_ctx = l < n_layers - 1

        def weight_pair(w):
            hi = w.astype(BF16)
            return hi, ((w - hi.astype(F32)).astype(BF16) if precise_ctx else None)

        w_main = weight_pair(w_in[l, :, :MAIN_WIDTH])
        w_gate = w_in[l, :, MAIN_WIDTH:][:, gate_perm]
        b_gate = jnp.concatenate([b_igate[l].reshape(-1), b_fgate[l].reshape(-1)])[gate_perm]
        q, k, v, mq, mk, mv, og, gcol, grow = _inproj(
            *x, mod3, n_ctx_tiles, tiles_per_lat_seq, ctx_row, g_mix[l], w_main, w_gate, b_gate,
            g_q[l], g_k[l], rope)

        att_ctx = _attention(q[0], k, v, 0, n_ctx_seq, ctx_len)
        past = cache_k.shape[2]
        ck = cache_k[:, l].reshape(n_lat_seq * past, LANES)
        cv = cache_v[:, l].reshape(n_lat_seq * past, LANES)
        att_lat = _attention(q[1], k, v, n_ctx, n_lat_seq, lat_len, cache=(ck, cv))

        ml_ctx, c_fin, nm_fin = _mlstm(mq[0], mk[0], mv[0], og, g_mh[l], gcol, grow, 0,
                                       n_ctx_seq, ctx_len, emit_state=True)
        n0 = state_n[:, l].transpose(0, 2, 1, 3)
        m0 = jnp.broadcast_to(state_m[:, l].transpose(0, 2, 1)[..., None], n0.shape)
        nm0 = jnp.concatenate([n0, m0, jnp.zeros_like(n0), jnp.zeros_like(n0)], axis=2)
        (ml_lat,) = _mlstm(mq[1], mk[1], mv[1], og, g_mh[l], gcol, grow, n_ctx, n_lat_seq,
                           lat_len, init=(state_C, nm0, l))

        x1, hp, logits = _outproj(
            (att_ctx, att_lat), (ml_ctx, ml_lat), x, mod3, n_ctx_tiles, tiles_per_lat_seq, ctx_row,
            weight_pair(w_out[l]), g_ffn[l], w_router)
        x = _moe(hp, logits, b_router, x1, mod3, l, n_ctx // MOE_OUT_TILE,
                 lat_len // MOE_OUT_TILE, ctx_row, wg_b, wu_b, wd_b)

        ks.append(k[:n_ctx].reshape(n_ctx_seq, ctx_len, N_KV_HEADS, HEAD_DIM))
        vs.append(v[:n_ctx].reshape(n_ctx_seq, ctx_len, N_KV_HEADS, HEAD_DIM))
        cs.append(c_fin)
        ns.append(nm_fin[:, :, 0:2, :].transpose(0, 2, 1, 3))
        ms.append(nm_fin[:, :, 2:4, 0].transpose(0, 2, 1))

    y_prompt = x[0].reshape(x_prompt.shape)
    y_sample = x[1].reshape(x_sample.shape)
    return (y_prompt, y_sample, jnp.stack(ks, axis=1), jnp.stack(vs, axis=1),
            jnp.stack(cs, axis=1), jnp.stack(ns, axis=1), jnp.stack(ms, axis=1))
```

```python
import functools

import numpy as np
import jax
import jax.numpy as jnp
from jax import lax
from jax.experimental import pallas as pl
from jax.experimental.pallas import tpu as pltpu

F32 = jnp.float32
BF16 = jnp.bfloat16

D_MODEL = 1024
HEAD_DIM = 64
ATT_WIDTH = 512
N_KV_HEADS = 2
ML_WIDTH = 512
N_ML_HEADS = 4
ML_HEAD_DIM = 128
GRID_W = 64
ROPE_BASE = 10000.0
N_EXPERTS = 16
N_GROUPS = 4
GROUP_SIZE = 4
D_EXPERT = 512
EPS = 1e-6
MAIN_WIDTH = 2816
N_GATE_COLS = 16
LANES = 128
TOKEN_TILE = 512
SEQ_BLOCK = 256
MOE_TILE = 2048
MOE_CHUNK = 320
MLSTM_HEADS_PER_STEP = 4
MOE_OUT_TILE = 256
ROUTE_SLOT_BLOCKS = 9
MOE_SLOTS = -(-(2 * MOE_TILE + 8 * N_EXPERTS + MOE_CHUNK) // (512 * ROUTE_SLOT_BLOCKS)) * 512 * ROUTE_SLOT_BLOCKS
VMEM_LIMIT = 56 * 1024 * 1024
NEG_INF = float("-inf")
HIGHEST = lax.Precision.HIGHEST
NT_DIMS = (((1,), (1,)), ((), ()))
TN_DIMS = (((0,), (0,)), ((), ()))


def _params(semantics):
    return pltpu.CompilerParams(dimension_semantics=semantics, vmem_limit_bytes=VMEM_LIMIT)


def _log_sigmoid(z):
    return jnp.minimum(z, 0.0) - jnp.log1p(jnp.exp(-jnp.abs(z)))


def _split3(x):
    h1 = x.astype(BF16)
    r1 = x - h1.astype(F32)
    h2 = r1.astype(BF16)
    h3 = (r1 - h2.astype(F32)).astype(BF16)
    return h1, h2, h3


def _mod_kernel(cond_ref, w_ref, b_ref, o_ref):
    c = cond_ref[...]
    s = c * jax.nn.sigmoid(c)
    o_ref[...] = _dot_x3(_hi_lo(s), _hi_lo(w_ref[...])) + b_ref[...]


def _modulation(cond, w_mod, b_mod):
    n_layers = w_mod.shape[0]
    n_chunks = w_mod.shape[2] // D_MODEL
    return pl.pallas_call(
        _mod_kernel,
        grid=(n_layers, n_chunks),
        in_specs=[
            pl.BlockSpec((16, D_MODEL), lambda l, j: (0, 0)),
            pl.BlockSpec((None, D_MODEL, D_MODEL), lambda l, j: (l, 0, j)),
            pl.BlockSpec((None, 1, D_MODEL), lambda l, j: (l, 0, j)),
        ],
        out_specs=pl.BlockSpec((None, 16, D_MODEL), lambda l, j: (l, 0, j)),
        out_shape=jax.ShapeDtypeStruct((n_layers, 16, w_mod.shape[2]), F32),
        compiler_params=_params(("parallel", "parallel")),
        name="modulation",
    )(cond, w_mod, b_mod.reshape(n_layers, 1, -1))


def _two_part(i, n_first, a_ref, b_ref):
    return jnp.where(i < n_first, a_ref[...], b_ref[...])


def _hi_lo(x):
    hi = x.astype(BF16)
    return hi, (x - hi.astype(F32)).astype(BF16)


def _dot_x3(a, b, dims=None):
    (ah, al), (bh, bl) = a, b
    if dims is None:
        d = lambda x, y: jnp.dot(x, y, preferred_element_type=F32)
    else:
        d = lambda x, y: lax.dot_general(x, y, dims, preferred_element_type=F32)
    return d(ah, bh) + (d(al, bh) + d(ah, bl))


def _two_part_out_specs(width, n_first, tile=TOKEN_TILE):
    return _two_part_specs(width, n_first, tile)


def _two_part_specs(width, n_first, tile=TOKEN_TILE):
    return [pl.BlockSpec((tile, width), lambda i: (jnp.minimum(i, n_first - 1), 0)),
            pl.BlockSpec((tile, width), lambda i: (jnp.maximum(i - n_first, 0), 0))]


def _inproj_kernel(*refs, n_ctx_tiles, precise_ctx):
    it = iter(refs)
    xa_ref, xb_ref, sh_ref, sc_ref, g_ref, w_ref = [next(it) for _ in range(6)]
    wl_ref = next(it) if precise_ctx else None
    (wgt_ref, brow_ref, gq_ref, gk_ref, eye_ref, cos_ref, sa_ref, sb_ref, gsum_ref,
     qa_ref, qb_ref, k_ref, v_ref, mqa_ref, mqb_ref, mka_ref, mkb_ref, mva_ref, mvb_ref,
     og_ref, gc_ref, gr_ref) = it
    tile = pl.program_id(0)
    x = _two_part(tile, n_ctx_tiles, xa_ref, xb_ref)
    ms = jnp.mean(x * x, axis=-1, keepdims=True)
    h = x * lax.rsqrt(ms + EPS) * g_ref[...]
    h = h * (1.0 + sc_ref[...]) + sh_ref[...]
    hb = h.astype(BF16)

    zr = lax.dot_general(wgt_ref[...], h, NT_DIMS, precision=HIGHEST,
                         preferred_element_type=F32) + brow_ref[...]
    sub = lax.broadcasted_iota(jnp.int32, zr.shape, 0)
    gr = jnp.where(sub % 4 < 2, zr, _log_sigmoid(zr))
    eye = eye_ref[...]
    gc = functools.reduce(jnp.add, [lax.dot_general(eye, p, NT_DIMS, preferred_element_type=F32)
                                    for p in _split3(gr)])
    gr_ref[...] = jnp.zeros_like(gr_ref)
    for hd in range(N_ML_HEADS):
        gc_ref[hd] = gc[:, 4 * hd:4 * hd + 4]
        gr_ref[hd, 0:4, :] = gr[4 * hd:4 * hd + 4, :]

    cos = cos_ref[...]
    sa = sa_ref[...]
    sb = sb_ref[...]
    gsum = gsum_ref[...]

    def project(precise, q_ref, mq_ref, mk_ref, mv_ref):
        if precise:
            h_pair = _hi_lo(h)
            proj = lambda c0, width: _dot_x3(
                h_pair, (w_ref[:, c0:c0 + width], wl_ref[:, c0:c0 + width]))
        else:
            proj = lambda c0, width: jnp.dot(hb, w_ref[:, c0:c0 + width],
                                             preferred_element_type=F32)
        act = F32 if precise else BF16

        n_qk = ATT_WIDTH // LANES + 1
        t_rows = hb.shape[0]
        zqk = proj(0, n_qk * LANES)
        zs = [zqk[:, c * LANES:(c + 1) * LANES] for c in range(n_qk)]
        sq = jnp.concatenate([z * z for z in zs], axis=0)
        ss = jnp.dot(jnp.concatenate(_hi_lo(sq), axis=0), gsum, preferred_element_type=F32)
        ss = ss[:n_qk * t_rows] + ss[n_qk * t_rows:]

        def headnorm_rope(c, gain):
            zn = zs[c] * lax.rsqrt(ss[c * t_rows:(c + 1) * t_rows] * (1.0 / HEAD_DIM) + EPS) * gain
            return zn * cos + pltpu.roll(zn, LANES - 16, 1) * sa + pltpu.roll(zn, 16, 1) * sb

        for c in range(n_qk - 1):
            q_ref[:, c * LANES:(c + 1) * LANES] = (headnorm_rope(c, gq_ref[...]) * 0.125).astype(act)
        k_ref[...] = headnorm_rope(n_qk - 1, gk_ref[...])
        v_ref[...] = proj(640, LANES)
        mq_ref[...] = proj(768, ML_WIDTH).astype(act)
        mk_ref[...] = (proj(1280, ML_WIDTH) * (ML_HEAD_DIM ** -0.5)).astype(act)
        mv_ref[...] = proj(1792, ML_WIDTH).astype(act)
        og_ref[...] = jax.nn.sigmoid(proj(2304, ML_WIDTH))

    @pl.when(tile < n_ctx_tiles)
    def _ctx():
        project(precise_ctx, qa_ref, mqa_ref, mka_ref, mva_ref)

    @pl.when(tile >= n_ctx_tiles)
    def _lat():
        project(False, qb_ref, mqb_ref, mkb_ref, mvb_ref)


def _inproj(xa, xb, mod3, n_ctx_tiles, tiles_per_lat_seq, ctx_row, g_mix, w_main, w_gate, b_gate,
            g_q, g_k, rope):
    w_hi, w_lo = w_main
    precise_ctx = w_lo is not None
    n_ctx, n_lat = xa.shape[0], xb.shape[0]
    n = n_ctx + n_lat
    n_tiles = n // TOKEN_TILE

    def mod_row(i):
        return jnp.where(i < n_ctx_tiles, ctx_row, (i - n_ctx_tiles) // tiles_per_lat_seq)

    def rope_blk(i):
        return jnp.where(i < n_ctx_tiles, 0, 1 + (i - n_ctx_tiles) % tiles_per_lat_seq)

    cos_t, sa_t, sb_t = rope
    lane = np.arange(LANES)
    gsum = jnp.asarray((lane[:, None] // HEAD_DIM) == (lane[None, :] // HEAD_DIM), BF16)
    tok = lambda w: pl.BlockSpec((TOKEN_TILE, w), lambda i: (i, 0))
    full = lambda a: pl.BlockSpec(a.shape, lambda i: (0,) * a.ndim)
    modspec = lambda j: pl.BlockSpec((None, 1, D_MODEL), lambda i: (mod_row(i), 0, j))
    ropespec = pl.BlockSpec((TOKEN_TILE, LANES), lambda i: (rope_blk(i), 0))
    consts = (g_mix.reshape(1, -1), w_hi) + ((w_lo,) if precise_ctx else ()) + (
        w_gate.T, b_gate.reshape(-1, 1),
        jnp.tile(g_q, 2).reshape(1, -1), jnp.tile(g_k, 2).reshape(1, -1),
        jnp.asarray(np.eye(TOKEN_TILE), BF16))
    args = (xa, xb, mod3, mod3) + consts + (cos_t, sa_t, sb_t, gsum)
    in_specs = _two_part_specs(D_MODEL, n_ctx_tiles) + [modspec(0), modspec(1)] \
        + [full(a) for a in consts] + [ropespec, ropespec, ropespec, full(gsum)]
    ctx_act = F32 if precise_ctx else BF16
    pair_shape = lambda w: [jax.ShapeDtypeStruct((n_ctx, w), ctx_act),
                            jax.ShapeDtypeStruct((n_lat, w), BF16)]
    pair_spec = lambda w: _two_part_out_specs(w, n_ctx_tiles)
    out_shape = (
        pair_shape(ATT_WIDTH)
        + [jax.ShapeDtypeStruct((n, LANES), F32),
           jax.ShapeDtypeStruct((n, LANES), F32)]
        + pair_shape(ML_WIDTH) + pair_shape(ML_WIDTH) + pair_shape(ML_WIDTH)
        + [jax.ShapeDtypeStruct((n, ML_WIDTH), F32),
           jax.ShapeDtypeStruct((N_ML_HEADS, n, 4), F32),
           jax.ShapeDtypeStruct((N_ML_HEADS, 8, n), F32)])
    out_specs = (pair_spec(ATT_WIDTH) + [tok(LANES), tok(LANES)]
                 + pair_spec(ML_WIDTH) + pair_spec(ML_WIDTH) + pair_spec(ML_WIDTH)
                 + [tok(ML_WIDTH),
                    pl.BlockSpec((N_ML_HEADS, TOKEN_TILE, 4), lambda i: (0, i, 0)),
                    pl.BlockSpec((N_ML_HEADS, 8, TOKEN_TILE), lambda i: (0, 0, i))])
    outs = pl.pallas_call(
        functools.partial(_inproj_kernel, n_ctx_tiles=n_ctx_tiles, precise_ctx=precise_ctx),
        grid=(n_tiles,), in_specs=in_specs, out_specs=tuple(out_specs),
        out_shape=tuple(out_shape), compiler_params=_params(("arbitrary",)), name="inproj",
    )(*args)
    qa, qb, k, v, mqa, mqb, mka, mkb, mva, mvb, og, gcol, grow = outs
    return (qa, qb), k, v, (mqa, mqb), (mka, mkb), (mva, mvb), og, gcol, grow


def _attn_kernel(*refs, n_kv, precise):
    q_ref = refs[0]
    kv_refs = refs[1:1 + 2 * n_kv]
    o_ref = refs[-1]
    tq = q_ref.shape[0]
    lo_q = lax.broadcasted_iota(jnp.int32, (tq, LANES), 1) < HEAD_DIM
    operand = _hi_lo if precise else (lambda a: a.astype(BF16))
    if precise:
        qk = lambda a, b: _dot_x3(a, b, NT_DIMS)
        pv = _dot_x3
    else:
        qk = lambda a, b: lax.dot_general(a, b, NT_DIMS, preferred_element_type=F32)
        pv = lambda a, b: jnp.dot(a, b, preferred_element_type=F32)

    def dup_half(ref, g):
        a = ref[...]
        r = pltpu.roll(a, HEAD_DIM, 1)
        lo = lax.broadcasted_iota(jnp.int32, a.shape, 1) < HEAD_DIM
        return operand(jnp.where(lo, a, r) if g == 0 else jnp.where(lo, r, a))

    for g in range(N_KV_HEADS):
        ks = [dup_half(kv_refs[2 * p], g) for p in range(n_kv)]
        vs = [dup_half(kv_refs[2 * p + 1], g) for p in range(n_kv)]
        for hb in range(2):
            c0 = (2 * g + hb) * LANES
            qb = q_ref[:, c0:c0 + LANES]
            outs = []
            for half in range(2):
                keep = lo_q if half == 0 else jnp.logical_not(lo_q)
                qm = operand(jnp.where(keep, qb, jnp.zeros_like(qb)))
                ss = [qk(qm, kd) for kd in ks]
                m = functools.reduce(jnp.maximum, [jnp.max(s, axis=1, keepdims=True) for s in ss])
                ps = [jnp.exp(s - m) for s in ss]
                den = functools.reduce(jnp.add, [jnp.sum(p, axis=1, keepdims=True) for p in ps])
                o = functools.reduce(jnp.add, [pv(operand(p), vd) for p, vd in zip(ps, vs)])
                outs.append(o / den)
            o_ref[:, c0:c0 + LANES] = jnp.where(lo_q, outs[0], outs[1]).astype(o_ref.dtype)


def _attention(q, k, v, kv_row0, n_seq, seq_len, cache=None):
    precise = q.dtype == F32
    nq = seq_len // SEQ_BLOCK
    sb0 = kv_row0 // seq_len
    in_specs = [
        pl.BlockSpec((SEQ_BLOCK, ATT_WIDTH), lambda b, i: (b * nq + i, 0)),
        pl.BlockSpec((seq_len, LANES), lambda b, i: (sb0 + b, 0)),
        pl.BlockSpec((seq_len, LANES), lambda b, i: (sb0 + b, 0)),
    ]
    args = [q, k, v]
    n_kv = 1
    if cache is not None:
        ck, cv = cache
        past = ck.shape[0] // n_seq
        in_specs += [pl.BlockSpec((past, LANES), lambda b, i: (b, 0))] * 2
        args += [ck, cv]
        n_kv = 2
    return pl.pallas_call(
        functools.partial(_attn_kernel, n_kv=n_kv, precise=precise),
        grid=(n_seq, nq), in_specs=in_specs,
        out_specs=pl.BlockSpec((SEQ_BLOCK, ATT_WIDTH), lambda b, i: (b * nq + i, 0)),
        out_shape=jax.ShapeDtypeStruct((n_seq * seq_len, ATT_WIDTH), q.dtype),
        compiler_params=_params(("parallel", "parallel")), name="attention",
    )(*args)


def _mlstm_kernel(*refs, heads, **static):
    it = iter(refs)
    q_ref, k_ref, v_ref, og_ref, gmh_ref, gcol_ref, grow_ref, u_ref, l_ref = [next(it) for _ in range(9)]
    init_refs = (next(it), next(it)) if static["has_init"] else ()
    ml_ref = next(it)
    state_refs = (next(it), next(it)) if static["emit_state"] else ()
    for hd in range(heads):
        lanes = pl.ds(hd * ML_HEAD_DIM, ML_HEAD_DIM)
        head_refs = [q_ref.at[:, lanes], k_ref.at[:, lanes], v_ref.at[:, lanes], og_ref.at[:, lanes],
                     gmh_ref.at[:, lanes], gcol_ref.at[hd], grow_ref.at[hd], u_ref, l_ref]
        if init_refs:
            head_refs += [init_refs[0].at[:, hd], init_refs[1].at[hd]]
        head_refs.append(ml_ref.at[:, lanes])
        if state_refs:
            head_refs += [state_refs[0].at[:, hd], state_refs[1].at[hd]]
        _mlstm_head(*head_refs, **static)


def _mlstm_head(*refs, seq_len, has_init, emit_state, precise):
    it = iter(refs)
    q_ref, k_ref, v_ref, og_ref, gmh_ref, gcol_ref, grow_ref, u_ref, l_ref = [next(it) for _ in range(9)]
    if has_init:
        c0_ref, nm0_ref = next(it), next(it)
    ml_ref = next(it)
    if emit_state:
        cf_ref, nmf_ref = next(it), next(it)

    operand = _hi_lo if precise else (lambda a: a.astype(BF16))
    if precise:
        qk = lambda a, b: _dot_x3(a, b, NT_DIMS)
        pv = _dot_x3
    else:
        qk = lambda a, b: lax.dot_general(a, b, NT_DIMS, preferred_element_type=F32)
        pv = lambda a, b: jnp.dot(a, b, preferred_element_type=F32)

    bq = SEQ_BLOCK
    nb = seq_len // bq
    blk = lambda j: slice(j * bq, (j + 1) * bq)
    upper_incl = u_ref[...]
    lower_incl = l_ref[...]

    def tri_dot(x, tri):
        return functools.reduce(jnp.add, [jnp.dot(p, tri, preferred_element_type=F32)
                                          for p in _split3(x)])

    ig_row = [[None] * nb for _ in range(2)]
    lf_row = [[None] * nb for _ in range(2)]
    within = [[None] * nb for _ in range(2)]
    bsum = [[None] * nb for _ in range(2)]
    for j in range(nb):
        g8 = grow_ref[:, blk(j)]
        cum_f = tri_dot(g8, upper_incl)
        cum_b = tri_dot(g8, lower_incl)
        for d in range(2):
            ig_row[d][j] = g8[d:d + 1, :]
            lf_row[d][j] = g8[2 + d:3 + d, :]
            within[d][j] = (cum_f if d == 0 else cum_b)[2 + d:3 + d, :]
            bsum[d][j] = jnp.sum(lf_row[d][j], axis=1, keepdims=True)
    zero11 = jnp.zeros((1, 1), F32)
    r_i = lax.broadcasted_iota(jnp.int32, (bq, bq), 0)
    c_i = lax.broadcasted_iota(jnp.int32, (bq, bq), 1)
    causal = [c_i <= r_i, c_i >= r_i]
    gmh = gmh_ref[...]

    q_blocks = [q_ref[blk(i), :] for i in range(nb)]
    scores = [qk(operand(q_blocks[i]), operand(k_ref[blk(i), :])) for i in range(nb)]

    h_dir = [[None] * nb for _ in range(2)]
    final = [None, None]
    for d in range(2):
        if has_init:
            state = (c0_ref[d], nm0_ref[d:d + 1, :])
            m = nm0_ref[2 + d:3 + d, 0:1]
        else:
            state, m = None, zero11
        scan = range(nb) if d == 0 else range(nb - 1, -1, -1)
        for step, i in enumerate(scan):
            q_i = q_blocks[i]
            a_loc = ig_row[d][i] - within[d][i]
            b_loc = jnp.sum(jnp.where(causal[d], lf_row[d][i], 0.0), axis=1, keepdims=True)
            m_col = jnp.maximum(m, jnp.max(jnp.where(causal[d], a_loc, NEG_INF),
                                           axis=1, keepdims=True))
            p = jnp.exp(jnp.where(causal[d], a_loc - m_col, NEG_INF)) * scores[i]
            den = jnp.sum(p, axis=1, keepdims=True)
            num = pv(operand(p), operand(v_ref[blk(i), :]))
            if state is not None:
                c_prev, n_prev = state
                w_inter = jnp.exp(m - m_col)
                qc = jnp.dot(q_i.astype(BF16), c_prev.astype(BF16), preferred_element_type=F32)
                qn = jnp.sum(q_i.astype(F32) * n_prev, axis=1, keepdims=True)
                num = num + w_inter * qc
                den = den + w_inter * qn
            nrm = jnp.maximum(jnp.abs(den), jnp.exp(-(b_loc + m_col)))
            h_dir[d][i] = num / nrm
            if step == nb - 1 and not emit_state:
                break
            m_last = jnp.maximum(m, jnp.max(a_loc, axis=1, keepdims=True))
            a_col = gcol_ref[blk(i), d:d + 1] - b_loc
            kw = k_ref[blk(i), :].astype(F32) * jnp.exp(a_col - m_last)
            c_new = lax.dot_general(kw.astype(BF16), v_ref[blk(i), :].astype(BF16), TN_DIMS,
                                    preferred_element_type=F32)
            n_new = jnp.sum(kw, axis=0, keepdims=True)
            if state is not None:
                decay = jnp.exp(m - m_last)
                c_new = c_new + decay * state[0]
                n_new = n_new + decay * state[1]
            state = (c_new, n_new)
            m = bsum[d][i] + m_last
        final[d] = (state, m)

    for i in range(nb):
        h = h_dir[0][i] + h_dir[1][i]
        hn = h * lax.rsqrt(jnp.mean(h * h, axis=-1, keepdims=True) + EPS) * gmh
        ml_ref[blk(i), :] = (og_ref[blk(i), :] * hn).astype(ml_ref.dtype)

    if emit_state:
        nmf_ref[...] = jnp.zeros_like(nmf_ref)
        for d in range(2):
            (c_fin, n_fin), m_fin = final[d]
            cf_ref[d] = c_fin
            nmf_ref[d:d + 1, :] = n_fin
            nmf_ref[2 + d:3 + d, :] = jnp.broadcast_to(m_fin, (1, ML_HEAD_DIM))


def _mlstm(mq, mk, mv, og, g_mh, gcol, grow, row0, n_seq, seq_len, init=None, emit_state=False):
    precise = mq.dtype == F32
    assert not (precise and init is not None)
    sb0 = row0 // seq_len
    tri = np.arange(SEQ_BLOCK)
    upper_incl = jnp.asarray(tri[:, None] <= tri[None, :], BF16)
    lower_incl = jnp.asarray(tri[:, None] >= tri[None, :], BF16)
    heads = MLSTM_HEADS_PER_STEP
    width = heads * ML_HEAD_DIM
    ownblk = lambda: pl.BlockSpec((seq_len, width), lambda b, h: (b, h))
    headblk = lambda: pl.BlockSpec((seq_len, width), lambda b, h: (sb0 + b, h))
    const = lambda a: pl.BlockSpec(a.shape, lambda b, h: (0,) * a.ndim)
    in_specs = [ownblk(), ownblk(), ownblk(), headblk(),
                pl.BlockSpec((1, width), lambda b, h: (0, h)),
                pl.BlockSpec((heads, seq_len, 4), lambda b, h: (h, sb0 + b, 0)),
                pl.BlockSpec((heads, 8, seq_len), lambda b, h: (h, 0, sb0 + b)),
                const(upper_incl), const(lower_incl)]
    args = [mq, mk, mv, og, g_mh.reshape(1, -1), gcol, grow, upper_incl, lower_incl]
    if init is not None:
        c0, nm0, layer = init
        in_specs += [
            pl.BlockSpec((None, None, 2, heads, ML_HEAD_DIM, ML_HEAD_DIM),
                         lambda b, h: (b, layer, 0, h, 0, 0)),
            pl.BlockSpec((None, heads, 8, ML_HEAD_DIM), lambda b, h: (b, h, 0, 0))]
        args += [c0, nm0]
    out_shape = [jax.ShapeDtypeStruct((n_seq * seq_len, ML_WIDTH), mq.dtype)]
    out_specs = [pl.BlockSpec((seq_len, width), lambda b, h: (b, h))]
    if emit_state:
        out_shape += [jax.ShapeDtypeStruct((n_seq, 2, N_ML_HEADS, ML_HEAD_DIM, ML_HEAD_DIM), F32),
                      jax.ShapeDtypeStruct((n_seq, N_ML_HEADS, 8, ML_HEAD_DIM), F32)]
        out_specs += [pl.BlockSpec((None, 2, heads, ML_HEAD_DIM, ML_HEAD_DIM),
                                   lambda b, h: (b, 0, h, 0, 0)),
                      pl.BlockSpec((None, heads, 8, ML_HEAD_DIM), lambda b, h: (b, h, 0, 0))]
    return pl.pallas_call(
        functools.partial(_mlstm_kernel, heads=heads, seq_len=seq_len, has_init=init is not None,
                          emit_state=emit_state, precise=precise),
        grid=(n_seq, N_ML_HEADS // heads), in_specs=in_specs, out_specs=tuple(out_specs),
        out_shape=tuple(out_shape),
        compiler_params=_params(("parallel", "parallel")), name="mlstm",
    )(*args)


def _outproj_kernel(*refs, n_ctx_tiles, precise_ctx):
    it = iter(refs)
    atta_ref, attb_ref, mla_ref, mlb_ref, xa_ref, xb_ref, w_ref = [next(it) for _ in range(7)]
    wl_ref = next(it) if precise_ctx else None
    gt_ref, sh_ref, sc_ref, g_ref, wrt_ref, x1_ref, hp_ref, logits_ref, y_scr = it
    i = pl.program_id(0)

    def mix(att, ml):
        return (jnp.dot(att, w_ref[:ATT_WIDTH, :], preferred_element_type=F32)
                + jnp.dot(ml, w_ref[ATT_WIDTH:, :], preferred_element_type=F32))

    @pl.when(i < n_ctx_tiles)
    def _ctx():
        if precise_ctx:
            y_scr[...] = (
                _dot_x3(_hi_lo(atta_ref[...]), (w_ref[:ATT_WIDTH, :], wl_ref[:ATT_WIDTH, :]))
                + _dot_x3(_hi_lo(mla_ref[...]), (w_ref[ATT_WIDTH:, :], wl_ref[ATT_WIDTH:, :])))
        else:
            y_scr[...] = mix(atta_ref[...], mla_ref[...])

    @pl.when(i >= n_ctx_tiles)
    def _lat():
        y_scr[...] = mix(attb_ref[...], mlb_ref[...])

    x1 = _two_part(i, n_ctx_tiles, xa_ref, xb_ref) + gt_ref[...] * y_scr[...]
    x1_ref[...] = x1
    ms = jnp.mean(x1 * x1, axis=-1, keepdims=True)
    h2 = x1 * lax.rsqrt(ms + EPS) * g_ref[...]
    h2 = h2 * (1.0 + sc_ref[...]) + sh_ref[...]
    hp_ref[...] = h2

    logits_ref[...] = lax.dot_general(wrt_ref[...], h2, NT_DIMS, precision=HIGHEST,
                                      preferred_element_type=F32)


def _select_experts(logits, b_col):
    ex = jnp.exp(logits - jnp.max(logits, axis=0, keepdims=True))
    scores = ex / jnp.sum(ex, axis=0, keepdims=True)
    sel = scores + b_col
    row = lambda a, e: a[e:e + 1, :]
    grp_score = []
    for g in range(N_GROUPS):
        xs = [row(sel, g * GROUP_SIZE + j) for j in range(GROUP_SIZE)]
        pairs = [xs[a] + xs[b] for a in range(GROUP_SIZE) for b in range(a + 1, GROUP_SIZE)]
        grp_score.append(functools.reduce(jnp.maximum, pairs))
    best = grp_score[0]
    grp = jnp.zeros_like(best, dtype=jnp.int32)
    for g in range(1, N_GROUPS):
        better = grp_score[g] > best
        grp = jnp.where(better, g, grp)
        best = jnp.where(better, grp_score[g], best)
    pick = lambda a, j: functools.reduce(
        lambda acc, g: jnp.where(grp == g, row(a, g * GROUP_SIZE + j), acc),
        range(1, N_GROUPS), row(a, j))
    xs = [pick(sel, j) for j in range(GROUP_SIZE)]
    ws = [pick(scores, j) for j in range(GROUP_SIZE)]

    def argmax4(vals):
        bv, bi = vals[0], jnp.zeros_like(grp)
        for j in range(1, GROUP_SIZE):
            better = vals[j] > bv
            bi = jnp.where(better, j, bi)
            bv = jnp.where(better, vals[j], bv)
        return bi

    i1 = argmax4(xs)
    i2 = argmax4([jnp.where(i1 == j, NEG_INF, xs[j]) for j in range(GROUP_SIZE)])
    take = lambda vals, idx: functools.reduce(
        lambda acc, j: jnp.where(idx == j, vals[j], acc), range(1, GROUP_SIZE), vals[0])
    w1, w2 = take(ws, i1), take(ws, i2)
    wsum = w1 + w2
    w1, w2 = w1 / wsum, w2 / wsum
    return grp * GROUP_SIZE + i1, grp * GROUP_SIZE + i2, w1, w2


def _outproj(att, ml, x, mod3, n_ctx_tiles, tiles_per_lat_seq, ctx_row, w_out, g_ffn, w_router):
    w_hi, w_lo = w_out
    precise_ctx = w_lo is not None
    n = x[0].shape[0] + x[1].shape[0]

    def mod_row(i):
        return jnp.where(i < n_ctx_tiles, ctx_row, (i - n_ctx_tiles) // tiles_per_lat_seq)

    tok = lambda w: pl.BlockSpec((TOKEN_TILE, w), lambda i: (i, 0))
    full = lambda a: pl.BlockSpec(a.shape, lambda i: (0,) * a.ndim)
    modspec = lambda j: pl.BlockSpec((None, 1, D_MODEL), lambda i: (mod_row(i), 0, j))
    weights = (w_hi, w_lo) if precise_ctx else (w_hi,)
    consts = (g_ffn.reshape(1, -1), w_router.T)
    args = (*att, *ml, *x, *weights, mod3, mod3, mod3) + consts
    in_specs = (_two_part_specs(ATT_WIDTH, n_ctx_tiles) + _two_part_specs(ML_WIDTH, n_ctx_tiles)
                + _two_part_specs(D_MODEL, n_ctx_tiles) + [full(w) for w in weights]
                + [modspec(2), modspec(3), modspec(4)] + [full(a) for a in consts])
    return pl.pallas_call(
        functools.partial(_outproj_kernel, n_ctx_tiles=n_ctx_tiles, precise_ctx=precise_ctx),
        grid=(n // TOKEN_TILE,), in_specs=in_specs,
        scratch_shapes=[pltpu.VMEM((TOKEN_TILE, D_MODEL), F32)],
        out_specs=(tok(D_MODEL), tok(D_MODEL),
                   pl.BlockSpec((N_EXPERTS, TOKEN_TILE), lambda i: (0, i))),
        out_shape=(jax.ShapeDtypeStruct((n, D_MODEL), F32),
                   jax.ShapeDtypeStruct((n, D_MODEL), F32),
                   jax.ShapeDtypeStruct((N_EXPERTS, n), F32)),
        compiler_params=_params(("parallel",)), name="outproj_router",
    )(*args)


def _moe_kernel(order_ref, pos0_ref, pos1_ref, off_ref, cnt_ref,
                h_ref, wg_ref, wu_ref, wd_ref, x1_ref, gt_ref, wcol_ref, ya_ref, yb_ref,
                o_scr, xs_scr, comb0_scr, comb1_scr, *, n_ctx_tiles):
    t = pl.program_id(0)
    s = pl.program_id(1)

    groups = MOE_CHUNK // 8

    def gather_rows(buf, slot0):
        for j in range(MOE_CHUNK):
            src = order_ref[slot0 + j]
            xs_scr[buf, j // 8, pl.ds(j % 8, 1), :] = h_ref[pl.ds(src, 1), :]

    def ffn(buf, base):
        xs = xs_scr[buf].reshape(MOE_CHUNK, D_MODEL).astype(BF16)
        a = jnp.dot(xs, wg_ref[...], preferred_element_type=F32)
        b = jnp.dot(xs, wu_ref[...], preferred_element_type=F32)
        hid = (a * jax.nn.sigmoid(a)) * b
        o_scr[pl.ds(base, MOE_CHUNK), :] = jnp.dot(hid.astype(BF16), wd_ref[...],
                                                   preferred_element_type=F32)

    @pl.when(s == 0)
    def _first():
        gather_rows(0, t * MOE_SLOTS + pl.multiple_of(off_ref[t * N_EXPERTS], 8))

    @pl.when(s < N_EXPERTS)
    def _expert():
        seg = t * N_EXPERTS + s
        off = pl.multiple_of(off_ref[seg], 8)
        cur = s % 2
        nxt_seg = t * N_EXPERTS + jnp.minimum(s + 1, N_EXPERTS - 1)
        ffn(cur, off)
        gather_rows(1 - cur, t * MOE_SLOTS + pl.multiple_of(off_ref[nxt_seg], 8))

        def extra(c, carry):
            base = pl.multiple_of(off + c * MOE_CHUNK, 8)

            def gather8(i, carry2):
                slot = t * MOE_SLOTS + base + i * 8
                for k in range(8):
                    src = order_ref[slot + k]
                    xs_scr[2, i, pl.ds(k, 1), :] = h_ref[pl.ds(src, 1), :]
                return carry2

            lax.fori_loop(0, groups, gather8, 0)
            ffn(2, base)
            return carry

        lax.fori_loop(1, (cnt_ref[seg] + MOE_CHUNK - 1) // MOE_CHUNK, extra, 0)

    @pl.when(s >= N_EXPERTS)
    def _combine():
        tok0 = t * MOE_TILE + (s - N_EXPERTS) * MOE_OUT_TILE

        def body8(i, carry):
            for k in range(8):
                p0 = pos0_ref[tok0 + i * 8 + k]
                p1 = pos1_ref[tok0 + i * 8 + k]
                comb0_scr[i, pl.ds(k, 1), :] = o_scr[pl.ds(p0, 1), :]
                comb1_scr[i, pl.ds(k, 1), :] = o_scr[pl.ds(p1, 1), :]
            return carry

        lax.fori_loop(0, MOE_OUT_TILE // 8, body8, 0)
        wcol = wcol_ref[...]
        comb = (wcol[:, 2:3] * comb0_scr[...].reshape(MOE_OUT_TILE, D_MODEL)
                + wcol[:, 3:4] * comb1_scr[...].reshape(MOE_OUT_TILE, D_MODEL))
        y = x1_ref[...] + gt_ref[...] * comb
        chunk_ix = t * (MOE_TILE // MOE_OUT_TILE) + s - N_EXPERTS

        @pl.when(chunk_ix < n_ctx_tiles)
        def _ctx():
            ya_ref[...] = y

        @pl.when(chunk_ix >= n_ctx_tiles)
        def _lat():
            yb_ref[...] = y


def _route_tables(logits, b_router):
    n = logits.shape[1]
    nt = n // MOE_TILE
    tri = np.arange(SEQ_BLOCK)
    strict_upper = jnp.asarray(tri[:, None] < tri[None, :], BF16)
    tok = np.arange(MOE_TILE)
    digits = np.zeros((MOE_TILE, LANES), np.float32)
    digits[:, 0] = tok % 256
    digits[:, 1] = tok // 256
    pos, order, meta, wcol = pl.pallas_call(
        _route_kernel, grid=(nt,),
        in_specs=[pl.BlockSpec((N_EXPERTS, MOE_TILE), lambda t: (0, t)),
                  pl.BlockSpec((N_EXPERTS, 1), lambda t: (0, 0)),
                  pl.BlockSpec((SEQ_BLOCK, SEQ_BLOCK), lambda t: (0, 0)),
                  pl.BlockSpec((SEQ_BLOCK, SEQ_BLOCK), lambda t: (0, 0)),
                  pl.BlockSpec((MOE_TILE, LANES), lambda t: (0, 0))],
        out_specs=(pl.BlockSpec((8, MOE_TILE), lambda t: (0, t)),
                   pl.BlockSpec((None, MOE_SLOTS, 1), lambda t: (t, 0, 0)),
                   pl.BlockSpec((None, N_EXPERTS, 8), lambda t: (t, 0, 0)),
                   pl.BlockSpec((MOE_TILE, 8), lambda t: (t, 0))),
        out_shape=(jax.ShapeDtypeStruct((8, n), jnp.int32),
                   jax.ShapeDtypeStruct((nt, MOE_SLOTS, 1), jnp.int32),
                   jax.ShapeDtypeStruct((nt, N_EXPERTS, 8), jnp.int32),
                   jax.ShapeDtypeStruct((n, 8), F32)),
        scratch_shapes=[pltpu.VMEM((8, MOE_TILE), F32)],
        compiler_params=_params(("parallel",)), name="route_tables",
    )(logits, b_router.reshape(-1, 1), strict_upper, jnp.asarray(np.eye(SEQ_BLOCK), BF16),
      jnp.asarray(digits, BF16))
    return (order.reshape(-1), pos[0], pos[1], meta[:, :, 0].reshape(-1),
            meta[:, :, 1].reshape(-1), wcol)


def _route_kernel(logits_ref, br_ref, su_ref, eye_ref, digits_ref,
                  pos_ref, order_ref, meta_ref, wcol_ref, wrow_scr):
    e1, e2, w1, w2 = _select_experts(logits_ref[...], br_ref[...])
    wrow_scr[...] = jnp.zeros_like(wrow_scr)
    wrow_scr[2:3, :] = w1
    wrow_scr[3:4, :] = w2
    eye = eye_ref[...]
    for b in range(MOE_TILE // SEQ_BLOCK):
        cols = slice(b * SEQ_BLOCK, (b + 1) * SEQ_BLOCK)
        wcol_ref[cols, :] = functools.reduce(jnp.add, [
            lax.dot_general(eye, p, NT_DIMS, preferred_element_type=F32)
            for p in _split3(wrow_scr[:, cols])])
    eid = lax.broadcasted_iota(jnp.int32, (N_EXPERTS, MOE_TILE), 0)
    oh1, oh2 = eid == e1, eid == e2
    oh = jnp.where(oh1, 1.0, 0.0) + jnp.where(oh2, 1.0, 0.0)
    nblk = MOE_TILE // SEQ_BLOCK
    blocks = [oh[:, b * SEQ_BLOCK:(b + 1) * SEQ_BLOCK] for b in range(nblk)]
    inner = jnp.dot(jnp.concatenate(blocks, axis=0).astype(BF16), su_ref[...],
                    preferred_element_type=F32)
    run = jnp.zeros((N_EXPERTS, 1), F32)
    ranks = []
    for b in range(nblk):
        ranks.append(inner[b * N_EXPERTS:(b + 1) * N_EXPERTS, :] + run)
        run = run + jnp.sum(blocks[b], axis=1, keepdims=True)
    count = run
    seg = jnp.floor((count + 7.0) * 0.125) * 8.0
    sub = lax.broadcasted_iota(jnp.int32, (N_EXPERTS, 1), 0)
    off = jnp.zeros((N_EXPERTS, 1), F32)
    for e in range(N_EXPERTS - 1):
        off = off + jnp.where(sub > e, seg[e:e + 1, :], 0.0)
    slot = jnp.concatenate(ranks, axis=1) + off
    pos1 = jnp.sum(jnp.where(oh1, slot, 0.0), axis=0, keepdims=True).astype(jnp.int32)
    pos2 = jnp.sum(jnp.where(oh2, slot, 0.0), axis=0, keepdims=True).astype(jnp.int32)
    pos_ref[...] = jnp.zeros_like(pos_ref)
    pos_ref[0:1, :] = pos1
    pos_ref[1:2, :] = pos2
    meta_ref[...] = jnp.zeros_like(meta_ref)
    meta_ref[:, 0:1] = off.astype(jnp.int32)
    meta_ref[:, 1:2] = count.astype(jnp.int32)
    digits = digits_ref[...]
    rows = MOE_SLOTS // ROUTE_SLOT_BLOCKS
    for sb in range(ROUTE_SLOT_BLOCKS):
        j = lax.broadcasted_iota(jnp.int32, (rows, MOE_TILE), 0) + sb * rows
        hit = jnp.where(j == pos1, 1.0, 0.0) + jnp.where(j == pos2, 1.0, 0.0)
        d = jnp.dot(hit.astype(BF16), digits, preferred_element_type=F32)
        order_ref[sb * rows:(sb + 1) * rows, :] = (d[:, 0:1] + 256.0 * d[:, 1:2]).astype(jnp.int32)


def _moe(hp, logits, b_router, x1, mod3, layer, n_ctx_tiles, tiles_per_lat_seq, ctx_row,
         wg, wu, wd):
    n = hp.shape[0]
    order, pos0, pos1, off, count, wcol = _route_tables(logits, b_router)
    chunks_per_tile = MOE_TILE // MOE_OUT_TILE
    n_steps = N_EXPERTS + chunks_per_tile

    def chunk_ix(t, s):
        return t * chunks_per_tile + jnp.maximum(s - N_EXPERTS, 0)

    def mod_row(g):
        return jnp.where(g < n_ctx_tiles, ctx_row, (g - n_ctx_tiles) // tiles_per_lat_seq)

    wspec = lambda r, c: pl.BlockSpec(
        (None, None, r, c), lambda t, s, *_: (layer, jnp.minimum(s, N_EXPERTS - 1), 0, 0))
    chunk_spec = pl.BlockSpec((MOE_OUT_TILE, D_MODEL), lambda t, s, *_: (chunk_ix(t, s), 0))
    grid_spec = pltpu.PrefetchScalarGridSpec(
        num_scalar_prefetch=5,
        grid=(n // MOE_TILE, n_steps),
        in_specs=[
            pl.BlockSpec((MOE_TILE, D_MODEL), lambda t, s, *_: (t, 0)),
            wspec(D_MODEL, D_EXPERT), wspec(D_MODEL, D_EXPERT), wspec(D_EXPERT, D_MODEL),
            chunk_spec,
            pl.BlockSpec((None, 1, D_MODEL), lambda t, s, *_: (mod_row(chunk_ix(t, s)), 0, 5)),
            pl.BlockSpec((MOE_OUT_TILE, 8), lambda t, s, *_: (chunk_ix(t, s), 0)),
        ],
        out_specs=(
            pl.BlockSpec((MOE_OUT_TILE, D_MODEL),
                         lambda t, s, *_: (jnp.minimum(chunk_ix(t, s), n_ctx_tiles - 1), 0)),
            pl.BlockSpec((MOE_OUT_TILE, D_MODEL),
                         lambda t, s, *_: (jnp.maximum(chunk_ix(t, s) - n_ctx_tiles, 0), 0))),
        scratch_shapes=[pltpu.VMEM((MOE_SLOTS, D_MODEL), F32),
                        pltpu.VMEM((3, MOE_CHUNK // 8, 8, D_MODEL), F32),
                        pltpu.VMEM((MOE_OUT_TILE // 8, 8, D_MODEL), F32),
                        pltpu.VMEM((MOE_OUT_TILE // 8, 8, D_MODEL), F32)],
    )
    n_ctx = n_ctx_tiles * MOE_OUT_TILE
    return pl.pallas_call(
        functools.partial(_moe_kernel, n_ctx_tiles=n_ctx_tiles), grid_spec=grid_spec,
        out_shape=(jax.ShapeDtypeStruct((n_ctx, D_MODEL), F32),
                   jax.ShapeDtypeStruct((n - n_ctx, D_MODEL), F32)),
        compiler_params=_params(("arbitrary", "arbitrary")), name="experts",
    )(order, pos0, pos1, off, count, hp, wg, wu, wd, x1, mod3, wcol)


def _rope_tables(seq_len):
    half = HEAD_DIM // 2
    freqs = ROPE_BASE ** (-np.arange(0, half, 2, dtype=np.float64) / half)
    pos = np.arange(seq_len)
    row, col = pos // GRID_W, pos % GRID_W
    d = np.arange(HEAD_DIM)
    position = np.where(d[None, :] < half, row[:, None], col[:, None]).astype(np.float64)
    ang = (position.astype(np.float32) * freqs.astype(np.float32)[d % (half // 2)][None, :]).astype(np.float32)
    cos, sin = np.cos(ang), np.sin(ang)
    first = (d % half) < half // 2
    sa = np.where(first[None, :], -sin, 0.0)
    sb = np.where(first[None, :], 0.0, sin)
    ident = lambda v: np.full((TOKEN_TILE, HEAD_DIM), v, np.float32)
    stack = lambda ctx, lat: jnp.asarray(
        np.tile(np.concatenate([ctx, lat.astype(np.float32)], axis=0), (1, 2)), F32)
    return stack(ident(1.0), cos), stack(ident(0.0), sa), stack(ident(0.0), sb)


def kernel(x_prompt, x_sample, c, cache_k, cache_v, state_C, state_n, state_m, c_ctx, w_mod, b_mod,
           g_mix, g_ffn, w_in, b_igate, b_fgate, g_q, g_k, g_mh, w_out, w_router, b_router,
           w_e_gate, w_e_up, w_e_down):
    n_ctx_seq, ctx_len, _ = x_prompt.shape
    n_lat_seq, lat_len, _ = x_sample.shape
    n_layers = w_mod.shape[0]
    n_ctx = n_ctx_seq * ctx_len
    assert ctx_len == SEQ_BLOCK and lat_len % TOKEN_TILE == 0
    assert n_ctx % MOE_TILE == 0 and (n_lat_seq * lat_len) % MOE_TILE == 0
    assert n_lat_seq < 16 and n_ctx % lat_len == 0
    n_ctx_tiles = n_ctx // TOKEN_TILE
    tiles_per_lat_seq = lat_len // TOKEN_TILE
    ctx_row = n_lat_seq

    x = (x_prompt.reshape(n_ctx, D_MODEL), x_sample.reshape(-1, D_MODEL))
    cond = jnp.zeros((16, D_MODEL), F32).at[:n_lat_seq].set(c).at[ctx_row].set(c_ctx)
    mod = _modulation(cond, w_mod, b_mod)
    rope = _rope_tables(lat_len)

    wg_b, wu_b, wd_b = w_e_gate.astype(BF16), w_e_up.astype(BF16), w_e_down.astype(BF16)
    gate_perm = np.array([(q % 2) * N_ML_HEADS + hd + 2 * N_ML_HEADS * (q // 2)
                          for hd in range(N_ML_HEADS) for q in range(4)])

    ks, vs, cs, ns, ms = [], [], [], [], []
    for l in range(n_layers):
        mod3 = mod[l].reshape(16, 1, -1)
        precise_ctx = l < n_layers - 1

        def weight_pair(w):
            hi = w.astype(BF16)
            return hi, ((w - hi.astype(F32)).astype(BF16) if precise_ctx else None)

        w_main = weight_pair(w_in[l, :, :MAIN_WIDTH])
        w_gate = w_in[l, :, MAIN_WIDTH:][:, gate_perm]
        b_gate = jnp.concatenate([b_igate[l].reshape(-1), b_fgate[l].reshape(-1)])[gate_perm]
        q, k, v, mq, mk, mv, og, gcol, grow = _inproj(
            *x, mod3, n_ctx_tiles, tiles_per_lat_seq, ctx_row, g_mix[l], w_main, w_gate, b_gate,
            g_q[l], g_k[l], rope)

        att_ctx = _attention(q[0], k, v, 0, n_ctx_seq, ctx_len)
        past = cache_k.shape[2]
        ck = cache_k[:, l].reshape(n_lat_seq * past, LANES)
        cv = cache_v[:, l].reshape(n_lat_seq * past, LANES)
        att_lat = _attention(q[1], k, v, n_ctx, n_lat_seq, lat_len, cache=(ck, cv))

        ml_ctx, c_fin, nm_fin = _mlstm(mq[0], mk[0], mv[0], og, g_mh[l], gcol, grow, 0,
                                       n_ctx_seq, ctx_len, emit_state=True)
        n0 = state_n[:, l].transpose(0, 2, 1, 3)
        m0 = jnp.broadcast_to(state_m[:, l].transpose(0, 2, 1)[..., None], n0.shape)
        nm0 = jnp.concatenate([n0, m0, jnp.zeros_like(n0), jnp.zeros_like(n0)], axis=2)
        (ml_lat,) = _mlstm(mq[1], mk[1], mv[1], og, g_mh[l], gcol, grow, n_ctx, n_lat_seq,
                           lat_len, init=(state_C, nm0, l))

        x1, hp, logits = _outproj(
            (att_ctx, att_lat), (ml_ctx, ml_lat), x, mod3, n_ctx_tiles, tiles_per_lat_seq, ctx_row,
            weight_pair(w_out[l]), g_ffn[l], w_router)
        x = _moe(hp, logits, b_router, x1, mod3, l, n_ctx // MOE_OUT_TILE,
                 lat_len // MOE_OUT_TILE, ctx_row, wg_b, wu_b, wd_b)

        ks.append(k[:n_ctx].reshape(n_ctx_seq, ctx_len, N_KV_HEADS, HEAD_DIM))
        vs.append(v[:n_ctx].reshape(n_ctx_seq, ctx_len, N_KV_HEADS, HEAD_DIM))
        cs.append(c_fin)
        ns.append(nm_fin[:, :, 0:2, :].transpose(0, 2, 1, 3))
        ms.append(nm_fin[:, :, 2:4, 0].transpose(0, 2, 1))

    y_prompt = x[0].reshape(x_prompt.shape)
    y_sample = x[1].reshape(x_sample.shape)
    return (y_prompt, y_sample, jnp.stack(ks, axis=1), jnp.stack(vs, axis=1),
            jnp.stack(cs, axis=1), jnp.stack(ns, axis=1), jnp.stack(ms, axis=1))
```

```python
import functools

import numpy as np
import jax
import jax.numpy as jnp
from jax import lax
from jax.experimental import pallas as pl
from jax.experimental.pallas import tpu as pltpu

F32 = jnp.float32
BF16 = jnp.bfloat16

D_MODEL = 1024
HEAD_DIM = 64
ATT_WIDTH = 512
N_KV_HEADS = 2
ML_WIDTH = 512
N_ML_HEADS = 4
ML_HEAD_DIM = 128
GRID_W = 64
ROPE_BASE = 10000.0
N_EXPERTS = 16
N_GROUPS = 4
GROUP_SIZE = 4
D_EXPERT = 512
EPS = 1e-6
MAIN_WIDTH = 2816
N_GATE_COLS = 16
LANES = 128
TOKEN_TILE = 512
SEQ_BLOCK = 256
MOE_TILE = 2048
MOE_CHUNK = 320
ROW_TILE_SUBLANES = D_MODEL // LANES
MOE_SCATTER_BATCH = 16
MLSTM_HEADS_PER_STEP = 4
MOE_OUT_TILE = 256
ROUTE_SLOT_BLOCKS = 9
MOE_SLOTS = -(-(2 * MOE_TILE + 8 * N_EXPERTS + MOE_CHUNK) // (512 * ROUTE_SLOT_BLOCKS)) * 512 * ROUTE_SLOT_BLOCKS
VMEM_LIMIT = 56 * 1024 * 1024
NEG_INF = float("-inf")
HIGHEST = lax.Precision.HIGHEST
NT_DIMS = (((1,), (1,)), ((), ()))
TN_DIMS = (((0,), (0,)), ((), ()))


def _params(semantics):
    return pltpu.CompilerParams(dimension_semantics=semantics, vmem_limit_bytes=VMEM_LIMIT)


def _log_sigmoid(z):
    return jnp.minimum(z, 0.0) - jnp.log1p(jnp.exp(-jnp.abs(z)))


def _split3(x):
    h1 = x.astype(BF16)
    r1 = x - h1.astype(F32)
    h2 = r1.astype(BF16)
    h3 = (r1 - h2.astype(F32)).astype(BF16)
    return h1, h2, h3


def _mod_kernel(cond_ref, w_ref, b_ref, o_ref):
    c = cond_ref[...]
    s = c * jax.nn.sigmoid(c)
    o_ref[...] = _dot_x3(_hi_lo(s), _hi_lo(w_ref[...])) + b_ref[...]


def _modulation(cond, w_mod, b_mod):
    n_layers = w_mod.shape[0]
    n_chunks = w_mod.shape[2] // D_MODEL
    return pl.pallas_call(
        _mod_kernel,
        grid=(n_layers, n_chunks),
        in_specs=[
            pl.BlockSpec((16, D_MODEL), lambda l, j: (0, 0)),
            pl.BlockSpec((None, D_MODEL, D_MODEL), lambda l, j: (l, 0, j)),
            pl.BlockSpec((None, 1, D_MODEL), lambda l, j: (l, 0, j)),
        ],
        out_specs=pl.BlockSpec((None, 16, D_MODEL), lambda l, j: (l, 0, j)),
        out_shape=jax.ShapeDtypeStruct((n_layers, 16, w_mod.shape[2]), F32),
        compiler_params=_params(("parallel", "parallel")),
        name="modulation",
    )(cond, w_mod, b_mod.reshape(n_layers, 1, -1))


def _two_part(i, n_first, a_ref, b_ref):
    return jnp.where(i < n_first, a_ref[...], b_ref[...])


def _hi_lo(x):
    hi = x.astype(BF16)
    return hi, (x - hi.astype(F32)).astype(BF16)


def _dot_x3(a, b, dims=None):
    (ah, al), (bh, bl) = a, b
    if dims is None:
        d = lambda x, y: jnp.dot(x, y, preferred_element_type=F32)
    else:
        d = lambda x, y: lax.dot_general(x, y, dims, preferred_element_type=F32)
    return d(ah, bh) + (d(al, bh) + d(ah, bl))


def _two_part_out_specs(width, n_first, tile=TOKEN_TILE):
    return _two_part_specs(width, n_first, tile)


def _two_part_specs(width, n_first, tile=TOKEN_TILE):
    return [pl.BlockSpec((tile, width), lambda i: (jnp.minimum(i, n_first - 1), 0)),
            pl.BlockSpec((tile, width), lambda i: (jnp.maximum(i - n_first, 0), 0))]


def _inproj_kernel(*refs, n_ctx_tiles, precise_ctx):
    it = iter(refs)
    xa_ref, xb_ref, sh_ref, sc_ref, g_ref, w_ref = [next(it) for _ in range(6)]
    wl_ref = next(it) if precise_ctx else None
    (wgt_ref, brow_ref, gq_ref, gk_ref, eye_ref, cos_ref, sa_ref, sb_ref, gsum_ref,
     qa_ref, qb_ref, k_ref, v_ref, mqa_ref, mqb_ref, mka_ref, mkb_ref, mva_ref, mvb_ref,
     og_ref, gc_ref, gr_ref) = it
    tile = pl.program_id(0)
    x = _two_part(tile, n_ctx_tiles, xa_ref, xb_ref)
    ms = jnp.mean(x * x, axis=-1, keepdims=True)
    h = x * lax.rsqrt(ms + EPS) * g_ref[...]
    h = h * (1.0 + sc_ref[...]) + sh_ref[...]
    hb = h.astype(BF16)

    zr = lax.dot_general(wgt_ref[...], h, NT_DIMS, precision=HIGHEST,
                         preferred_element_type=F32) + brow_ref[...]
    sub = lax.broadcasted_iota(jnp.int32, zr.shape, 0)
    gr = jnp.where(sub % 4 < 2, zr, _log_sigmoid(zr))
    eye = eye_ref[...]
    gc = functools.reduce(jnp.add, [lax.dot_general(eye, p, NT_DIMS, preferred_element_type=F32)
                                    for p in _split3(gr)])
    gr_ref[...] = jnp.zeros_like(gr_ref)
    for hd in range(N_ML_HEADS):
        gc_ref[hd] = gc[:, 4 * hd:4 * hd + 4]
        gr_ref[hd, 0:4, :] = gr[4 * hd:4 * hd + 4, :]

    cos = cos_ref[...]
    sa = sa_ref[...]
    sb = sb_ref[...]
    gsum = gsum_ref[...]

    def project(precise, q_ref, mq_ref, mk_ref, mv_ref):
        if precise:
            h_pair = _hi_lo(h)
            proj = lambda c0, width: _dot_x3(
                h_pair, (w_ref[:, c0:c0 + width], wl_ref[:, c0:c0 + width]))
        else:
            proj = lambda c0, width: jnp.dot(hb, w_ref[:, c0:c0 + width],
                                             preferred_element_type=F32)
        act = F32 if precise else BF16

        n_qk = ATT_WIDTH // LANES + 1
        t_rows = hb.shape[0]
        zqk = proj(0, n_qk * LANES)
        zs = [zqk[:, c * LANES:(c + 1) * LANES] for c in range(n_qk)]
        sq = jnp.concatenate([z * z for z in zs], axis=0)
        ss = jnp.dot(jnp.concatenate(_hi_lo(sq), axis=0), gsum, preferred_element_type=F32)
        ss = ss[:n_qk * t_rows] + ss[n_qk * t_rows:]

        def headnorm_rope(c, gain):
            zn = zs[c] * lax.rsqrt(ss[c * t_rows:(c + 1) * t_rows] * (1.0 / HEAD_DIM) + EPS) * gain
            return zn * cos + pltpu.roll(zn, LANES - 16, 1) * sa + pltpu.roll(zn, 16, 1) * sb

        for c in range(n_qk - 1):
            q_ref[:, c * LANES:(c + 1) * LANES] = (headnorm_rope(c, gq_ref[...]) * 0.125).astype(act)
        k_ref[...] = headnorm_rope(n_qk - 1, gk_ref[...])
        v_ref[...] = proj(640, LANES)
        mq_ref[...] = proj(768, ML_WIDTH).astype(act)
        mk_ref[...] = (proj(1280, ML_WIDTH) * (ML_HEAD_DIM ** -0.5)).astype(act)
        mv_ref[...] = proj(1792, ML_WIDTH).astype(act)
        og_ref[...] = jax.nn.sigmoid(proj(2304, ML_WIDTH))

    @pl.when(tile < n_ctx_tiles)
    def _ctx():
        project(precise_ctx, qa_ref, mqa_ref, mka_ref, mva_ref)

    @pl.when(tile >= n_ctx_tiles)
    def _lat():
        project(False, qb_ref, mqb_ref, mkb_ref, mvb_ref)


def _inproj(xa, xb, mod3, n_ctx_tiles, tiles_per_lat_seq, ctx_row, g_mix, w_main, w_gate, b_gate,
            g_q, g_k, rope):
    w_hi, w_lo = w_main
    precise_ctx = w_lo is not None
    n_ctx, n_lat = xa.shape[0], xb.shape[0]
    n = n_ctx + n_lat
    n_tiles = n // TOKEN_TILE

    def mod_row(i):
        return jnp.where(i < n_ctx_tiles, ctx_row, (i - n_ctx_tiles) // tiles_per_lat_seq)

    def rope_blk(i):
        return jnp.where(i < n_ctx_tiles, 0, 1 + (i - n_ctx_tiles) % tiles_per_lat_seq)

    cos_t, sa_t, sb_t = rope
    lane = np.arange(LANES)
    gsum = jnp.asarray((lane[:, None] // HEAD_DIM) == (lane[None, :] // HEAD_DIM), BF16)
    tok = lambda w: pl.BlockSpec((TOKEN_TILE, w), lambda i: (i, 0))
    full = lambda a: pl.BlockSpec(a.shape, lambda i: (0,) * a.ndim)
    modspec = lambda j: pl.BlockSpec((None, 1, D_MODEL), lambda i: (mod_row(i), 0, j))
    ropespec = pl.BlockSpec((TOKEN_TILE, LANES), lambda i: (rope_blk(i), 0))
    consts = (g_mix.reshape(1, -1), w_hi) + ((w_lo,) if precise_ctx else ()) + (
        w_gate.T, b_gate.reshape(-1, 1),
        jnp.tile(g_q, 2).reshape(1, -1), jnp.tile(g_k, 2).reshape(1, -1),
        jnp.asarray(np.eye(TOKEN_TILE), BF16))
    args = (xa, xb, mod3, mod3) + consts + (cos_t, sa_t, sb_t, gsum)
    in_specs = _two_part_specs(D_MODEL, n_ctx_tiles) + [modspec(0), modspec(1)] \
        + [full(a) for a in consts] + [ropespec, ropespec, ropespec, full(gsum)]
    ctx_act = F32 if precise_ctx else BF16
    pair_shape = lambda w: [jax.ShapeDtypeStruct((n_ctx, w), ctx_act),
                            jax.ShapeDtypeStruct((n_lat, w), BF16)]
    pair_spec = lambda w: _two_part_out_specs(w, n_ctx_tiles)
    out_shape = (
        pair_shape(ATT_WIDTH)
        + [jax.ShapeDtypeStruct((n, LANES), F32),
           jax.ShapeDtypeStruct((n, LANES), F32)]
        + pair_shape(ML_WIDTH) + pair_shape(ML_WIDTH) + pair_shape(ML_WIDTH)
        + [jax.ShapeDtypeStruct((n, ML_WIDTH), F32),
           jax.ShapeDtypeStruct((N_ML_HEADS, n, 4), F32),
           jax.ShapeDtypeStruct((N_ML_HEADS, 8, n), F32)])
    out_specs = (pair_spec(ATT_WIDTH) + [tok(LANES), tok(LANES)]
                 + pair_spec(ML_WIDTH) + pair_spec(ML_WIDTH) + pair_spec(ML_WIDTH)
                 + [tok(ML_WIDTH),
                    pl.BlockSpec((N_ML_HEADS, TOKEN_TILE, 4), lambda i: (0, i, 0)),
                    pl.BlockSpec((N_ML_HEADS, 8, TOKEN_TILE), lambda i: (0, 0, i))])
    outs = pl.pallas_call(
        functools.partial(_inproj_kernel, n_ctx_tiles=n_ctx_tiles, precise_ctx=precise_ctx),
        grid=(n_tiles,), in_specs=in_specs, out_specs=tuple(out_specs),
        out_shape=tuple(out_shape), compiler_params=_params(("arbitrary",)), name="inproj",
    )(*args)
    qa, qb, k, v, mqa, mqb, mka, mkb, mva, mvb, og, gcol, grow = outs
    return (qa, qb), k, v, (mqa, mqb), (mka, mkb), (mva, mvb), og, gcol, grow


def _attn_kernel(*refs, n_kv, precise):
    q_ref = refs[0]
    kv_refs = refs[1:1 + 2 * n_kv]
    o_ref = refs[-1]
    tq = q_ref.shape[0]
    lo_q = lax.broadcasted_iota(jnp.int32, (tq, LANES), 1) < HEAD_DIM
    operand = _hi_lo if precise else (lambda a: a.astype(BF16))
    if precise:
        qk = lambda a, b: _dot_x3(a, b, NT_DIMS)
        pv = _dot_x3
    else:
        qk = lambda a, b: lax.dot_general(a, b, NT_DIMS, preferred_element_type=F32)
        pv = lambda a, b: jnp.dot(a, b, preferred_element_type=F32)

    def dup_half(ref, g):
        a = ref[...]
        r = pltpu.roll(a, HEAD_DIM, 1)
        lo = lax.broadcasted_iota(jnp.int32, a.shape, 1) < HEAD_DIM
        return operand(jnp.where(lo, a, r) if g == 0 else jnp.where(lo, r, a))

    for g in range(N_KV_HEADS):
        ks = [dup_half(kv_refs[2 * p], g) for p in range(n_kv)]
        vs = [dup_half(kv_refs[2 * p + 1], g) for p in range(n_kv)]
        for hb in range(2):
            c0 = (2 * g + hb) * LANES
            qb = q_ref[:, c0:c0 + LANES]
            outs = []
            for half in range(2):
                keep = lo_q if half == 0 else jnp.logical_not(lo_q)
                qm = operand(jnp.where(keep, qb, jnp.zeros_like(qb)))
                ss = [qk(qm, kd) for kd in ks]
                m = functools.reduce(jnp.maximum, [jnp.max(s, axis=1, keepdims=True) for s in ss])
                ps = [jnp.exp(s - m) for s in ss]
                den = functools.reduce(jnp.add, [jnp.sum(p, axis=1, keepdims=True) for p in ps])
                o = functools.reduce(jnp.add, [pv(operand(p), vd) for p, vd in zip(ps, vs)])
                outs.append(o / den)
            o_ref[:, c0:c0 + LANES] = jnp.where(lo_q, outs[0], outs[1]).astype(o_ref.dtype)


def _attention(q, k, v, kv_row0, n_seq, seq_len, cache=None):
    precise = q.dtype == F32
    nq = seq_len // SEQ_BLOCK
    sb0 = kv_row0 // seq_len
    in_specs = [
        pl.BlockSpec((SEQ_BLOCK, ATT_WIDTH), lambda b, i: (b * nq + i, 0)),
        pl.BlockSpec((seq_len, LANES), lambda b, i: (sb0 + b, 0)),
        pl.BlockSpec((seq_len, LANES), lambda b, i: (sb0 + b, 0)),
    ]
    args = [q, k, v]
    n_kv = 1
    if cache is not None:
        ck, cv = cache
        past = ck.shape[0] // n_seq
        in_specs += [pl.BlockSpec((past, LANES), lambda b, i: (b, 0))] * 2
        args += [ck, cv]
        n_kv = 2
    return pl.pallas_call(
        functools.partial(_attn_kernel, n_kv=n_kv, precise=precise),
        grid=(n_seq, nq), in_specs=in_specs,
        out_specs=pl.BlockSpec((SEQ_BLOCK, ATT_WIDTH), lambda b, i: (b * nq + i, 0)),
        out_shape=jax.ShapeDtypeStruct((n_seq * seq_len, ATT_WIDTH), q.dtype),
        compiler_params=_params(("parallel", "parallel")), name="attention",
    )(*args)


def _mlstm_kernel(*refs, heads, **static):
    it = iter(refs)
    q_ref, k_ref, v_ref, og_ref, gmh_ref, gcol_ref, grow_ref, u_ref, l_ref = [next(it) for _ in range(9)]
    init_refs = (next(it), next(it)) if static["has_init"] else ()
    ml_ref = next(it)
    state_refs = (next(it), next(it)) if static["emit_state"] else ()
    for hd in range(heads):
        lanes = pl.ds(hd * ML_HEAD_DIM, ML_HEAD_DIM)
        head_refs = [q_ref.at[:, lanes], k_ref.at[:, lanes], v_ref.at[:, lanes], og_ref.at[:, lanes],
                     gmh_ref.at[:, lanes], gcol_ref.at[hd], grow_ref.at[hd], u_ref, l_ref]
        if init_refs:
            head_refs += [init_refs[0].at[:, hd], init_refs[1].at[hd]]
        head_refs.append(ml_ref.at[:, lanes])
        if state_refs:
            head_refs += [state_refs[0].at[:, hd], state_refs[1].at[hd]]
        _mlstm_head(*head_refs, **static)


def _mlstm_head(*refs, seq_len, has_init, emit_state, precise):
    it = iter(refs)
    q_ref, k_ref, v_ref, og_ref, gmh_ref, gcol_ref, grow_ref, u_ref, l_ref = [next(it) for _ in range(9)]
    if has_init:
        c0_ref, nm0_ref = next(it), next(it)
    ml_ref = next(it)
    if emit_state:
        cf_ref, nmf_ref = next(it), next(it)

    operand = _hi_lo if precise else (lambda a: a.astype(BF16))
    if precise:
        qk = lambda a, b: _dot_x3(a, b, NT_DIMS)
        pv = _dot_x3
    else:
        qk = lambda a, b: lax.dot_general(a, b, NT_DIMS, preferred_element_type=F32)
        pv = lambda a, b: jnp.dot(a, b, preferred_element_type=F32)

    bq = SEQ_BLOCK
    nb = seq_len // bq
    blk = lambda j: slice(j * bq, (j + 1) * bq)
    upper_incl = u_ref[...]
    lower_incl = l_ref[...]

    def tri_dot(x, tri):
        return functools.reduce(jnp.add, [jnp.dot(p, tri, preferred_element_type=F32)
                                          for p in _split3(x)])

    ig_row = [[None] * nb for _ in range(2)]
    lf_row = [[None] * nb for _ in range(2)]
    within = [[None] * nb for _ in range(2)]
    bsum = [[None] * nb for _ in range(2)]
    for j in range(nb):
        g8 = grow_ref[:, blk(j)]
        cum_f = tri_dot(g8, upper_incl)
        cum_b = tri_dot(g8, lower_incl)
        for d in range(2):
            ig_row[d][j] = g8[d:d + 1, :]
            lf_row[d][j] = g8[2 + d:3 + d, :]
            within[d][j] = (cum_f if d == 0 else cum_b)[2 + d:3 + d, :]
            bsum[d][j] = jnp.sum(lf_row[d][j], axis=1, keepdims=True)
    zero11 = jnp.zeros((1, 1), F32)
    r_i = lax.broadcasted_iota(jnp.int32, (bq, bq), 0)
    c_i = lax.broadcasted_iota(jnp.int32, (bq, bq), 1)
    causal = [c_i <= r_i, c_i >= r_i]
    gmh = gmh_ref[...]

    q_blocks = [q_ref[blk(i), :] for i in range(nb)]
    scores = [qk(operand(q_blocks[i]), operand(k_ref[blk(i), :])) for i in range(nb)]

    h_dir = [[None] * nb for _ in range(2)]
    final = [None, None]
    for d in range(2):
        if has_init:
            state = (c0_ref[d], nm0_ref[d:d + 1, :])
            m = nm0_ref[2 + d:3 + d, 0:1]
        else:
            state, m = None, zero11
        scan = range(nb) if d == 0 else range(nb - 1, -1, -1)
        for step, i in enumerate(scan):
            q_i = q_blocks[i]
            a_loc = ig_row[d][i] - within[d][i]
            b_loc = jnp.sum(jnp.where(causal[d], lf_row[d][i], 0.0), axis=1, keepdims=True)
            m_col = jnp.maximum(m, jnp.max(jnp.where(causal[d], a_loc, NEG_INF),
                                           axis=1, keepdims=True))
            p = jnp.exp(jnp.where(causal[d], a_loc - m_col, NEG_INF)) * scores[i]
            den = jnp.sum(p, axis=1, keepdims=True)
            num = pv(operand(p), operand(v_ref[blk(i), :]))
            if state is not None:
                c_prev, n_prev = state
                w_inter = jnp.exp(m - m_col)
                qc = jnp.dot(q_i.astype(BF16), c_prev.astype(BF16), preferred_element_type=F32)
                qn = jnp.sum(q_i.astype(F32) * n_prev, axis=1, keepdims=True)
                num = num + w_inter * qc
                den = den + w_inter * qn
            nrm = jnp.maximum(jnp.abs(den), jnp.exp(-(b_loc + m_col)))
            h_dir[d][i] = num / nrm
            if step == nb - 1 and not emit_state:
                break
            m_last = jnp.maximum(m, jnp.max(a_loc, axis=1, keepdims=True))
            a_col = gcol_ref[blk(i), d:d + 1] - b_loc
            kw = k_ref[blk(i), :].astype(F32) * jnp.exp(a_col - m_last)
            c_new = lax.dot_general(kw.astype(BF16), v_ref[blk(i), :].astype(BF16), TN_DIMS,
                                    preferred_element_type=F32)
            n_new = jnp.sum(kw, axis=0, keepdims=True)
            if state is not None:
                decay = jnp.exp(m - m_last)
                c_new = c_new + decay * state[0]
                n_new = n_new + decay * state[1]
            state = (c_new, n_new)
            m = bsum[d][i] + m_last
        final[d] = (state, m)

    for i in range(nb):
        h = h_dir[0][i] + h_dir[1][i]
        hn = h * lax.rsqrt(jnp.mean(h * h, axis=-1, keepdims=True) + EPS) * gmh
        ml_ref[blk(i), :] = (og_ref[blk(i), :] * hn).astype(ml_ref.dtype)

    if emit_state:
        nmf_ref[...] = jnp.zeros_like(nmf_ref)
        for d in range(2):
            (c_fin, n_fin), m_fin = final[d]
            cf_ref[d] = c_fin
            nmf_ref[d:d + 1, :] = n_fin
            nmf_ref[2 + d:3 + d, :] = jnp.broadcast_to(m_fin, (1, ML_HEAD_DIM))


def _mlstm(mq, mk, mv, og, g_mh, gcol, grow, row0, n_seq, seq_len, init=None, emit_state=False):
    precise = mq.dtype == F32
    assert not (precise and init is not None)
    sb0 = row0 // seq_len
    tri = np.arange(SEQ_BLOCK)
    upper_incl = jnp.asarray(tri[:, None] <= tri[None, :], BF16)
    lower_incl = jnp.asarray(tri[:, None] >= tri[None, :], BF16)
    heads = MLSTM_HEADS_PER_STEP
    width = heads * ML_HEAD_DIM
    ownblk = lambda: pl.BlockSpec((seq_len, width), lambda b, h: (b, h))
    headblk = lambda: pl.BlockSpec((seq_len, width), lambda b, h: (sb0 + b, h))
    const = lambda a: pl.BlockSpec(a.shape, lambda b, h: (0,) * a.ndim)
    in_specs = [ownblk(), ownblk(), ownblk(), headblk(),
                pl.BlockSpec((1, width), lambda b, h: (0, h)),
                pl.BlockSpec((heads, seq_len, 4), lambda b, h: (h, sb0 + b, 0)),
                pl.BlockSpec((heads, 8, seq_len), lambda b, h: (h, 0, sb0 + b)),
                const(upper_incl), const(lower_incl)]
    args = [mq, mk, mv, og, g_mh.reshape(1, -1), gcol, grow, upper_incl, lower_incl]
    if init is not None:
        c0, nm0, layer = init
        in_specs += [
            pl.BlockSpec((None, None, 2, heads, ML_HEAD_DIM, ML_HEAD_DIM),
                         lambda b, h: (b, layer, 0, h, 0, 0)),
            pl.BlockSpec((None, heads, 8, ML_HEAD_DIM), lambda b, h: (b, h, 0, 0))]
        args += [c0, nm0]
    out_shape = [jax.ShapeDtypeStruct((n_seq * seq_len, ML_WIDTH), mq.dtype)]
    out_specs = [pl.BlockSpec((seq_len, width), lambda b, h: (b, h))]
    if emit_state:
        out_shape += [jax.ShapeDtypeStruct((n_seq, 2, N_ML_HEADS, ML_HEAD_DIM, ML_HEAD_DIM), F32),
                      jax.ShapeDtypeStruct((n_seq, N_ML_HEADS, 8, ML_HEAD_DIM), F32)]
        out_specs += [pl.BlockSpec((None, 2, heads, ML_HEAD_DIM, ML_HEAD_DIM),
                                   lambda b, h: (b, 0, h, 0, 0)),
                      pl.BlockSpec((None, heads, 8, ML_HEAD_DIM), lambda b, h: (b, h, 0, 0))]
    return pl.pallas_call(
        functools.partial(_mlstm_kernel, heads=heads, seq_len=seq_len, has_init=init is not None,
                          emit_state=emit_state, precise=precise),
        grid=(n_seq, N_ML_HEADS // heads), in_specs=in_specs, out_specs=tuple(out_specs),
        out_shape=tuple(out_shape),
        compiler_params=_params(("parallel", "parallel")), name="mlstm",
    )(*args)


def _outproj_kernel(*refs, n_ctx_tiles, precise_ctx):
    it = iter(refs)
    atta_ref, attb_ref, mla_ref, mlb_ref, xa_ref, xb_ref, w_ref = [next(it) for _ in range(7)]
    wl_ref = next(it) if precise_ctx else None
    gt_ref, sh_ref, sc_ref, g_ref, wrt_ref, x1_ref, hp_ref, logits_ref, y_scr = it
    i = pl.program_id(0)

    def mix(att, ml):
        return (jnp.dot(att, w_ref[:ATT_WIDTH, :], preferred_element_type=F32)
                + jnp.dot(ml, w_ref[ATT_WIDTH:, :], preferred_element_type=F32))

    @pl.when(i < n_ctx_tiles)
    def _ctx():
        if precise_ctx:
            y_scr[...] = (
                _dot_x3(_hi_lo(atta_ref[...]), (w_ref[:ATT_WIDTH, :], wl_ref[:ATT_WIDTH, :]))
                + _dot_x3(_hi_lo(mla_ref[...]), (w_ref[ATT_WIDTH:, :], wl_ref[ATT_WIDTH:, :])))
        else:
            y_scr[...] = mix(atta_ref[...], mla_ref[...])

    @pl.when(i >= n_ctx_tiles)
    def _lat():
        y_scr[...] = mix(attb_ref[...], mlb_ref[...])

    x1 = _two_part(i, n_ctx_tiles, xa_ref, xb_ref) + gt_ref[...] * y_scr[...]
    x1_ref[...] = x1
    ms = jnp.mean(x1 * x1, axis=-1, keepdims=True)
    h2 = x1 * lax.rsqrt(ms + EPS) * g_ref[...]
    h2 = h2 * (1.0 + sc_ref[...]) + sh_ref[...]
    hp_ref[...] = h2

    logits_ref[...] = lax.dot_general(wrt_ref[...], h2, NT_DIMS, precision=HIGHEST,
                                      preferred_element_type=F32)


def _select_experts(logits, b_col):
    ex = jnp.exp(logits - jnp.max(logits, axis=0, keepdims=True))
    scores = ex / jnp.sum(ex, axis=0, keepdims=True)
    sel = scores + b_col
    row = lambda a, e: a[e:e + 1, :]
    grp_score = []
    for g in range(N_GROUPS):
        xs = [row(sel, g * GROUP_SIZE + j) for j in range(GROUP_SIZE)]
        pairs = [xs[a] + xs[b] for a in range(GROUP_SIZE) for b in range(a + 1, GROUP_SIZE)]
        grp_score.append(functools.reduce(jnp.maximum, pairs))
    best = grp_score[0]
    grp = jnp.zeros_like(best, dtype=jnp.int32)
    for g in range(1, N_GROUPS):
        better = grp_score[g] > best
        grp = jnp.where(better, g, grp)
        best = jnp.where(better, grp_score[g], best)
    pick = lambda a, j: functools.reduce(
        lambda acc, g: jnp.where(grp == g, row(a, g * GROUP_SIZE + j), acc),
        range(1, N_GROUPS), row(a, j))
    xs = [pick(sel, j) for j in range(GROUP_SIZE)]
    ws = [pick(scores, j) for j in range(GROUP_SIZE)]

    def argmax4(vals):
        bv, bi = vals[0], jnp.zeros_like(grp)
        for j in range(1, GROUP_SIZE):
            better = vals[j] > bv
            bi = jnp.where(better, j, bi)
            bv = jnp.where(better, vals[j], bv)
        return bi

    i1 = argmax4(xs)
    i2 = argmax4([jnp.where(i1 == j, NEG_INF, xs[j]) for j in range(GROUP_SIZE)])
    take = lambda vals, idx: functools.reduce(
        lambda acc, j: jnp.where(idx == j, vals[j], acc), range(1, GROUP_SIZE), vals[0])
    w1, w2 = take(ws, i1), take(ws, i2)
    wsum = w1 + w2
    w1, w2 = w1 / wsum, w2 / wsum
    return grp * GROUP_SIZE + i1, grp * GROUP_SIZE + i2, w1, w2


def _outproj(att, ml, x, mod3, n_ctx_tiles, tiles_per_lat_seq, ctx_row, w_out, g_ffn, w_router):
    w_hi, w_lo = w_out
    precise_ctx = w_lo is not None
    n = x[0].shape[0] + x[1].shape[0]

    def mod_row(i):
        return jnp.where(i < n_ctx_tiles, ctx_row, (i - n_ctx_tiles) // tiles_per_lat_seq)

    tok = lambda w: pl.BlockSpec((TOKEN_TILE, w), lambda i: (i, 0))
    full = lambda a: pl.BlockSpec(a.shape, lambda i: (0,) * a.ndim)
    modspec = lambda j: pl.BlockSpec((None, 1, D_MODEL), lambda i: (mod_row(i), 0, j))
    weights = (w_hi, w_lo) if precise_ctx else (w_hi,)
    consts = (g_ffn.reshape(1, -1), w_router.T)
    args = (*att, *ml, *x, *weights, mod3, mod3, mod3) + consts
    in_specs = (_two_part_specs(ATT_WIDTH, n_ctx_tiles) + _two_part_specs(ML_WIDTH, n_ctx_tiles)
                + _two_part_specs(D_MODEL, n_ctx_tiles) + [full(w) for w in weights]
                + [modspec(2), modspec(3), modspec(4)] + [full(a) for a in consts])
    return pl.pallas_call(
        functools.partial(_outproj_kernel, n_ctx_tiles=n_ctx_tiles, precise_ctx=precise_ctx),
        grid=(n // TOKEN_TILE,), in_specs=in_specs,
        scratch_shapes=[pltpu.VMEM((TOKEN_TILE, D_MODEL), F32)],
        out_specs=(tok(D_MODEL), tok(D_MODEL),
                   pl.BlockSpec((N_EXPERTS, TOKEN_TILE), lambda i: (0, i))),
        out_shape=(jax.ShapeDtypeStruct((n, D_MODEL), F32),
                   jax.ShapeDtypeStruct((n, D_MODEL), F32),
                   jax.ShapeDtypeStruct((N_EXPERTS, n), F32)),
        compiler_params=_params(("parallel",)), name="outproj_router",
    )(*args)


def _moe_kernel(order_ref, off_ref, cnt_ref,
                h_ref, gw_ref, wg_ref, wu_ref, wd_ref, x1_ref, gt_ref, ya_ref, yb_ref,
                acc_scr, xs_scr, out_scr, *, n_ctx_tiles):
    t = pl.program_id(0)
    s = pl.program_id(1)
    groups = MOE_CHUNK // 8
    dummy_row = MOE_TILE

    sub = ROW_TILE_SUBLANES

    def tile(ref, row):
        return ref.at[pl.ds(pl.multiple_of(row * sub, sub), sub), :]

    def slab(ref, row0, n_rows, k):
        return ref.at[pl.ds(row0 * sub + k, n_rows, stride=sub), :]

    def gather_rows(buf, slot0):
        for j in range(MOE_CHUNK):
            tile(xs_scr, buf * MOE_CHUNK + j)[...] = h_ref[order_ref[slot0 + j]]

    def ffn(buf, base):
        a = jnp.zeros((MOE_CHUNK, D_EXPERT), F32)
        b = jnp.zeros((MOE_CHUNK, D_EXPERT), F32)
        for p in range(ROW_TILE_SUBLANES // 2):
            lhs = jnp.concatenate([slab(xs_scr, buf * MOE_CHUNK, MOE_CHUNK, 2 * p)[...],
                                   slab(xs_scr, buf * MOE_CHUNK, MOE_CHUNK, 2 * p + 1)[...]],
                                  axis=1).astype(BF16)
            rows = slice(2 * p * LANES, (2 * p + 2) * LANES)
            a = a + jnp.dot(lhs, wg_ref[rows, :], preferred_element_type=F32)
            b = b + jnp.dot(lhs, wu_ref[rows, :], preferred_element_type=F32)
        hid = (a * jax.nn.sigmoid(a)) * b * gw_ref[pl.ds(base, MOE_CHUNK), :]
        out = jnp.dot(hid.astype(BF16), wd_ref[...], preferred_element_type=F32)
        for k in range(ROW_TILE_SUBLANES):
            slab(out_scr, buf * MOE_CHUNK, MOE_CHUNK, k)[...] = out[:, k * LANES:(k + 1) * LANES]

    def scatter_rows(buf, slot0, valid):
        for j0 in range(0, MOE_CHUNK, MOE_SCATTER_BATCH):
            js = range(j0, j0 + MOE_SCATTER_BATCH)
            toks = [jnp.where(j < valid, order_ref[slot0 + j], dummy_row) for j in js]
            rows = [tile(acc_scr, tok)[...] for tok in toks]
            for j, tok, row in zip(js, toks, rows):
                tile(acc_scr, tok)[...] = row + tile(out_scr, buf * MOE_CHUNK + j)[...]

    def slot_base(seg):
        return t * MOE_SLOTS + pl.multiple_of(off_ref[seg], 8)

    @pl.when(jnp.logical_and(t == 0, s == 0))
    def _zero():
        acc_scr[...] = jnp.zeros_like(acc_scr)

    @pl.when(s == 0)
    def _first():
        gather_rows(0, slot_base(t * N_EXPERTS))

    @pl.when(s < N_EXPERTS)
    def _expert():
        seg = t * N_EXPERTS + s
        off = pl.multiple_of(off_ref[seg], 8)
        cur = s % 2
        prev_seg = t * N_EXPERTS + jnp.maximum(s - 1, 0)
        next_seg = t * N_EXPERTS + jnp.minimum(s + 1, N_EXPERTS - 1)
        scatter_rows(1 - cur, slot_base(prev_seg), jnp.where(s > 0, cnt_ref[prev_seg], 0))
        ffn(cur, off)
        gather_rows(1 - cur, slot_base(next_seg))

        def extra(c, carry):
            base = pl.multiple_of(off + c * MOE_CHUNK, 8)

            def gather8(i, carry2):
                for k in range(8):
                    tile(xs_scr, 2 * MOE_CHUNK + i * 8 + k)[...] = h_ref[
                        order_ref[t * MOE_SLOTS + base + i * 8 + k]]
                return carry2

            lax.fori_loop(0, groups, gather8, 0)
            ffn(2, base)

            def scatter8(i, carry2):
                for k in range(8):
                    row = c * MOE_CHUNK + i * 8 + k
                    tok = jnp.where(row < cnt_ref[seg],
                                    order_ref[t * MOE_SLOTS + base + i * 8 + k], dummy_row)
                    tile(acc_scr, tok)[...] = (tile(acc_scr, tok)[...]
                                               + tile(out_scr, 2 * MOE_CHUNK + i * 8 + k)[...])
                return carry2

            lax.fori_loop(0, groups, scatter8, 0)
            return carry

        lax.fori_loop(1, (cnt_ref[seg] + MOE_CHUNK - 1) // MOE_CHUNK, extra, 0)

    @pl.when(s == N_EXPERTS)
    def _last_scatter():
        last = t * N_EXPERTS + N_EXPERTS - 1
        scatter_rows((N_EXPERTS - 1) % 2, slot_base(last), cnt_ref[last])

    @pl.when(s >= N_EXPERTS)
    def _output():
        r0 = pl.multiple_of((s - N_EXPERTS) * MOE_OUT_TILE, MOE_OUT_TILE)
        moe = jnp.concatenate([slab(acc_scr, r0, MOE_OUT_TILE, k)[...] for k in range(sub)], axis=1)
        y = x1_ref[...] + gt_ref[...] * moe
        acc_scr[pl.ds(pl.multiple_of(r0 * sub, sub), MOE_OUT_TILE * sub), :] = jnp.zeros(
            (MOE_OUT_TILE * sub, LANES), F32)
        chunk_ix = t * (MOE_TILE // MOE_OUT_TILE) + s - N_EXPERTS

        @pl.when(chunk_ix < n_ctx_tiles)
        def _ctx():
            ya_ref[...] = y

        @pl.when(chunk_ix >= n_ctx_tiles)
        def _lat():
            yb_ref[...] = y


def _route_tables(logits, b_router):
    n = logits.shape[1]
    nt = n // MOE_TILE
    tri = np.arange(SEQ_BLOCK)
    strict_upper = jnp.asarray(tri[:, None] < tri[None, :], BF16)
    order, gw, meta = pl.pallas_call(
        _route_kernel, grid=(nt,),
        in_specs=[pl.BlockSpec((N_EXPERTS, MOE_TILE), lambda t: (0, t)),
                  pl.BlockSpec((N_EXPERTS, 1), lambda t: (0, 0)),
                  pl.BlockSpec((SEQ_BLOCK, SEQ_BLOCK), lambda t: (0, 0)),
                  pl.BlockSpec((SEQ_BLOCK, SEQ_BLOCK), lambda t: (0, 0))],
        out_specs=(pl.BlockSpec((None, MOE_SLOTS, 1), lambda t: (t, 0, 0)),
                   pl.BlockSpec((None, MOE_SLOTS, 1), lambda t: (t, 0, 0)),
                   pl.BlockSpec((None, N_EXPERTS, 8), lambda t: (t, 0, 0))),
        out_shape=(jax.ShapeDtypeStruct((nt, MOE_SLOTS, 1), jnp.int32),
                   jax.ShapeDtypeStruct((nt, MOE_SLOTS, 1), F32),
                   jax.ShapeDtypeStruct((nt, N_EXPERTS, 8), jnp.int32)),
        scratch_shapes=[pltpu.VMEM((LANES, MOE_TILE), F32),
                        pltpu.VMEM((MOE_TILE, LANES), BF16)],
        compiler_params=_params(("parallel",)), name="route_tables",
    )(logits, b_router.reshape(-1, 1), strict_upper, jnp.asarray(np.eye(SEQ_BLOCK), BF16))
    return order.reshape(-1), gw, meta[:, :, 0].reshape(-1), meta[:, :, 1].reshape(-1)


def _route_kernel(logits_ref, br_ref, su_ref, eye_ref, order_ref, gw_ref, meta_ref,
                  rows_scr, table_scr):
    e1, e2, w1, w2 = _select_experts(logits_ref[...], br_ref[...])
    tok = lax.broadcasted_iota(jnp.int32, (1, MOE_TILE), 1)
    rows_scr[...] = jnp.zeros_like(rows_scr)
    rows_scr[0:1, :] = (tok % 256).astype(F32)
    rows_scr[1:2, :] = (tok // 256).astype(F32)
    for k, piece in enumerate(_split3(w1)):
        rows_scr[2 + k:3 + k, :] = piece.astype(F32)
    for k, piece in enumerate(_split3(w2)):
        rows_scr[5 + k:6 + k, :] = piece.astype(F32)
    eye = eye_ref[...]
    for b in range(MOE_TILE // SEQ_BLOCK):
        cols = slice(b * SEQ_BLOCK, (b + 1) * SEQ_BLOCK)
        table_scr[cols, :] = lax.dot_general(eye, rows_scr[:, cols].astype(BF16), NT_DIMS,
                                             preferred_element_type=F32).astype(BF16)
    eid = lax.broadcasted_iota(jnp.int32, (N_EXPERTS, MOE_TILE), 0)
    oh1, oh2 = eid == e1, eid == e2
    oh = jnp.where(oh1, 1.0, 0.0) + jnp.where(oh2, 1.0, 0.0)
    nblk = MOE_TILE // SEQ_BLOCK
    blocks = [oh[:, b * SEQ_BLOCK:(b + 1) * SEQ_BLOCK] for b in range(nblk)]
    inner = jnp.dot(jnp.concatenate(blocks, axis=0).astype(BF16), su_ref[...],
                    preferred_element_type=F32)
    run = jnp.zeros((N_EXPERTS, 1), F32)
    ranks = []
    for b in range(nblk):
        ranks.append(inner[b * N_EXPERTS:(b + 1) * N_EXPERTS, :] + run)
        run = run + jnp.sum(blocks[b], axis=1, keepdims=True)
    count = run
    seg = jnp.floor((count + 7.0) * 0.125) * 8.0
    sub = lax.broadcasted_iota(jnp.int32, (N_EXPERTS, 1), 0)
    off = jnp.zeros((N_EXPERTS, 1), F32)
    for e in range(N_EXPERTS - 1):
        off = off + jnp.where(sub > e, seg[e:e + 1, :], 0.0)
    slot = jnp.concatenate(ranks, axis=1) + off
    pos1 = jnp.sum(jnp.where(oh1, slot, 0.0), axis=0, keepdims=True).astype(jnp.int32)
    pos2 = jnp.sum(jnp.where(oh2, slot, 0.0), axis=0, keepdims=True).astype(jnp.int32)
    meta_ref[...] = jnp.zeros_like(meta_ref)
    meta_ref[:, 0:1] = off.astype(jnp.int32)
    meta_ref[:, 1:2] = count.astype(jnp.int32)
    table = table_scr[...]
    rows = MOE_SLOTS // ROUTE_SLOT_BLOCKS
    for sb in range(ROUTE_SLOT_BLOCKS):
        j = lax.broadcasted_iota(jnp.int32, (rows, MOE_TILE), 0) + sb * rows
        d1 = jnp.dot(jnp.where(j == pos1, 1.0, 0.0).astype(BF16), table,
                     preferred_element_type=F32)
        d2 = jnp.dot(jnp.where(j == pos2, 1.0, 0.0).astype(BF16), table,
                     preferred_element_type=F32)
        d = d1 + d2
        blk = slice(sb * rows, (sb + 1) * rows)
        order_ref[blk, :] = (d[:, 0:1] + 256.0 * d[:, 1:2]).astype(jnp.int32)
        gw_ref[blk, :] = (d1[:, 2:3] + d1[:, 3:4] + d1[:, 4:5]) + (d2[:, 5:6] + d2[:, 6:7] + d2[:, 7:8])


def _moe(hp, logits, b_router, x1, mod3, layer, n_ctx_tiles, tiles_per_lat_seq, ctx_row,
         wg, wu, wd):
    n = hp.shape[0]
    order, gw, off, count = _route_tables(logits, b_router)
    chunks_per_tile = MOE_TILE // MOE_OUT_TILE
    n_steps = N_EXPERTS + chunks_per_tile

    def chunk_ix(t, s):
        return t * chunks_per_tile + jnp.maximum(s - N_EXPERTS, 0)

    def mod_row(g):
        return jnp.where(g < n_ctx_tiles, ctx_row, (g - n_ctx_tiles) // tiles_per_lat_seq)

    wspec = lambda r, c: pl.BlockSpec(
        (None, None, r, c), lambda t, s, *_: (layer, jnp.minimum(s, N_EXPERTS - 1), 0, 0))
    chunk_spec = pl.BlockSpec((MOE_OUT_TILE, D_MODEL), lambda t, s, *_: (chunk_ix(t, s), 0))
    grid_spec = pltpu.PrefetchScalarGridSpec(
        num_scalar_prefetch=3,
        grid=(n // MOE_TILE, n_steps),
        in_specs=[
            pl.BlockSpec((MOE_TILE, ROW_TILE_SUBLANES, LANES), lambda t, s, *_: (t, 0, 0)),
            pl.BlockSpec((None, MOE_SLOTS, 1), lambda t, s, *_: (t, 0, 0)),
            wspec(D_MODEL, D_EXPERT), wspec(D_MODEL, D_EXPERT), wspec(D_EXPERT, D_MODEL),
            chunk_spec,
            pl.BlockSpec((None, 1, D_MODEL), lambda t, s, *_: (mod_row(chunk_ix(t, s)), 0, 5)),
        ],
        out_specs=(
            pl.BlockSpec((MOE_OUT_TILE, D_MODEL),
                         lambda t, s, *_: (jnp.minimum(chunk_ix(t, s), n_ctx_tiles - 1), 0)),
            pl.BlockSpec((MOE_OUT_TILE, D_MODEL),
                         lambda t, s, *_: (jnp.maximum(chunk_ix(t, s) - n_ctx_tiles, 0), 0))),
        scratch_shapes=[pltpu.VMEM(((MOE_TILE + 8) * ROW_TILE_SUBLANES, LANES), F32),
                        pltpu.VMEM((3 * MOE_CHUNK * ROW_TILE_SUBLANES, LANES), F32),
                        pltpu.VMEM((3 * MOE_CHUNK * ROW_TILE_SUBLANES, LANES), F32)],
    )
    n_ctx = n_ctx_tiles * MOE_OUT_TILE
    return pl.pallas_call(
        functools.partial(_moe_kernel, n_ctx_tiles=n_ctx_tiles), grid_spec=grid_spec,
        out_shape=(jax.ShapeDtypeStruct((n_ctx, D_MODEL), F32),
                   jax.ShapeDtypeStruct((n - n_ctx, D_MODEL), F32)),
        compiler_params=_params(("arbitrary", "arbitrary")), name="experts",
    )(order, off, count, hp.reshape(n, ROW_TILE_SUBLANES, LANES), gw, wg, wu, wd, x1, mod3)


def _rope_tables(seq_len):
    half = HEAD_DIM // 2
    freqs = ROPE_BASE ** (-np.arange(0, half, 2, dtype=np.float64) / half)
    pos = np.arange(seq_len)
    row, col = pos // GRID_W, pos % GRID_W
    d = np.arange(HEAD_DIM)
    position = np.where(d[None, :] < half, row[:, None], col[:, None]).astype(np.float64)
    ang = (position.astype(np.float32) * freqs.astype(np.float32)[d % (half // 2)][None, :]).astype(np.float32)
    cos, sin = np.cos(ang), np.sin(ang)
    first = (d % half) < half // 2
    sa = np.where(first[None, :], -sin, 0.0)
    sb = np.where(first[None, :], 0.0, sin)
    ident = lambda v: np.full((TOKEN_TILE, HEAD_DIM), v, np.float32)
    stack = lambda ctx, lat: jnp.asarray(
        np.tile(np.concatenate([ctx, lat.astype(np.float32)], axis=0), (1, 2)), F32)
    return stack(ident(1.0), cos), stack(ident(0.0), sa), stack(ident(0.0), sb)


def kernel(x_prompt, x_sample, c, cache_k, cache_v, state_C, state_n, state_m, c_ctx, w_mod, b_mod,
           g_mix, g_ffn, w_in, b_igate, b_fgate, g_q, g_k, g_mh, w_out, w_router, b_router,
           w_e_gate, w_e_up, w_e_down):
    n_ctx_seq, ctx_len, _ = x_prompt.shape
    n_lat_seq, lat_len, _ = x_sample.shape
    n_layers = w_mod.shape[0]
    n_ctx = n_ctx_seq * ctx_len
    assert ctx_len == SEQ_BLOCK and lat_len % TOKEN_TILE == 0
    assert n_ctx % MOE_TILE == 0 and (n_lat_seq * lat_len) % MOE_TILE == 0
    assert n_lat_seq < 16 and n_ctx % lat_len == 0
    n_ctx_tiles = n_ctx // TOKEN_TILE
    tiles_per_lat_seq = lat_len // TOKEN_TILE
    ctx_row = n_lat_seq

    x = (x_prompt.reshape(n_ctx, D_MODEL), x_sample.reshape(-1, D_MODEL))
    cond = jnp.zeros((16, D_MODEL), F32).at[:n_lat_seq].set(c).at[ctx_row].set(c_ctx)
    mod = _modulation(cond, w_mod, b_mod)
    rope = _rope_tables(lat_len)

    wg_b, wu_b, wd_b = w_e_gate.astype(BF16), w_e_up.astype(BF16), w_e_down.astype(BF16)
    gate_perm = np.array([(q % 2) * N_ML_HEADS + hd + 2 * N_ML_HEADS * (q // 2)
                          for hd in range(N_ML_HEADS) for q in range(4)])

    ks, vs, cs, ns, ms = [], [], [], [], []
    for l in range(n_layers):
        mod3 = mod[l].reshape(16, 1, -1)
        precise_ctx = l < n_layers - 1

        def weight_pair(w):
            hi = w.astype(BF16)
            return hi, ((w - hi.astype(F32)).astype(BF16) if precise_ctx else None)

        w_main = weight_pair(w_in[l, :, :MAIN_WIDTH])
        w_gate = w_in[l, :, MAIN_WIDTH:][:, gate_perm]
        b_gate = jnp.concatenate([b_igate[l].reshape(-1), b_fgate[l].reshape(-1)])[gate_perm]
        q, k, v, mq, mk, mv, og, gcol, grow = _inproj(
            *x, mod3, n_ctx_tiles, tiles_per_lat_seq, ctx_row, g_mix[l], w_main, w_gate, b_gate,
            g_q[l], g_k[l], rope)

        att_ctx = _attention(q[0], k, v, 0, n_ctx_seq, ctx_len)
        past = cache_k.shape[2]
        ck = cache_k[:, l].reshape(n_lat_seq * past, LANES)
        cv = cache_v[:, l].reshape(n_lat_seq * past, LANES)
        att_lat = _attention(q[1], k, v, n_ctx, n_lat_seq, lat_len, cache=(ck, cv))

        ml_ctx, c_fin, nm_fin = _mlstm(mq[0], mk[0], mv[0], og, g_mh[l], gcol, grow, 0,
                                       n_ctx_seq, ctx_len, emit_state=True)
        n0 = state_n[:, l].transpose(0, 2, 1, 3)
        m0 = jnp.broadcast_to(state_m[:, l].transpose(0, 2, 1)[..., None], n0.shape)
        nm0 = jnp.concatenate([n0, m0, jnp.zeros_like(n0), jnp.zeros_like(n0)], axis=2)
        (ml_lat,) = _mlstm(mq[1], mk[1], mv[1], og, g_mh[l], gcol, grow, n_ctx, n_lat_seq,
                           lat_len, init=(state_C, nm0, l))

        x1, hp, logits = _outproj(
            (att_ctx, att_lat), (ml_ctx, ml_lat), x, mod3, n_ctx_tiles, tiles_per_lat_seq, ctx_row,
            weight_pair(w_out[l]), g_ffn[l], w_router)
        x = _moe(hp, logits, b_router, x1, mod3, l, n_ctx // MOE_OUT_TILE,
                 lat_len // MOE_OUT_TILE, ctx_row, wg_b, wu_b, wd_b)

        ks.append(k[:n_ctx].reshape(n_ctx_seq, ctx_len, N_KV_HEADS, HEAD_DIM))
        vs.append(v[:n_ctx].reshape(n_ctx_seq, ctx_len, N_KV_HEADS, HEAD_DIM))
        cs.append(c_fin)
        ns.append(nm_fin[:, :, 0:2, :].transpose(0, 2, 1, 3))
        ms.append(nm_fin[:, :, 2:4, 0].transpose(0, 2, 1))

    y_prompt = x[0].reshape(x_prompt.shape)
    y_sample = x[1].reshape(x_sample.shape)
    return (y_prompt, y_sample, jnp.stack(ks, axis=1), jnp.stack(vs, axis=1),
            jnp.stack(cs, axis=1), jnp.stack(ns, axis=1), jnp.stack(ms, axis=1))
```

```python
import functools

import numpy as np
import jax
import jax.numpy as jnp
from jax import lax
from jax.experimental import pallas as pl
from jax.experimental.pallas import tpu as pltpu

F32 = jnp.float32
BF16 = jnp.bfloat16

D_MODEL = 1024
HEAD_DIM = 64
ATT_WIDTH = 512
N_KV_HEADS = 2
ML_WIDTH = 512
N_ML_HEADS = 4
ML_HEAD_DIM = 128
GRID_W = 64
ROPE_BASE = 10000.0
N_EXPERTS = 16
N_GROUPS = 4
GROUP_SIZE = 4
D_EXPERT = 512
EPS = 1e-6
MAIN_WIDTH = 2816
N_GATE_COLS = 16
LANES = 128
TOKEN_TILE = 512
SEQ_BLOCK = 256
MOE_TILE = 2048
MOE_CHUNK = 320
MLSTM_HEADS_PER_STEP = 4
MOE_OUT_TILE = 256
ROUTE_SLOT_BLOCKS = 9
MOE_SLOTS = -(-(2 * MOE_TILE + 8 * N_EXPERTS + MOE_CHUNK) // (512 * ROUTE_SLOT_BLOCKS)) * 512 * ROUTE_SLOT_BLOCKS
VMEM_LIMIT = 56 * 1024 * 1024
NEG_INF = float("-inf")
HIGHEST = lax.Precision.HIGHEST
NT_DIMS = (((1,), (1,)), ((), ()))
TN_DIMS = (((0,), (0,)), ((), ()))


def _params(semantics):
    return pltpu.CompilerParams(dimension_semantics=semantics, vmem_limit_bytes=VMEM_LIMIT)


def _log_sigmoid(z):
    return jnp.minimum(z, 0.0) - jnp.log1p(jnp.exp(-jnp.abs(z)))


def _split3(x):
    h1 = x.astype(BF16)
    r1 = x - h1.astype(F32)
    h2 = r1.astype(BF16)
    h3 = (r1 - h2.astype(F32)).astype(BF16)
    return h1, h2, h3


def _mod_kernel(cond_ref, w_ref, b_ref, o_ref):
    c = cond_ref[...]
    s = c * jax.nn.sigmoid(c)
    o_ref[...] = _dot_x3(_hi_lo(s), _hi_lo(w_ref[...])) + b_ref[...]


def _modulation(cond, w_mod, b_mod):
    n_layers = w_mod.shape[0]
    n_chunks = w_mod.shape[2] // D_MODEL
    return pl.pallas_call(
        _mod_kernel,
        grid=(n_layers, n_chunks),
        in_specs=[
            pl.BlockSpec((16, D_MODEL), lambda l, j: (0, 0)),
            pl.BlockSpec((None, D_MODEL, D_MODEL), lambda l, j: (l, 0, j)),
            pl.BlockSpec((None, 1, D_MODEL), lambda l, j: (l, 0, j)),
        ],
        out_specs=pl.BlockSpec((None, 16, D_MODEL), lambda l, j: (l, 0, j)),
        out_shape=jax.ShapeDtypeStruct((n_layers, 16, w_mod.shape[2]), F32),
        compiler_params=_params(("parallel", "parallel")),
        name="modulation",
    )(cond, w_mod, b_mod.reshape(n_layers, 1, -1))


def _two_part(i, n_first, a_ref, b_ref):
    return jnp.where(i < n_first, a_ref[...], b_ref[...])


def _hi_lo(x):
    hi = x.astype(BF16)
    return hi, (x - hi.astype(F32)).astype(BF16)


def _dot_x3(a, b, dims=None):
    (ah, al), (bh, bl) = a, b
    if dims is None:
        d = lambda x, y: jnp.dot(x, y, preferred_element_type=F32)
    else:
        d = lambda x, y: lax.dot_general(x, y, dims, preferred_element_type=F32)
    return d(ah, bh) + (d(al, bh) + d(ah, bl))


def _two_part_out_specs(width, n_first, tile=TOKEN_TILE):
    return _two_part_specs(width, n_first, tile)


def _two_part_specs(width, n_first, tile=TOKEN_TILE):
    return [pl.BlockSpec((tile, width), lambda i: (jnp.minimum(i, n_first - 1), 0)),
            pl.BlockSpec((tile, width), lambda i: (jnp.maximum(i - n_first, 0), 0))]


def _inproj_kernel(*refs, n_ctx_tiles, precise_ctx):
    it = iter(refs)
    xa_ref, xb_ref, sh_ref, sc_ref, g_ref, w_ref = [next(it) for _ in range(6)]
    wl_ref = next(it) if precise_ctx else None
    (wgt_ref, brow_ref, gq_ref, gk_ref, eye_ref, cos_ref, sa_ref, sb_ref, gsum_ref,
     qa_ref, qb_ref, k_ref, v_ref, mqa_ref, mqb_ref, mka_ref, mkb_ref, mva_ref, mvb_ref,
     og_ref, gc_ref, gr_ref) = it
    tile = pl.program_id(0)
    x = _two_part(tile, n_ctx_tiles, xa_ref, xb_ref)
    ms = jnp.mean(x * x, axis=-1, keepdims=True)
    h = x * lax.rsqrt(ms + EPS) * g_ref[...]
    h = h * (1.0 + sc_ref[...]) + sh_ref[...]
    hb = h.astype(BF16)

    zr = lax.dot_general(wgt_ref[...], h, NT_DIMS, precision=HIGHEST,
                         preferred_element_type=F32) + brow_ref[...]
    sub = lax.broadcasted_iota(jnp.int32, zr.shape, 0)
    gr = jnp.where(sub % 4 < 2, zr, _log_sigmoid(zr))
    eye = eye_ref[...]
    gc = functools.reduce(jnp.add, [lax.dot_general(eye, p, NT_DIMS, preferred_element_type=F32)
                                    for p in _split3(gr)])
    gr_ref[...] = jnp.zeros_like(gr_ref)
    for hd in range(N_ML_HEADS):
        gc_ref[hd] = gc[:, 4 * hd:4 * hd + 4]
        gr_ref[hd, 0:4, :] = gr[4 * hd:4 * hd + 4, :]

    cos = cos_ref[...]
    sa = sa_ref[...]
    sb = sb_ref[...]
    gsum = gsum_ref[...]

    def project(precise, q_ref, mq_ref, mk_ref, mv_ref):
        if precise:
            z = _dot_x3(_hi_lo(h), (w_ref[...], wl_ref[...]))
        else:
            z = jnp.dot(hb, w_ref[...], preferred_element_type=F32)
        proj = lambda c0, width: z[:, c0:c0 + width]
        act = F32 if precise else BF16

        n_qk = ATT_WIDTH // LANES + 1
        t_rows = hb.shape[0]
        zqk = proj(0, n_qk * LANES)
        zs = [zqk[:, c * LANES:(c + 1) * LANES] for c in range(n_qk)]
        sq = jnp.concatenate([z * z for z in zs], axis=0)
        ss = jnp.dot(jnp.concatenate(_hi_lo(sq), axis=0), gsum, preferred_element_type=F32)
        ss = ss[:n_qk * t_rows] + ss[n_qk * t_rows:]

        def headnorm_rope(c, gain):
            zn = zs[c] * lax.rsqrt(ss[c * t_rows:(c + 1) * t_rows] * (1.0 / HEAD_DIM) + EPS) * gain
            return zn * cos + pltpu.roll(zn, LANES - 16, 1) * sa + pltpu.roll(zn, 16, 1) * sb

        for c in range(n_qk - 1):
            q_ref[:, c * LANES:(c + 1) * LANES] = (headnorm_rope(c, gq_ref[...]) * 0.125).astype(act)
        k_ref[...] = headnorm_rope(n_qk - 1, gk_ref[...])
        v_ref[...] = proj(640, LANES)
        mq_ref[...] = proj(768, ML_WIDTH).astype(act)
        mk_ref[...] = (proj(1280, ML_WIDTH) * (ML_HEAD_DIM ** -0.5)).astype(act)
        mv_ref[...] = proj(1792, ML_WIDTH).astype(act)
        og_ref[...] = jax.nn.sigmoid(proj(2304, ML_WIDTH))

    @pl.when(tile < n_ctx_tiles)
    def _ctx():
        project(precise_ctx, qa_ref, mqa_ref, mka_ref, mva_ref)

    @pl.when(tile >= n_ctx_tiles)
    def _lat():
        project(False, qb_ref, mqb_ref, mkb_ref, mvb_ref)


def _inproj(xa, xb, mod3, n_ctx_tiles, tiles_per_lat_seq, ctx_row, g_mix, w_main, w_gate, b_gate,
            g_q, g_k, rope):
    w_hi, w_lo = w_main
    precise_ctx = w_lo is not None
    n_ctx, n_lat = xa.shape[0], xb.shape[0]
    n = n_ctx + n_lat
    n_tiles = n // TOKEN_TILE

    def mod_row(i):
        return jnp.where(i < n_ctx_tiles, ctx_row, (i - n_ctx_tiles) // tiles_per_lat_seq)

    def rope_blk(i):
        return jnp.where(i < n_ctx_tiles, 0, 1 + (i - n_ctx_tiles) % tiles_per_lat_seq)

    cos_t, sa_t, sb_t = rope
    lane = np.arange(LANES)
    gsum = jnp.asarray((lane[:, None] // HEAD_DIM) == (lane[None, :] // HEAD_DIM), BF16)
    tok = lambda w: pl.BlockSpec((TOKEN_TILE, w), lambda i: (i, 0))
    full = lambda a: pl.BlockSpec(a.shape, lambda i: (0,) * a.ndim)
    modspec = lambda j: pl.BlockSpec((None, 1, D_MODEL), lambda i: (mod_row(i), 0, j))
    ropespec = pl.BlockSpec((TOKEN_TILE, LANES), lambda i: (rope_blk(i), 0))
    consts = (g_mix.reshape(1, -1), w_hi) + ((w_lo,) if precise_ctx else ()) + (
        w_gate.T, b_gate.reshape(-1, 1),
        jnp.tile(g_q, 2).reshape(1, -1), jnp.tile(g_k, 2).reshape(1, -1),
        jnp.asarray(np.eye(TOKEN_TILE), BF16))
    args = (xa, xb, mod3, mod3) + consts + (cos_t, sa_t, sb_t, gsum)
    in_specs = _two_part_specs(D_MODEL, n_ctx_tiles) + [modspec(0), modspec(1)] \
        + [full(a) for a in consts] + [ropespec, ropespec, ropespec, full(gsum)]
    ctx_act = F32 if precise_ctx else BF16
    pair_shape = lambda w: [jax.ShapeDtypeStruct((n_ctx, w), ctx_act),
                            jax.ShapeDtypeStruct((n_lat, w), BF16)]
    pair_spec = lambda w: _two_part_out_specs(w, n_ctx_tiles)
    out_shape = (
        pair_shape(ATT_WIDTH)
        + [jax.ShapeDtypeStruct((n, LANES), F32),
           jax.ShapeDtypeStruct((n, LANES), F32)]
        + pair_shape(ML_WIDTH) + pair_shape(ML_WIDTH) + pair_shape(ML_WIDTH)
        + [jax.ShapeDtypeStruct((n, ML_WIDTH), F32),
           jax.ShapeDtypeStruct((N_ML_HEADS, n, 4), F32),
           jax.ShapeDtypeStruct((N_ML_HEADS, 8, n), F32)])
    out_specs = (pair_spec(ATT_WIDTH) + [tok(LANES), tok(LANES)]
                 + pair_spec(ML_WIDTH) + pair_spec(ML_WIDTH) + pair_spec(ML_WIDTH)
                 + [tok(ML_WIDTH),
                    pl.BlockSpec((N_ML_HEADS, TOKEN_TILE, 4), lambda i: (0, i, 0)),
                    pl.BlockSpec((N_ML_HEADS, 8, TOKEN_TILE), lambda i: (0, 0, i))])
    outs = pl.pallas_call(
        functools.partial(_inproj_kernel, n_ctx_tiles=n_ctx_tiles, precise_ctx=precise_ctx),
        grid=(n_tiles,), in_specs=in_specs, out_specs=tuple(out_specs),
        out_shape=tuple(out_shape), compiler_params=_params(("arbitrary",)), name="inproj",
    )(*args)
    qa, qb, k, v, mqa, mqb, mka, mkb, mva, mvb, og, gcol, grow = outs
    return (qa, qb), k, v, (mqa, mqb), (mka, mkb), (mva, mvb), og, gcol, grow


def _attn_kernel(*refs, n_kv, precise):
    q_ref = refs[0]
    kv_refs = refs[1:1 + 2 * n_kv]
    o_ref = refs[-1]
    tq = q_ref.shape[0]
    lo_q = lax.broadcasted_iota(jnp.int32, (tq, LANES), 1) < HEAD_DIM
    operand = _hi_lo if precise else (lambda a: a.astype(BF16))
    if precise:
        qk = lambda a, b: _dot_x3(a, b, NT_DIMS)
        pv = _dot_x3
    else:
        qk = lambda a, b: lax.dot_general(a, b, NT_DIMS, preferred_element_type=F32)
        pv = lambda a, b: jnp.dot(a, b, preferred_element_type=F32)

    def dup_half(ref, g):
        a = ref[...]
        r = pltpu.roll(a, HEAD_DIM, 1)
        lo = lax.broadcasted_iota(jnp.int32, a.shape, 1) < HEAD_DIM
        return operand(jnp.where(lo, a, r) if g == 0 else jnp.where(lo, r, a))

    for g in range(N_KV_HEADS):
        ks = [dup_half(kv_refs[2 * p], g) for p in range(n_kv)]
        vs = [dup_half(kv_refs[2 * p + 1], g) for p in range(n_kv)]
        for hb in range(2):
            c0 = (2 * g + hb) * LANES
            qb = q_ref[:, c0:c0 + LANES]
            outs = []
            for half in range(2):
                keep = lo_q if half == 0 else jnp.logical_not(lo_q)
                qm = operand(jnp.where(keep, qb, jnp.zeros_like(qb)))
                ss = [qk(qm, kd) for kd in ks]
                m = functools.reduce(jnp.maximum, [jnp.max(s, axis=1, keepdims=True) for s in ss])
                ps = [jnp.exp(s - m) for s in ss]
                den = functools.reduce(jnp.add, [jnp.sum(p, axis=1, keepdims=True) for p in ps])
                o = functools.reduce(jnp.add, [pv(operand(p), vd) for p, vd in zip(ps, vs)])
                outs.append(o / den)
            o_ref[:, c0:c0 + LANES] = jnp.where(lo_q, outs[0], outs[1]).astype(o_ref.dtype)


def _attention(q, k, v, kv_row0, n_seq, seq_len, cache=None):
    precise = q.dtype == F32
    nq = seq_len // SEQ_BLOCK
    sb0 = kv_row0 // seq_len
    in_specs = [
        pl.BlockSpec((SEQ_BLOCK, ATT_WIDTH), lambda b, i: (b * nq + i, 0)),
        pl.BlockSpec((seq_len, LANES), lambda b, i: (sb0 + b, 0)),
        pl.BlockSpec((seq_len, LANES), lambda b, i: (sb0 + b, 0)),
    ]
    args = [q, k, v]
    n_kv = 1
    if cache is not None:
        ck, cv = cache
        past = ck.shape[0] // n_seq
        in_specs += [pl.BlockSpec((past, LANES), lambda b, i: (b, 0))] * 2
        args += [ck, cv]
        n_kv = 2
    return pl.pallas_call(
        functools.partial(_attn_kernel, n_kv=n_kv, precise=precise),
        grid=(n_seq, nq), in_specs=in_specs,
        out_specs=pl.BlockSpec((SEQ_BLOCK, ATT_WIDTH), lambda b, i: (b * nq + i, 0)),
        out_shape=jax.ShapeDtypeStruct((n_seq * seq_len, ATT_WIDTH), q.dtype),
        compiler_params=_params(("parallel", "parallel")), name="attention",
    )(*args)


def _mlstm_kernel(*refs, heads, **static):
    it = iter(refs)
    q_ref, k_ref, v_ref, og_ref, gmh_ref, gcol_ref, grow_ref, u_ref, l_ref = [next(it) for _ in range(9)]
    init_refs = (next(it), next(it)) if static["has_init"] else ()
    ml_ref = next(it)
    state_refs = (next(it), next(it)) if static["emit_state"] else ()
    for hd in range(heads):
        lanes = pl.ds(hd * ML_HEAD_DIM, ML_HEAD_DIM)
        head_refs = [q_ref.at[:, lanes], k_ref.at[:, lanes], v_ref.at[:, lanes], og_ref.at[:, lanes],
                     gmh_ref.at[:, lanes], gcol_ref.at[hd], grow_ref.at[hd], u_ref, l_ref]
        if init_refs:
            head_refs += [init_refs[0].at[:, hd], init_refs[1].at[hd]]
        head_refs.append(ml_ref.at[:, lanes])
        if state_refs:
            head_refs += [state_refs[0].at[:, hd], state_refs[1].at[hd]]
        _mlstm_head(*head_refs, **static)


def _mlstm_head(*refs, seq_len, has_init, emit_state, precise):
    it = iter(refs)
    q_ref, k_ref, v_ref, og_ref, gmh_ref, gcol_ref, grow_ref, u_ref, l_ref = [next(it) for _ in range(9)]
    if has_init:
        c0_ref, nm0_ref = next(it), next(it)
    ml_ref = next(it)
    if emit_state:
        cf_ref, nmf_ref = next(it), next(it)

    operand = _hi_lo if precise else (lambda a: a.astype(BF16))
    if precise:
        qk = lambda a, b: _dot_x3(a, b, NT_DIMS)
        pv = _dot_x3
    else:
        qk = lambda a, b: lax.dot_general(a, b, NT_DIMS, preferred_element_type=F32)
        pv = lambda a, b: jnp.dot(a, b, preferred_element_type=F32)

    bq = SEQ_BLOCK
    nb = seq_len // bq
    blk = lambda j: slice(j * bq, (j + 1) * bq)
    upper_incl = u_ref[...]
    lower_incl = l_ref[...]

    def tri_dot(x, tri):
        return functools.reduce(jnp.add, [jnp.dot(p, tri, preferred_element_type=F32)
                                          for p in _split3(x)])

    ig_row = [[None] * nb for _ in range(2)]
    lf_row = [[None] * nb for _ in range(2)]
    within = [[None] * nb for _ in range(2)]
    bsum = [[None] * nb for _ in range(2)]
    for j in range(nb):
        g8 = grow_ref[:, blk(j)]
        cum_f = tri_dot(g8, upper_incl)
        cum_b = tri_dot(g8, lower_incl)
        for d in range(2):
            ig_row[d][j] = g8[d:d + 1, :]
            lf_row[d][j] = g8[2 + d:3 + d, :]
            within[d][j] = (cum_f if d == 0 else cum_b)[2 + d:3 + d, :]
            bsum[d][j] = jnp.sum(lf_row[d][j], axis=1, keepdims=True)
    zero11 = jnp.zeros((1, 1), F32)
    r_i = lax.broadcasted_iota(jnp.int32, (bq, bq), 0)
    c_i = lax.broadcasted_iota(jnp.int32, (bq, bq), 1)
    causal = [c_i <= r_i, c_i >= r_i]
    gmh = gmh_ref[...]

    q_blocks = [q_ref[blk(i), :] for i in range(nb)]
    scores = [qk(operand(q_blocks[i]), operand(k_ref[blk(i), :])) for i in range(nb)]

    h_dir = [[None] * nb for _ in range(2)]
    final = [None, None]
    for d in range(2):
        if has_init:
            state = (c0_ref[d], nm0_ref[d:d + 1, :])
            m = nm0_ref[2 + d:3 + d, 0:1]
        else:
            state, m = None, zero11
        scan = range(nb) if d == 0 else range(nb - 1, -1, -1)
        for step, i in enumerate(scan):
            q_i = q_blocks[i]
            a_loc = ig_row[d][i] - within[d][i]
            m_col = jnp.maximum(m, jnp.max(jnp.where(causal[d], a_loc, NEG_INF),
                                           axis=1, keepdims=True))
            if nb > 1:
                widen = lambda col: jnp.broadcast_to(col, (bq, LANES))
                tri = lower_incl if d == 0 else upper_incl
                b_rep = functools.reduce(jnp.add, [
                    lax.dot_general(tri, jnp.broadcast_to(piece, (LANES, bq)), NT_DIMS,
                                    preferred_element_type=F32)
                    for piece in _split3(lf_row[d][i])])
                m_rep = widen(m_col)
                m_wide = jnp.concatenate([m_rep] * (bq // LANES), axis=1)
            else:
                widen = lambda col: col
                b_rep = jnp.sum(jnp.where(causal[d], lf_row[d][i], 0.0), axis=1, keepdims=True)
                m_rep = m_wide = m_col
            p = jnp.exp(jnp.where(causal[d], a_loc - m_wide, NEG_INF)) * scores[i]
            den = widen(jnp.sum(p, axis=1, keepdims=True))
            num = pv(operand(p), operand(v_ref[blk(i), :]))
            if state is not None:
                c_prev, n_prev = state
                w_inter = jnp.exp(m - m_rep)
                q_b = q_i.astype(BF16)
                qc = jnp.dot(q_b, c_prev.astype(BF16), preferred_element_type=F32)
                n_rows = jnp.broadcast_to(n_prev, (LANES, ML_HEAD_DIM)).astype(BF16)
                qn = lax.dot_general(q_b, n_rows, NT_DIMS, preferred_element_type=F32)
                num = num + w_inter * qc
                den = den + w_inter * qn
            nrm = jnp.maximum(jnp.abs(den), jnp.exp(-(b_rep + m_rep)))
            h_dir[d][i] = num / nrm
            if step == nb - 1 and not emit_state:
                break
            m_last = jnp.maximum(m, jnp.max(a_loc, axis=1, keepdims=True))
            a_col = widen(gcol_ref[blk(i), d:d + 1]) - b_rep
            kw = k_ref[blk(i), :].astype(F32) * jnp.exp(a_col - m_last)
            c_new = lax.dot_general(kw.astype(BF16), v_ref[blk(i), :].astype(BF16), TN_DIMS,
                                    preferred_element_type=F32)
            n_new = jnp.sum(kw, axis=0, keepdims=True)
            if state is not None:
                decay = jnp.exp(m - m_last)
                c_new = c_new + decay * state[0]
                n_new = n_new + decay * state[1]
            state = (c_new, n_new)
            m = bsum[d][i] + m_last
        final[d] = (state, m)

    for i in range(nb):
        h = h_dir[0][i] + h_dir[1][i]
        hn = h * lax.rsqrt(jnp.mean(h * h, axis=-1, keepdims=True) + EPS) * gmh
        ml_ref[blk(i), :] = (og_ref[blk(i), :] * hn).astype(ml_ref.dtype)

    if emit_state:
        nmf_ref[...] = jnp.zeros_like(nmf_ref)
        for d in range(2):
            (c_fin, n_fin), m_fin = final[d]
            cf_ref[d] = c_fin
            nmf_ref[d:d + 1, :] = n_fin
            nmf_ref[2 + d:3 + d, :] = jnp.broadcast_to(m_fin, (1, ML_HEAD_DIM))


def _mlstm(mq, mk, mv, og, g_mh, gcol, grow, row0, n_seq, seq_len, init=None, emit_state=False):
    precise = mq.dtype == F32
    assert not (precise and init is not None)
    sb0 = row0 // seq_len
    tri = np.arange(SEQ_BLOCK)
    upper_incl = jnp.asarray(tri[:, None] <= tri[None, :], BF16)
    lower_incl = jnp.asarray(tri[:, None] >= tri[None, :], BF16)
    heads = MLSTM_HEADS_PER_STEP
    width = heads * ML_HEAD_DIM
    ownblk = lambda: pl.BlockSpec((seq_len, width), lambda b, h: (b, h))
    headblk = lambda: pl.BlockSpec((seq_len, width), lambda b, h: (sb0 + b, h))
    const = lambda a: pl.BlockSpec(a.shape, lambda b, h: (0,) * a.ndim)
    in_specs = [ownblk(), ownblk(), ownblk(), headblk(),
                pl.BlockSpec((1, width), lambda b, h: (0, h)),
                pl.BlockSpec((heads, seq_len, 4), lambda b, h: (h, sb0 + b, 0)),
                pl.BlockSpec((heads, 8, seq_len), lambda b, h: (h, 0, sb0 + b)),
                const(upper_incl), const(lower_incl)]
    args = [mq, mk, mv, og, g_mh.reshape(1, -1), gcol, grow, upper_incl, lower_incl]
    if init is not None:
        c0, nm0, layer = init
        in_specs += [
            pl.BlockSpec((None, None, 2, heads, ML_HEAD_DIM, ML_HEAD_DIM),
                         lambda b, h: (b, layer, 0, h, 0, 0)),
            pl.BlockSpec((None, heads, 8, ML_HEAD_DIM), lambda b, h: (b, h, 0, 0))]
        args += [c0, nm0]
    out_shape = [jax.ShapeDtypeStruct((n_seq * seq_len, ML_WIDTH), mq.dtype)]
    out_specs = [pl.BlockSpec((seq_len, width), lambda b, h: (b, h))]
    if emit_state:
        out_shape += [jax.ShapeDtypeStruct((n_seq, 2, N_ML_HEADS, ML_HEAD_DIM, ML_HEAD_DIM), F32),
                      jax.ShapeDtypeStruct((n_seq, N_ML_HEADS, 8, ML_HEAD_DIM), F32)]
        out_specs += [pl.BlockSpec((None, 2, heads, ML_HEAD_DIM, ML_HEAD_DIM),
                                   lambda b, h: (b, 0, h, 0, 0)),
                      pl.BlockSpec((None, heads, 8, ML_HEAD_DIM), lambda b, h: (b, h, 0, 0))]
    return pl.pallas_call(
        functools.partial(_mlstm_kernel, heads=heads, seq_len=seq_len, has_init=init is not None,
                          emit_state=emit_state, precise=precise),
        grid=(n_seq, N_ML_HEADS // heads), in_specs=in_specs, out_specs=tuple(out_specs),
        out_shape=tuple(out_shape),
        compiler_params=_params(("parallel", "parallel")), name="mlstm",
    )(*args)


def _outproj_kernel(*refs, n_ctx_tiles, precise_ctx):
    it = iter(refs)
    atta_ref, attb_ref, mla_ref, mlb_ref, xa_ref, xb_ref, w_ref = [next(it) for _ in range(7)]
    wl_ref = next(it) if precise_ctx else None
    gt_ref, sh_ref, sc_ref, g_ref, wrt_ref, x1_ref, hp_ref, logits_ref, y_scr = it
    i = pl.program_id(0)

    def mix(att, ml):
        return (jnp.dot(att, w_ref[:ATT_WIDTH, :], preferred_element_type=F32)
                + jnp.dot(ml, w_ref[ATT_WIDTH:, :], preferred_element_type=F32))

    @pl.when(i < n_ctx_tiles)
    def _ctx():
        if precise_ctx:
            y_scr[...] = (
                _dot_x3(_hi_lo(atta_ref[...]), (w_ref[:ATT_WIDTH, :], wl_ref[:ATT_WIDTH, :]))
                + _dot_x3(_hi_lo(mla_ref[...]), (w_ref[ATT_WIDTH:, :], wl_ref[ATT_WIDTH:, :])))
        else:
            y_scr[...] = mix(atta_ref[...], mla_ref[...])

    @pl.when(i >= n_ctx_tiles)
    def _lat():
        y_scr[...] = mix(attb_ref[...], mlb_ref[...])

    x1 = _two_part(i, n_ctx_tiles, xa_ref, xb_ref) + gt_ref[...] * y_scr[...]
    x1_ref[...] = x1
    ms = jnp.mean(x1 * x1, axis=-1, keepdims=True)
    h2 = x1 * lax.rsqrt(ms + EPS) * g_ref[...]
    h2 = h2 * (1.0 + sc_ref[...]) + sh_ref[...]
    hp_ref[...] = h2

    logits_ref[...] = lax.dot_general(wrt_ref[...], h2, NT_DIMS, precision=HIGHEST,
                                      preferred_element_type=F32)


def _select_experts(logits, b_col):
    ex = jnp.exp(logits - jnp.max(logits, axis=0, keepdims=True))
    scores = ex / jnp.sum(ex, axis=0, keepdims=True)
    sel = scores + b_col
    row = lambda a, e: a[e:e + 1, :]
    grp_score = []
    for g in range(N_GROUPS):
        xs = [row(sel, g * GROUP_SIZE + j) for j in range(GROUP_SIZE)]
        pairs = [xs[a] + xs[b] for a in range(GROUP_SIZE) for b in range(a + 1, GROUP_SIZE)]
        grp_score.append(functools.reduce(jnp.maximum, pairs))
    best = grp_score[0]
    grp = jnp.zeros_like(best, dtype=jnp.int32)
    for g in range(1, N_GROUPS):
        better = grp_score[g] > best
        grp = jnp.where(better, g, grp)
        best = jnp.where(better, grp_score[g], best)
    pick = lambda a, j: functools.reduce(
        lambda acc, g: jnp.where(grp == g, row(a, g * GROUP_SIZE + j), acc),
        range(1, N_GROUPS), row(a, j))
    xs = [pick(sel, j) for j in range(GROUP_SIZE)]
    ws = [pick(scores, j) for j in range(GROUP_SIZE)]

    def argmax4(vals):
        bv, bi = vals[0], jnp.zeros_like(grp)
        for j in range(1, GROUP_SIZE):
            better = vals[j] > bv
            bi = jnp.where(better, j, bi)
            bv = jnp.where(better, vals[j], bv)
        return bi

    i1 = argmax4(xs)
    i2 = argmax4([jnp.where(i1 == j, NEG_INF, xs[j]) for j in range(GROUP_SIZE)])
    take = lambda vals, idx: functools.reduce(
        lambda acc, j: jnp.where(idx == j, vals[j], acc), range(1, GROUP_SIZE), vals[0])
    w1, w2 = take(ws, i1), take(ws, i2)
    wsum = w1 + w2
    w1, w2 = w1 / wsum, w2 / wsum
    return grp * GROUP_SIZE + i1, grp * GROUP_SIZE + i2, w1, w2


def _outproj(att, ml, x, mod3, n_ctx_tiles, tiles_per_lat_seq, ctx_row, w_out, g_ffn, w_router):
    w_hi, w_lo = w_out
    precise_ctx = w_lo is not None
    n = x[0].shape[0] + x[1].shape[0]

    def mod_row(i):
        return jnp.where(i < n_ctx_tiles, ctx_row, (i - n_ctx_tiles) // tiles_per_lat_seq)

    tok = lambda w: pl.BlockSpec((TOKEN_TILE, w), lambda i: (i, 0))
    full = lambda a: pl.BlockSpec(a.shape, lambda i: (0,) * a.ndim)
    modspec = lambda j: pl.BlockSpec((None, 1, D_MODEL), lambda i: (mod_row(i), 0, j))
    weights = (w_hi, w_lo) if precise_ctx else (w_hi,)
    consts = (g_ffn.reshape(1, -1), w_router.T)
    args = (*att, *ml, *x, *weights, mod3, mod3, mod3) + consts
    in_specs = (_two_part_specs(ATT_WIDTH, n_ctx_tiles) + _two_part_specs(ML_WIDTH, n_ctx_tiles)
                + _two_part_specs(D_MODEL, n_ctx_tiles) + [full(w) for w in weights]
                + [modspec(2), modspec(3), modspec(4)] + [full(a) for a in consts])
    return pl.pallas_call(
        functools.partial(_outproj_kernel, n_ctx_tiles=n_ctx_tiles, precise_ctx=precise_ctx),
        grid=(n // TOKEN_TILE,), in_specs=in_specs,
        scratch_shapes=[pltpu.VMEM((TOKEN_TILE, D_MODEL), F32)],
        out_specs=(tok(D_MODEL), tok(D_MODEL),
                   pl.BlockSpec((N_EXPERTS, TOKEN_TILE), lambda i: (0, i))),
        out_shape=(jax.ShapeDtypeStruct((n, D_MODEL), F32),
                   jax.ShapeDtypeStruct((n, D_MODEL), F32),
                   jax.ShapeDtypeStruct((N_EXPERTS, n), F32)),
        compiler_params=_params(("parallel",)), name="outproj_router",
    )(*args)


def _moe_kernel(order_ref, pos0_ref, pos1_ref, off_ref, cnt_ref,
                h_ref, wg_ref, wu_ref, wd_ref, x1_ref, gt_ref, wcol_ref, ya_ref, yb_ref,
                o_scr, xs_scr, comb0_scr, comb1_scr, *, n_ctx_tiles):
    t = pl.program_id(0)
    s = pl.program_id(1)

    groups = MOE_CHUNK // 8

    def gather_rows(buf, slot0):
        for j in range(MOE_CHUNK):
            src = order_ref[slot0 + j]
            xs_scr[buf, j // 8, pl.ds(j % 8, 1), :] = h_ref[pl.ds(src, 1), :]

    def ffn(buf, base):
        xs = xs_scr[buf].reshape(MOE_CHUNK, D_MODEL).astype(BF16)
        a = jnp.dot(xs, wg_ref[...], preferred_element_type=F32)
        b = jnp.dot(xs, wu_ref[...], preferred_element_type=F32)
        hid = (a * jax.nn.sigmoid(a)) * b
        o_scr[pl.ds(base, MOE_CHUNK), :] = jnp.dot(hid.astype(BF16), wd_ref[...],
                                                   preferred_element_type=F32)

    @pl.when(s == 0)
    def _first():
        gather_rows(0, t * MOE_SLOTS + pl.multiple_of(off_ref[t * N_EXPERTS], 8))

    @pl.when(s < N_EXPERTS)
    def _expert():
        seg = t * N_EXPERTS + s
        off = pl.multiple_of(off_ref[seg], 8)
        cur = s % 2
        nxt_seg = t * N_EXPERTS + jnp.minimum(s + 1, N_EXPERTS - 1)
        ffn(cur, off)
        gather_rows(1 - cur, t * MOE_SLOTS + pl.multiple_of(off_ref[nxt_seg], 8))

        def extra(c, carry):
            base = pl.multiple_of(off + c * MOE_CHUNK, 8)

            def gather8(i, carry2):
                slot = t * MOE_SLOTS + base + i * 8
                for k in range(8):
                    src = order_ref[slot + k]
                    xs_scr[2, i, pl.ds(k, 1), :] = h_ref[pl.ds(src, 1), :]
                return carry2

            lax.fori_loop(0, groups, gather8, 0)
            ffn(2, base)
            return carry

        lax.fori_loop(1, (cnt_ref[seg] + MOE_CHUNK - 1) // MOE_CHUNK, extra, 0)

    @pl.when(s >= N_EXPERTS)
    def _combine():
        tok0 = t * MOE_TILE + (s - N_EXPERTS) * MOE_OUT_TILE

        def body8(i, carry):
            for k in range(8):
                p0 = pos0_ref[tok0 + i * 8 + k]
                p1 = pos1_ref[tok0 + i * 8 + k]
                comb0_scr[i, pl.ds(k, 1), :] = o_scr[pl.ds(p0, 1), :]
                comb1_scr[i, pl.ds(k, 1), :] = o_scr[pl.ds(p1, 1), :]
            return carry

        lax.fori_loop(0, MOE_OUT_TILE // 8, body8, 0)
        wcol = wcol_ref[...]
        comb = (wcol[:, 2:3] * comb0_scr[...].reshape(MOE_OUT_TILE, D_MODEL)
                + wcol[:, 3:4] * comb1_scr[...].reshape(MOE_OUT_TILE, D_MODEL))
        y = x1_ref[...] + gt_ref[...] * comb
        chunk_ix = t * (MOE_TILE // MOE_OUT_TILE) + s - N_EXPERTS

        @pl.when(chunk_ix < n_ctx_tiles)
        def _ctx():
            ya_ref[...] = y

        @pl.when(chunk_ix >= n_ctx_tiles)
        def _lat():
            yb_ref[...] = y


def _route_tables(logits, b_router):
    n = logits.shape[1]
    nt = n // MOE_TILE
    tri = np.arange(SEQ_BLOCK)
    strict_upper = jnp.asarray(tri[:, None] < tri[None, :], BF16)
    tok = np.arange(MOE_TILE)
    digits = np.zeros((MOE_TILE, LANES), np.float32)
    digits[:, 0] = tok % 256
    digits[:, 1] = tok // 256
    pos, order, meta, wcol = pl.pallas_call(
        _route_kernel, grid=(nt,),
        in_specs=[pl.BlockSpec((N_EXPERTS, MOE_TILE), lambda t: (0, t)),
                  pl.BlockSpec((N_EXPERTS, 1), lambda t: (0, 0)),
                  pl.BlockSpec((SEQ_BLOCK, SEQ_BLOCK), lambda t: (0, 0)),
                  pl.BlockSpec((SEQ_BLOCK, SEQ_BLOCK), lambda t: (0, 0)),
                  pl.BlockSpec((MOE_TILE, LANES), lambda t: (0, 0))],
        out_specs=(pl.BlockSpec((8, MOE_TILE), lambda t: (0, t)),
                   pl.BlockSpec((None, MOE_SLOTS, 1), lambda t: (t, 0, 0)),
                   pl.BlockSpec((None, N_EXPERTS, 8), lambda t: (t, 0, 0)),
                   pl.BlockSpec((MOE_TILE, 8), lambda t: (t, 0))),
        out_shape=(jax.ShapeDtypeStruct((8, n), jnp.int32),
                   jax.ShapeDtypeStruct((nt, MOE_SLOTS, 1), jnp.int32),
                   jax.ShapeDtypeStruct((nt, N_EXPERTS, 8), jnp.int32),
                   jax.ShapeDtypeStruct((n, 8), F32)),
        scratch_shapes=[pltpu.VMEM((8, MOE_TILE), F32)],
        compiler_params=_params(("parallel",)), name="route_tables",
    )(logits, b_router.reshape(-1, 1), strict_upper, jnp.asarray(np.eye(SEQ_BLOCK), BF16),
      jnp.asarray(digits, BF16))
    return (order.reshape(-1), pos[0], pos[1], meta[:, :, 0].reshape(-1),
            meta[:, :, 1].reshape(-1), wcol)


def _route_kernel(logits_ref, br_ref, su_ref, eye_ref, digits_ref,
                  pos_ref, order_ref, meta_ref, wcol_ref, wrow_scr):
    e1, e2, w1, w2 = _select_experts(logits_ref[...], br_ref[...])
    wrow_scr[...] = jnp.zeros_like(wrow_scr)
    wrow_scr[2:3, :] = w1
    wrow_scr[3:4, :] = w2
    eye = eye_ref[...]
    for b in range(MOE_TILE // SEQ_BLOCK):
        cols = slice(b * SEQ_BLOCK, (b + 1) * SEQ_BLOCK)
        wcol_ref[cols, :] = functools.reduce(jnp.add, [
            lax.dot_general(eye, p, NT_DIMS, preferred_element_type=F32)
            for p in _split3(wrow_scr[:, cols])])
    eid = lax.broadcasted_iota(jnp.int32, (N_EXPERTS, MOE_TILE), 0)
    oh1, oh2 = eid == e1, eid == e2
    oh = jnp.where(oh1, 1.0, 0.0) + jnp.where(oh2, 1.0, 0.0)
    nblk = MOE_TILE // SEQ_BLOCK
    blocks = [oh[:, b * SEQ_BLOCK:(b + 1) * SEQ_BLOCK] for b in range(nblk)]
    inner = jnp.dot(jnp.concatenate(blocks, axis=0).astype(BF16), su_ref[...],
                    preferred_element_type=F32)
    run = jnp.zeros((N_EXPERTS, 1), F32)
    ranks = []
    for b in range(nblk):
        ranks.append(inner[b * N_EXPERTS:(b + 1) * N_EXPERTS, :] + run)
        run = run + jnp.sum(blocks[b], axis=1, keepdims=True)
    count = run
    seg = jnp.floor((count + 7.0) * 0.125) * 8.0
    sub = lax.broadcasted_iota(jnp.int32, (N_EXPERTS, 1), 0)
    off = jnp.zeros((N_EXPERTS, 1), F32)
    for e in range(N_EXPERTS - 1):
        off = off + jnp.where(sub > e, seg[e:e + 1, :], 0.0)
    slot = jnp.concatenate(ranks, axis=1) + off
    pos1 = jnp.sum(jnp.where(oh1, slot, 0.0), axis=0, keepdims=True).astype(jnp.int32)
    pos2 = jnp.sum(jnp.where(oh2, slot, 0.0), axis=0, keepdims=True).astype(jnp.int32)
    pos_ref[...] = jnp.zeros_like(pos_ref)
    pos_ref[0:1, :] = pos1
    pos_ref[1:2, :] = pos2
    meta_ref[...] = jnp.zeros_like(meta_ref)
    meta_ref[:, 0:1] = off.astype(jnp.int32)
    meta_ref[:, 1:2] = count.astype(jnp.int32)
    digits = digits_ref[...]
    rows = MOE_SLOTS // ROUTE_SLOT_BLOCKS
    for sb in range(ROUTE_SLOT_BLOCKS):
        j = lax.broadcasted_iota(jnp.int32, (rows, MOE_TILE), 0) + sb * rows
        hit = jnp.where(j == pos1, 1.0, 0.0) + jnp.where(j == pos2, 1.0, 0.0)
        d = jnp.dot(hit.astype(BF16), digits, preferred_element_type=F32)
        order_ref[sb * rows:(sb + 1) * rows, :] = (d[:, 0:1] + 256.0 * d[:, 1:2]).astype(jnp.int32)


def _moe(hp, logits, b_router, x1, mod3, layer, n_ctx_tiles, tiles_per_lat_seq, ctx_row,
         wg, wu, wd):
    n = hp.shape[0]
    order, pos0, pos1, off, count, wcol = _route_tables(logits, b_router)
    chunks_per_tile = MOE_TILE // MOE_OUT_TILE
    n_steps = N_EXPERTS + chunks_per_tile

    def chunk_ix(t, s):
        return t * chunks_per_tile + jnp.maximum(s - N_EXPERTS, 0)

    def mod_row(g):
        return jnp.where(g < n_ctx_tiles, ctx_row, (g - n_ctx_tiles) // tiles_per_lat_seq)

    wspec = lambda r, c: pl.BlockSpec(
        (None, None, r, c), lambda t, s, *_: (layer, jnp.minimum(s, N_EXPERTS - 1), 0, 0))
    chunk_spec = pl.BlockSpec((MOE_OUT_TILE, D_MODEL), lambda t, s, *_: (chunk_ix(t, s), 0))
    grid_spec = pltpu.PrefetchScalarGridSpec(
        num_scalar_prefetch=5,
        grid=(n // MOE_TILE, n_steps),
        in_specs=[
            pl.BlockSpec((MOE_TILE, D_MODEL), lambda t, s, *_: (t, 0)),
            wspec(D_MODEL, D_EXPERT), wspec(D_MODEL, D_EXPERT), wspec(D_EXPERT, D_MODEL),
            chunk_spec,
            pl.BlockSpec((None, 1, D_MODEL), lambda t, s, *_: (mod_row(chunk_ix(t, s)), 0, 5)),
            pl.BlockSpec((MOE_OUT_TILE, 8), lambda t, s, *_: (chunk_ix(t, s), 0)),
        ],
        out_specs=(
            pl.BlockSpec((MOE_OUT_TILE, D_MODEL),
                         lambda t, s, *_: (jnp.minimum(chunk_ix(t, s), n_ctx_tiles - 1), 0)),
            pl.BlockSpec((MOE_OUT_TILE, D_MODEL),
                         lambda t, s, *_: (jnp.maximum(chunk_ix(t, s) - n_ctx_tiles, 0), 0))),
        scratch_shapes=[pltpu.VMEM((MOE_SLOTS, D_MODEL), F32),
                        pltpu.VMEM((3, MOE_CHUNK // 8, 8, D_MODEL), F32),
                        pltpu.VMEM((MOE_OUT_TILE // 8, 8, D_MODEL), F32),
                        pltpu.VMEM((MOE_OUT_TILE // 8, 8, D_MODEL), F32)],
    )
    n_ctx = n_ctx_tiles * MOE_OUT_TILE
    return pl.pallas_call(
        functools.partial(_moe_kernel, n_ctx_tiles=n_ctx_tiles), grid_spec=grid_spec,
        out_shape=(jax.ShapeDtypeStruct((n_ctx, D_MODEL), F32),
                   jax.ShapeDtypeStruct((n - n_ctx, D_MODEL), F32)),
        compiler_params=_params(("arbitrary", "arbitrary")), name="experts",
    )(order, pos0, pos1, off, count, hp, wg, wu, wd, x1, mod3, wcol)


def _rope_tables(seq_len):
    half = HEAD_DIM // 2
    freqs = ROPE_BASE ** (-np.arange(0, half, 2, dtype=np.float64) / half)
    pos = np.arange(seq_len)
    row, col = pos // GRID_W, pos % GRID_W
    d = np.arange(HEAD_DIM)
    position = np.where(d[None, :] < half, row[:, None], col[:, None]).astype(np.float64)
    ang = (position.astype(np.float32) * freqs.astype(np.float32)[d % (half // 2)][None, :]).astype(np.float32)
    cos, sin = np.cos(ang), np.sin(ang)
    first = (d % half) < half // 2
    sa = np.where(first[None, :], -sin, 0.0)
    sb = np.where(first[None, :], 0.0, sin)
    ident = lambda v: np.full((TOKEN_TILE, HEAD_DIM), v, np.float32)
    stack = lambda ctx, lat: jnp.asarray(
        np.tile(np.concatenate([ctx, lat.astype(np.float32)], axis=0), (1, 2)), F32)
    return stack(ident(1.0), cos), stack(ident(0.0), sa), stack(ident(0.0), sb)


def kernel(x_prompt, x_sample, c, cache_k, cache_v, state_C, state_n, state_m, c_ctx, w_mod, b_mod,
           g_mix, g_ffn, w_in, b_igate, b_fgate, g_q, g_k, g_mh, w_out, w_router, b_router,
           w_e_gate, w_e_up, w_e_down):
    n_ctx_seq, ctx_len, _ = x_prompt.shape
    n_lat_seq, lat_len, _ = x_sample.shape
    n_layers = w_mod.shape[0]
    n_ctx = n_ctx_seq * ctx_len
    assert ctx_len == SEQ_BLOCK and lat_len % TOKEN_TILE == 0
    assert n_ctx % MOE_TILE == 0 and (n_lat_seq * lat_len) % MOE_TILE == 0
    assert n_lat_seq < 16 and n_ctx % lat_len == 0
    n_ctx_tiles = n_ctx // TOKEN_TILE
    tiles_per_lat_seq = lat_len // TOKEN_TILE
    ctx_row = n_lat_seq

    x = (x_prompt.reshape(n_ctx, D_MODEL), x_sample.reshape(-1, D_MODEL))
    cond = jnp.zeros((16, D_MODEL), F32).at[:n_lat_seq].set(c).at[ctx_row].set(c_ctx)
    mod = _modulation(cond, w_mod, b_mod)
    rope = _rope_tables(lat_len)

    wg_b, wu_b, wd_b = w_e_gate.astype(BF16), w_e_up.astype(BF16), w_e_down.astype(BF16)
    gate_perm = np.array([(q % 2) * N_ML_HEADS + hd + 2 * N_ML_HEADS * (q // 2)
                          for hd in range(N_ML_HEADS) for q in range(4)])

    ks, vs, cs, ns, ms = [], [], [], [], []
    for l in range(n_layers):
        mod3 = mod[l].reshape(16, 1, -1)
        precise_ctx = l < n_layers - 1

        def weight_pair(w):
            hi = w.astype(BF16)
            return hi, ((w - hi.astype(F32)).astype(BF16) if precise_ctx else None)

        w_main = weight_pair(w_in[l, :, :MAIN_WIDTH])
        w_gate = w_in[l, :, MAIN_WIDTH:][:, gate_perm]
        b_gate = jnp.concatenate([b_igate[l].reshape(-1), b_fgate[l].reshape(-1)])[gate_perm]
        q, k, v, mq, mk, mv, og, gcol, grow = _inproj(
            *x, mod3, n_ctx_tiles, tiles_per_lat_seq, ctx_row, g_mix[l], w_main, w_gate, b_gate,
            g_q[l], g_k[l], rope)

        att_ctx = _attention(q[0], k, v, 0, n_ctx_seq, ctx_len)
        past = cache_k.shape[2]
        ck = cache_k[:, l].reshape(n_lat_seq * past, LANES)
        cv = cache_v[:, l].reshape(n_lat_seq * past, LANES)
        att_lat = _attention(q[1], k, v, n_ctx, n_lat_seq, lat_len, cache=(ck, cv))

        ml_ctx, c_fin, nm_fin = _mlstm(mq[0], mk[0], mv[0], og, g_mh[l], gcol, grow, 0,
                                       n_ctx_seq, ctx_len, emit_state=True)
        n0 = state_n[:, l].transpose(0, 2, 1, 3)
        m0 = jnp.broadcast_to(state_m[:, l].transpose(0, 2, 1)[..., None], n0.shape)
        nm0 = jnp.concatenate([n0, m0, jnp.zeros_like(n0), jnp.zeros_like(n0)], axis=2)
        (ml_lat,) = _mlstm(mq[1], mk[1], mv[1], og, g_mh[l], gcol, grow, n_ctx, n_lat_seq,
                           lat_len, init=(state_C, nm0, l))

        x1, hp, logits = _outproj(
            (att_ctx, att_lat), (ml_ctx, ml_lat), x, mod3, n_ctx_tiles, tiles_per_lat_seq, ctx_row,
            weight_pair(w_out[l]), g_ffn[l], w_router)
        x = _moe(hp, logits, b_router, x1, mod3, l, n_ctx // MOE_OUT_TILE,
                 lat_len // MOE_OUT_TILE, ctx_row, wg_b, wu_b, wd_b)

        ks.append(k[:n_ctx].reshape(n_ctx_seq, ctx_len, N_KV_HEADS, HEAD_DIM))
        vs.append(v[:n_ctx].reshape(n_ctx_seq, ctx_len, N_KV_HEADS, HEAD_DIM))
        cs.append(c_fin)
        ns.append(nm_fin[:, :, 0:2, :].transpose(0, 2, 1, 3))
        ms.append(nm_fin[:, :, 2:4, 0].transpose(0, 2, 1))

    y_prompt = x[0].reshape(x_prompt.shape)
    y_sample = x[1].reshape(x_sample.shape)
    return (y_prompt, y_sample, jnp.stack(ks, axis=1), jnp.stack(vs, axis=1),
            jnp.stack(cs, axis=1), jnp.stack(ns, axis=1), jnp.stack(ms, axis=1))
```

```python
import functools

import numpy as np
import jax
import jax.numpy as jnp
from jax import lax
from jax.experimental import pallas as pl
from jax.experimental.pallas import tpu as pltpu

F32 = jnp.float32
BF16 = jnp.bfloat16

D_MODEL = 1024
HEAD_DIM = 64
ATT_WIDTH = 512
N_KV_HEADS = 2
ML_WIDTH = 512
N_ML_HEADS = 4
ML_HEAD_DIM = 128
GRID_W = 64
ROPE_BASE = 10000.0
N_EXPERTS = 16
N_GROUPS = 4
GROUP_SIZE = 4
D_EXPERT = 512
EPS = 1e-6
MAIN_WIDTH = 2816
N_GATE_COLS = 16
LANES = 128
TOKEN_TILE = 512
SEQ_BLOCK = 256
MOE_TILE = 2048
MOE_CHUNK = 320
MLSTM_HEADS_PER_STEP = 4
MOE_OUT_TILE = 256
ROUTE_SLOT_BLOCKS = 9
MOE_SLOTS = -(-(2 * MOE_TILE + 8 * N_EXPERTS + MOE_CHUNK) // (512 * ROUTE_SLOT_BLOCKS)) * 512 * ROUTE_SLOT_BLOCKS
VMEM_LIMIT = 56 * 1024 * 1024
NEG_INF = float("-inf")
HIGHEST = lax.Precision.HIGHEST
NT_DIMS = (((1,), (1,)), ((), ()))
TN_DIMS = (((0,), (0,)), ((), ()))


def _params(semantics):
    return pltpu.CompilerParams(dimension_semantics=semantics, vmem_limit_bytes=VMEM_LIMIT)


def _log_sigmoid(z):
    return jnp.minimum(z, 0.0) - jnp.log1p(jnp.exp(-jnp.abs(z)))


def _split3(x):
    h1 = x.astype(BF16)
    r1 = x - h1.astype(F32)
    h2 = r1.astype(BF16)
    h3 = (r1 - h2.astype(F32)).astype(BF16)
    return h1, h2, h3


def _mod_kernel(cond_ref, w_ref, b_ref, o_ref):
    c = cond_ref[...]
    s = c * jax.nn.sigmoid(c)
    o_ref[...] = _dot_x3(_hi_lo(s), _hi_lo(w_ref[...])) + b_ref[...]


def _modulation(cond, w_mod, b_mod):
    n_layers = w_mod.shape[0]
    n_chunks = w_mod.shape[2] // D_MODEL
    return pl.pallas_call(
        _mod_kernel,
        grid=(n_layers, n_chunks),
        in_specs=[
            pl.BlockSpec((16, D_MODEL), lambda l, j: (0, 0)),
            pl.BlockSpec((None, D_MODEL, D_MODEL), lambda l, j: (l, 0, j)),
            pl.BlockSpec((None, 1, D_MODEL), lambda l, j: (l, 0, j)),
        ],
        out_specs=pl.BlockSpec((None, 16, D_MODEL), lambda l, j: (l, 0, j)),
        out_shape=jax.ShapeDtypeStruct((n_layers, 16, w_mod.shape[2]), F32),
        compiler_params=_params(("parallel", "parallel")),
        name="modulation",
    )(cond, w_mod, b_mod.reshape(n_layers, 1, -1))


def _two_part(i, n_first, a_ref, b_ref):
    return jnp.where(i < n_first, a_ref[...], b_ref[...])


def _hi_lo(x):
    hi = x.astype(BF16)
    return hi, (x - hi.astype(F32)).astype(BF16)


def _dot_x3(a, b, dims=None):
    (ah, al), (bh, bl) = a, b
    if dims is None:
        d = lambda x, y: jnp.dot(x, y, preferred_element_type=F32)
    else:
        d = lambda x, y: lax.dot_general(x, y, dims, preferred_element_type=F32)
    return d(ah, bh) + (d(al, bh) + d(ah, bl))


def _two_part_out_specs(width, n_first, tile=TOKEN_TILE):
    return _two_part_specs(width, n_first, tile)


def _two_part_specs(width, n_first, tile=TOKEN_TILE):
    return [pl.BlockSpec((tile, width), lambda i: (jnp.minimum(i, n_first - 1), 0)),
            pl.BlockSpec((tile, width), lambda i: (jnp.maximum(i - n_first, 0), 0))]


def _inproj_kernel(*refs, n_ctx_tiles, precise_ctx):
    it = iter(refs)
    xa_ref, xb_ref, sh_ref, sc_ref, g_ref, w_ref = [next(it) for _ in range(6)]
    wl_ref = next(it) if precise_ctx else None
    (wgt_ref, wgtl_ref, brow_ref, gq_ref, gk_ref, eye_ref, cos_ref, sa_ref, sb_ref, gsum_ref,
     qa_ref, qb_ref, k_ref, v_ref, mqa_ref, mqb_ref, mka_ref, mkb_ref, mva_ref, mvb_ref,
     og_ref, gc_ref, gr_ref) = it
    tile = pl.program_id(0)
    x = _two_part(tile, n_ctx_tiles, xa_ref, xb_ref)
    ms = jnp.mean(x * x, axis=-1, keepdims=True)
    h = x * lax.rsqrt(ms + EPS) * g_ref[...]
    h = h * (1.0 + sc_ref[...]) + sh_ref[...]
    hb = h.astype(BF16)

    zr = _dot_x3((wgt_ref[...], wgtl_ref[...]), _hi_lo(h), NT_DIMS) + brow_ref[...]
    sub = lax.broadcasted_iota(jnp.int32, zr.shape, 0)
    gr = jnp.where(sub % 4 < 2, zr, _log_sigmoid(zr))
    eye = eye_ref[...]
    gc = functools.reduce(jnp.add, [lax.dot_general(eye, p, NT_DIMS, preferred_element_type=F32)
                                    for p in _split3(gr)])
    gr_ref[...] = jnp.zeros_like(gr_ref)
    for hd in range(N_ML_HEADS):
        gc_ref[hd] = gc[:, 4 * hd:4 * hd + 4]
        gr_ref[hd, 0:4, :] = gr[4 * hd:4 * hd + 4, :]

    cos = cos_ref[...]
    sa = sa_ref[...]
    sb = sb_ref[...]
    gsum = gsum_ref[...]

    def project(precise, q_ref, mq_ref, mk_ref, mv_ref):
        if precise:
            z = _dot_x3(_hi_lo(h), (w_ref[...], wl_ref[...]))
        else:
            z = jnp.dot(hb, w_ref[...], preferred_element_type=F32)
        proj = lambda c0, width: z[:, c0:c0 + width]
        act = F32 if precise else BF16

        n_qk = ATT_WIDTH // LANES + 1
        t_rows = hb.shape[0]
        zqk = proj(0, n_qk * LANES)
        zs = [zqk[:, c * LANES:(c + 1) * LANES] for c in range(n_qk)]
        sq = jnp.concatenate([z * z for z in zs], axis=0)
        ss = jnp.dot(jnp.concatenate(_hi_lo(sq), axis=0), gsum, preferred_element_type=F32)
        ss = ss[:n_qk * t_rows] + ss[n_qk * t_rows:]

        def headnorm_rope(c, gain):
            zn = zs[c] * lax.rsqrt(ss[c * t_rows:(c + 1) * t_rows] * (1.0 / HEAD_DIM) + EPS) * gain
            return zn * cos + pltpu.roll(zn, LANES - 16, 1) * sa + pltpu.roll(zn, 16, 1) * sb

        for c in range(n_qk - 1):
            q_ref[:, c * LANES:(c + 1) * LANES] = (headnorm_rope(c, gq_ref[...]) * 0.125).astype(act)
        k_ref[...] = headnorm_rope(n_qk - 1, gk_ref[...])
        v_ref[...] = proj(640, LANES)
        mq_ref[...] = proj(768, ML_WIDTH).astype(act)
        mk_ref[...] = (proj(1280, ML_WIDTH) * (ML_HEAD_DIM ** -0.5)).astype(act)
        mv_ref[...] = proj(1792, ML_WIDTH).astype(act)
        og_ref[...] = jax.nn.sigmoid(proj(2304, ML_WIDTH))

    @pl.when(tile < n_ctx_tiles)
    def _ctx():
        project(precise_ctx, qa_ref, mqa_ref, mka_ref, mva_ref)

    @pl.when(tile >= n_ctx_tiles)
    def _lat():
        project(False, qb_ref, mqb_ref, mkb_ref, mvb_ref)


def _inproj(xa, xb, mod3, n_ctx_tiles, tiles_per_lat_seq, ctx_row, g_mix, w_main, w_gate, b_gate,
            g_q, g_k, rope):
    w_hi, w_lo = w_main
    precise_ctx = w_lo is not None
    n_ctx, n_lat = xa.shape[0], xb.shape[0]
    n = n_ctx + n_lat
    n_tiles = n // TOKEN_TILE

    def mod_row(i):
        return jnp.where(i < n_ctx_tiles, ctx_row, (i - n_ctx_tiles) // tiles_per_lat_seq)

    def rope_blk(i):
        return jnp.where(i < n_ctx_tiles, 0, 1 + (i - n_ctx_tiles) % tiles_per_lat_seq)

    cos_t, sa_t, sb_t = rope
    lane = np.arange(LANES)
    gsum = jnp.asarray((lane[:, None] // HEAD_DIM) == (lane[None, :] // HEAD_DIM), BF16)
    tok = lambda w: pl.BlockSpec((TOKEN_TILE, w), lambda i: (i, 0))
    full = lambda a: pl.BlockSpec(a.shape, lambda i: (0,) * a.ndim)
    modspec = lambda j: pl.BlockSpec((None, 1, D_MODEL), lambda i: (mod_row(i), 0, j))
    ropespec = pl.BlockSpec((TOKEN_TILE, LANES), lambda i: (rope_blk(i), 0))
    consts = (g_mix.reshape(1, -1), w_hi) + ((w_lo,) if precise_ctx else ()) + (
        *_hi_lo(w_gate.T), b_gate.reshape(-1, 1),
        jnp.tile(g_q, 2).reshape(1, -1), jnp.tile(g_k, 2).reshape(1, -1),
        jnp.asarray(np.eye(TOKEN_TILE), BF16))
    args = (xa, xb, mod3, mod3) + consts + (cos_t, sa_t, sb_t, gsum)
    in_specs = _two_part_specs(D_MODEL, n_ctx_tiles) + [modspec(0), modspec(1)] \
        + [full(a) for a in consts] + [ropespec, ropespec, ropespec, full(gsum)]
    ctx_act = F32 if precise_ctx else BF16
    pair_shape = lambda w: [jax.ShapeDtypeStruct((n_ctx, w), ctx_act),
                            jax.ShapeDtypeStruct((n_lat, w), BF16)]
    pair_spec = lambda w: _two_part_out_specs(w, n_ctx_tiles)
    out_shape = (
        pair_shape(ATT_WIDTH)
        + [jax.ShapeDtypeStruct((n, LANES), F32),
           jax.ShapeDtypeStruct((n, LANES), F32)]
        + pair_shape(ML_WIDTH) + pair_shape(ML_WIDTH) + pair_shape(ML_WIDTH)
        + [jax.ShapeDtypeStruct((n, ML_WIDTH), F32),
           jax.ShapeDtypeStruct((N_ML_HEADS, n, 4), F32),
           jax.ShapeDtypeStruct((N_ML_HEADS, 8, n), F32)])
    out_specs = (pair_spec(ATT_WIDTH) + [tok(LANES), tok(LANES)]
                 + pair_spec(ML_WIDTH) + pair_spec(ML_WIDTH) + pair_spec(ML_WIDTH)
                 + [tok(ML_WIDTH),
                    pl.BlockSpec((N_ML_HEADS, TOKEN_TILE, 4), lambda i: (0, i, 0)),
                    pl.BlockSpec((N_ML_HEADS, 8, TOKEN_TILE), lambda i: (0, 0, i))])
    outs = pl.pallas_call(
        functools.partial(_inproj_kernel, n_ctx_tiles=n_ctx_tiles, precise_ctx=precise_ctx),
        grid=(n_tiles,), in_specs=in_specs, out_specs=tuple(out_specs),
        out_shape=tuple(out_shape), compiler_params=_params(("arbitrary",)), name="inproj",
    )(*args)
    qa, qb, k, v, mqa, mqb, mka, mkb, mva, mvb, og, gcol, grow = outs
    return (qa, qb), k, v, (mqa, mqb), (mka, mkb), (mva, mvb), og, gcol, grow


def _attn_kernel(*refs, n_kv, precise):
    q_ref = refs[0]
    kv_refs = refs[1:1 + 2 * n_kv]
    o_ref = refs[-1]
    tq = q_ref.shape[0]
    lo_q = lax.broadcasted_iota(jnp.int32, (tq, LANES), 1) < HEAD_DIM
    operand = _hi_lo if precise else (lambda a: a.astype(BF16))
    if precise:
        qk = lambda a, b: _dot_x3(a, b, NT_DIMS)
        pv = _dot_x3
    else:
        qk = lambda a, b: lax.dot_general(a, b, NT_DIMS, preferred_element_type=F32)
        pv = lambda a, b: jnp.dot(a, b, preferred_element_type=F32)

    def dup_half(ref, g):
        a = ref[...]
        r = pltpu.roll(a, HEAD_DIM, 1)
        lo = lax.broadcasted_iota(jnp.int32, a.shape, 1) < HEAD_DIM
        return operand(jnp.where(lo, a, r) if g == 0 else jnp.where(lo, r, a))

    for g in range(N_KV_HEADS):
        ks = [dup_half(kv_refs[2 * p], g) for p in range(n_kv)]
        vs = [dup_half(kv_refs[2 * p + 1], g) for p in range(n_kv)]
        for hb in range(2):
            c0 = (2 * g + hb) * LANES
            qb = q_ref[:, c0:c0 + LANES]
            outs = []
            for half in range(2):
                keep = lo_q if half == 0 else jnp.logical_not(lo_q)
                qm = operand(jnp.where(keep, qb, jnp.zeros_like(qb)))
                ss = [qk(qm, kd) for kd in ks]
                m = functools.reduce(jnp.maximum, [jnp.max(s, axis=1, keepdims=True) for s in ss])
                ps = [jnp.exp(s - m) for s in ss]
                den = functools.reduce(jnp.add, [jnp.sum(p, axis=1, keepdims=True) for p in ps])
                o = functools.reduce(jnp.add, [pv(operand(p), vd) for p, vd in zip(ps, vs)])
                outs.append(o / den)
            o_ref[:, c0:c0 + LANES] = jnp.where(lo_q, outs[0], outs[1]).astype(o_ref.dtype)


def _attention(q, k, v, kv_row0, n_seq, seq_len, cache=None):
    precise = q.dtype == F32
    nq = seq_len // SEQ_BLOCK
    sb0 = kv_row0 // seq_len
    in_specs = [
        pl.BlockSpec((SEQ_BLOCK, ATT_WIDTH), lambda b, i: (b * nq + i, 0)),
        pl.BlockSpec((seq_len, LANES), lambda b, i: (sb0 + b, 0)),
        pl.BlockSpec((seq_len, LANES), lambda b, i: (sb0 + b, 0)),
    ]
    args = [q, k, v]
    n_kv = 1
    if cache is not None:
        ck, cv = cache
        past = ck.shape[0] // n_seq
        in_specs += [pl.BlockSpec((past, LANES), lambda b, i: (b, 0))] * 2
        args += [ck, cv]
        n_kv = 2
    return pl.pallas_call(
        functools.partial(_attn_kernel, n_kv=n_kv, precise=precise),
        grid=(n_seq, nq), in_specs=in_specs,
        out_specs=pl.BlockSpec((SEQ_BLOCK, ATT_WIDTH), lambda b, i: (b * nq + i, 0)),
        out_shape=jax.ShapeDtypeStruct((n_seq * seq_len, ATT_WIDTH), q.dtype),
        compiler_params=_params(("parallel", "parallel")), name="attention",
    )(*args)


def _mlstm_kernel(*refs, heads, **static):
    it = iter(refs)
    q_ref, k_ref, v_ref, og_ref, gmh_ref, gcol_ref, grow_ref, u_ref, l_ref = [next(it) for _ in range(9)]
    init_refs = (next(it), next(it)) if static["has_init"] else ()
    ml_ref = next(it)
    state_refs = (next(it), next(it)) if static["emit_state"] else ()
    for hd in range(heads):
        lanes = pl.ds(hd * ML_HEAD_DIM, ML_HEAD_DIM)
        head_refs = [q_ref.at[:, lanes], k_ref.at[:, lanes], v_ref.at[:, lanes], og_ref.at[:, lanes],
                     gmh_ref.at[:, lanes], gcol_ref.at[hd], grow_ref.at[hd], u_ref, l_ref]
        if init_refs:
            head_refs += [init_refs[0].at[:, hd], init_refs[1].at[hd]]
        head_refs.append(ml_ref.at[:, lanes])
        if state_refs:
            head_refs += [state_refs[0].at[:, hd], state_refs[1].at[hd]]
        _mlstm_head(*head_refs, **static)


def _mlstm_head(*refs, seq_len, has_init, emit_state, precise):
    it = iter(refs)
    q_ref, k_ref, v_ref, og_ref, gmh_ref, gcol_ref, grow_ref, u_ref, l_ref = [next(it) for _ in range(9)]
    if has_init:
        c0_ref, nm0_ref = next(it), next(it)
    ml_ref = next(it)
    if emit_state:
        cf_ref, nmf_ref = next(it), next(it)

    operand = _hi_lo if precise else (lambda a: a.astype(BF16))
    if precise:
        qk = lambda a, b: _dot_x3(a, b, NT_DIMS)
        pv = _dot_x3
    else:
        qk = lambda a, b: lax.dot_general(a, b, NT_DIMS, preferred_element_type=F32)
        pv = lambda a, b: jnp.dot(a, b, preferred_element_type=F32)

    bq = SEQ_BLOCK
    nb = seq_len // bq
    blk = lambda j: slice(j * bq, (j + 1) * bq)
    upper_incl = u_ref[...]
    lower_incl = l_ref[...]

    def tri_dot(x, tri):
        return functools.reduce(jnp.add, [jnp.dot(p, tri, preferred_element_type=F32)
                                          for p in _split3(x)])

    ig_row = [[None] * nb for _ in range(2)]
    lf_row = [[None] * nb for _ in range(2)]
    within = [[None] * nb for _ in range(2)]
    bsum = [[None] * nb for _ in range(2)]
    for j in range(nb):
        g8 = grow_ref[:, blk(j)]
        cum_f = tri_dot(g8, upper_incl)
        cum_b = tri_dot(g8, lower_incl)
        for d in range(2):
            ig_row[d][j] = g8[d:d + 1, :]
            lf_row[d][j] = g8[2 + d:3 + d, :]
            within[d][j] = (cum_f if d == 0 else cum_b)[2 + d:3 + d, :]
            bsum[d][j] = jnp.sum(lf_row[d][j], axis=1, keepdims=True)
    zero11 = jnp.zeros((1, 1), F32)
    r_i = lax.broadcasted_iota(jnp.int32, (bq, bq), 0)
    c_i = lax.broadcasted_iota(jnp.int32, (bq, bq), 1)
    causal = [c_i <= r_i, c_i >= r_i]
    gmh = gmh_ref[...]

    q_blocks = [q_ref[blk(i), :] for i in range(nb)]
    scores = [qk(operand(q_blocks[i]), operand(k_ref[blk(i), :])) for i in range(nb)]

    h_dir = [[None] * nb for _ in range(2)]
    final = [None, None]
    for d in range(2):
        if has_init:
            state = (c0_ref[d], nm0_ref[d:d + 1, :])
            m = nm0_ref[2 + d:3 + d, 0:1]
        else:
            state, m = None, zero11
        scan = range(nb) if d == 0 else range(nb - 1, -1, -1)
        for step, i in enumerate(scan):
            q_i = q_blocks[i]
            a_loc = ig_row[d][i] - within[d][i]
            m_col = jnp.maximum(m, jnp.max(jnp.where(causal[d], a_loc, NEG_INF),
                                           axis=1, keepdims=True))
            if nb > 1:
                widen = lambda col: jnp.broadcast_to(col, (bq, LANES))
                tri = lower_incl if d == 0 else upper_incl
                b_rep = functools.reduce(jnp.add, [
                    lax.dot_general(tri, jnp.broadcast_to(piece, (LANES, bq)), NT_DIMS,
                                    preferred_element_type=F32)
                    for piece in _hi_lo(lf_row[d][i])])
                m_rep = widen(m_col)
                m_wide = jnp.concatenate([m_rep] * (bq // LANES), axis=1)
            else:
                widen = lambda col: col
                b_rep = jnp.sum(jnp.where(causal[d], lf_row[d][i], 0.0), axis=1, keepdims=True)
                m_rep = m_wide = m_col
            p = jnp.exp(jnp.where(causal[d], a_loc - m_wide, NEG_INF)) * scores[i]
            den = widen(jnp.sum(p, axis=1, keepdims=True))
            num = pv(operand(p), operand(v_ref[blk(i), :]))
            if state is not None:
                c_prev, n_prev = state
                w_inter = jnp.exp(m - m_rep)
                q_b = q_i.astype(BF16)
                qc = jnp.dot(q_b, c_prev.astype(BF16), preferred_element_type=F32)
                n_rows = jnp.broadcast_to(n_prev, (LANES, ML_HEAD_DIM)).astype(BF16)
                qn = lax.dot_general(q_b, n_rows, NT_DIMS, preferred_element_type=F32)
                num = num + w_inter * qc
                den = den + w_inter * qn
            nrm = jnp.maximum(jnp.abs(den), jnp.exp(-(b_rep + m_rep)))
            h_dir[d][i] = num / nrm
            if step == nb - 1 and not emit_state:
                break
            m_last = jnp.maximum(m, jnp.max(a_loc, axis=1, keepdims=True))
            a_col = widen(gcol_ref[blk(i), d:d + 1]) - b_rep
            kw = k_ref[blk(i), :].astype(F32) * jnp.exp(a_col - m_last)
            c_new = lax.dot_general(kw.astype(BF16), v_ref[blk(i), :].astype(BF16), TN_DIMS,
                                    preferred_element_type=F32)
            n_new = jnp.sum(kw, axis=0, keepdims=True)
            if state is not None:
                decay = jnp.exp(m - m_last)
                c_new = c_new + decay * state[0]
                n_new = n_new + decay * state[1]
            state = (c_new, n_new)
            m = bsum[d][i] + m_last
        final[d] = (state, m)

    for i in range(nb):
        h = h_dir[0][i] + h_dir[1][i]
        hn = h * lax.rsqrt(jnp.mean(h * h, axis=-1, keepdims=True) + EPS) * gmh
        ml_ref[blk(i), :] = (og_ref[blk(i), :] * hn).astype(ml_ref.dtype)

    if emit_state:
        nmf_ref[...] = jnp.zeros_like(nmf_ref)
        for d in range(2):
            (c_fin, n_fin), m_fin = final[d]
            cf_ref[d] = c_fin
            nmf_ref[d:d + 1, :] = n_fin
            nmf_ref[2 + d:3 + d, :] = jnp.broadcast_to(m_fin, (1, ML_HEAD_DIM))


def _mlstm(mq, mk, mv, og, g_mh, gcol, grow, row0, n_seq, seq_len, init=None, emit_state=False):
    precise = mq.dtype == F32
    assert not (precise and init is not None)
    sb0 = row0 // seq_len
    tri = np.arange(SEQ_BLOCK)
    upper_incl = jnp.asarray(tri[:, None] <= tri[None, :], BF16)
    lower_incl = jnp.asarray(tri[:, None] >= tri[None, :], BF16)
    heads = MLSTM_HEADS_PER_STEP
    width = heads * ML_HEAD_DIM
    ownblk = lambda: pl.BlockSpec((seq_len, width), lambda b, h: (b, h))
    headblk = lambda: pl.BlockSpec((seq_len, width), lambda b, h: (sb0 + b, h))
    const = lambda a: pl.BlockSpec(a.shape, lambda b, h: (0,) * a.ndim)
    in_specs = [ownblk(), ownblk(), ownblk(), headblk(),
                pl.BlockSpec((1, width), lambda b, h: (0, h)),
                pl.BlockSpec((heads, seq_len, 4), lambda b, h: (h, sb0 + b, 0)),
                pl.BlockSpec((heads, 8, seq_len), lambda b, h: (h, 0, sb0 + b)),
                const(upper_incl), const(lower_incl)]
    args = [mq, mk, mv, og, g_mh.reshape(1, -1), gcol, grow, upper_incl, lower_incl]
    if init is not None:
        c0, nm0, layer = init
        in_specs += [
            pl.BlockSpec((None, None, 2, heads, ML_HEAD_DIM, ML_HEAD_DIM),
                         lambda b, h: (b, layer, 0, h, 0, 0)),
            pl.BlockSpec((None, heads, 8, ML_HEAD_DIM), lambda b, h: (b, h, 0, 0))]
        args += [c0, nm0]
    out_shape = [jax.ShapeDtypeStruct((n_seq * seq_len, ML_WIDTH), mq.dtype)]
    out_specs = [pl.BlockSpec((seq_len, width), lambda b, h: (b, h))]
    if emit_state:
        out_shape += [jax.ShapeDtypeStruct((n_seq, 2, N_ML_HEADS, ML_HEAD_DIM, ML_HEAD_DIM), F32),
                      jax.ShapeDtypeStruct((n_seq, N_ML_HEADS, 8, ML_HEAD_DIM), F32)]
        out_specs += [pl.BlockSpec((None, 2, heads, ML_HEAD_DIM, ML_HEAD_DIM),
                                   lambda b, h: (b, 0, h, 0, 0)),
                      pl.BlockSpec((None, heads, 8, ML_HEAD_DIM), lambda b, h: (b, h, 0, 0))]
    return pl.pallas_call(
        functools.partial(_mlstm_kernel, heads=heads, seq_len=seq_len, has_init=init is not None,
                          emit_state=emit_state, precise=precise),
        grid=(n_seq, N_ML_HEADS // heads), in_specs=in_specs, out_specs=tuple(out_specs),
        out_shape=tuple(out_shape),
        compiler_params=_params(("parallel", "parallel")), name="mlstm",
    )(*args)


def _outproj_kernel(*refs, n_ctx_tiles, precise_ctx):
    it = iter(refs)
    atta_ref, attb_ref, mla_ref, mlb_ref, xa_ref, xb_ref, w_ref = [next(it) for _ in range(7)]
    wl_ref = next(it) if precise_ctx else None
    gt_ref, sh_ref, sc_ref, g_ref, wrt_ref, wrtl_ref, x1_ref, hp_ref, logits_ref, y_scr = it
    i = pl.program_id(0)

    def mix(att, ml):
        return (jnp.dot(att, w_ref[:ATT_WIDTH, :], preferred_element_type=F32)
                + jnp.dot(ml, w_ref[ATT_WIDTH:, :], preferred_element_type=F32))

    @pl.when(i < n_ctx_tiles)
    def _ctx():
        if precise_ctx:
            y_scr[...] = (
                _dot_x3(_hi_lo(atta_ref[...]), (w_ref[:ATT_WIDTH, :], wl_ref[:ATT_WIDTH, :]))
                + _dot_x3(_hi_lo(mla_ref[...]), (w_ref[ATT_WIDTH:, :], wl_ref[ATT_WIDTH:, :])))
        else:
            y_scr[...] = mix(atta_ref[...], mla_ref[...])

    @pl.when(i >= n_ctx_tiles)
    def _lat():
        y_scr[...] = mix(attb_ref[...], mlb_ref[...])

    x1 = _two_part(i, n_ctx_tiles, xa_ref, xb_ref) + gt_ref[...] * y_scr[...]
    x1_ref[...] = x1
    ms = jnp.mean(x1 * x1, axis=-1, keepdims=True)
    h2 = x1 * lax.rsqrt(ms + EPS) * g_ref[...]
    h2 = h2 * (1.0 + sc_ref[...]) + sh_ref[...]
    hp_ref[...] = h2

    logits_ref[...] = _dot_x3((wrt_ref[...], wrtl_ref[...]), _hi_lo(h2), NT_DIMS)


def _select_experts(logits, b_col):
    ex = jnp.exp(logits - jnp.max(logits, axis=0, keepdims=True))
    scores = ex / jnp.sum(ex, axis=0, keepdims=True)
    sel = scores + b_col
    row = lambda a, e: a[e:e + 1, :]
    grp_score = []
    for g in range(N_GROUPS):
        xs = [row(sel, g * GROUP_SIZE + j) for j in range(GROUP_SIZE)]
        pairs = [xs[a] + xs[b] for a in range(GROUP_SIZE) for b in range(a + 1, GROUP_SIZE)]
        grp_score.append(functools.reduce(jnp.maximum, pairs))
    best = grp_score[0]
    grp = jnp.zeros_like(best, dtype=jnp.int32)
    for g in range(1, N_GROUPS):
        better = grp_score[g] > best
        grp = jnp.where(better, g, grp)
        best = jnp.where(better, grp_score[g], best)
    pick = lambda a, j: functools.reduce(
        lambda acc, g: jnp.where(grp == g, row(a, g * GROUP_SIZE + j), acc),
        range(1, N_GROUPS), row(a, j))
    xs = [pick(sel, j) for j in range(GROUP_SIZE)]
    ws = [pick(scores, j) for j in range(GROUP_SIZE)]

    def argmax4(vals):
        bv, bi = vals[0], jnp.zeros_like(grp)
        for j in range(1, GROUP_SIZE):
            better = vals[j] > bv
            bi = jnp.where(better, j, bi)
            bv = jnp.where(better, vals[j], bv)
        return bi

    i1 = argmax4(xs)
    i2 = argmax4([jnp.where(i1 == j, NEG_INF, xs[j]) for j in range(GROUP_SIZE)])
    take = lambda vals, idx: functools.reduce(
        lambda acc, j: jnp.where(idx == j, vals[j], acc), range(1, GROUP_SIZE), vals[0])
    w1, w2 = take(ws, i1), take(ws, i2)
    wsum = w1 + w2
    w1, w2 = w1 / wsum, w2 / wsum
    return grp * GROUP_SIZE + i1, grp * GROUP_SIZE + i2, w1, w2


def _outproj(att, ml, x, mod3, n_ctx_tiles, tiles_per_lat_seq, ctx_row, w_out, g_ffn, w_router):
    w_hi, w_lo = w_out
    precise_ctx = w_lo is not None
    n = x[0].shape[0] + x[1].shape[0]

    def mod_row(i):
        return jnp.where(i < n_ctx_tiles, ctx_row, (i - n_ctx_tiles) // tiles_per_lat_seq)

    tok = lambda w: pl.BlockSpec((TOKEN_TILE, w), lambda i: (i, 0))
    full = lambda a: pl.BlockSpec(a.shape, lambda i: (0,) * a.ndim)
    modspec = lambda j: pl.BlockSpec((None, 1, D_MODEL), lambda i: (mod_row(i), 0, j))
    weights = (w_hi, w_lo) if precise_ctx else (w_hi,)
    consts = (g_ffn.reshape(1, -1), *_hi_lo(w_router.T))
    args = (*att, *ml, *x, *weights, mod3, mod3, mod3) + consts
    in_specs = (_two_part_specs(ATT_WIDTH, n_ctx_tiles) + _two_part_specs(ML_WIDTH, n_ctx_tiles)
                + _two_part_specs(D_MODEL, n_ctx_tiles) + [full(w) for w in weights]
                + [modspec(2), modspec(3), modspec(4)] + [full(a) for a in consts])
    return pl.pallas_call(
        functools.partial(_outproj_kernel, n_ctx_tiles=n_ctx_tiles, precise_ctx=precise_ctx),
        grid=(n // TOKEN_TILE,), in_specs=in_specs,
        scratch_shapes=[pltpu.VMEM((TOKEN_TILE, D_MODEL), F32)],
        out_specs=(tok(D_MODEL), tok(D_MODEL),
                   pl.BlockSpec((N_EXPERTS, TOKEN_TILE), lambda i: (0, i))),
        out_shape=(jax.ShapeDtypeStruct((n, D_MODEL), F32),
                   jax.ShapeDtypeStruct((n, D_MODEL), F32),
                   jax.ShapeDtypeStruct((N_EXPERTS, n), F32)),
        compiler_params=_params(("parallel",)), name="outproj_router",
    )(*args)


def _moe_kernel(order_ref, pos0_ref, pos1_ref, off_ref, cnt_ref,
                h_ref, wg_ref, wu_ref, wd_ref, x1_ref, gt_ref, wcol_ref, ya_ref, yb_ref,
                o_scr, xs_scr, comb0_scr, comb1_scr, *, n_ctx_tiles):
    t = pl.program_id(0)
    s = pl.program_id(1)

    groups = MOE_CHUNK // 8

    def gather_rows(buf, slot0):
        for j in range(MOE_CHUNK):
            src = order_ref[slot0 + j]
            xs_scr[buf, j // 8, pl.ds(j % 8, 1), :] = h_ref[pl.ds(src, 1), :]

    def ffn(buf, base):
        xs = xs_scr[buf].reshape(MOE_CHUNK, D_MODEL).astype(BF16)
        a = jnp.dot(xs, wg_ref[...], preferred_element_type=F32)
        b = jnp.dot(xs, wu_ref[...], preferred_element_type=F32)
        hid = (a * jax.nn.sigmoid(a)) * b
        o_scr[pl.ds(base, MOE_CHUNK), :] = jnp.dot(hid.astype(BF16), wd_ref[...],
                                                   preferred_element_type=F32)

    @pl.when(s == 0)
    def _first():
        gather_rows(0, t * MOE_SLOTS + pl.multiple_of(off_ref[t * N_EXPERTS], 8))

    @pl.when(s < N_EXPERTS)
    def _expert():
        seg = t * N_EXPERTS + s
        off = pl.multiple_of(off_ref[seg], 8)
        cur = s % 2
        nxt_seg = t * N_EXPERTS + jnp.minimum(s + 1, N_EXPERTS - 1)
        ffn(cur, off)
        gather_rows(1 - cur, t * MOE_SLOTS + pl.multiple_of(off_ref[nxt_seg], 8))

        def extra(c, carry):
            base = pl.multiple_of(off + c * MOE_CHUNK, 8)

            def gather8(i, carry2):
                slot = t * MOE_SLOTS + base + i * 8
                for k in range(8):
                    src = order_ref[slot + k]
                    xs_scr[2, i, pl.ds(k, 1), :] = h_ref[pl.ds(src, 1), :]
                return carry2

            lax.fori_loop(0, groups, gather8, 0)
            ffn(2, base)
            return carry

        lax.fori_loop(1, (cnt_ref[seg] + MOE_CHUNK - 1) // MOE_CHUNK, extra, 0)

    @pl.when(s >= N_EXPERTS)
    def _combine():
        tok0 = t * MOE_TILE + (s - N_EXPERTS) * MOE_OUT_TILE

        def body8(i, carry):
            for k in range(8):
                p0 = pos0_ref[tok0 + i * 8 + k]
                p1 = pos1_ref[tok0 + i * 8 + k]
                comb0_scr[i, pl.ds(k, 1), :] = o_scr[pl.ds(p0, 1), :]
                comb1_scr[i, pl.ds(k, 1), :] = o_scr[pl.ds(p1, 1), :]
            return carry

        lax.fori_loop(0, MOE_OUT_TILE // 8, body8, 0)
        wcol = wcol_ref[...]
        comb = (wcol[:, 2:3] * comb0_scr[...].reshape(MOE_OUT_TILE, D_MODEL)
                + wcol[:, 3:4] * comb1_scr[...].reshape(MOE_OUT_TILE, D_MODEL))
        y = x1_ref[...] + gt_ref[...] * comb
        chunk_ix = t * (MOE_TILE // MOE_OUT_TILE) + s - N_EXPERTS

        @pl.when(chunk_ix < n_ctx_tiles)
        def _ctx():
            ya_ref[...] = y

        @pl.when(chunk_ix >= n_ctx_tiles)
        def _lat():
            yb_ref[...] = y


def _route_tables(logits, b_router):
    n = logits.shape[1]
    nt = n // MOE_TILE
    tri = np.arange(SEQ_BLOCK)
    strict_upper = jnp.asarray(tri[:, None] < tri[None, :], BF16)
    tok = np.arange(MOE_TILE)
    digits = np.zeros((MOE_TILE, LANES), np.float32)
    digits[:, 0] = tok % 256
    digits[:, 1] = tok // 256
    pos, order, meta, wcol = pl.pallas_call(
        _route_kernel, grid=(nt,),
        in_specs=[pl.BlockSpec((N_EXPERTS, MOE_TILE), lambda t: (0, t)),
                  pl.BlockSpec((N_EXPERTS, 1), lambda t: (0, 0)),
                  pl.BlockSpec((SEQ_BLOCK, SEQ_BLOCK), lambda t: (0, 0)),
                  pl.BlockSpec((SEQ_BLOCK, SEQ_BLOCK), lambda t: (0, 0)),
                  pl.BlockSpec((MOE_TILE, LANES), lambda t: (0, 0))],
        out_specs=(pl.BlockSpec((8, MOE_TILE), lambda t: (0, t)),
                   pl.BlockSpec((None, MOE_SLOTS, 1), lambda t: (t, 0, 0)),
                   pl.BlockSpec((None, N_EXPERTS, 8), lambda t: (t, 0, 0)),
                   pl.BlockSpec((MOE_TILE, 8), lambda t: (t, 0))),
        out_shape=(jax.ShapeDtypeStruct((8, n), jnp.int32),
                   jax.ShapeDtypeStruct((nt, MOE_SLOTS, 1), jnp.int32),
                   jax.ShapeDtypeStruct((nt, N_EXPERTS, 8), jnp.int32),
                   jax.ShapeDtypeStruct((n, 8), F32)),
        scratch_shapes=[pltpu.VMEM((8, MOE_TILE), F32)],
        compiler_params=_params(("parallel",)), name="route_tables",
    )(logits, b_router.reshape(-1, 1), strict_upper, jnp.asarray(np.eye(SEQ_BLOCK), BF16),
      jnp.asarray(digits, BF16))
    return (order.reshape(-1), pos[0], pos[1], meta[:, :, 0].reshape(-1),
            meta[:, :, 1].reshape(-1), wcol)


def _route_kernel(logits_ref, br_ref, su_ref, eye_ref, digits_ref,
                  pos_ref, order_ref, meta_ref, wcol_ref, wrow_scr):
    e1, e2, w1, w2 = _select_experts(logits_ref[...], br_ref[...])
    wrow_scr[...] = jnp.zeros_like(wrow_scr)
    wrow_scr[2:3, :] = w1
    wrow_scr[3:4, :] = w2
    eye = eye_ref[...]
    for b in range(MOE_TILE // SEQ_BLOCK):
        cols = slice(b * SEQ_BLOCK, (b + 1) * SEQ_BLOCK)
        wcol_ref[cols, :] = functools.reduce(jnp.add, [
            lax.dot_general(eye, p, NT_DIMS, preferred_element_type=F32)
            for p in _split3(wrow_scr[:, cols])])
    eid = lax.broadcasted_iota(jnp.int32, (N_EXPERTS, MOE_TILE), 0)
    oh1, oh2 = eid == e1, eid == e2
    oh = jnp.where(oh1, 1.0, 0.0) + jnp.where(oh2, 1.0, 0.0)
    nblk = MOE_TILE // SEQ_BLOCK
    blocks = [oh[:, b * SEQ_BLOCK:(b + 1) * SEQ_BLOCK] for b in range(nblk)]
    inner = jnp.dot(jnp.concatenate(blocks, axis=0).astype(BF16), su_ref[...],
                    preferred_element_type=F32)
    run = jnp.zeros((N_EXPERTS, 1), F32)
    ranks = []
    for b in range(nblk):
        ranks.append(inner[b * N_EXPERTS:(b + 1) * N_EXPERTS, :] + run)
        run = run + jnp.sum(blocks[b], axis=1, keepdims=True)
    count = run
    seg = jnp.floor((count + 7.0) * 0.125) * 8.0
    sub = lax.broadcasted_iota(jnp.int32, (N_EXPERTS, 1), 0)
    off = jnp.zeros((N_EXPERTS, 1), F32)
    for e in range(N_EXPERTS - 1):
        off = off + jnp.where(sub > e, seg[e:e + 1, :], 0.0)
    slot = jnp.concatenate(ranks, axis=1) + off
    pos1 = jnp.sum(jnp.where(oh1, slot, 0.0), axis=0, keepdims=True).astype(jnp.int32)
    pos2 = jnp.sum(jnp.where(oh2, slot, 0.0), axis=0, keepdims=True).astype(jnp.int32)
    pos_ref[...] = jnp.zeros_like(pos_ref)
    pos_ref[0:1, :] = pos1
    pos_ref[1:2, :] = pos2
    meta_ref[...] = jnp.zeros_like(meta_ref)
    meta_ref[:, 0:1] = off.astype(jnp.int32)
    meta_ref[:, 1:2] = count.astype(jnp.int32)
    digits = digits_ref[...]
    rows = MOE_SLOTS // ROUTE_SLOT_BLOCKS
    for sb in range(ROUTE_SLOT_BLOCKS):
        j = lax.broadcasted_iota(jnp.int32, (rows, MOE_TILE), 0) + sb * rows
        hit = jnp.where(j == pos1, 1.0, 0.0) + jnp.where(j == pos2, 1.0, 0.0)
        d = jnp.dot(hit.astype(BF16), digits, preferred_element_type=F32)
        order_ref[sb * rows:(sb + 1) * rows, :] = (d[:, 0:1] + 256.0 * d[:, 1:2]).astype(jnp.int32)


def _moe(hp, logits, b_router, x1, mod3, layer, n_ctx_tiles, tiles_per_lat_seq, ctx_row,
         wg, wu, wd):
    n = hp.shape[0]
    order, pos0, pos1, off, count, wcol = _route_tables(logits, b_router)
    chunks_per_tile = MOE_TILE // MOE_OUT_TILE
    n_steps = N_EXPERTS + chunks_per_tile

    def chunk_ix(t, s):
        return t * chunks_per_tile + jnp.maximum(s - N_EXPERTS, 0)

    def mod_row(g):
        return jnp.where(g < n_ctx_tiles, ctx_row, (g - n_ctx_tiles) // tiles_per_lat_seq)

    wspec = lambda r, c: pl.BlockSpec(
        (None, None, r, c), lambda t, s, *_: (layer, jnp.minimum(s, N_EXPERTS - 1), 0, 0))
    chunk_spec = pl.BlockSpec((MOE_OUT_TILE, D_MODEL), lambda t, s, *_: (chunk_ix(t, s), 0))
    grid_spec = pltpu.PrefetchScalarGridSpec(
        num_scalar_prefetch=5,
        grid=(n // MOE_TILE, n_steps),
        in_specs=[
            pl.BlockSpec((MOE_TILE, D_MODEL), lambda t, s, *_: (t, 0)),
            wspec(D_MODEL, D_EXPERT), wspec(D_MODEL, D_EXPERT), wspec(D_EXPERT, D_MODEL),
            chunk_spec,
            pl.BlockSpec((None, 1, D_MODEL), lambda t, s, *_: (mod_row(chunk_ix(t, s)), 0, 5)),
            pl.BlockSpec((MOE_OUT_TILE, 8), lambda t, s, *_: (chunk_ix(t, s), 0)),
        ],
        out_specs=(
            pl.BlockSpec((MOE_OUT_TILE, D_MODEL),
                         lambda t, s, *_: (jnp.minimum(chunk_ix(t, s), n_ctx_tiles - 1), 0)),
            pl.BlockSpec((MOE_OUT_TILE, D_MODEL),
                         lambda t, s, *_: (jnp.maximum(chunk_ix(t, s) - n_ctx_tiles, 0), 0))),
        scratch_shapes=[pltpu.VMEM((MOE_SLOTS, D_MODEL), F32),
                        pltpu.VMEM((3, MOE_CHUNK // 8, 8, D_MODEL), F32),
                        pltpu.VMEM((MOE_OUT_TILE // 8, 8, D_MODEL), F32),
                        pltpu.VMEM((MOE_OUT_TILE // 8, 8, D_MODEL), F32)],
    )
    n_ctx = n_ctx_tiles * MOE_OUT_TILE
    return pl.pallas_call(
        functools.partial(_moe_kernel, n_ctx_tiles=n_ctx_tiles), grid_spec=grid_spec,
        out_shape=(jax.ShapeDtypeStruct((n_ctx, D_MODEL), F32),
                   jax.ShapeDtypeStruct((n - n_ctx, D_MODEL), F32)),
        compiler_params=_params(("arbitrary", "arbitrary")), name="experts",
    )(order, pos0, pos1, off, count, hp, wg, wu, wd, x1, mod3, wcol)


def _rope_tables(seq_len):
    half = HEAD_DIM // 2
    freqs = ROPE_BASE ** (-np.arange(0, half, 2, dtype=np.float64) / half)
    pos = np.arange(seq_len)
    row, col = pos // GRID_W, pos % GRID_W
    d = np.arange(HEAD_DIM)
    position = np.where(d[None, :] < half, row[:, None], col[:, None]).astype(np.float64)
    ang = (position.astype(np.float32) * freqs.astype(np.float32)[d % (half // 2)][None, :]).astype(np.float32)
    cos, sin = np.cos(ang), np.sin(ang)
    first = (d % half) < half // 2
    sa = np.where(first[None, :], -sin, 0.0)
    sb = np.where(first[None, :], 0.0, sin)
    ident = lambda v: np.full((TOKEN_TILE, HEAD_DIM), v, np.float32)
    stack = lambda ctx, lat: jnp.asarray(
        np.tile(np.concatenate([ctx, lat.astype(np.float32)], axis=0), (1, 2)), F32)
    return stack(ident(1.0), cos), stack(ident(0.0), sa), stack(ident(0.0), sb)


def kernel(x_prompt, x_sample, c, cache_k, cache_v, state_C, state_n, state_m, c_ctx, w_mod, b_mod,
           g_mix, g_ffn, w_in, b_igate, b_fgate, g_q, g_k, g_mh, w_out, w_router, b_router,
           w_e_gate, w_e_up, w_e_down):
    n_ctx_seq, ctx_len, _ = x_prompt.shape
    n_lat_seq, lat_len, _ = x_sample.shape
    n_layers = w_mod.shape[0]
    n_ctx = n_ctx_seq * ctx_len
    assert ctx_len == SEQ_BLOCK and lat_len % TOKEN_TILE == 0
    assert n_ctx % MOE_TILE == 0 and (n_lat_seq * lat_len) % MOE_TILE == 0
    assert n_lat_seq < 16 and n_ctx % lat_len == 0
    n_ctx_tiles = n_ctx // TOKEN_TILE
    tiles_per_lat_seq = lat_len // TOKEN_TILE
    ctx_row = n_lat_seq

    x = (x_prompt.reshape(n_ctx, D_MODEL), x_sample.reshape(-1, D_MODEL))
    cond = jnp.zeros((16, D_MODEL), F32).at[:n_lat_seq].set(c).at[ctx_row].set(c_ctx)
    mod = _modulation(cond, w_mod, b_mod)
    rope = _rope_tables(lat_len)

    wg_b, wu_b, wd_b = w_e_gate.astype(BF16), w_e_up.astype(BF16), w_e_down.astype(BF16)
    gate_perm = np.array([(q % 2) * N_ML_HEADS + hd + 2 * N_ML_HEADS * (q // 2)
                          for hd in range(N_ML_HEADS) for q in range(4)])

    ks, vs, cs, ns, ms = [], [], [], [], []
    for l in range(n_layers):
        mod3 = mod[l].reshape(16, 1, -1)
        precise_ctx = l < n_layers - 1

        def weight_pair(w):
            hi = w.astype(BF16)
            return hi, ((w - hi.astype(F32)).astype(BF16) if precise_ctx else None)

        w_main = weight_pair(w_in[l, :, :MAIN_WIDTH])
        w_gate = w_in[l, :, MAIN_WIDTH:][:, gate_perm]
        b_gate = jnp.concatenate([b_igate[l].reshape(-1), b_fgate[l].reshape(-1)])[gate_perm]
        q, k, v, mq, mk, mv, og, gcol, grow = _inproj(
            *x, mod3, n_ctx_tiles, tiles_per_lat_seq, ctx_row, g_mix[l], w_main, w_gate, b_gate,
            g_q[l], g_k[l], rope)

        att_ctx = _attention(q[0], k, v, 0, n_ctx_seq, ctx_len)
        past = cache_k.shape[2]
        ck = cache_k[:, l].reshape(n_lat_seq * past, LANES)
        cv = cache_v[:, l].reshape(n_lat_seq * past, LANES)
        att_lat = _attention(q[1], k, v, n_ctx, n_lat_seq, lat_len, cache=(ck, cv))

        ml_ctx, c_fin, nm_fin = _mlstm(mq[0], mk[0], mv[0], og, g_mh[l], gcol, grow, 0,
                                       n_ctx_seq, ctx_len, emit_state=True)
        n0 = state_n[:, l].transpose(0, 2, 1, 3)
        m0 = jnp.broadcast_to(state_m[:, l].transpose(0, 2, 1)[..., None], n0.shape)
        nm0 = jnp.concatenate([n0, m0, jnp.zeros_like(n0), jnp.zeros_like(n0)], axis=2)
        (ml_lat,) = _mlstm(mq[1], mk[1], mv[1], og, g_mh[l], gcol, grow, n_ctx, n_lat_seq,
                           lat_len, init=(state_C, nm0, l))

        x1, hp, logits = _outproj(
            (att_ctx, att_lat), (ml_ctx, ml_lat), x, mod3, n_ctx_tiles, tiles_per_lat_seq, ctx_row,
            weight_pair(w_out[l]), g_ffn[l], w_router)
        x = _moe(hp, logits, b_router, x1, mod3, l, n_ctx // MOE_OUT_TILE,
                 lat_len // MOE_OUT_TILE, ctx_row, wg_b, wu_b, wd_b)

        ks.append(k[:n_ctx].reshape(n_ctx_seq, ctx_len, N_KV_HEADS, HEAD_DIM))
        vs.append(v[:n_ctx].reshape(n_ctx_seq, ctx_len, N_KV_HEADS, HEAD_DIM))
        cs.append(c_fin)
        ns.append(nm_fin[:, :, 0:2, :].transpose(0, 2, 1, 3))
        ms.append(nm_fin[:, :, 2:4, 0].transpose(0, 2, 1))

    y_prompt = x[0].reshape(x_prompt.shape)
    y_sample = x[1].reshape(x_sample.shape)
    return (y_prompt, y_sample, jnp.stack(ks, axis=1), jnp.stack(vs, axis=1),
            jnp.stack(cs, axis=1), jnp.stack(ns, axis=1), jnp.stack(ms, axis=1))
```

```python
import functools

import numpy as np
import jax
import jax.numpy as jnp
from jax import lax
from jax.experimental import pallas as pl
from jax.experimental.pallas import tpu as pltpu

F32 = jnp.float32
BF16 = jnp.bfloat16

D_MODEL = 1024
HEAD_DIM = 64
ATT_WIDTH = 512
N_KV_HEADS = 2
ML_WIDTH = 512
N_ML_HEADS = 4
ML_HEAD_DIM = 128
GRID_W = 64
ROPE_BASE = 10000.0
N_EXPERTS = 16
N_GROUPS = 4
GROUP_SIZE = 4
D_EXPERT = 512
EPS = 1e-6
MAIN_WIDTH = 2816
N_GATE_COLS = 16
LANES = 128
TOKEN_TILE = 512
SEQ_BLOCK = 256
MOE_TILE = 2048
MOE_CHUNK = 320
ROW_TILE_SUBLANES = D_MODEL // LANES
MLSTM_HEADS_PER_STEP = 4
MOE_OUT_TILE = 256
ROUTE_SLOT_BLOCKS = 9
MOE_SLOTS = -(-(2 * MOE_TILE + 8 * N_EXPERTS + MOE_CHUNK) // (512 * ROUTE_SLOT_BLOCKS)) * 512 * ROUTE_SLOT_BLOCKS
VMEM_LIMIT = 56 * 1024 * 1024
NEG_INF = float("-inf")
HIGHEST = lax.Precision.HIGHEST
NT_DIMS = (((1,), (1,)), ((), ()))
TN_DIMS = (((0,), (0,)), ((), ()))


def _params(semantics):
    return pltpu.CompilerParams(dimension_semantics=semantics, vmem_limit_bytes=VMEM_LIMIT)


def _log_sigmoid(z):
    return jnp.minimum(z, 0.0) - jnp.log1p(jnp.exp(-jnp.abs(z)))


def _split3(x):
    h1 = x.astype(BF16)
    r1 = x - h1.astype(F32)
    h2 = r1.astype(BF16)
    h3 = (r1 - h2.astype(F32)).astype(BF16)
    return h1, h2, h3


def _mod_kernel(cond_ref, w_ref, b_ref, o_ref):
    c = cond_ref[...]
    s = c * jax.nn.sigmoid(c)
    o_ref[...] = _dot_x3(_hi_lo(s), _hi_lo(w_ref[...])) + b_ref[...]


def _modulation(cond, w_mod, b_mod):
    n_layers = w_mod.shape[0]
    n_chunks = w_mod.shape[2] // D_MODEL
    return pl.pallas_call(
        _mod_kernel,
        grid=(n_layers, n_chunks),
        in_specs=[
            pl.BlockSpec((16, D_MODEL), lambda l, j: (0, 0)),
            pl.BlockSpec((None, D_MODEL, D_MODEL), lambda l, j: (l, 0, j)),
            pl.BlockSpec((None, 1, D_MODEL), lambda l, j: (l, 0, j)),
        ],
        out_specs=pl.BlockSpec((None, 16, D_MODEL), lambda l, j: (l, 0, j)),
        out_shape=jax.ShapeDtypeStruct((n_layers, 16, w_mod.shape[2]), F32),
        compiler_params=_params(("parallel", "parallel")),
        name="modulation",
    )(cond, w_mod, b_mod.reshape(n_layers, 1, -1))


def _two_part(i, n_first, a_ref, b_ref):
    return jnp.where(i < n_first, a_ref[...], b_ref[...])


def _hi_lo(x):
    hi = x.astype(BF16)
    return hi, (x - hi.astype(F32)).astype(BF16)


def _dot_x3(a, b, dims=None):
    (ah, al), (bh, bl) = a, b
    if dims is None:
        d = lambda x, y: jnp.dot(x, y, preferred_element_type=F32)
    else:
        d = lambda x, y: lax.dot_general(x, y, dims, preferred_element_type=F32)
    return d(ah, bh) + (d(al, bh) + d(ah, bl))


def _two_part_out_specs(width, n_first, tile=TOKEN_TILE):
    return _two_part_specs(width, n_first, tile)


def _two_part_specs(width, n_first, tile=TOKEN_TILE):
    return [pl.BlockSpec((tile, width), lambda i: (jnp.minimum(i, n_first - 1), 0)),
            pl.BlockSpec((tile, width), lambda i: (jnp.maximum(i - n_first, 0), 0))]


def _inproj_kernel(*refs, n_ctx_tiles, precise_ctx):
    it = iter(refs)
    xa_ref, xb_ref, sh_ref, sc_ref, g_ref, w_ref = [next(it) for _ in range(6)]
    wl_ref = next(it) if precise_ctx else None
    (wgt_ref, wgtl_ref, brow_ref, gq_ref, gk_ref, eye_ref, cos_ref, sa_ref, sb_ref, gsum_ref,
     qa_ref, qb_ref, k_ref, v_ref, mqa_ref, mqb_ref, mka_ref, mkb_ref, mva_ref, mvb_ref,
     og_ref, gc_ref, gr_ref) = it
    tile = pl.program_id(0)
    x = _two_part(tile, n_ctx_tiles, xa_ref, xb_ref)
    ms = jnp.mean(x * x, axis=-1, keepdims=True)
    h = x * lax.rsqrt(ms + EPS) * g_ref[...]
    h = h * (1.0 + sc_ref[...]) + sh_ref[...]
    hb = h.astype(BF16)

    zr = _dot_x3((wgt_ref[...], wgtl_ref[...]), _hi_lo(h), NT_DIMS) + brow_ref[...]
    sub = lax.broadcasted_iota(jnp.int32, zr.shape, 0)
    gr = jnp.where(sub % 4 < 2, zr, _log_sigmoid(zr))
    eye = eye_ref[...]
    gc = functools.reduce(jnp.add, [lax.dot_general(eye, p, NT_DIMS, preferred_element_type=F32)
                                    for p in _split3(gr)])
    gr_ref[...] = jnp.zeros_like(gr_ref)
    for hd in range(N_ML_HEADS):
        gc_ref[hd] = gc[:, 4 * hd:4 * hd + 4]
        gr_ref[hd, 0:4, :] = gr[4 * hd:4 * hd + 4, :]

    cos = cos_ref[...]
    sa = sa_ref[...]
    sb = sb_ref[...]
    gsum = gsum_ref[...]

    def project(precise, q_ref, mq_ref, mk_ref, mv_ref):
        if precise:
            z = _dot_x3(_hi_lo(h), (w_ref[...], wl_ref[...]))
        else:
            z = jnp.dot(hb, w_ref[...], preferred_element_type=F32)
        proj = lambda c0, width: z[:, c0:c0 + width]
        act = F32 if precise else BF16

        n_qk = ATT_WIDTH // LANES + 1
        t_rows = hb.shape[0]
        zqk = proj(0, n_qk * LANES)
        zs = [zqk[:, c * LANES:(c + 1) * LANES] for c in range(n_qk)]
        sq = jnp.concatenate([z * z for z in zs], axis=0)
        ss = jnp.dot(jnp.concatenate(_hi_lo(sq), axis=0), gsum, preferred_element_type=F32)
        ss = ss[:n_qk * t_rows] + ss[n_qk * t_rows:]

        def headnorm_rope(c, gain):
            zn = zs[c] * lax.rsqrt(ss[c * t_rows:(c + 1) * t_rows] * (1.0 / HEAD_DIM) + EPS) * gain
            return zn * cos + pltpu.roll(zn, LANES - 16, 1) * sa + pltpu.roll(zn, 16, 1) * sb

        for c in range(n_qk - 1):
            q_ref[:, c * LANES:(c + 1) * LANES] = (headnorm_rope(c, gq_ref[...]) * 0.125).astype(act)
        k_ref[...] = headnorm_rope(n_qk - 1, gk_ref[...])
        v_ref[...] = proj(640, LANES)
        mq_ref[...] = proj(768, ML_WIDTH).astype(act)
        mk_ref[...] = (proj(1280, ML_WIDTH) * (ML_HEAD_DIM ** -0.5)).astype(act)
        mv_ref[...] = proj(1792, ML_WIDTH).astype(act)
        og_ref[...] = jax.nn.sigmoid(proj(2304, ML_WIDTH))

    @pl.when(tile < n_ctx_tiles)
    def _ctx():
        project(precise_ctx, qa_ref, mqa_ref, mka_ref, mva_ref)

    @pl.when(tile >= n_ctx_tiles)
    def _lat():
        project(False, qb_ref, mqb_ref, mkb_ref, mvb_ref)


def _inproj(xa, xb, mod3, n_ctx_tiles, tiles_per_lat_seq, ctx_row, g_mix, w_main, w_gate, b_gate,
            g_q, g_k, rope):
    w_hi, w_lo = w_main
    precise_ctx = w_lo is not None
    n_ctx, n_lat = xa.shape[0], xb.shape[0]
    n = n_ctx + n_lat
    n_tiles = n // TOKEN_TILE

    def mod_row(i):
        return jnp.where(i < n_ctx_tiles, ctx_row, (i - n_ctx_tiles) // tiles_per_lat_seq)

    def rope_blk(i):
        return jnp.where(i < n_ctx_tiles, 0, 1 + (i - n_ctx_tiles) % tiles_per_lat_seq)

    cos_t, sa_t, sb_t = rope
    lane = np.arange(LANES)
    gsum = jnp.asarray((lane[:, None] // HEAD_DIM) == (lane[None, :] // HEAD_DIM), BF16)
    tok = lambda w: pl.BlockSpec((TOKEN_TILE, w), lambda i: (i, 0))
    full = lambda a: pl.BlockSpec(a.shape, lambda i: (0,) * a.ndim)
    modspec = lambda j: pl.BlockSpec((None, 1, D_MODEL), lambda i: (mod_row(i), 0, j))
    ropespec = pl.BlockSpec((TOKEN_TILE, LANES), lambda i: (rope_blk(i), 0))
    consts = (g_mix.reshape(1, -1), w_hi) + ((w_lo,) if precise_ctx else ()) + (
        *_hi_lo(w_gate.T), b_gate.reshape(-1, 1),
        jnp.tile(g_q, 2).reshape(1, -1), jnp.tile(g_k, 2).reshape(1, -1),
        jnp.asarray(np.eye(TOKEN_TILE), BF16))
    args = (xa, xb, mod3, mod3) + consts + (cos_t, sa_t, sb_t, gsum)
    in_specs = _two_part_specs(D_MODEL, n_ctx_tiles) + [modspec(0), modspec(1)] \
        + [full(a) for a in consts] + [ropespec, ropespec, ropespec, full(gsum)]
    ctx_act = F32 if precise_ctx else BF16
    pair_shape = lambda w: [jax.ShapeDtypeStruct((n_ctx, w), ctx_act),
                            jax.ShapeDtypeStruct((n_lat, w), BF16)]
    pair_spec = lambda w: _two_part_out_specs(w, n_ctx_tiles)
    out_shape = (
        pair_shape(ATT_WIDTH)
        + [jax.ShapeDtypeStruct((n, LANES), F32),
           jax.ShapeDtypeStruct((n, LANES), F32)]
        + pair_shape(ML_WIDTH) + pair_shape(ML_WIDTH) + pair_shape(ML_WIDTH)
        + [jax.ShapeDtypeStruct((n, ML_WIDTH), F32),
           jax.ShapeDtypeStruct((N_ML_HEADS, n, 4), F32),
           jax.ShapeDtypeStruct((N_ML_HEADS, 8, n), F32)])
    out_specs = (pair_spec(ATT_WIDTH) + [tok(LANES), tok(LANES)]
                 + pair_spec(ML_WIDTH) + pair_spec(ML_WIDTH) + pair_spec(ML_WIDTH)
                 + [tok(ML_WIDTH),
                    pl.BlockSpec((N_ML_HEADS, TOKEN_TILE, 4), lambda i: (0, i, 0)),
                    pl.BlockSpec((N_ML_HEADS, 8, TOKEN_TILE), lambda i: (0, 0, i))])
    outs = pl.pallas_call(
        functools.partial(_inproj_kernel, n_ctx_tiles=n_ctx_tiles, precise_ctx=precise_ctx),
        grid=(n_tiles,), in_specs=in_specs, out_specs=tuple(out_specs),
        out_shape=tuple(out_shape), compiler_params=_params(("arbitrary",)), name="inproj",
    )(*args)
    qa, qb, k, v, mqa, mqb, mka, mkb, mva, mvb, og, gcol, grow = outs
    return (qa, qb), k, v, (mqa, mqb), (mka, mkb), (mva, mvb), og, gcol, grow


def _attn_kernel(*refs, n_kv, precise):
    q_ref = refs[0]
    kv_refs = refs[1:1 + 2 * n_kv]
    o_ref = refs[-1]
    tq = q_ref.shape[0]
    lo_q = lax.broadcasted_iota(jnp.int32, (tq, LANES), 1) < HEAD_DIM
    operand = _hi_lo if precise else (lambda a: a.astype(BF16))
    if precise:
        qk = lambda a, b: _dot_x3(a, b, NT_DIMS)
        pv = _dot_x3
    else:
        qk = lambda a, b: lax.dot_general(a, b, NT_DIMS, preferred_element_type=F32)
        pv = lambda a, b: jnp.dot(a, b, preferred_element_type=F32)

    def dup_half(ref, g):
        a = ref[...]
        r = pltpu.roll(a, HEAD_DIM, 1)
        lo = lax.broadcasted_iota(jnp.int32, a.shape, 1) < HEAD_DIM
        return operand(jnp.where(lo, a, r) if g == 0 else jnp.where(lo, r, a))

    for g in range(N_KV_HEADS):
        ks = [dup_half(kv_refs[2 * p], g) for p in range(n_kv)]
        vs = [dup_half(kv_refs[2 * p + 1], g) for p in range(n_kv)]
        for hb in range(2):
            c0 = (2 * g + hb) * LANES
            qb = q_ref[:, c0:c0 + LANES]
            outs = []
            for half in range(2):
                keep = lo_q if half == 0 else jnp.logical_not(lo_q)
                qm = operand(jnp.where(keep, qb, jnp.zeros_like(qb)))
                ss = [qk(qm, kd) for kd in ks]
                m = functools.reduce(jnp.maximum, [jnp.max(s, axis=1, keepdims=True) for s in ss])
                ps = [jnp.exp(s - m) for s in ss]
                den = functools.reduce(jnp.add, [jnp.sum(p, axis=1, keepdims=True) for p in ps])
                o = functools.reduce(jnp.add, [pv(operand(p), vd) for p, vd in zip(ps, vs)])
                outs.append(o / den)
            o_ref[:, c0:c0 + LANES] = jnp.where(lo_q, outs[0], outs[1]).astype(o_ref.dtype)


def _attention(q, k, v, kv_row0, n_seq, seq_len, cache=None):
    precise = q.dtype == F32
    nq = seq_len // SEQ_BLOCK
    sb0 = kv_row0 // seq_len
    in_specs = [
        pl.BlockSpec((SEQ_BLOCK, ATT_WIDTH), lambda b, i: (b * nq + i, 0)),
        pl.BlockSpec((seq_len, LANES), lambda b, i: (sb0 + b, 0)),
        pl.BlockSpec((seq_len, LANES), lambda b, i: (sb0 + b, 0)),
    ]
    args = [q, k, v]
    n_kv = 1
    if cache is not None:
        ck, cv = cache
        past = ck.shape[0] // n_seq
        in_specs += [pl.BlockSpec((past, LANES), lambda b, i: (b, 0))] * 2
        args += [ck, cv]
        n_kv = 2
    return pl.pallas_call(
        functools.partial(_attn_kernel, n_kv=n_kv, precise=precise),
        grid=(n_seq, nq), in_specs=in_specs,
        out_specs=pl.BlockSpec((SEQ_BLOCK, ATT_WIDTH), lambda b, i: (b * nq + i, 0)),
        out_shape=jax.ShapeDtypeStruct((n_seq * seq_len, ATT_WIDTH), q.dtype),
        compiler_params=_params(("parallel", "parallel")), name="attention",
    )(*args)


def _mlstm_kernel(*refs, heads, **static):
    it = iter(refs)
    q_ref, k_ref, v_ref, og_ref, gmh_ref, gcol_ref, grow_ref, u_ref, l_ref = [next(it) for _ in range(9)]
    init_refs = (next(it), next(it)) if static["has_init"] else ()
    ml_ref = next(it)
    state_refs = (next(it), next(it)) if static["emit_state"] else ()
    for hd in range(heads):
        lanes = pl.ds(hd * ML_HEAD_DIM, ML_HEAD_DIM)
        head_refs = [q_ref.at[:, lanes], k_ref.at[:, lanes], v_ref.at[:, lanes], og_ref.at[:, lanes],
                     gmh_ref.at[:, lanes], gcol_ref.at[hd], grow_ref.at[hd], u_ref, l_ref]
        if init_refs:
            head_refs += [init_refs[0].at[:, hd], init_refs[1].at[hd]]
        head_refs.append(ml_ref.at[:, lanes])
        if state_refs:
            head_refs += [state_refs[0].at[:, hd], state_refs[1].at[hd]]
        _mlstm_head(*head_refs, **static)


def _mlstm_head(*refs, seq_len, has_init, emit_state, precise):
    it = iter(refs)
    q_ref, k_ref, v_ref, og_ref, gmh_ref, gcol_ref, grow_ref, u_ref, l_ref = [next(it) for _ in range(9)]
    if has_init:
        c0_ref, nm0_ref = next(it), next(it)
    ml_ref = next(it)
    if emit_state:
        cf_ref, nmf_ref = next(it), next(it)

    operand = _hi_lo if precise else (lambda a: a.astype(BF16))
    if precise:
        qk = lambda a, b: _dot_x3(a, b, NT_DIMS)
        pv = _dot_x3
    else:
        qk = lambda a, b: lax.dot_general(a, b, NT_DIMS, preferred_element_type=F32)
        pv = lambda a, b: jnp.dot(a, b, preferred_element_type=F32)

    bq = SEQ_BLOCK
    nb = seq_len // bq
    blk = lambda j: slice(j * bq, (j + 1) * bq)
    upper_incl = u_ref[...]
    lower_incl = l_ref[...]

    def tri_dot(x, tri):
        return functools.reduce(jnp.add, [jnp.dot(p, tri, preferred_element_type=F32)
                                          for p in _split3(x)])

    ig_row = [[None] * nb for _ in range(2)]
    lf_row = [[None] * nb for _ in range(2)]
    within = [[None] * nb for _ in range(2)]
    bsum = [[None] * nb for _ in range(2)]
    for j in range(nb):
        g8 = grow_ref[:, blk(j)]
        cum_f = tri_dot(g8, upper_incl)
        cum_b = tri_dot(g8, lower_incl)
        for d in range(2):
            ig_row[d][j] = g8[d:d + 1, :]
            lf_row[d][j] = g8[2 + d:3 + d, :]
            within[d][j] = (cum_f if d == 0 else cum_b)[2 + d:3 + d, :]
            bsum[d][j] = jnp.sum(lf_row[d][j], axis=1, keepdims=True)
    zero11 = jnp.zeros((1, 1), F32)
    r_i = lax.broadcasted_iota(jnp.int32, (bq, bq), 0)
    c_i = lax.broadcasted_iota(jnp.int32, (bq, bq), 1)
    causal = [c_i <= r_i, c_i >= r_i]
    gmh = gmh_ref[...]

    q_blocks = [q_ref[blk(i), :] for i in range(nb)]
    scores = [qk(operand(q_blocks[i]), operand(k_ref[blk(i), :])) for i in range(nb)]

    h_dir = [[None] * nb for _ in range(2)]
    final = [None, None]
    for d in range(2):
        if has_init:
            state = (c0_ref[d], nm0_ref[d:d + 1, :])
            m = nm0_ref[2 + d:3 + d, 0:1]
        else:
            state, m = None, zero11
        scan = range(nb) if d == 0 else range(nb - 1, -1, -1)
        for step, i in enumerate(scan):
            q_i = q_blocks[i]
            a_loc = ig_row[d][i] - within[d][i]
            m_col = jnp.maximum(m, jnp.max(jnp.where(causal[d], a_loc, NEG_INF),
                                           axis=1, keepdims=True))
            if nb > 1:
                widen = lambda col: jnp.broadcast_to(col, (bq, LANES))
                tri = lower_incl if d == 0 else upper_incl
                b_rep = functools.reduce(jnp.add, [
                    lax.dot_general(tri, jnp.broadcast_to(piece, (LANES, bq)), NT_DIMS,
                                    preferred_element_type=F32)
                    for piece in _hi_lo(lf_row[d][i])])
                m_rep = widen(m_col)
                m_wide = jnp.concatenate([m_rep] * (bq // LANES), axis=1)
            else:
                widen = lambda col: col
                b_rep = jnp.sum(jnp.where(causal[d], lf_row[d][i], 0.0), axis=1, keepdims=True)
                m_rep = m_wide = m_col
            p = jnp.exp(jnp.where(causal[d], a_loc - m_wide, NEG_INF)) * scores[i]
            den = widen(jnp.sum(p, axis=1, keepdims=True))
            num = pv(operand(p), operand(v_ref[blk(i), :]))
            if state is not None:
                c_prev, n_prev = state
                w_inter = jnp.exp(m - m_rep)
                q_b = q_i.astype(BF16)
                qc = jnp.dot(q_b, c_prev.astype(BF16), preferred_element_type=F32)
                n_rows = jnp.broadcast_to(n_prev, (LANES, ML_HEAD_DIM)).astype(BF16)
                qn = lax.dot_general(q_b, n_rows, NT_DIMS, preferred_element_type=F32)
                num = num + w_inter * qc
                den = den + w_inter * qn
            nrm = jnp.maximum(jnp.abs(den), jnp.exp(-(b_rep + m_rep)))
            h_dir[d][i] = num / nrm
            if step == nb - 1 and not emit_state:
                break
            m_last = jnp.maximum(m, jnp.max(a_loc, axis=1, keepdims=True))
            a_col = widen(gcol_ref[blk(i), d:d + 1]) - b_rep
            kw = k_ref[blk(i), :].astype(F32) * jnp.exp(a_col - m_last)
            c_new = lax.dot_general(kw.astype(BF16), v_ref[blk(i), :].astype(BF16), TN_DIMS,
                                    preferred_element_type=F32)
            n_new = jnp.sum(kw, axis=0, keepdims=True)
            if state is not None:
                decay = jnp.exp(m - m_last)
                c_new = c_new + decay * state[0]
                n_new = n_new + decay * state[1]
            state = (c_new, n_new)
            m = bsum[d][i] + m_last
        final[d] = (state, m)

    for i in range(nb):
        h = h_dir[0][i] + h_dir[1][i]
        hn = h * lax.rsqrt(jnp.mean(h * h, axis=-1, keepdims=True) + EPS) * gmh
        ml_ref[blk(i), :] = (og_ref[blk(i), :] * hn).astype(ml_ref.dtype)

    if emit_state:
        nmf_ref[...] = jnp.zeros_like(nmf_ref)
        for d in range(2):
            (c_fin, n_fin), m_fin = final[d]
            cf_ref[d] = c_fin
            nmf_ref[d:d + 1, :] = n_fin
            nmf_ref[2 + d:3 + d, :] = jnp.broadcast_to(m_fin, (1, ML_HEAD_DIM))


def _mlstm(mq, mk, mv, og, g_mh, gcol, grow, row0, n_seq, seq_len, init=None, emit_state=False):
    precise = mq.dtype == F32
    assert not (precise and init is not None)
    sb0 = row0 // seq_len
    tri = np.arange(SEQ_BLOCK)
    upper_incl = jnp.asarray(tri[:, None] <= tri[None, :], BF16)
    lower_incl = jnp.asarray(tri[:, None] >= tri[None, :], BF16)
    heads = MLSTM_HEADS_PER_STEP
    width = heads * ML_HEAD_DIM
    ownblk = lambda: pl.BlockSpec((seq_len, width), lambda b, h: (b, h))
    headblk = lambda: pl.BlockSpec((seq_len, width), lambda b, h: (sb0 + b, h))
    const = lambda a: pl.BlockSpec(a.shape, lambda b, h: (0,) * a.ndim)
    in_specs = [ownblk(), ownblk(), ownblk(), headblk(),
                pl.BlockSpec((1, width), lambda b, h: (0, h)),
                pl.BlockSpec((heads, seq_len, 4), lambda b, h: (h, sb0 + b, 0)),
                pl.BlockSpec((heads, 8, seq_len), lambda b, h: (h, 0, sb0 + b)),
                const(upper_incl), const(lower_incl)]
    args = [mq, mk, mv, og, g_mh.reshape(1, -1), gcol, grow, upper_incl, lower_incl]
    if init is not None:
        c0, nm0, layer = init
        in_specs += [
            pl.BlockSpec((None, None, 2, heads, ML_HEAD_DIM, ML_HEAD_DIM),
                         lambda b, h: (b, layer, 0, h, 0, 0)),
            pl.BlockSpec((None, heads, 8, ML_HEAD_DIM), lambda b, h: (b, h, 0, 0))]
        args += [c0, nm0]
    out_shape = [jax.ShapeDtypeStruct((n_seq * seq_len, ML_WIDTH), mq.dtype)]
    out_specs = [pl.BlockSpec((seq_len, width), lambda b, h: (b, h))]
    if emit_state:
        out_shape += [jax.ShapeDtypeStruct((n_seq, 2, N_ML_HEADS, ML_HEAD_DIM, ML_HEAD_DIM), F32),
                      jax.ShapeDtypeStruct((n_seq, N_ML_HEADS, 8, ML_HEAD_DIM), F32)]
        out_specs += [pl.BlockSpec((None, 2, heads, ML_HEAD_DIM, ML_HEAD_DIM),
                                   lambda b, h: (b, 0, h, 0, 0)),
                      pl.BlockSpec((None, heads, 8, ML_HEAD_DIM), lambda b, h: (b, h, 0, 0))]
    return pl.pallas_call(
        functools.partial(_mlstm_kernel, heads=heads, seq_len=seq_len, has_init=init is not None,
                          emit_state=emit_state, precise=precise),
        grid=(n_seq, N_ML_HEADS // heads), in_specs=in_specs, out_specs=tuple(out_specs),
        out_shape=tuple(out_shape),
        compiler_params=_params(("parallel", "parallel")), name="mlstm",
    )(*args)


def _outproj_kernel(*refs, n_ctx_tiles, precise_ctx):
    it = iter(refs)
    atta_ref, attb_ref, mla_ref, mlb_ref, xa_ref, xb_ref, w_ref = [next(it) for _ in range(7)]
    wl_ref = next(it) if precise_ctx else None
    gt_ref, sh_ref, sc_ref, g_ref, wrt_ref, wrtl_ref, x1_ref, hp_ref, logits_ref, y_scr = it
    i = pl.program_id(0)

    def mix(att, ml):
        return (jnp.dot(att, w_ref[:ATT_WIDTH, :], preferred_element_type=F32)
                + jnp.dot(ml, w_ref[ATT_WIDTH:, :], preferred_element_type=F32))

    @pl.when(i < n_ctx_tiles)
    def _ctx():
        if precise_ctx:
            y_scr[...] = (
                _dot_x3(_hi_lo(atta_ref[...]), (w_ref[:ATT_WIDTH, :], wl_ref[:ATT_WIDTH, :]))
                + _dot_x3(_hi_lo(mla_ref[...]), (w_ref[ATT_WIDTH:, :], wl_ref[ATT_WIDTH:, :])))
        else:
            y_scr[...] = mix(atta_ref[...], mla_ref[...])

    @pl.when(i >= n_ctx_tiles)
    def _lat():
        y_scr[...] = mix(attb_ref[...], mlb_ref[...])

    x1 = _two_part(i, n_ctx_tiles, xa_ref, xb_ref) + gt_ref[...] * y_scr[...]
    x1_ref[...] = x1
    ms = jnp.mean(x1 * x1, axis=-1, keepdims=True)
    h2 = x1 * lax.rsqrt(ms + EPS) * g_ref[...]
    h2 = h2 * (1.0 + sc_ref[...]) + sh_ref[...]
    for k in range(ROW_TILE_SUBLANES):
        hp_ref[pl.ds(k, TOKEN_TILE, stride=ROW_TILE_SUBLANES), :] = h2[:, k * LANES:(k + 1) * LANES]

    logits_ref[...] = _dot_x3((wrt_ref[...], wrtl_ref[...]), _hi_lo(h2), NT_DIMS)


def _select_experts(logits, b_col):
    ex = jnp.exp(logits - jnp.max(logits, axis=0, keepdims=True))
    scores = ex / jnp.sum(ex, axis=0, keepdims=True)
    sel = scores + b_col
    row = lambda a, e: a[e:e + 1, :]
    grp_score = []
    for g in range(N_GROUPS):
        xs = [row(sel, g * GROUP_SIZE + j) for j in range(GROUP_SIZE)]
        pairs = [xs[a] + xs[b] for a in range(GROUP_SIZE) for b in range(a + 1, GROUP_SIZE)]
        grp_score.append(functools.reduce(jnp.maximum, pairs))
    best = grp_score[0]
    grp = jnp.zeros_like(best, dtype=jnp.int32)
    for g in range(1, N_GROUPS):
        better = grp_score[g] > best
        grp = jnp.where(better, g, grp)
        best = jnp.where(better, grp_score[g], best)
    pick = lambda a, j: functools.reduce(
        lambda acc, g: jnp.where(grp == g, row(a, g * GROUP_SIZE + j), acc),
        range(1, N_GROUPS), row(a, j))
    xs = [pick(sel, j) for j in range(GROUP_SIZE)]
    ws = [pick(scores, j) for j in range(GROUP_SIZE)]

    def argmax4(vals):
        bv, bi = vals[0], jnp.zeros_like(grp)
        for j in range(1, GROUP_SIZE):
            better = vals[j] > bv
            bi = jnp.where(better, j, bi)
            bv = jnp.where(better, vals[j], bv)
        return bi

    i1 = argmax4(xs)
    i2 = argmax4([jnp.where(i1 == j, NEG_INF, xs[j]) for j in range(GROUP_SIZE)])
    take = lambda vals, idx: functools.reduce(
        lambda acc, j: jnp.where(idx == j, vals[j], acc), range(1, GROUP_SIZE), vals[0])
    w1, w2 = take(ws, i1), take(ws, i2)
    wsum = w1 + w2
    w1, w2 = w1 / wsum, w2 / wsum
    return grp * GROUP_SIZE + i1, grp * GROUP_SIZE + i2, w1, w2


def _outproj(att, ml, x, mod3, n_ctx_tiles, tiles_per_lat_seq, ctx_row, w_out, g_ffn, w_router):
    w_hi, w_lo = w_out
    precise_ctx = w_lo is not None
    n = x[0].shape[0] + x[1].shape[0]

    def mod_row(i):
        return jnp.where(i < n_ctx_tiles, ctx_row, (i - n_ctx_tiles) // tiles_per_lat_seq)

    tok = lambda w: pl.BlockSpec((TOKEN_TILE, w), lambda i: (i, 0))
    full = lambda a: pl.BlockSpec(a.shape, lambda i: (0,) * a.ndim)
    modspec = lambda j: pl.BlockSpec((None, 1, D_MODEL), lambda i: (mod_row(i), 0, j))
    weights = (w_hi, w_lo) if precise_ctx else (w_hi,)
    consts = (g_ffn.reshape(1, -1), *_hi_lo(w_router.T))
    args = (*att, *ml, *x, *weights, mod3, mod3, mod3) + consts
    in_specs = (_two_part_specs(ATT_WIDTH, n_ctx_tiles) + _two_part_specs(ML_WIDTH, n_ctx_tiles)
                + _two_part_specs(D_MODEL, n_ctx_tiles) + [full(w) for w in weights]
                + [modspec(2), modspec(3), modspec(4)] + [full(a) for a in consts])
    return pl.pallas_call(
        functools.partial(_outproj_kernel, n_ctx_tiles=n_ctx_tiles, precise_ctx=precise_ctx),
        grid=(n // TOKEN_TILE,), in_specs=in_specs,
        scratch_shapes=[pltpu.VMEM((TOKEN_TILE, D_MODEL), F32)],
        out_specs=(tok(D_MODEL),
                   pl.BlockSpec((TOKEN_TILE * ROW_TILE_SUBLANES, LANES), lambda i: (i, 0)),
                   pl.BlockSpec((N_EXPERTS, TOKEN_TILE), lambda i: (0, i))),
        out_shape=(jax.ShapeDtypeStruct((n, D_MODEL), F32),
                   jax.ShapeDtypeStruct((n * ROW_TILE_SUBLANES, LANES), F32),
                   jax.ShapeDtypeStruct((N_EXPERTS, n), F32)),
        compiler_params=_params(("parallel",)), name="outproj_router",
    )(*args)


def _moe_kernel(order_ref, pos0_ref, pos1_ref, off_ref, cnt_ref,
                h_ref, wg_ref, wu_ref, wd_ref, x1_ref, gt_ref, wcol_ref, ya_ref, yb_ref,
                o_scr, xs_scr, comb0_scr, comb1_scr, *, n_ctx_tiles):
    t = pl.program_id(0)
    s = pl.program_id(1)
    sub = ROW_TILE_SUBLANES
    groups = MOE_CHUNK // 8

    def tile(ref, row):
        return ref.at[pl.ds(pl.multiple_of(row * sub, sub), sub), :]

    def slab(ref, row0, n_rows, k):
        return ref.at[pl.ds(row0 * sub + k, n_rows, stride=sub), :]

    def gather_rows(buf, slot0):
        for j in range(MOE_CHUNK):
            tile(xs_scr, buf * MOE_CHUNK + j)[...] = h_ref[order_ref[slot0 + j]]

    def ffn(buf, base):
        a = jnp.zeros((MOE_CHUNK, D_EXPERT), F32)
        b = jnp.zeros((MOE_CHUNK, D_EXPERT), F32)
        for p in range(sub // 2):
            lhs = jnp.concatenate([slab(xs_scr, buf * MOE_CHUNK, MOE_CHUNK, 2 * p)[...],
                                   slab(xs_scr, buf * MOE_CHUNK, MOE_CHUNK, 2 * p + 1)[...]],
                                  axis=1).astype(BF16)
            rows = slice(2 * p * LANES, (2 * p + 2) * LANES)
            a = a + jnp.dot(lhs, wg_ref[rows, :], preferred_element_type=F32)
            b = b + jnp.dot(lhs, wu_ref[rows, :], preferred_element_type=F32)
        hid = (a * jax.nn.sigmoid(a)) * b
        out = jnp.dot(hid.astype(BF16), wd_ref[...], preferred_element_type=F32)
        for k in range(sub):
            slab(o_scr, base, MOE_CHUNK, k)[...] = out[:, k * LANES:(k + 1) * LANES]

    @pl.when(s == 0)
    def _first():
        gather_rows(0, t * MOE_SLOTS + pl.multiple_of(off_ref[t * N_EXPERTS], 8))

    @pl.when(s < N_EXPERTS)
    def _expert():
        seg = t * N_EXPERTS + s
        off = pl.multiple_of(off_ref[seg], 8)
        cur = s % 2
        nxt_seg = t * N_EXPERTS + jnp.minimum(s + 1, N_EXPERTS - 1)
        ffn(cur, off)
        gather_rows(1 - cur, t * MOE_SLOTS + pl.multiple_of(off_ref[nxt_seg], 8))

        def extra(c, carry):
            base = pl.multiple_of(off + c * MOE_CHUNK, 8)

            def gather8(i, carry2):
                slot = t * MOE_SLOTS + base + i * 8
                for k in range(8):
                    tile(xs_scr, 2 * MOE_CHUNK + i * 8 + k)[...] = h_ref[order_ref[slot + k]]
                return carry2

            lax.fori_loop(0, groups, gather8, 0)
            ffn(2, base)
            return carry

        lax.fori_loop(1, (cnt_ref[seg] + MOE_CHUNK - 1) // MOE_CHUNK, extra, 0)

    @pl.when(s >= N_EXPERTS)
    def _combine():
        tok0 = t * MOE_TILE + (s - N_EXPERTS) * MOE_OUT_TILE

        def body8(i, carry):
            for k in range(8):
                p0 = pos0_ref[tok0 + i * 8 + k]
                p1 = pos1_ref[tok0 + i * 8 + k]
                tile(comb0_scr, i * 8 + k)[...] = tile(o_scr, p0)[...]
                tile(comb1_scr, i * 8 + k)[...] = tile(o_scr, p1)[...]
            return carry

        lax.fori_loop(0, MOE_OUT_TILE // 8, body8, 0)
        wcol = wcol_ref[...]
        rows_of = lambda ref: jnp.concatenate(
            [slab(ref, 0, MOE_OUT_TILE, k)[...] for k in range(sub)], axis=1)
        comb = wcol[:, 2:3] * rows_of(comb0_scr) + wcol[:, 3:4] * rows_of(comb1_scr)
        y = x1_ref[...] + gt_ref[...] * comb
        chunk_ix = t * (MOE_TILE // MOE_OUT_TILE) + s - N_EXPERTS

        @pl.when(chunk_ix < n_ctx_tiles)
        def _ctx():
            ya_ref[...] = y

        @pl.when(chunk_ix >= n_ctx_tiles)
        def _lat():
            yb_ref[...] = y


def _route_tables(logits, b_router):
    n = logits.shape[1]
    nt = n // MOE_TILE
    tri = np.arange(SEQ_BLOCK)
    strict_upper = jnp.asarray(tri[:, None] < tri[None, :], BF16)
    tok = np.arange(MOE_TILE)
    digits = np.zeros((MOE_TILE, LANES), np.float32)
    digits[:, 0] = tok % 256
    digits[:, 1] = tok // 256
    pos, order, meta, wcol = pl.pallas_call(
        _route_kernel, grid=(nt,),
        in_specs=[pl.BlockSpec((N_EXPERTS, MOE_TILE), lambda t: (0, t)),
                  pl.BlockSpec((N_EXPERTS, 1), lambda t: (0, 0)),
                  pl.BlockSpec((SEQ_BLOCK, SEQ_BLOCK), lambda t: (0, 0)),
                  pl.BlockSpec((SEQ_BLOCK, SEQ_BLOCK), lambda t: (0, 0)),
                  pl.BlockSpec((MOE_TILE, LANES), lambda t: (0, 0))],
        out_specs=(pl.BlockSpec((8, MOE_TILE), lambda t: (0, t)),
                   pl.BlockSpec((None, MOE_SLOTS, 1), lambda t: (t, 0, 0)),
                   pl.BlockSpec((None, N_EXPERTS, 8), lambda t: (t, 0, 0)),
                   pl.BlockSpec((MOE_TILE, 8), lambda t: (t, 0))),
        out_shape=(jax.ShapeDtypeStruct((8, n), jnp.int32),
                   jax.ShapeDtypeStruct((nt, MOE_SLOTS, 1), jnp.int32),
                   jax.ShapeDtypeStruct((nt, N_EXPERTS, 8), jnp.int32),
                   jax.ShapeDtypeStruct((n, 8), F32)),
        scratch_shapes=[pltpu.VMEM((8, MOE_TILE), F32)],
        compiler_params=_params(("parallel",)), name="route_tables",
    )(logits, b_router.reshape(-1, 1), strict_upper, jnp.asarray(np.eye(SEQ_BLOCK), BF16),
      jnp.asarray(digits, BF16))
    return (order.reshape(-1), pos[0], pos[1], meta[:, :, 0].reshape(-1),
            meta[:, :, 1].reshape(-1), wcol)


def _route_kernel(logits_ref, br_ref, su_ref, eye_ref, digits_ref,
                  pos_ref, order_ref, meta_ref, wcol_ref, wrow_scr):
    e1, e2, w1, w2 = _select_experts(logits_ref[...], br_ref[...])
    wrow_scr[...] = jnp.zeros_like(wrow_scr)
    wrow_scr[2:3, :] = w1
    wrow_scr[3:4, :] = w2
    eye = eye_ref[...]
    for b in range(MOE_TILE // SEQ_BLOCK):
        cols = slice(b * SEQ_BLOCK, (b + 1) * SEQ_BLOCK)
        wcol_ref[cols, :] = functools.reduce(jnp.add, [
            lax.dot_general(eye, p, NT_DIMS, preferred_element_type=F32)
            for p in _split3(wrow_scr[:, cols])])
    eid = lax.broadcasted_iota(jnp.int32, (N_EXPERTS, MOE_TILE), 0)
    oh1, oh2 = eid == e1, eid == e2
    oh = jnp.where(oh1, 1.0, 0.0) + jnp.where(oh2, 1.0, 0.0)
    nblk = MOE_TILE // SEQ_BLOCK
    blocks = [oh[:, b * SEQ_BLOCK:(b + 1) * SEQ_BLOCK] for b in range(nblk)]
    inner = jnp.dot(jnp.concatenate(blocks, axis=0).astype(BF16), su_ref[...],
                    preferred_element_type=F32)
    run = jnp.zeros((N_EXPERTS, 1), F32)
    ranks = []
    for b in range(nblk):
        ranks.append(inner[b * N_EXPERTS:(b + 1) * N_EXPERTS, :] + run)
        run = run + jnp.sum(blocks[b], axis=1, keepdims=True)
    count = run
    seg = jnp.floor((count + 7.0) * 0.125) * 8.0
    sub = lax.broadcasted_iota(jnp.int32, (N_EXPERTS, 1), 0)
    off = jnp.zeros((N_EXPERTS, 1), F32)
    for e in range(N_EXPERTS - 1):
        off = off + jnp.where(sub > e, seg[e:e + 1, :], 0.0)
    slot = jnp.concatenate(ranks, axis=1) + off
    pos1 = jnp.sum(jnp.where(oh1, slot, 0.0), axis=0, keepdims=True).astype(jnp.int32)
    pos2 = jnp.sum(jnp.where(oh2, slot, 0.0), axis=0, keepdims=True).astype(jnp.int32)
    pos_ref[...] = jnp.zeros_like(pos_ref)
    pos_ref[0:1, :] = pos1
    pos_ref[1:2, :] = pos2
    meta_ref[...] = jnp.zeros_like(meta_ref)
    meta_ref[:, 0:1] = off.astype(jnp.int32)
    meta_ref[:, 1:2] = count.astype(jnp.int32)
    digits = digits_ref[...]
    rows = MOE_SLOTS // ROUTE_SLOT_BLOCKS
    for sb in range(ROUTE_SLOT_BLOCKS):
        j = lax.broadcasted_iota(jnp.int32, (rows, MOE_TILE), 0) + sb * rows
        hit = jnp.where(j == pos1, 1.0, 0.0) + jnp.where(j == pos2, 1.0, 0.0)
        d = jnp.dot(hit.astype(BF16), digits, preferred_element_type=F32)
        order_ref[sb * rows:(sb + 1) * rows, :] = (d[:, 0:1] + 256.0 * d[:, 1:2]).astype(jnp.int32)


def _moe(hp, logits, b_router, x1, mod3, layer, n_ctx_tiles, tiles_per_lat_seq, ctx_row,
         wg, wu, wd):
    n = hp.shape[0] // ROW_TILE_SUBLANES
    order, pos0, pos1, off, count, wcol = _route_tables(logits, b_router)
    chunks_per_tile = MOE_TILE // MOE_OUT_TILE
    n_steps = N_EXPERTS + chunks_per_tile

    def chunk_ix(t, s):
        return t * chunks_per_tile + jnp.maximum(s - N_EXPERTS, 0)

    def mod_row(g):
        return jnp.where(g < n_ctx_tiles, ctx_row, (g - n_ctx_tiles) // tiles_per_lat_seq)

    wspec = lambda r, c: pl.BlockSpec(
        (None, None, r, c), lambda t, s, *_: (layer, jnp.minimum(s, N_EXPERTS - 1), 0, 0))
    chunk_spec = pl.BlockSpec((MOE_OUT_TILE, D_MODEL), lambda t, s, *_: (chunk_ix(t, s), 0))
    grid_spec = pltpu.PrefetchScalarGridSpec(
        num_scalar_prefetch=5,
        grid=(n // MOE_TILE, n_steps),
        in_specs=[
            pl.BlockSpec((MOE_TILE, ROW_TILE_SUBLANES, LANES), lambda t, s, *_: (t, 0, 0)),
            wspec(D_MODEL, D_EXPERT), wspec(D_MODEL, D_EXPERT), wspec(D_EXPERT, D_MODEL),
            chunk_spec,
            pl.BlockSpec((None, 1, D_MODEL), lambda t, s, *_: (mod_row(chunk_ix(t, s)), 0, 5)),
            pl.BlockSpec((MOE_OUT_TILE, 8), lambda t, s, *_: (chunk_ix(t, s), 0)),
        ],
        out_specs=(
            pl.BlockSpec((MOE_OUT_TILE, D_MODEL),
                         lambda t, s, *_: (jnp.minimum(chunk_ix(t, s), n_ctx_tiles - 1), 0)),
            pl.BlockSpec((MOE_OUT_TILE, D_MODEL),
                         lambda t, s, *_: (jnp.maximum(chunk_ix(t, s) - n_ctx_tiles, 0), 0))),
        scratch_shapes=[pltpu.VMEM((MOE_SLOTS * ROW_TILE_SUBLANES, LANES), F32),
                        pltpu.VMEM((3 * MOE_CHUNK * ROW_TILE_SUBLANES, LANES), F32),
                        pltpu.VMEM((MOE_OUT_TILE * ROW_TILE_SUBLANES, LANES), F32),
                        pltpu.VMEM((MOE_OUT_TILE * ROW_TILE_SUBLANES, LANES), F32)],
    )
    n_ctx = n_ctx_tiles * MOE_OUT_TILE
    return pl.pallas_call(
        functools.partial(_moe_kernel, n_ctx_tiles=n_ctx_tiles), grid_spec=grid_spec,
        out_shape=(jax.ShapeDtypeStruct((n_ctx, D_MODEL), F32),
                   jax.ShapeDtypeStruct((n - n_ctx, D_MODEL), F32)),
        compiler_params=_params(("arbitrary", "arbitrary")), name="experts",
    )(order, pos0, pos1, off, count, hp.reshape(n, ROW_TILE_SUBLANES, LANES), wg, wu, wd, x1,
      mod3, wcol)


def _rope_tables(seq_len):
    half = HEAD_DIM // 2
    freqs = ROPE_BASE ** (-np.arange(0, half, 2, dtype=np.float64) / half)
    pos = np.arange(seq_len)
    row, col = pos // GRID_W, pos % GRID_W
    d = np.arange(HEAD_DIM)
    position = np.where(d[None, :] < half, row[:, None], col[:, None]).astype(np.float64)
    ang = (position.astype(np.float32) * freqs.astype(np.float32)[d % (half // 2)][None, :]).astype(np.float32)
    cos, sin = np.cos(ang), np.sin(ang)
    first = (d % half) < half // 2
    sa = np.where(first[None, :], -sin, 0.0)
    sb = np.where(first[None, :], 0.0, sin)
    ident = lambda v: np.full((TOKEN_TILE, HEAD_DIM), v, np.float32)
    stack = lambda ctx, lat: jnp.asarray(
        np.tile(np.concatenate([ctx, lat.astype(np.float32)], axis=0), (1, 2)), F32)
    return stack(ident(1.0), cos), stack(ident(0.0), sa), stack(ident(0.0), sb)


def kernel(x_prompt, x_sample, c, cache_k, cache_v, state_C, state_n, state_m, c_ctx, w_mod, b_mod,
           g_mix, g_ffn, w_in, b_igate, b_fgate, g_q, g_k, g_mh, w_out, w_router, b_router,
           w_e_gate, w_e_up, w_e_down):
    n_ctx_seq, ctx_len, _ = x_prompt.shape
    n_lat_seq, lat_len, _ = x_sample.shape
    n_layers = w_mod.shape[0]
    n_ctx = n_ctx_seq * ctx_len
    assert ctx_len == SEQ_BLOCK and lat_len % TOKEN_TILE == 0
    assert n_ctx % MOE_TILE == 0 and (n_lat_seq * lat_len) % MOE_TILE == 0
    assert n_lat_seq < 16 and n_ctx % lat_len == 0
    n_ctx_tiles = n_ctx // TOKEN_TILE
    tiles_per_lat_seq = lat_len // TOKEN_TILE
    ctx_row = n_lat_seq

    x = (x_prompt.reshape(n_ctx, D_MODEL), x_sample.reshape(-1, D_MODEL))
    cond = jnp.zeros((16, D_MODEL), F32).at[:n_lat_seq].set(c).at[ctx_row].set(c_ctx)
    mod = _modulation(cond, w_mod, b_mod)
    rope = _rope_tables(lat_len)

    wg_b, wu_b, wd_b = w_e_gate.astype(BF16), w_e_up.astype(BF16), w_e_down.astype(BF16)
    gate_perm = np.array([(q % 2) * N_ML_HEADS + hd + 2 * N_ML_HEADS * (q // 2)
                          for hd in range(N_ML_HEADS) for q in range(4)])

    ks, vs, cs, ns, ms = [], [], [], [], []
    for l in range(n_layers):
        mod3 = mod[l].reshape(16, 1, -1)
        precise_ctx = l < n_layers - 1

        def weight_pair(w):
            hi = w.astype(BF16)
            return hi, ((w - hi.astype(F32)).astype(BF16) if precise_ctx else None)

        w_main = weight_pair(w_in[l, :, :MAIN_WIDTH])
        w_gate = w_in[l, :, MAIN_WIDTH:][:, gate_perm]
        b_gate = jnp.concatenate([b_igate[l].reshape(-1), b_fgate[l].reshape(-1)])[gate_perm]
        q, k, v, mq, mk, mv, og, gcol, grow = _inproj(
            *x, mod3, n_ctx_tiles, tiles_per_lat_seq, ctx_row, g_mix[l], w_main, w_gate, b_gate,
            g_q[l], g_k[l], rope)

        att_ctx = _attention(q[0], k, v, 0, n_ctx_seq, ctx_len)
        past = cache_k.shape[2]
        ck = cache_k[:, l].reshape(n_lat_seq * past, LANES)
        cv = cache_v[:, l].reshape(n_lat_seq * past, LANES)
        att_lat = _attention(q[1], k, v, n_ctx, n_lat_seq, lat_len, cache=(ck, cv))

        ml_ctx, c_fin, nm_fin = _mlstm(mq[0], mk[0], mv[0], og, g_mh[l], gcol, grow, 0,
                                       n_ctx_seq, ctx_len, emit_state=True)
        n0 = state_n[:, l].transpose(0, 2, 1, 3)
        m0 = jnp.broadcast_to(state_m[:, l].transpose(0, 2, 1)[..., None], n0.shape)
        nm0 = jnp.concatenate([n0, m0, jnp.zeros_like(n0), jnp.zeros_like(n0)], axis=2)
        (ml_lat,) = _mlstm(mq[1], mk[1], mv[1], og, g_mh[l], gcol, grow, n_ctx, n_lat_seq,
                           lat_len, init=(state_C, nm0, l))

        x1, hp, logits = _outproj(
            (att_ctx, att_lat), (ml_ctx, ml_lat), x, mod3, n_ctx_tiles, tiles_per_lat_seq, ctx_row,
            weight_pair(w_out[l]), g_ffn[l], w_router)
        x = _moe(hp, logits, b_router, x1, mod3, l, n_ctx // MOE_OUT_TILE,
                 lat_len // MOE_OUT_TILE, ctx_row, wg_b, wu_b, wd_b)

        ks.append(k[:n_ctx].reshape(n_ctx_seq, ctx_len, N_KV_HEADS, HEAD_DIM))
        vs.append(v[:n_ctx].reshape(n_ctx_seq, ctx_len, N_KV_HEADS, HEAD_DIM))
        cs.append(c_fin)
        ns.append(nm_fin[:, :, 0:2, :].transpose(0, 2, 1, 3))
        ms.append(nm_fin[:, :, 2:4, 0].transpose(0, 2, 1))

    y_prompt = x[0].reshape(x_prompt.shape)
    y_sample = x[1].reshape(x_sample.shape)
    return (y_prompt, y_sample, jnp.stack(ks, axis=1), jnp.stack(vs, axis=1),
            jnp.stack(cs, axis=1), jnp.stack(ns, axis=1), jnp.stack(ms, axis=1))
```

```python
import functools

import numpy as np
import jax
import jax.numpy as jnp
from jax import lax
from jax.experimental import pallas as pl
from jax.experimental.pallas import tpu as pltpu

F32 = jnp.float32
BF16 = jnp.bfloat16

D_MODEL = 1024
HEAD_DIM = 64
ATT_WIDTH = 512
N_KV_HEADS = 2
ML_WIDTH = 512
N_ML_HEADS = 4
ML_HEAD_DIM = 128
GRID_W = 64
ROPE_BASE = 10000.0
N_EXPERTS = 16
N_GROUPS = 4
GROUP_SIZE = 4
D_EXPERT = 512
EPS = 1e-6
MAIN_WIDTH = 2816
N_GATE_COLS = 16
LANES = 128
TOKEN_TILE = 512
SEQ_BLOCK = 256
MOE_TILE = 2048
MOE_CHUNK = 288
ROW_TILE_SUBLANES = D_MODEL // LANES
MLSTM_HEADS_PER_STEP = 4
MOE_OUT_TILE = 256
ROUTE_SLOT_BLOCKS = 9
MOE_SLOTS = -(-(2 * MOE_TILE + 8 * N_EXPERTS + MOE_CHUNK) // (512 * ROUTE_SLOT_BLOCKS)) * 512 * ROUTE_SLOT_BLOCKS
VMEM_LIMIT = 56 * 1024 * 1024
NEG_INF = float("-inf")
HIGHEST = lax.Precision.HIGHEST
NT_DIMS = (((1,), (1,)), ((), ()))
TN_DIMS = (((0,), (0,)), ((), ()))


def _params(semantics):
    return pltpu.CompilerParams(dimension_semantics=semantics, vmem_limit_bytes=VMEM_LIMIT)


def _log_sigmoid(z):
    return jnp.minimum(z, 0.0) - jnp.log1p(jnp.exp(-jnp.abs(z)))


def _split3(x):
    h1 = x.astype(BF16)
    r1 = x - h1.astype(F32)
    h2 = r1.astype(BF16)
    h3 = (r1 - h2.astype(F32)).astype(BF16)
    return h1, h2, h3


def _mod_kernel(cond_ref, w_ref, b_ref, o_ref):
    c = cond_ref[...]
    s = c * jax.nn.sigmoid(c)
    o_ref[...] = _dot_x3(_hi_lo(s), _hi_lo(w_ref[...])) + b_ref[...]


def _modulation(cond, w_mod, b_mod):
    n_layers = w_mod.shape[0]
    n_chunks = w_mod.shape[2] // D_MODEL
    return pl.pallas_call(
        _mod_kernel,
        grid=(n_layers, n_chunks),
        in_specs=[
            pl.BlockSpec((16, D_MODEL), lambda l, j: (0, 0)),
            pl.BlockSpec((None, D_MODEL, D_MODEL), lambda l, j: (l, 0, j)),
            pl.BlockSpec((None, 1, D_MODEL), lambda l, j: (l, 0, j)),
        ],
        out_specs=pl.BlockSpec((None, 16, D_MODEL), lambda l, j: (l, 0, j)),
        out_shape=jax.ShapeDtypeStruct((n_layers, 16, w_mod.shape[2]), F32),
        compiler_params=_params(("parallel", "parallel")),
        name="modulation",
    )(cond, w_mod, b_mod.reshape(n_layers, 1, -1))


def _two_part(i, n_first, a_ref, b_ref):
    return jnp.where(i < n_first, a_ref[...], b_ref[...])


def _hi_lo(x):
    hi = x.astype(BF16)
    return hi, (x - hi.astype(F32)).astype(BF16)


def _dot_x3(a, b, dims=None):
    (ah, al), (bh, bl) = a, b
    if dims is None:
        d = lambda x, y: jnp.dot(x, y, preferred_element_type=F32)
    else:
        d = lambda x, y: lax.dot_general(x, y, dims, preferred_element_type=F32)
    return d(ah, bh) + (d(al, bh) + d(ah, bl))


def _two_part_out_specs(width, n_first, tile=TOKEN_TILE):
    return _two_part_specs(width, n_first, tile)


def _two_part_specs(width, n_first, tile=TOKEN_TILE):
    return [pl.BlockSpec((tile, width), lambda i: (jnp.minimum(i, n_first - 1), 0)),
            pl.BlockSpec((tile, width), lambda i: (jnp.maximum(i - n_first, 0), 0))]


def _inproj_kernel(*refs, n_ctx_tiles, precise_ctx):
    it = iter(refs)
    xa_ref, xb_ref, sh_ref, sc_ref, g_ref, w_ref = [next(it) for _ in range(6)]
    wl_ref = next(it) if precise_ctx else None
    (wgt_ref, wgtl_ref, brow_ref, gq_ref, gk_ref, eye_ref, cos_ref, sa_ref, sb_ref, gsum_ref,
     qa_ref, qb_ref, k_ref, v_ref, mqa_ref, mqb_ref, mka_ref, mkb_ref, mva_ref, mvb_ref,
     og_ref, gc_ref, gr_ref) = it
    tile = pl.program_id(0)
    x = _two_part(tile, n_ctx_tiles, xa_ref, xb_ref)
    ms = jnp.mean(x * x, axis=-1, keepdims=True)
    h = x * lax.rsqrt(ms + EPS) * g_ref[...]
    h = h * (1.0 + sc_ref[...]) + sh_ref[...]
    hb = h.astype(BF16)

    zr = _dot_x3((wgt_ref[...], wgtl_ref[...]), _hi_lo(h), NT_DIMS) + brow_ref[...]
    sub = lax.broadcasted_iota(jnp.int32, zr.shape, 0)
    gr = jnp.where(sub % 4 < 2, zr, _log_sigmoid(zr))
    eye = eye_ref[...]
    gc = functools.reduce(jnp.add, [lax.dot_general(eye, p, NT_DIMS, preferred_element_type=F32)
                                    for p in _split3(gr)])
    gr_ref[...] = jnp.zeros_like(gr_ref)
    for hd in range(N_ML_HEADS):
        gc_ref[hd] = gc[:, 4 * hd:4 * hd + 4]
        gr_ref[hd, 0:4, :] = gr[4 * hd:4 * hd + 4, :]

    cos = cos_ref[...]
    sa = sa_ref[...]
    sb = sb_ref[...]
    gsum = gsum_ref[...]

    def project(precise, q_ref, mq_ref, mk_ref, mv_ref):
        if precise:
            z = _dot_x3(_hi_lo(h), (w_ref[...], wl_ref[...]))
        else:
            z = jnp.dot(hb, w_ref[...], preferred_element_type=F32)
        proj = lambda c0, width: z[:, c0:c0 + width]
        act = F32 if precise else BF16

        n_qk = ATT_WIDTH // LANES + 1
        t_rows = hb.shape[0]
        zqk = proj(0, n_qk * LANES)
        zs = [zqk[:, c * LANES:(c + 1) * LANES] for c in range(n_qk)]
        sq = jnp.concatenate([z * z for z in zs], axis=0)
        ss = jnp.dot(jnp.concatenate(_hi_lo(sq), axis=0), gsum, preferred_element_type=F32)
        ss = ss[:n_qk * t_rows] + ss[n_qk * t_rows:]

        def headnorm_rope(c, gain):
            zn = zs[c] * lax.rsqrt(ss[c * t_rows:(c + 1) * t_rows] * (1.0 / HEAD_DIM) + EPS) * gain
            return zn * cos + pltpu.roll(zn, LANES - 16, 1) * sa + pltpu.roll(zn, 16, 1) * sb

        for c in range(n_qk - 1):
            q_ref[:, c * LANES:(c + 1) * LANES] = (headnorm_rope(c, gq_ref[...]) * 0.125).astype(act)
        k_ref[...] = headnorm_rope(n_qk - 1, gk_ref[...])
        v_ref[...] = proj(640, LANES)
        mq_ref[...] = proj(768, ML_WIDTH).astype(act)
        mk_ref[...] = (proj(1280, ML_WIDTH) * (ML_HEAD_DIM ** -0.5)).astype(act)
        mv_ref[...] = proj(1792, ML_WIDTH).astype(act)
        og_ref[...] = jax.nn.sigmoid(proj(2304, ML_WIDTH))

    @pl.when(tile < n_ctx_tiles)
    def _ctx():
        project(precise_ctx, qa_ref, mqa_ref, mka_ref, mva_ref)

    @pl.when(tile >= n_ctx_tiles)
    def _lat():
        project(False, qb_ref, mqb_ref, mkb_ref, mvb_ref)


def _inproj(xa, xb, mod3, n_ctx_tiles, tiles_per_lat_seq, ctx_row, g_mix, w_main, w_gate, b_gate,
            g_q, g_k, rope):
    w_hi, w_lo = w_main
    precise_ctx = w_lo is not None
    n_ctx, n_lat = xa.shape[0], xb.shape[0]
    n = n_ctx + n_lat
    n_tiles = n // TOKEN_TILE

    def mod_row(i):
        return jnp.where(i < n_ctx_tiles, ctx_row, (i - n_ctx_tiles) // tiles_per_lat_seq)

    def rope_blk(i):
        return jnp.where(i < n_ctx_tiles, 0, 1 + (i - n_ctx_tiles) % tiles_per_lat_seq)

    cos_t, sa_t, sb_t = rope
    lane = np.arange(LANES)
    gsum = jnp.asarray((lane[:, None] // HEAD_DIM) == (lane[None, :] // HEAD_DIM), BF16)
    tok = lambda w: pl.BlockSpec((TOKEN_TILE, w), lambda i: (i, 0))
    full = lambda a: pl.BlockSpec(a.shape, lambda i: (0,) * a.ndim)
    modspec = lambda j: pl.BlockSpec((None, 1, D_MODEL), lambda i: (mod_row(i), 0, j))
    ropespec = pl.BlockSpec((TOKEN_TILE, LANES), lambda i: (rope_blk(i), 0))
    consts = (g_mix.reshape(1, -1), w_hi) + ((w_lo,) if precise_ctx else ()) + (
        *_hi_lo(w_gate.T), b_gate.reshape(-1, 1),
        jnp.tile(g_q, 2).reshape(1, -1), jnp.tile(g_k, 2).reshape(1, -1),
        jnp.asarray(np.eye(TOKEN_TILE), BF16))
    args = (xa, xb, mod3, mod3) + consts + (cos_t, sa_t, sb_t, gsum)
    in_specs = _two_part_specs(D_MODEL, n_ctx_tiles) + [modspec(0), modspec(1)] \
        + [full(a) for a in consts] + [ropespec, ropespec, ropespec, full(gsum)]
    ctx_act = F32 if precise_ctx else BF16
    pair_shape = lambda w: [jax.ShapeDtypeStruct((n_ctx, w), ctx_act),
                            jax.ShapeDtypeStruct((n_lat, w), BF16)]
    pair_spec = lambda w: _two_part_out_specs(w, n_ctx_tiles)
    out_shape = (
        pair_shape(ATT_WIDTH)
        + [jax.ShapeDtypeStruct((n, LANES), F32),
           jax.ShapeDtypeStruct((n, LANES), F32)]
        + pair_shape(ML_WIDTH) + pair_shape(ML_WIDTH) + pair_shape(ML_WIDTH)
        + [jax.ShapeDtypeStruct((n, ML_WIDTH), F32),
           jax.ShapeDtypeStruct((N_ML_HEADS, n, 4), F32),
           jax.ShapeDtypeStruct((N_ML_HEADS, 8, n), F32)])
    out_specs = (pair_spec(ATT_WIDTH) + [tok(LANES), tok(LANES)]
                 + pair_spec(ML_WIDTH) + pair_spec(ML_WIDTH) + pair_spec(ML_WIDTH)
                 + [tok(ML_WIDTH),
                    pl.BlockSpec((N_ML_HEADS, TOKEN_TILE, 4), lambda i: (0, i, 0)),
                    pl.BlockSpec((N_ML_HEADS, 8, TOKEN_TILE), lambda i: (0, 0, i))])
    outs = pl.pallas_call(
        functools.partial(_inproj_kernel, n_ctx_tiles=n_ctx_tiles, precise_ctx=precise_ctx),
        grid=(n_tiles,), in_specs=in_specs, out_specs=tuple(out_specs),
        out_shape=tuple(out_shape), compiler_params=_params(("arbitrary",)), name="inproj",
    )(*args)
    qa, qb, k, v, mqa, mqb, mka, mkb, mva, mvb, og, gcol, grow = outs
    return (qa, qb), k, v, (mqa, mqb), (mka, mkb), (mva, mvb), og, gcol, grow


def _attn_kernel(*refs, n_kv, precise, cached):
    q_ref = refs[0]
    kv_refs = refs[1:1 + 2 * n_kv]
    o_ref = refs[1 + 2 * n_kv]
    dup_scr = refs[2 + 2 * n_kv:]
    tq = q_ref.shape[0]
    lo_q = lax.broadcasted_iota(jnp.int32, (tq, LANES), 1) < HEAD_DIM
    operand = _hi_lo if precise else (lambda a: a.astype(BF16))
    if precise:
        qk = lambda a, b: _dot_x3(a, b, NT_DIMS)
        pv = _dot_x3
    else:
        qk = lambda a, b: lax.dot_general(a, b, NT_DIMS, preferred_element_type=F32)
        pv = lambda a, b: jnp.dot(a, b, preferred_element_type=F32)

    def dup_half(ref, g):
        a = ref[...]
        r = pltpu.roll(a, HEAD_DIM, 1)
        lo = lax.broadcasted_iota(jnp.int32, a.shape, 1) < HEAD_DIM
        return operand(jnp.where(lo, a, r) if g == 0 else jnp.where(lo, r, a))

    if cached:
        @pl.when(pl.program_id(1) == 0)
        def _fill():
            for g in range(N_KV_HEADS):
                for j in range(2 * n_kv):
                    dup_scr[g * 2 * n_kv + j][...] = dup_half(kv_refs[j], g)

        dup = lambda j, g: dup_scr[g * 2 * n_kv + j][...]
    else:
        dup = lambda j, g: dup_half(kv_refs[j], g)

    for g in range(N_KV_HEADS):
        ks = [dup(2 * p, g) for p in range(n_kv)]
        vs = [dup(2 * p + 1, g) for p in range(n_kv)]
        for hb in range(2):
            c0 = (2 * g + hb) * LANES
            qb = q_ref[:, c0:c0 + LANES]
            outs = []
            for half in range(2):
                keep = lo_q if half == 0 else jnp.logical_not(lo_q)
                qm = operand(jnp.where(keep, qb, jnp.zeros_like(qb)))
                ss = [qk(qm, kd) for kd in ks]
                m = functools.reduce(jnp.maximum, [jnp.max(s, axis=1, keepdims=True) for s in ss])
                ps = [jnp.exp(s - m) for s in ss]
                den = functools.reduce(jnp.add, [jnp.sum(p, axis=1, keepdims=True) for p in ps])
                o = functools.reduce(jnp.add, [pv(operand(p), vd) for p, vd in zip(ps, vs)])
                outs.append(o / den)
            o_ref[:, c0:c0 + LANES] = jnp.where(lo_q, outs[0], outs[1]).astype(o_ref.dtype)


def _attention(q, k, v, kv_row0, n_seq, seq_len, cache=None):
    precise = q.dtype == F32
    nq = seq_len // SEQ_BLOCK
    sb0 = kv_row0 // seq_len
    in_specs = [
        pl.BlockSpec((SEQ_BLOCK, ATT_WIDTH), lambda b, i: (b * nq + i, 0)),
        pl.BlockSpec((seq_len, LANES), lambda b, i: (sb0 + b, 0)),
        pl.BlockSpec((seq_len, LANES), lambda b, i: (sb0 + b, 0)),
    ]
    args = [q, k, v]
    n_kv = 1
    if cache is not None:
        ck, cv = cache
        past = ck.shape[0] // n_seq
        in_specs += [pl.BlockSpec((past, LANES), lambda b, i: (b, 0))] * 2
        args += [ck, cv]
        n_kv = 2
    cached = nq > 1 and not precise
    scratch = [pltpu.VMEM((a.shape[0] // n_seq if j >= 2 else seq_len, LANES), BF16)
               for _ in range(N_KV_HEADS) for j, a in enumerate(args[1:])] if cached else []
    return pl.pallas_call(
        functools.partial(_attn_kernel, n_kv=n_kv, precise=precise, cached=cached),
        grid=(n_seq, nq), in_specs=in_specs,
        out_specs=pl.BlockSpec((SEQ_BLOCK, ATT_WIDTH), lambda b, i: (b * nq + i, 0)),
        out_shape=jax.ShapeDtypeStruct((n_seq * seq_len, ATT_WIDTH), q.dtype),
        scratch_shapes=scratch,
        compiler_params=_params(("parallel", "arbitrary" if cached else "parallel")),
        name="attention",
    )(*args)


def _mlstm_kernel(*refs, heads, **static):
    it = iter(refs)
    q_ref, k_ref, v_ref, og_ref, gmh_ref, gcol_ref, grow_ref, u_ref, l_ref = [next(it) for _ in range(9)]
    init_refs = (next(it), next(it)) if static["has_init"] else ()
    ml_ref = next(it)
    state_refs = (next(it), next(it)) if static["emit_state"] else ()
    for hd in range(heads):
        lanes = pl.ds(hd * ML_HEAD_DIM, ML_HEAD_DIM)
        head_refs = [q_ref.at[:, lanes], k_ref.at[:, lanes], v_ref.at[:, lanes], og_ref.at[:, lanes],
                     gmh_ref.at[:, lanes], gcol_ref.at[hd], grow_ref.at[hd], u_ref, l_ref]
        if init_refs:
            head_refs += [init_refs[0].at[:, hd], init_refs[1].at[hd]]
        head_refs.append(ml_ref.at[:, lanes])
        if state_refs:
            head_refs += [state_refs[0].at[:, hd], state_refs[1].at[hd]]
        _mlstm_head(*head_refs, **static)


def _mlstm_head(*refs, seq_len, has_init, emit_state, precise):
    it = iter(refs)
    q_ref, k_ref, v_ref, og_ref, gmh_ref, gcol_ref, grow_ref, u_ref, l_ref = [next(it) for _ in range(9)]
    if has_init:
        c0_ref, nm0_ref = next(it), next(it)
    ml_ref = next(it)
    if emit_state:
        cf_ref, nmf_ref = next(it), next(it)

    operand = _hi_lo if precise else (lambda a: a.astype(BF16))
    if precise:
        qk = lambda a, b: _dot_x3(a, b, NT_DIMS)
        pv = _dot_x3
    else:
        qk = lambda a, b: lax.dot_general(a, b, NT_DIMS, preferred_element_type=F32)
        pv = lambda a, b: jnp.dot(a, b, preferred_element_type=F32)

    bq = SEQ_BLOCK
    nb = seq_len // bq
    blk = lambda j: slice(j * bq, (j + 1) * bq)
    upper_incl = u_ref[...]
    lower_incl = l_ref[...]

    def tri_dot(x, tri):
        return functools.reduce(jnp.add, [jnp.dot(p, tri, preferred_element_type=F32)
                                          for p in _split3(x)])

    ig_row = [[None] * nb for _ in range(2)]
    lf_row = [[None] * nb for _ in range(2)]
    within = [[None] * nb for _ in range(2)]
    bsum = [[None] * nb for _ in range(2)]
    for j in range(nb):
        g8 = grow_ref[:, blk(j)]
        cum_f = tri_dot(g8, upper_incl)
        cum_b = tri_dot(g8, lower_incl)
        for d in range(2):
            ig_row[d][j] = g8[d:d + 1, :]
            lf_row[d][j] = g8[2 + d:3 + d, :]
            within[d][j] = (cum_f if d == 0 else cum_b)[2 + d:3 + d, :]
            bsum[d][j] = jnp.sum(lf_row[d][j], axis=1, keepdims=True)
    zero11 = jnp.zeros((1, 1), F32)
    r_i = lax.broadcasted_iota(jnp.int32, (bq, bq), 0)
    c_i = lax.broadcasted_iota(jnp.int32, (bq, bq), 1)
    causal = [c_i <= r_i, c_i >= r_i]
    gmh = gmh_ref[...]

    q_blocks = [q_ref[blk(i), :] for i in range(nb)]
    scores = [qk(operand(q_blocks[i]), operand(k_ref[blk(i), :])) for i in range(nb)]

    h_dir = [[None] * nb for _ in range(2)]
    final = [None, None]
    for d in range(2):
        if has_init:
            state = (c0_ref[d], nm0_ref[d:d + 1, :])
            m = nm0_ref[2 + d:3 + d, 0:1]
        else:
            state, m = None, zero11
        scan = range(nb) if d == 0 else range(nb - 1, -1, -1)
        for step, i in enumerate(scan):
            q_i = q_blocks[i]
            a_loc = ig_row[d][i] - within[d][i]
            m_col = jnp.maximum(m, jnp.max(jnp.where(causal[d], a_loc, NEG_INF),
                                           axis=1, keepdims=True))
            if nb > 1:
                widen = lambda col: jnp.broadcast_to(col, (bq, LANES))
                tri = lower_incl if d == 0 else upper_incl
                b_rep = functools.reduce(jnp.add, [
                    lax.dot_general(tri, jnp.broadcast_to(piece, (LANES, bq)), NT_DIMS,
                                    preferred_element_type=F32)
                    for piece in _hi_lo(lf_row[d][i])])
                m_rep = widen(m_col)
                m_wide = jnp.concatenate([m_rep] * (bq // LANES), axis=1)
            else:
                widen = lambda col: col
                b_rep = jnp.sum(jnp.where(causal[d], lf_row[d][i], 0.0), axis=1, keepdims=True)
                m_rep = m_wide = m_col
            p = jnp.exp(jnp.where(causal[d], a_loc - m_wide, NEG_INF)) * scores[i]
            den = widen(jnp.sum(p, axis=1, keepdims=True))
            num = pv(operand(p), operand(v_ref[blk(i), :]))
            if state is not None:
                c_prev, n_prev = state
                w_inter = jnp.exp(m - m_rep)
                q_b = q_i.astype(BF16)
                qc = jnp.dot(q_b, c_prev.astype(BF16), preferred_element_type=F32)
                n_rows = jnp.broadcast_to(n_prev, (LANES, ML_HEAD_DIM)).astype(BF16)
                qn = lax.dot_general(q_b, n_rows, NT_DIMS, preferred_element_type=F32)
                num = num + w_inter * qc
                den = den + w_inter * qn
            nrm = jnp.maximum(jnp.abs(den), jnp.exp(-(b_rep + m_rep)))
            h_dir[d][i] = num / nrm
            if step == nb - 1 and not emit_state:
                break
            m_last = jnp.maximum(m, jnp.max(a_loc, axis=1, keepdims=True))
            a_col = widen(gcol_ref[blk(i), d:d + 1]) - b_rep
            kw = k_ref[blk(i), :].astype(F32) * jnp.exp(a_col - m_last)
            c_new = lax.dot_general(kw.astype(BF16), v_ref[blk(i), :].astype(BF16), TN_DIMS,
                                    preferred_element_type=F32)
            n_new = jnp.sum(kw, axis=0, keepdims=True)
            if state is not None:
                decay = jnp.exp(m - m_last)
                c_new = c_new + decay * state[0]
                n_new = n_new + decay * state[1]
            state = (c_new, n_new)
            m = bsum[d][i] + m_last
        final[d] = (state, m)

    for i in range(nb):
        h = h_dir[0][i] + h_dir[1][i]
        hn = h * lax.rsqrt(jnp.mean(h * h, axis=-1, keepdims=True) + EPS) * gmh
        ml_ref[blk(i), :] = (og_ref[blk(i), :] * hn).astype(ml_ref.dtype)

    if emit_state:
        nmf_ref[...] = jnp.zeros_like(nmf_ref)
        for d in range(2):
            (c_fin, n_fin), m_fin = final[d]
            cf_ref[d] = c_fin
            nmf_ref[d:d + 1, :] = n_fin
            nmf_ref[2 + d:3 + d, :] = jnp.broadcast_to(m_fin, (1, ML_HEAD_DIM))


def _mlstm(mq, mk, mv, og, g_mh, gcol, grow, row0, n_seq, seq_len, init=None, emit_state=False):
    precise = mq.dtype == F32
    assert not (precise and init is not None)
    sb0 = row0 // seq_len
    tri = np.arange(SEQ_BLOCK)
    upper_incl = jnp.asarray(tri[:, None] <= tri[None, :], BF16)
    lower_incl = jnp.asarray(tri[:, None] >= tri[None, :], BF16)
    heads = MLSTM_HEADS_PER_STEP
    width = heads * ML_HEAD_DIM
    ownblk = lambda: pl.BlockSpec((seq_len, width), lambda b, h: (b, h))
    headblk = lambda: pl.BlockSpec((seq_len, width), lambda b, h: (sb0 + b, h))
    const = lambda a: pl.BlockSpec(a.shape, lambda b, h: (0,) * a.ndim)
    in_specs = [ownblk(), ownblk(), ownblk(), headblk(),
                pl.BlockSpec((1, width), lambda b, h: (0, h)),
                pl.BlockSpec((heads, seq_len, 4), lambda b, h: (h, sb0 + b, 0)),
                pl.BlockSpec((heads, 8, seq_len), lambda b, h: (h, 0, sb0 + b)),
                const(upper_incl), const(lower_incl)]
    args = [mq, mk, mv, og, g_mh.reshape(1, -1), gcol, grow, upper_incl, lower_incl]
    if init is not None:
        c0, nm0, layer = init
        in_specs += [
            pl.BlockSpec((None, None, 2, heads, ML_HEAD_DIM, ML_HEAD_DIM),
                         lambda b, h: (b, layer, 0, h, 0, 0)),
            pl.BlockSpec((None, heads, 8, ML_HEAD_DIM), lambda b, h: (b, h, 0, 0))]
        args += [c0, nm0]
    out_shape = [jax.ShapeDtypeStruct((n_seq * seq_len, ML_WIDTH), mq.dtype)]
    out_specs = [pl.BlockSpec((seq_len, width), lambda b, h: (b, h))]
    if emit_state:
        out_shape += [jax.ShapeDtypeStruct((n_seq, 2, N_ML_HEADS, ML_HEAD_DIM, ML_HEAD_DIM), F32),
                      jax.ShapeDtypeStruct((n_seq, N_ML_HEADS, 8, ML_HEAD_DIM), F32)]
        out_specs += [pl.BlockSpec((None, 2, heads, ML_HEAD_DIM, ML_HEAD_DIM),
                                   lambda b, h: (b, 0, h, 0, 0)),
                      pl.BlockSpec((None, heads, 8, ML_HEAD_DIM), lambda b, h: (b, h, 0, 0))]
    return pl.pallas_call(
        functools.partial(_mlstm_kernel, heads=heads, seq_len=seq_len, has_init=init is not None,
                          emit_state=emit_state, precise=precise),
        grid=(n_seq, N_ML_HEADS // heads), in_specs=in_specs, out_specs=tuple(out_specs),
        out_shape=tuple(out_shape),
        compiler_params=_params(("parallel", "parallel")), name="mlstm",
    )(*args)


def _outproj_kernel(*refs, n_ctx_tiles, precise_ctx):
    it = iter(refs)
    atta_ref, attb_ref, mla_ref, mlb_ref, xa_ref, xb_ref, w_ref = [next(it) for _ in range(7)]
    wl_ref = next(it) if precise_ctx else None
    gt_ref, sh_ref, sc_ref, g_ref, wrt_ref, wrtl_ref, x1_ref, hp_ref, logits_ref, y_scr = it
    i = pl.program_id(0)

    def mix(att, ml):
        return (jnp.dot(att, w_ref[:ATT_WIDTH, :], preferred_element_type=F32)
                + jnp.dot(ml, w_ref[ATT_WIDTH:, :], preferred_element_type=F32))

    @pl.when(i < n_ctx_tiles)
    def _ctx():
        if precise_ctx:
            y_scr[...] = (
                _dot_x3(_hi_lo(atta_ref[...]), (w_ref[:ATT_WIDTH, :], wl_ref[:ATT_WIDTH, :]))
                + _dot_x3(_hi_lo(mla_ref[...]), (w_ref[ATT_WIDTH:, :], wl_ref[ATT_WIDTH:, :])))
        else:
            y_scr[...] = mix(atta_ref[...], mla_ref[...])

    @pl.when(i >= n_ctx_tiles)
    def _lat():
        y_scr[...] = mix(attb_ref[...], mlb_ref[...])

    x1 = _two_part(i, n_ctx_tiles, xa_ref, xb_ref) + gt_ref[...] * y_scr[...]
    x1_ref[...] = x1
    ms = jnp.mean(x1 * x1, axis=-1, keepdims=True)
    h2 = x1 * lax.rsqrt(ms + EPS) * g_ref[...]
    h2 = h2 * (1.0 + sc_ref[...]) + sh_ref[...]
    for k in range(ROW_TILE_SUBLANES):
        hp_ref[pl.ds(k, TOKEN_TILE, stride=ROW_TILE_SUBLANES), :] = h2[:, k * LANES:(k + 1) * LANES]

    logits_ref[...] = _dot_x3((wrt_ref[...], wrtl_ref[...]), _hi_lo(h2), NT_DIMS)


def _select_experts(logits, b_col):
    ex = jnp.exp(logits - jnp.max(logits, axis=0, keepdims=True))
    scores = ex / jnp.sum(ex, axis=0, keepdims=True)
    sel = scores + b_col
    row = lambda a, e: a[e:e + 1, :]
    grp_score = []
    for g in range(N_GROUPS):
        xs = [row(sel, g * GROUP_SIZE + j) for j in range(GROUP_SIZE)]
        pairs = [xs[a] + xs[b] for a in range(GROUP_SIZE) for b in range(a + 1, GROUP_SIZE)]
        grp_score.append(functools.reduce(jnp.maximum, pairs))
    best = grp_score[0]
    grp = jnp.zeros_like(best, dtype=jnp.int32)
    for g in range(1, N_GROUPS):
        better = grp_score[g] > best
        grp = jnp.where(better, g, grp)
        best = jnp.where(better, grp_score[g], best)
    pick = lambda a, j: functools.reduce(
        lambda acc, g: jnp.where(grp == g, row(a, g * GROUP_SIZE + j), acc),
        range(1, N_GROUPS), row(a, j))
    xs = [pick(sel, j) for j in range(GROUP_SIZE)]
    ws = [pick(scores, j) for j in range(GROUP_SIZE)]

    def argmax4(vals):
        bv, bi = vals[0], jnp.zeros_like(grp)
        for j in range(1, GROUP_SIZE):
            better = vals[j] > bv
            bi = jnp.where(better, j, bi)
            bv = jnp.where(better, vals[j], bv)
        return bi

    i1 = argmax4(xs)
    i2 = argmax4([jnp.where(i1 == j, NEG_INF, xs[j]) for j in range(GROUP_SIZE)])
    take = lambda vals, idx: functools.reduce(
        lambda acc, j: jnp.where(idx == j, vals[j], acc), range(1, GROUP_SIZE), vals[0])
    w1, w2 = take(ws, i1), take(ws, i2)
    wsum = w1 + w2
    w1, w2 = w1 / wsum, w2 / wsum
    return grp * GROUP_SIZE + i1, grp * GROUP_SIZE + i2, w1, w2


def _outproj(att, ml, x, mod3, n_ctx_tiles, tiles_per_lat_seq, ctx_row, w_out, g_ffn, w_router):
    w_hi, w_lo = w_out
    precise_ctx = w_lo is not None
    n = x[0].shape[0] + x[1].shape[0]

    def mod_row(i):
        return jnp.where(i < n_ctx_tiles, ctx_row, (i - n_ctx_tiles) // tiles_per_lat_seq)

    tok = lambda w: pl.BlockSpec((TOKEN_TILE, w), lambda i: (i, 0))
    full = lambda a: pl.BlockSpec(a.shape, lambda i: (0,) * a.ndim)
    modspec = lambda j: pl.BlockSpec((None, 1, D_MODEL), lambda i: (mod_row(i), 0, j))
    weights = (w_hi, w_lo) if precise_ctx else (w_hi,)
    consts = (g_ffn.reshape(1, -1), *_hi_lo(w_router.T))
    args = (*att, *ml, *x, *weights, mod3, mod3, mod3) + consts
    in_specs = (_two_part_specs(ATT_WIDTH, n_ctx_tiles) + _two_part_specs(ML_WIDTH, n_ctx_tiles)
                + _two_part_specs(D_MODEL, n_ctx_tiles) + [full(w) for w in weights]
                + [modspec(2), modspec(3), modspec(4)] + [full(a) for a in consts])
    return pl.pallas_call(
        functools.partial(_outproj_kernel, n_ctx_tiles=n_ctx_tiles, precise_ctx=precise_ctx),
        grid=(n // TOKEN_TILE,), in_specs=in_specs,
        scratch_shapes=[pltpu.VMEM((TOKEN_TILE, D_MODEL), F32)],
        out_specs=(tok(D_MODEL),
                   pl.BlockSpec((TOKEN_TILE * ROW_TILE_SUBLANES, LANES), lambda i: (i, 0)),
                   pl.BlockSpec((N_EXPERTS, TOKEN_TILE), lambda i: (0, i))),
        out_shape=(jax.ShapeDtypeStruct((n, D_MODEL), F32),
                   jax.ShapeDtypeStruct((n * ROW_TILE_SUBLANES, LANES), F32),
                   jax.ShapeDtypeStruct((N_EXPERTS, n), F32)),
        compiler_params=_params(("parallel",)), name="outproj_router",
    )(*args)


def _moe_kernel(order_ref, pos0_ref, pos1_ref, off_ref, cnt_ref,
                h_ref, wg_ref, wu_ref, wd_ref, x1_ref, gt_ref, wcol_ref, ya_ref, yb_ref,
                o_scr, xs_scr, comb0_scr, comb1_scr, *, n_ctx_tiles):
    t = pl.program_id(0)
    s = pl.program_id(1)
    sub = ROW_TILE_SUBLANES
    groups = MOE_CHUNK // 8

    def tile(ref, row):
        return ref.at[pl.ds(pl.multiple_of(row * sub, sub), sub), :]

    def slab(ref, row0, n_rows, k):
        return ref.at[pl.ds(row0 * sub + k, n_rows, stride=sub), :]

    def gather_rows(buf, slot0):
        for j in range(MOE_CHUNK):
            tile(xs_scr, buf * MOE_CHUNK + j)[...] = h_ref[order_ref[slot0 + j]]

    def ffn(buf, base):
        a = jnp.zeros((MOE_CHUNK, D_EXPERT), F32)
        b = jnp.zeros((MOE_CHUNK, D_EXPERT), F32)
        for p in range(sub // 2):
            lhs = jnp.concatenate([slab(xs_scr, buf * MOE_CHUNK, MOE_CHUNK, 2 * p)[...],
                                   slab(xs_scr, buf * MOE_CHUNK, MOE_CHUNK, 2 * p + 1)[...]],
                                  axis=1).astype(BF16)
            rows = slice(2 * p * LANES, (2 * p + 2) * LANES)
            a = a + jnp.dot(lhs, wg_ref[rows, :], preferred_element_type=F32)
            b = b + jnp.dot(lhs, wu_ref[rows, :], preferred_element_type=F32)
        hid = (a * jax.nn.sigmoid(a)) * b
        out = jnp.dot(hid.astype(BF16), wd_ref[...], preferred_element_type=F32)
        for k in range(sub):
            slab(o_scr, base, MOE_CHUNK, k)[...] = out[:, k * LANES:(k + 1) * LANES]

    @pl.when(s == 0)
    def _first():
        gather_rows(0, t * MOE_SLOTS + pl.multiple_of(off_ref[t * N_EXPERTS], 8))

    @pl.when(s < N_EXPERTS)
    def _expert():
        seg = t * N_EXPERTS + s
        off = pl.multiple_of(off_ref[seg], 8)
        cur = s % 2
        nxt_seg = t * N_EXPERTS + jnp.minimum(s + 1, N_EXPERTS - 1)
        ffn(cur, off)
        gather_rows(1 - cur, t * MOE_SLOTS + pl.multiple_of(off_ref[nxt_seg], 8))

        def extra(c, carry):
            base = pl.multiple_of(off + c * MOE_CHUNK, 8)

            def gather8(i, carry2):
                slot = t * MOE_SLOTS + base + i * 8
                for k in range(8):
                    tile(xs_scr, 2 * MOE_CHUNK + i * 8 + k)[...] = h_ref[order_ref[slot + k]]
                return carry2

            lax.fori_loop(0, groups, gather8, 0)
            ffn(2, base)
            return carry

        lax.fori_loop(1, (cnt_ref[seg] + MOE_CHUNK - 1) // MOE_CHUNK, extra, 0)

    @pl.when(s >= N_EXPERTS)
    def _combine():
        tok0 = t * MOE_TILE + (s - N_EXPERTS) * MOE_OUT_TILE

        def body8(i, carry):
            for k in range(8):
                p0 = pos0_ref[tok0 + i * 8 + k]
                p1 = pos1_ref[tok0 + i * 8 + k]
                tile(comb0_scr, i * 8 + k)[...] = tile(o_scr, p0)[...]
                tile(comb1_scr, i * 8 + k)[...] = tile(o_scr, p1)[...]
            return carry

        lax.fori_loop(0, MOE_OUT_TILE // 8, body8, 0)
        wcol = wcol_ref[...]
        rows_of = lambda ref: jnp.concatenate(
            [slab(ref, 0, MOE_OUT_TILE, k)[...] for k in range(sub)], axis=1)
        comb = wcol[:, 2:3] * rows_of(comb0_scr) + wcol[:, 3:4] * rows_of(comb1_scr)
        y = x1_ref[...] + gt_ref[...] * comb
        chunk_ix = t * (MOE_TILE // MOE_OUT_TILE) + s - N_EXPERTS

        @pl.when(chunk_ix < n_ctx_tiles)
        def _ctx():
            ya_ref[...] = y

        @pl.when(chunk_ix >= n_ctx_tiles)
        def _lat():
            yb_ref[...] = y


def _route_tables(logits, b_router):
    n = logits.shape[1]
    nt = n // MOE_TILE
    tri = np.arange(SEQ_BLOCK)
    strict_upper = jnp.asarray(tri[:, None] < tri[None, :], BF16)
    tok = np.arange(MOE_TILE)
    digits = np.zeros((MOE_TILE, LANES), np.float32)
    digits[:, 0] = tok % 256
    digits[:, 1] = tok // 256
    pos, order, meta, wcol = pl.pallas_call(
        _route_kernel, grid=(nt,),
        in_specs=[pl.BlockSpec((N_EXPERTS, MOE_TILE), lambda t: (0, t)),
                  pl.BlockSpec((N_EXPERTS, 1), lambda t: (0, 0)),
                  pl.BlockSpec((SEQ_BLOCK, SEQ_BLOCK), lambda t: (0, 0)),
                  pl.BlockSpec((SEQ_BLOCK, SEQ_BLOCK), lambda t: (0, 0)),
                  pl.BlockSpec((MOE_TILE, LANES), lambda t: (0, 0))],
        out_specs=(pl.BlockSpec((8, MOE_TILE), lambda t: (0, t)),
                   pl.BlockSpec((None, MOE_SLOTS, 1), lambda t: (t, 0, 0)),
                   pl.BlockSpec((None, N_EXPERTS, 8), lambda t: (t, 0, 0)),
                   pl.BlockSpec((MOE_TILE, 8), lambda t: (t, 0))),
        out_shape=(jax.ShapeDtypeStruct((8, n), jnp.int32),
                   jax.ShapeDtypeStruct((nt, MOE_SLOTS, 1), jnp.int32),
                   jax.ShapeDtypeStruct((nt, N_EXPERTS, 8), jnp.int32),
                   jax.ShapeDtypeStruct((n, 8), F32)),
        scratch_shapes=[pltpu.VMEM((8, MOE_TILE), F32)],
        compiler_params=_params(("parallel",)), name="route_tables",
    )(logits, b_router.reshape(-1, 1), strict_upper, jnp.asarray(np.eye(SEQ_BLOCK), BF16),
      jnp.asarray(digits, BF16))
    return (order.reshape(-1), pos[0], pos[1], meta[:, :, 0].reshape(-1),
            meta[:, :, 1].reshape(-1), wcol)


def _route_kernel(logits_ref, br_ref, su_ref, eye_ref, digits_ref,
                  pos_ref, order_ref, meta_ref, wcol_ref, wrow_scr):
    e1, e2, w1, w2 = _select_experts(logits_ref[...], br_ref[...])
    wrow_scr[...] = jnp.zeros_like(wrow_scr)
    wrow_scr[2:3, :] = w1
    wrow_scr[3:4, :] = w2
    eye = eye_ref[...]
    for b in range(MOE_TILE // SEQ_BLOCK):
        cols = slice(b * SEQ_BLOCK, (b + 1) * SEQ_BLOCK)
        wcol_ref[cols, :] = functools.reduce(jnp.add, [
            lax.dot_general(eye, p, NT_DIMS, preferred_element_type=F32)
            for p in _split3(wrow_scr[:, cols])])
    eid = lax.broadcasted_iota(jnp.int32, (N_EXPERTS, MOE_TILE), 0)
    oh1, oh2 = eid == e1, eid == e2
    oh = jnp.where(oh1, 1.0, 0.0) + jnp.where(oh2, 1.0, 0.0)
    nblk = MOE_TILE // SEQ_BLOCK
    blocks = [oh[:, b * SEQ_BLOCK:(b + 1) * SEQ_BLOCK] for b in range(nblk)]
    inner = jnp.dot(jnp.concatenate(blocks, axis=0).astype(BF16), su_ref[...],
                    preferred_element_type=F32)
    run = jnp.zeros((N_EXPERTS, 1), F32)
    ranks = []
    for b in range(nblk):
        ranks.append(inner[b * N_EXPERTS:(b + 1) * N_EXPERTS, :] + run)
        run = run + jnp.sum(blocks[b], axis=1, keepdims=True)
    count = run
    seg = jnp.floor((count + 7.0) * 0.125) * 8.0
    sub = lax.broadcasted_iota(jnp.int32, (N_EXPERTS, 1), 0)
    off = jnp.zeros((N_EXPERTS, 1), F32)
    for e in range(N_EXPERTS - 1):
        off = off + jnp.where(sub > e, seg[e:e + 1, :], 0.0)
    slot = jnp.concatenate(ranks, axis=1) + off
    pos1 = jnp.sum(jnp.where(oh1, slot, 0.0), axis=0, keepdims=True).astype(jnp.int32)
    pos2 = jnp.sum(jnp.where(oh2, slot, 0.0), axis=0, keepdims=True).astype(jnp.int32)
    pos_ref[...] = jnp.zeros_like(pos_ref)
    pos_ref[0:1, :] = pos1
    pos_ref[1:2, :] = pos2
    meta_ref[...] = jnp.zeros_like(meta_ref)
    meta_ref[:, 0:1] = off.astype(jnp.int32)
    meta_ref[:, 1:2] = count.astype(jnp.int32)
    digits = digits_ref[...]
    rows = MOE_SLOTS // ROUTE_SLOT_BLOCKS
    for sb in range(ROUTE_SLOT_BLOCKS):
        j = lax.broadcasted_iota(jnp.int32, (rows, MOE_TILE), 0) + sb * rows
        hit = jnp.where(j == pos1, 1.0, 0.0) + jnp.where(j == pos2, 1.0, 0.0)
        d = jnp.dot(hit.astype(BF16), digits, preferred_element_type=F32)
        order_ref[sb * rows:(sb + 1) * rows, :] = (d[:, 0:1] + 256.0 * d[:, 1:2]).astype(jnp.int32)


def _moe(hp, logits, b_router, x1, mod3, layer, n_ctx_tiles, tiles_per_lat_seq, ctx_row,
         wg, wu, wd):
    n = hp.shape[0] // ROW_TILE_SUBLANES
    order, pos0, pos1, off, count, wcol = _route_tables(logits, b_router)
    chunks_per_tile = MOE_TILE // MOE_OUT_TILE
    n_steps = N_EXPERTS + chunks_per_tile

    def chunk_ix(t, s):
        return t * chunks_per_tile + jnp.maximum(s - N_EXPERTS, 0)

    def mod_row(g):
        return jnp.where(g < n_ctx_tiles, ctx_row, (g - n_ctx_tiles) // tiles_per_lat_seq)

    wspec = lambda r, c: pl.BlockSpec(
        (None, None, r, c), lambda t, s, *_: (layer, jnp.minimum(s, N_EXPERTS - 1), 0, 0))
    chunk_spec = pl.BlockSpec((MOE_OUT_TILE, D_MODEL), lambda t, s, *_: (chunk_ix(t, s), 0))
    grid_spec = pltpu.PrefetchScalarGridSpec(
        num_scalar_prefetch=5,
        grid=(n // MOE_TILE, n_steps),
        in_specs=[
            pl.BlockSpec((MOE_TILE, ROW_TILE_SUBLANES, LANES), lambda t, s, *_: (t, 0, 0)),
            wspec(D_MODEL, D_EXPERT), wspec(D_MODEL, D_EXPERT), wspec(D_EXPERT, D_MODEL),
            chunk_spec,
            pl.BlockSpec((None, 1, D_MODEL), lambda t, s, *_: (mod_row(chunk_ix(t, s)), 0, 5)),
            pl.BlockSpec((MOE_OUT_TILE, 8), lambda t, s, *_: (chunk_ix(t, s), 0)),
        ],
        out_specs=(
            pl.BlockSpec((MOE_OUT_TILE, D_MODEL),
                         lambda t, s, *_: (jnp.minimum(chunk_ix(t, s), n_ctx_tiles - 1), 0)),
            pl.BlockSpec((MOE_OUT_TILE, D_MODEL),
                         lambda t, s, *_: (jnp.maximum(chunk_ix(t, s) - n_ctx_tiles, 0), 0))),
        scratch_shapes=[pltpu.VMEM((MOE_SLOTS * ROW_TILE_SUBLANES, LANES), F32),
                        pltpu.VMEM((3 * MOE_CHUNK * ROW_TILE_SUBLANES, LANES), F32),
                        pltpu.VMEM((MOE_OUT_TILE * ROW_TILE_SUBLANES, LANES), F32),
                        pltpu.VMEM((MOE_OUT_TILE * ROW_TILE_SUBLANES, LANES), F32)],
    )
    n_ctx = n_ctx_tiles * MOE_OUT_TILE
    return pl.pallas_call(
        functools.partial(_moe_kernel, n_ctx_tiles=n_ctx_tiles), grid_spec=grid_spec,
        out_shape=(jax.ShapeDtypeStruct((n_ctx, D_MODEL), F32),
                   jax.ShapeDtypeStruct((n - n_ctx, D_MODEL), F32)),
        compiler_params=_params(("arbitrary", "arbitrary")), name="experts",
    )(order, pos0, pos1, off, count, hp.reshape(n, ROW_TILE_SUBLANES, LANES), wg, wu, wd, x1,
      mod3, wcol)


def _rope_tables(seq_len):
    half = HEAD_DIM // 2
    freqs = ROPE_BASE ** (-np.arange(0, half, 2, dtype=np.float64) / half)
    pos = np.arange(seq_len)
    row, col = pos // GRID_W, pos % GRID_W
    d = np.arange(HEAD_DIM)
    position = np.where(d[None, :] < half, row[:, None], col[:, None]).astype(np.float64)
    ang = (position.astype(np.float32) * freqs.astype(np.float32)[d % (half // 2)][None, :]).astype(np.float32)
    cos, sin = np.cos(ang), np.sin(ang)
    first = (d % half) < half // 2
    sa = np.where(first[None, :], -sin, 0.0)
    sb = np.where(first[None, :], 0.0, sin)
    ident = lambda v: np.full((TOKEN_TILE, HEAD_DIM), v, np.float32)
    stack = lambda ctx, lat: jnp.asarray(
        np.tile(np.concatenate([ctx, lat.astype(np.float32)], axis=0), (1, 2)), F32)
    return stack(ident(1.0), cos), stack(ident(0.0), sa), stack(ident(0.0), sb)


def kernel(x_prompt, x_sample, c, cache_k, cache_v, state_C, state_n, state_m, c_ctx, w_mod, b_mod,
           g_mix, g_ffn, w_in, b_igate, b_fgate, g_q, g_k, g_mh, w_out, w_router, b_router,
           w_e_gate, w_e_up, w_e_down):
    n_ctx_seq, ctx_len, _ = x_prompt.shape
    n_lat_seq, lat_len, _ = x_sample.shape
    n_layers = w_mod.shape[0]
    n_ctx = n_ctx_seq * ctx_len
    assert ctx_len == SEQ_BLOCK and lat_len % TOKEN_TILE == 0
    assert n_ctx % MOE_TILE == 0 and (n_lat_seq * lat_len) % MOE_TILE == 0
    assert n_lat_seq < 16 and n_ctx % lat_len == 0
    n_ctx_tiles = n_ctx // TOKEN_TILE
    tiles_per_lat_seq = lat_len // TOKEN_TILE
    ctx_row = n_lat_seq

    x = (x_prompt.reshape(n_ctx, D_MODEL), x_sample.reshape(-1, D_MODEL))
    cond = jnp.zeros((16, D_MODEL), F32).at[:n_lat_seq].set(c).at[ctx_row].set(c_ctx)
    mod = _modulation(cond, w_mod, b_mod)
    rope = _rope_tables(lat_len)

    wg_b, wu_b, wd_b = w_e_gate.astype(BF16), w_e_up.astype(BF16), w_e_down.astype(BF16)
    gate_perm = np.array([(q % 2) * N_ML_HEADS + hd + 2 * N_ML_HEADS * (q // 2)
                          for hd in range(N_ML_HEADS) for q in range(4)])

    ks, vs, cs, ns, ms = [], [], [], [], []
    for l in range(n_layers):
        mod3 = mod[l].reshape(16, 1, -1)
        precise_ctx = l < n_layers - 1

        def weight_pair(w):
            hi = w.astype(BF16)
            return hi, ((w - hi.astype(F32)).astype(BF16) if precise_ctx else None)

        w_main = weight_pair(w_in[l, :, :MAIN_WIDTH])
        w_gate = w_in[l, :, MAIN_WIDTH:][:, gate_perm]
        b_gate = jnp.concatenate([b_igate[l].reshape(-1), b_fgate[l].reshape(-1)])[gate_perm]
        q, k, v, mq, mk, mv, og, gcol, grow = _inproj(
            *x, mod3, n_ctx_tiles, tiles_per_lat_seq, ctx_row, g_mix[l], w_main, w_gate, b_gate,
            g_q[l], g_k[l], rope)

        att_ctx = _attention(q[0], k, v, 0, n_ctx_seq, ctx_len)
        past = cache_k.shape[2]
        ck = cache_k[:, l].reshape(n_lat_seq * past, LANES)
        cv = cache_v[:, l].reshape(n_lat_seq * past, LANES)
        att_lat = _attention(q[1], k, v, n_ctx, n_lat_seq, lat_len, cache=(ck, cv))

        ml_ctx, c_fin, nm_fin = _mlstm(mq[0], mk[0], mv[0], og, g_mh[l], gcol, grow, 0,
                                       n_ctx_seq, ctx_len, emit_state=True)
        n0 = state_n[:, l].transpose(0, 2, 1, 3)
        m0 = jnp.broadcast_to(state_m[:, l].transpose(0, 2, 1)[..., None], n0.shape)
        nm0 = jnp.concatenate([n0, m0, jnp.zeros_like(n0), jnp.zeros_like(n0)], axis=2)
        (ml_lat,) = _mlstm(mq[1], mk[1], mv[1], og, g_mh[l], gcol, grow, n_ctx, n_lat_seq,
                           lat_len, init=(state_C, nm0, l))

        x1, hp, logits = _outproj(
            (att_ctx, att_lat), (ml_ctx, ml_lat), x, mod3, n_ctx_tiles, tiles_per_lat_seq, ctx_row,
            weight_pair(w_out[l]), g_ffn[l], w_router)
        x = _moe(hp, logits, b_router, x1, mod3, l, n_ctx // MOE_OUT_TILE,
                 lat_len // MOE_OUT_TILE, ctx_row, wg_b, wu_b, wd_b)

        ks.append(k[:n_ctx].reshape(n_ctx_seq, ctx_len, N_KV_HEADS, HEAD_DIM))
        vs.append(v[:n_ctx].reshape(n_ctx_seq, ctx_len, N_KV_HEADS, HEAD_DIM))
        cs.append(c_fin)
        ns.append(nm_fin[:, :, 0:2, :].transpose(0, 2, 1, 3))
        ms.append(nm_fin[:, :, 2:4, 0].transpose(0, 2, 1))

    y_prompt = x[0].reshape(x_prompt.shape)
    y_sample = x[1].reshape(x_sample.shape)
    return (y_prompt, y_sample, jnp.stack(ks, axis=1), jnp.stack(vs, axis=1),
            jnp.stack(cs, axis=1), jnp.stack(ns, axis=1), jnp.stack(ms, axis=1))
```

```python
import functools

import numpy as np
import jax
import jax.numpy as jnp
from jax import lax
from jax.experimental import pallas as pl
from jax.experimental.pallas import tpu as pltpu

F32 = jnp.float32
BF16 = jnp.bfloat16

D_MODEL = 1024
HEAD_DIM = 64
ATT_WIDTH = 512
N_KV_HEADS = 2
ML_WIDTH = 512
N_ML_HEADS = 4
ML_HEAD_DIM = 128
GRID_W = 64
ROPE_BASE = 10000.0
N_EXPERTS = 16
N_GROUPS = 4
GROUP_SIZE = 4
D_EXPERT = 512
EPS = 1e-6
MAIN_WIDTH = 2816
N_GATE_COLS = 16
LANES = 128
TOKEN_TILE = 512
SEQ_BLOCK = 256
MOE_TILE = 2048
MOE_CHUNK = 320
ROW_TILE_SUBLANES = D_MODEL // LANES
MLSTM_HEADS_PER_STEP = 4
MOE_OUT_TILE = 256
ROUTE_SLOT_BLOCKS = 9
MOE_SLOTS = -(-(2 * MOE_TILE + 8 * N_EXPERTS + MOE_CHUNK) // (512 * ROUTE_SLOT_BLOCKS)) * 512 * ROUTE_SLOT_BLOCKS
VMEM_LIMIT = 56 * 1024 * 1024
NEG_INF = float("-inf")
HIGHEST = lax.Precision.HIGHEST
NT_DIMS = (((1,), (1,)), ((), ()))
TN_DIMS = (((0,), (0,)), ((), ()))


def _params(semantics):
    return pltpu.CompilerParams(dimension_semantics=semantics, vmem_limit_bytes=VMEM_LIMIT)


def _log_sigmoid(z):
    return jnp.minimum(z, 0.0) - jnp.log1p(jnp.exp(-jnp.abs(z)))


def _split3(x):
    h1 = x.astype(BF16)
    r1 = x - h1.astype(F32)
    h2 = r1.astype(BF16)
    h3 = (r1 - h2.astype(F32)).astype(BF16)
    return h1, h2, h3


def _mod_kernel(cond_ref, w_ref, b_ref, o_ref):
    c = cond_ref[...]
    s = c * jax.nn.sigmoid(c)
    o_ref[...] = _dot_x3(_hi_lo(s), _hi_lo(w_ref[...])) + b_ref[...]


def _modulation(cond, w_mod, b_mod):
    n_layers = w_mod.shape[0]
    n_chunks = w_mod.shape[2] // D_MODEL
    return pl.pallas_call(
        _mod_kernel,
        grid=(n_layers, n_chunks),
        in_specs=[
            pl.BlockSpec((16, D_MODEL), lambda l, j: (0, 0)),
            pl.BlockSpec((None, D_MODEL, D_MODEL), lambda l, j: (l, 0, j)),
            pl.BlockSpec((None, 1, D_MODEL), lambda l, j: (l, 0, j)),
        ],
        out_specs=pl.BlockSpec((None, 16, D_MODEL), lambda l, j: (l, 0, j)),
        out_shape=jax.ShapeDtypeStruct((n_layers, 16, w_mod.shape[2]), F32),
        compiler_params=_params(("parallel", "parallel")),
        name="modulation",
    )(cond, w_mod, b_mod.reshape(n_layers, 1, -1))


def _two_part(i, n_first, a_ref, b_ref):
    return jnp.where(i < n_first, a_ref[...], b_ref[...])


def _hi_lo(x):
    hi = x.astype(BF16)
    return hi, (x - hi.astype(F32)).astype(BF16)


def _dot_x3(a, b, dims=None):
    (ah, al), (bh, bl) = a, b
    if dims is None:
        d = lambda x, y: jnp.dot(x, y, preferred_element_type=F32)
    else:
        d = lambda x, y: lax.dot_general(x, y, dims, preferred_element_type=F32)
    return d(ah, bh) + (d(al, bh) + d(ah, bl))


def _two_part_out_specs(width, n_first, tile=TOKEN_TILE):
    return _two_part_specs(width, n_first, tile)


def _two_part_specs(width, n_first, tile=TOKEN_TILE):
    return [pl.BlockSpec((tile, width), lambda i: (jnp.minimum(i, n_first - 1), 0)),
            pl.BlockSpec((tile, width), lambda i: (jnp.maximum(i - n_first, 0), 0))]


def _inproj_kernel(*refs, n_ctx_tiles, precise_ctx):
    it = iter(refs)
    xa_ref, xb_ref, sh_ref, sc_ref, g_ref, w_ref = [next(it) for _ in range(6)]
    wl_ref = next(it) if precise_ctx else None
    (wgt_ref, wgtl_ref, brow_ref, gq_ref, gk_ref, eye_ref, cos_ref, sa_ref, sb_ref, gsum_ref,
     qa_ref, qb_ref, k_ref, v_ref, mqa_ref, mqb_ref, mka_ref, mkb_ref, mva_ref, mvb_ref,
     og_ref, gc_ref, gr_ref) = it
    tile = pl.program_id(0)
    x = _two_part(tile, n_ctx_tiles, xa_ref, xb_ref)
    ms = jnp.mean(x * x, axis=-1, keepdims=True)
    h = x * lax.rsqrt(ms + EPS) * g_ref[...]
    h = h * (1.0 + sc_ref[...]) + sh_ref[...]
    hb = h.astype(BF16)

    zr = _dot_x3((wgt_ref[...], wgtl_ref[...]), _hi_lo(h), NT_DIMS) + brow_ref[...]
    sub = lax.broadcasted_iota(jnp.int32, zr.shape, 0)
    gr = jnp.where(sub % 4 < 2, zr, _log_sigmoid(zr))
    eye = eye_ref[...]
    gc = functools.reduce(jnp.add, [lax.dot_general(eye, p, NT_DIMS, preferred_element_type=F32)
                                    for p in _split3(gr)])
    gr_ref[...] = jnp.zeros_like(gr_ref)
    for hd in range(N_ML_HEADS):
        gc_ref[hd] = gc[:, 4 * hd:4 * hd + 4]
        gr_ref[hd, 0:4, :] = gr[4 * hd:4 * hd + 4, :]

    cos = cos_ref[...]
    sa = sa_ref[...]
    sb = sb_ref[...]
    gsum = gsum_ref[...]

    def project(precise, q_ref, mq_ref, mk_ref, mv_ref):
        if precise:
            z = _dot_x3(_hi_lo(h), (w_ref[...], wl_ref[...]))
        else:
            z = jnp.dot(hb, w_ref[...], preferred_element_type=F32)
        proj = lambda c0, width: z[:, c0:c0 + width]
        act = F32 if precise else BF16

        n_qk = ATT_WIDTH // LANES + 1
        t_rows = hb.shape[0]
        zqk = proj(0, n_qk * LANES)
        zs = [zqk[:, c * LANES:(c + 1) * LANES] for c in range(n_qk)]
        sq = jnp.concatenate([z * z for z in zs], axis=0)
        ss = jnp.dot(jnp.concatenate(_hi_lo(sq), axis=0), gsum, preferred_element_type=F32)
        ss = ss[:n_qk * t_rows] + ss[n_qk * t_rows:]

        def headnorm_rope(c, gain):
            zn = zs[c] * lax.rsqrt(ss[c * t_rows:(c + 1) * t_rows] * (1.0 / HEAD_DIM) + EPS) * gain
            return zn * cos + pltpu.roll(zn, LANES - 16, 1) * sa + pltpu.roll(zn, 16, 1) * sb

        for c in range(n_qk - 1):
            q_ref[:, c * LANES:(c + 1) * LANES] = (headnorm_rope(c, gq_ref[...]) * 0.125).astype(act)
        k_ref[...] = headnorm_rope(n_qk - 1, gk_ref[...])
        v_ref[...] = proj(640, LANES)
        mq_ref[...] = proj(768, ML_WIDTH).astype(act)
        mk_ref[...] = (proj(1280, ML_WIDTH) * (ML_HEAD_DIM ** -0.5)).astype(act)
        mv_ref[...] = proj(1792, ML_WIDTH).astype(act)
        og_ref[...] = jax.nn.sigmoid(proj(2304, ML_WIDTH))

    @pl.when(tile < n_ctx_tiles)
    def _ctx():
        project(precise_ctx, qa_ref, mqa_ref, mka_ref, mva_ref)

    @pl.when(tile >= n_ctx_tiles)
    def _lat():
        project(False, qb_ref, mqb_ref, mkb_ref, mvb_ref)


def _inproj(xa, xb, mod3, n_ctx_tiles, tiles_per_lat_seq, ctx_row, g_mix, w_main, w_gate, b_gate,
            g_q, g_k, rope):
    w_hi, w_lo = w_main
    precise_ctx = w_lo is not None
    n_ctx, n_lat = xa.shape[0], xb.shape[0]
    n = n_ctx + n_lat
    n_tiles = n // TOKEN_TILE

    def mod_row(i):
        return jnp.where(i < n_ctx_tiles, ctx_row, (i - n_ctx_tiles) // tiles_per_lat_seq)

    def rope_blk(i):
        return jnp.where(i < n_ctx_tiles, 0, 1 + (i - n_ctx_tiles) % tiles_per_lat_seq)

    cos_t, sa_t, sb_t = rope
    lane = np.arange(LANES)
    gsum = jnp.asarray((lane[:, None] // HEAD_DIM) == (lane[None, :] // HEAD_DIM), BF16)
    tok = lambda w: pl.BlockSpec((TOKEN_TILE, w), lambda i: (i, 0))
    full = lambda a: pl.BlockSpec(a.shape, lambda i: (0,) * a.ndim)
    modspec = lambda j: pl.BlockSpec((None, 1, D_MODEL), lambda i: (mod_row(i), 0, j))
    ropespec = pl.BlockSpec((TOKEN_TILE, LANES), lambda i: (rope_blk(i), 0))
    consts = (g_mix.reshape(1, -1), w_hi) + ((w_lo,) if precise_ctx else ()) + (
        *_hi_lo(w_gate.T), b_gate.reshape(-1, 1),
        jnp.tile(g_q, 2).reshape(1, -1), jnp.tile(g_k, 2).reshape(1, -1),
        jnp.asarray(np.eye(TOKEN_TILE), BF16))
    args = (xa, xb, mod3, mod3) + consts + (cos_t, sa_t, sb_t, gsum)
    in_specs = _two_part_specs(D_MODEL, n_ctx_tiles) + [modspec(0), modspec(1)] \
        + [full(a) for a in consts] + [ropespec, ropespec, ropespec, full(gsum)]
    ctx_act = F32 if precise_ctx else BF16
    pair_shape = lambda w: [jax.ShapeDtypeStruct((n_ctx, w), ctx_act),
                            jax.ShapeDtypeStruct((n_lat, w), BF16)]
    pair_spec = lambda w: _two_part_out_specs(w, n_ctx_tiles)
    out_shape = (
        pair_shape(ATT_WIDTH)
        + [jax.ShapeDtypeStruct((n, LANES), F32),
           jax.ShapeDtypeStruct((n, LANES), F32)]
        + pair_shape(ML_WIDTH) + pair_shape(ML_WIDTH) + pair_shape(ML_WIDTH)
        + [jax.ShapeDtypeStruct((n, ML_WIDTH), F32),
           jax.ShapeDtypeStruct((N_ML_HEADS, n, 4), F32),
           jax.ShapeDtypeStruct((N_ML_HEADS, 8, n), F32)])
    out_specs = (pair_spec(ATT_WIDTH) + [tok(LANES), tok(LANES)]
                 + pair_spec(ML_WIDTH) + pair_spec(ML_WIDTH) + pair_spec(ML_WIDTH)
                 + [tok(ML_WIDTH),
                    pl.BlockSpec((N_ML_HEADS, TOKEN_TILE, 4), lambda i: (0, i, 0)),
                    pl.BlockSpec((N_ML_HEADS, 8, TOKEN_TILE), lambda i: (0, 0, i))])
    outs = pl.pallas_call(
        functools.partial(_inproj_kernel, n_ctx_tiles=n_ctx_tiles, precise_ctx=precise_ctx),
        grid=(n_tiles,), in_specs=in_specs, out_specs=tuple(out_specs),
        out_shape=tuple(out_shape), compiler_params=_params(("arbitrary",)), name="inproj",
    )(*args)
    qa, qb, k, v, mqa, mqb, mka, mkb, mva, mvb, og, gcol, grow = outs
    return (qa, qb), k, v, (mqa, mqb), (mka, mkb), (mva, mvb), og, gcol, grow


def _attn_kernel(*refs, n_kv, precise, cached):
    q_ref = refs[0]
    kv_refs = refs[1:1 + 2 * n_kv]
    o_ref = refs[1 + 2 * n_kv]
    dup_scr = refs[2 + 2 * n_kv:]
    tq = q_ref.shape[0]
    lo_q = lax.broadcasted_iota(jnp.int32, (tq, LANES), 1) < HEAD_DIM
    operand = _hi_lo if precise else (lambda a: a.astype(BF16))
    if precise:
        qk = lambda a, b: _dot_x3(a, b, NT_DIMS)
        pv = _dot_x3
    else:
        qk = lambda a, b: lax.dot_general(a, b, NT_DIMS, preferred_element_type=F32)
        pv = lambda a, b: jnp.dot(a, b, preferred_element_type=F32)

    def dup_half(ref, g):
        a = ref[...]
        r = pltpu.roll(a, HEAD_DIM, 1)
        lo = lax.broadcasted_iota(jnp.int32, a.shape, 1) < HEAD_DIM
        return operand(jnp.where(lo, a, r) if g == 0 else jnp.where(lo, r, a))

    if cached:
        @pl.when(pl.program_id(1) == 0)
        def _fill():
            for g in range(N_KV_HEADS):
                for j in range(2 * n_kv):
                    dup_scr[g * 2 * n_kv + j][...] = dup_half(kv_refs[j], g)

        dup = lambda j, g: dup_scr[g * 2 * n_kv + j][...]
    else:
        dup = lambda j, g: dup_half(kv_refs[j], g)

    for g in range(N_KV_HEADS):
        ks = [dup(2 * p, g) for p in range(n_kv)]
        vs = [dup(2 * p + 1, g) for p in range(n_kv)]
        for hb in range(2):
            c0 = (2 * g + hb) * LANES
            qb = q_ref[:, c0:c0 + LANES]
            outs = []
            for half in range(2):
                keep = lo_q if half == 0 else jnp.logical_not(lo_q)
                qm = operand(jnp.where(keep, qb, jnp.zeros_like(qb)))
                ss = [qk(qm, kd) for kd in ks]
                m = functools.reduce(jnp.maximum, [jnp.max(s, axis=1, keepdims=True) for s in ss])
                ps = [jnp.exp(s - m) for s in ss]
                den = functools.reduce(jnp.add, [jnp.sum(p, axis=1, keepdims=True) for p in ps])
                o = functools.reduce(jnp.add, [pv(operand(p), vd) for p, vd in zip(ps, vs)])
                outs.append(o / den)
            o_ref[:, c0:c0 + LANES] = jnp.where(lo_q, outs[0], outs[1]).astype(o_ref.dtype)


def _attention(q, k, v, kv_row0, n_seq, seq_len, cache=None):
    precise = q.dtype == F32
    nq = seq_len // SEQ_BLOCK
    sb0 = kv_row0 // seq_len
    in_specs = [
        pl.BlockSpec((SEQ_BLOCK, ATT_WIDTH), lambda b, i: (b * nq + i, 0)),
        pl.BlockSpec((seq_len, LANES), lambda b, i: (sb0 + b, 0)),
        pl.BlockSpec((seq_len, LANES), lambda b, i: (sb0 + b, 0)),
    ]
    args = [q, k, v]
    n_kv = 1
    if cache is not None:
        ck, cv = cache
        past = ck.shape[0] // n_seq
        in_specs += [pl.BlockSpec((past, LANES), lambda b, i: (b, 0))] * 2
        args += [ck, cv]
        n_kv = 2
    cached = nq > 1 and not precise
    scratch = [pltpu.VMEM((a.shape[0] // n_seq if j >= 2 else seq_len, LANES), BF16)
               for _ in range(N_KV_HEADS) for j, a in enumerate(args[1:])] if cached else []
    return pl.pallas_call(
        functools.partial(_attn_kernel, n_kv=n_kv, precise=precise, cached=cached),
        grid=(n_seq, nq), in_specs=in_specs,
        out_specs=pl.BlockSpec((SEQ_BLOCK, ATT_WIDTH), lambda b, i: (b * nq + i, 0)),
        out_shape=jax.ShapeDtypeStruct((n_seq * seq_len, ATT_WIDTH), q.dtype),
        scratch_shapes=scratch,
        compiler_params=_params(("parallel", "arbitrary" if cached else "parallel")),
        name="attention",
    )(*args)


def _mlstm_kernel(*refs, heads, **static):
    it = iter(refs)
    q_ref, k_ref, v_ref, og_ref, gmh_ref, gcol_ref, grow_ref, u_ref, l_ref = [next(it) for _ in range(9)]
    init_refs = (next(it), next(it)) if static["has_init"] else ()
    ml_ref = next(it)
    state_refs = (next(it), next(it)) if static["emit_state"] else ()
    for hd in range(heads):
        lanes = pl.ds(hd * ML_HEAD_DIM, ML_HEAD_DIM)
        head_refs = [q_ref.at[:, lanes], k_ref.at[:, lanes], v_ref.at[:, lanes], og_ref.at[:, lanes],
                     gmh_ref.at[:, lanes], gcol_ref.at[hd], grow_ref.at[hd], u_ref, l_ref]
        if init_refs:
            head_refs += [init_refs[0].at[:, hd], init_refs[1].at[hd]]
        head_refs.append(ml_ref.at[:, lanes])
        if state_refs:
            head_refs += [state_refs[0].at[:, hd], state_refs[1].at[hd]]
        _mlstm_head(*head_refs, **static)


def _mlstm_head(*refs, seq_len, has_init, emit_state, precise):
    it = iter(refs)
    q_ref, k_ref, v_ref, og_ref, gmh_ref, gcol_ref, grow_ref, u_ref, l_ref = [next(it) for _ in range(9)]
    if has_init:
        c0_ref, nm0_ref = next(it), next(it)
    ml_ref = next(it)
    if emit_state:
        cf_ref, nmf_ref = next(it), next(it)

    operand = _hi_lo if precise else (lambda a: a.astype(BF16))
    if precise:
        qk = lambda a, b: _dot_x3(a, b, NT_DIMS)
        pv = _dot_x3
    else:
        qk = lambda a, b: lax.dot_general(a, b, NT_DIMS, preferred_element_type=F32)
        pv = lambda a, b: jnp.dot(a, b, preferred_element_type=F32)

    bq = SEQ_BLOCK
    nb = seq_len // bq
    blk = lambda j: slice(j * bq, (j + 1) * bq)
    upper_incl = u_ref[...]
    lower_incl = l_ref[...]

    def tri_dot(x, tri):
        return functools.reduce(jnp.add, [jnp.dot(p, tri, preferred_element_type=F32)
                                          for p in _split3(x)])

    ig_row = [[None] * nb for _ in range(2)]
    lf_row = [[None] * nb for _ in range(2)]
    within = [[None] * nb for _ in range(2)]
    bsum = [[None] * nb for _ in range(2)]
    for j in range(nb):
        g8 = grow_ref[:, blk(j)]
        cum_f = tri_dot(g8, upper_incl)
        cum_b = tri_dot(g8, lower_incl)
        for d in range(2):
            ig_row[d][j] = g8[d:d + 1, :]
            lf_row[d][j] = g8[2 + d:3 + d, :]
            within[d][j] = (cum_f if d == 0 else cum_b)[2 + d:3 + d, :]
            bsum[d][j] = jnp.sum(lf_row[d][j], axis=1, keepdims=True)
    zero11 = jnp.zeros((1, 1), F32)
    r_i = lax.broadcasted_iota(jnp.int32, (bq, bq), 0)
    c_i = lax.broadcasted_iota(jnp.int32, (bq, bq), 1)
    causal = [c_i <= r_i, c_i >= r_i]
    gmh = gmh_ref[...]

    q_blocks = [q_ref[blk(i), :] for i in range(nb)]
    scores = [qk(operand(q_blocks[i]), operand(k_ref[blk(i), :])) for i in range(nb)]

    h_dir = [[None] * nb for _ in range(2)]
    final = [None, None]
    for d in range(2):
        if has_init:
            state = (c0_ref[d], nm0_ref[d:d + 1, :])
            m = nm0_ref[2 + d:3 + d, 0:1]
        else:
            state, m = None, zero11
        scan = range(nb) if d == 0 else range(nb - 1, -1, -1)
        for step, i in enumerate(scan):
            q_i = q_blocks[i]
            a_loc = ig_row[d][i] - within[d][i]
            m_col = jnp.maximum(m, jnp.max(jnp.where(causal[d], a_loc, NEG_INF),
                                           axis=1, keepdims=True))
            if nb > 1:
                widen = lambda col: jnp.broadcast_to(col, (bq, LANES))
                tri = lower_incl if d == 0 else upper_incl
                b_rep = functools.reduce(jnp.add, [
                    lax.dot_general(tri, jnp.broadcast_to(piece, (LANES, bq)), NT_DIMS,
                                    preferred_element_type=F32)
                    for piece in _hi_lo(lf_row[d][i])])
                m_rep = widen(m_col)
                m_wide = jnp.concatenate([m_rep] * (bq // LANES), axis=1)
            else:
                widen = lambda col: col
                b_rep = jnp.sum(jnp.where(causal[d], lf_row[d][i], 0.0), axis=1, keepdims=True)
                m_rep = m_wide = m_col
            p = jnp.exp(jnp.where(causal[d], a_loc - m_wide, NEG_INF)) * scores[i]
            den = widen(jnp.sum(p, axis=1, keepdims=True))
            num = pv(operand(p), operand(v_ref[blk(i), :]))
            if state is not None:
                c_prev, n_prev = state
                w_inter = jnp.exp(m - m_rep)
                q_b = q_i.astype(BF16)
                qc = jnp.dot(q_b, c_prev.astype(BF16), preferred_element_type=F32)
                n_rows = jnp.broadcast_to(n_prev, (LANES, ML_HEAD_DIM)).astype(BF16)
                qn = lax.dot_general(q_b, n_rows, NT_DIMS, preferred_element_type=F32)
                num = num + w_inter * qc
                den = den + w_inter * qn
            nrm = jnp.maximum(jnp.abs(den), jnp.exp(-(b_rep + m_rep)))
            h_dir[d][i] = num / nrm
            if step == nb - 1 and not emit_state:
                break
            m_last = jnp.maximum(m, jnp.max(a_loc, axis=1, keepdims=True))
            a_col = widen(gcol_ref[blk(i), d:d + 1]) - b_rep
            kw = k_ref[blk(i), :].astype(F32) * jnp.exp(a_col - m_last)
            c_new = lax.dot_general(kw.astype(BF16), v_ref[blk(i), :].astype(BF16), TN_DIMS,
                                    preferred_element_type=F32)
            n_new = jnp.sum(kw, axis=0, keepdims=True)
            if state is not None:
                decay = jnp.exp(m - m_last)
                c_new = c_new + decay * state[0]
                n_new = n_new + decay * state[1]
            state = (c_new, n_new)
            m = bsum[d][i] + m_last
        final[d] = (state, m)

    for i in range(nb):
        h = h_dir[0][i] + h_dir[1][i]
        hn = h * lax.rsqrt(jnp.mean(h * h, axis=-1, keepdims=True) + EPS) * gmh
        ml_ref[blk(i), :] = (og_ref[blk(i), :] * hn).astype(ml_ref.dtype)

    if emit_state:
        nmf_ref[...] = jnp.zeros_like(nmf_ref)
        for d in range(2):
            (c_fin, n_fin), m_fin = final[d]
            cf_ref[d] = c_fin
            nmf_ref[d:d + 1, :] = n_fin
            nmf_ref[2 + d:3 + d, :] = jnp.broadcast_to(m_fin, (1, ML_HEAD_DIM))


def _mlstm(mq, mk, mv, og, g_mh, gcol, grow, row0, n_seq, seq_len, init=None, emit_state=False):
    precise = mq.dtype == F32
    assert not (precise and init is not None)
    sb0 = row0 // seq_len
    tri = np.arange(SEQ_BLOCK)
    upper_incl = jnp.asarray(tri[:, None] <= tri[None, :], BF16)
    lower_incl = jnp.asarray(tri[:, None] >= tri[None, :], BF16)
    heads = MLSTM_HEADS_PER_STEP
    width = heads * ML_HEAD_DIM
    ownblk = lambda: pl.BlockSpec((seq_len, width), lambda b, h: (b, h))
    headblk = lambda: pl.BlockSpec((seq_len, width), lambda b, h: (sb0 + b, h))
    const = lambda a: pl.BlockSpec(a.shape, lambda b, h: (0,) * a.ndim)
    in_specs = [ownblk(), ownblk(), ownblk(), headblk(),
                pl.BlockSpec((1, width), lambda b, h: (0, h)),
                pl.BlockSpec((heads, seq_len, 4), lambda b, h: (h, sb0 + b, 0)),
                pl.BlockSpec((heads, 8, seq_len), lambda b, h: (h, 0, sb0 + b)),
                const(upper_incl), const(lower_incl)]
    args = [mq, mk, mv, og, g_mh.reshape(1, -1), gcol, grow, upper_incl, lower_incl]
    if init is not None:
        c0, nm0, layer = init
        in_specs += [
            pl.BlockSpec((None, None, 2, heads, ML_HEAD_DIM, ML_HEAD_DIM),
                         lambda b, h: (b, layer, 0, h, 0, 0)),
            pl.BlockSpec((None, heads, 8, ML_HEAD_DIM), lambda b, h: (b, h, 0, 0))]
        args += [c0, nm0]
    out_shape = [jax.ShapeDtypeStruct((n_seq * seq_len, ML_WIDTH), mq.dtype)]
    out_specs = [pl.BlockSpec((seq_len, width), lambda b, h: (b, h))]
    if emit_state:
        out_shape += [jax.ShapeDtypeStruct((n_seq, 2, N_ML_HEADS, ML_HEAD_DIM, ML_HEAD_DIM), F32),
                      jax.ShapeDtypeStruct((n_seq, N_ML_HEADS, 8, ML_HEAD_DIM), F32)]
        out_specs += [pl.BlockSpec((None, 2, heads, ML_HEAD_DIM, ML_HEAD_DIM),
                                   lambda b, h: (b, 0, h, 0, 0)),
                      pl.BlockSpec((None, heads, 8, ML_HEAD_DIM), lambda b, h: (b, h, 0, 0))]
    return pl.pallas_call(
        functools.partial(_mlstm_kernel, heads=heads, seq_len=seq_len, has_init=init is not None,
                          emit_state=emit_state, precise=precise),
        grid=(n_seq, N_ML_HEADS // heads), in_specs=in_specs, out_specs=tuple(out_specs),
        out_shape=tuple(out_shape),
        compiler_params=_params(("parallel", "parallel")), name="mlstm",
    )(*args)


def _outproj_kernel(*refs, n_ctx_tiles, precise_ctx):
    it = iter(refs)
    atta_ref, attb_ref, mla_ref, mlb_ref, xa_ref, xb_ref, w_ref = [next(it) for _ in range(7)]
    wl_ref = next(it) if precise_ctx else None
    gt_ref, sh_ref, sc_ref, g_ref, wrt_ref, wrtl_ref, x1_ref, hp_ref, logits_ref, y_scr = it
    i = pl.program_id(0)

    def mix(att, ml):
        return (jnp.dot(att, w_ref[:ATT_WIDTH, :], preferred_element_type=F32)
                + jnp.dot(ml, w_ref[ATT_WIDTH:, :], preferred_element_type=F32))

    @pl.when(i < n_ctx_tiles)
    def _ctx():
        if precise_ctx:
            y_scr[...] = (
                _dot_x3(_hi_lo(atta_ref[...]), (w_ref[:ATT_WIDTH, :], wl_ref[:ATT_WIDTH, :]))
                + _dot_x3(_hi_lo(mla_ref[...]), (w_ref[ATT_WIDTH:, :], wl_ref[ATT_WIDTH:, :])))
        else:
            y_scr[...] = mix(atta_ref[...], mla_ref[...])

    @pl.when(i >= n_ctx_tiles)
    def _lat():
        y_scr[...] = mix(attb_ref[...], mlb_ref[...])

    x1 = _two_part(i, n_ctx_tiles, xa_ref, xb_ref) + gt_ref[...] * y_scr[...]
    x1_ref[...] = x1
    ms = jnp.mean(x1 * x1, axis=-1, keepdims=True)
    h2 = x1 * lax.rsqrt(ms + EPS) * g_ref[...]
    h2 = h2 * (1.0 + sc_ref[...]) + sh_ref[...]
    for k in range(ROW_TILE_SUBLANES):
        hp_ref[pl.ds(k, TOKEN_TILE, stride=ROW_TILE_SUBLANES), :] = h2[:, k * LANES:(k + 1) * LANES]

    logits_ref[...] = _dot_x3((wrt_ref[...], wrtl_ref[...]), _hi_lo(h2), NT_DIMS)


def _select_experts(logits, b_col):
    ex = jnp.exp(logits - jnp.max(logits, axis=0, keepdims=True))
    scores = ex / jnp.sum(ex, axis=0, keepdims=True)
    sel = scores + b_col
    row = lambda a, e: a[e:e + 1, :]
    grp_score = []
    for g in range(N_GROUPS):
        xs = [row(sel, g * GROUP_SIZE + j) for j in range(GROUP_SIZE)]
        pairs = [xs[a] + xs[b] for a in range(GROUP_SIZE) for b in range(a + 1, GROUP_SIZE)]
        grp_score.append(functools.reduce(jnp.maximum, pairs))
    best = grp_score[0]
    grp = jnp.zeros_like(best, dtype=jnp.int32)
    for g in range(1, N_GROUPS):
        better = grp_score[g] > best
        grp = jnp.where(better, g, grp)
        best = jnp.where(better, grp_score[g], best)
    pick = lambda a, j: functools.reduce(
        lambda acc, g: jnp.where(grp == g, row(a, g * GROUP_SIZE + j), acc),
        range(1, N_GROUPS), row(a, j))
    xs = [pick(sel, j) for j in range(GROUP_SIZE)]
    ws = [pick(scores, j) for j in range(GROUP_SIZE)]

    def argmax4(vals):
        bv, bi = vals[0], jnp.zeros_like(grp)
        for j in range(1, GROUP_SIZE):
            better = vals[j] > bv
            bi = jnp.where(better, j, bi)
            bv = jnp.where(better, vals[j], bv)
        return bi

    i1 = argmax4(xs)
    i2 = argmax4([jnp.where(i1 == j, NEG_INF, xs[j]) for j in range(GROUP_SIZE)])
    take = lambda vals, idx: functools.reduce(
        lambda acc, j: jnp.where(idx == j, vals[j], acc), range(1, GROUP_SIZE), vals[0])
    w1, w2 = take(ws, i1), take(ws, i2)
    wsum = w1 + w2
    w1, w2 = w1 / wsum, w2 / wsum
    return grp * GROUP_SIZE + i1, grp * GROUP_SIZE + i2, w1, w2


def _outproj(att, ml, x, mod3, n_ctx_tiles, tiles_per_lat_seq, ctx_row, w_out, g_ffn, w_router):
    w_hi, w_lo = w_out
    precise_ctx = w_lo is not None
    n = x[0].shape[0] + x[1].shape[0]

    def mod_row(i):
        return jnp.where(i < n_ctx_tiles, ctx_row, (i - n_ctx_tiles) // tiles_per_lat_seq)

    tok = lambda w: pl.BlockSpec((TOKEN_TILE, w), lambda i: (i, 0))
    full = lambda a: pl.BlockSpec(a.shape, lambda i: (0,) * a.ndim)
    modspec = lambda j: pl.BlockSpec((None, 1, D_MODEL), lambda i: (mod_row(i), 0, j))
    weights = (w_hi, w_lo) if precise_ctx else (w_hi,)
    consts = (g_ffn.reshape(1, -1), *_hi_lo(w_router.T))
    args = (*att, *ml, *x, *weights, mod3, mod3, mod3) + consts
    in_specs = (_two_part_specs(ATT_WIDTH, n_ctx_tiles) + _two_part_specs(ML_WIDTH, n_ctx_tiles)
                + _two_part_specs(D_MODEL, n_ctx_tiles) + [full(w) for w in weights]
                + [modspec(2), modspec(3), modspec(4)] + [full(a) for a in consts])
    return pl.pallas_call(
        functools.partial(_outproj_kernel, n_ctx_tiles=n_ctx_tiles, precise_ctx=precise_ctx),
        grid=(n // TOKEN_TILE,), in_specs=in_specs,
        scratch_shapes=[pltpu.VMEM((TOKEN_TILE, D_MODEL), F32)],
        out_specs=(tok(D_MODEL),
                   pl.BlockSpec((TOKEN_TILE * ROW_TILE_SUBLANES, LANES), lambda i: (i, 0)),
                   pl.BlockSpec((N_EXPERTS, TOKEN_TILE), lambda i: (0, i))),
        out_shape=(jax.ShapeDtypeStruct((n, D_MODEL), F32),
                   jax.ShapeDtypeStruct((n * ROW_TILE_SUBLANES, LANES), F32),
                   jax.ShapeDtypeStruct((N_EXPERTS, n), F32)),
        compiler_params=_params(("parallel",)), name="outproj_router",
    )(*args)


def _moe_kernel(order_ref, pos0_ref, pos1_ref, off_ref, cnt_ref,
                h_ref, wg_ref, wu_ref, wd_ref, x1_ref, gt_ref, wcol_ref, ya_ref, yb_ref,
                o_scr, xs_scr, comb0_scr, comb1_scr, *, n_ctx_tiles):
    t = pl.program_id(0)
    s = pl.program_id(1)
    sub = ROW_TILE_SUBLANES
    groups = MOE_CHUNK // 8

    def tile(ref, row):
        return ref.at[pl.ds(pl.multiple_of(row * sub, sub), sub), :]

    def slab(ref, row0, n_rows, k):
        return ref.at[pl.ds(row0 * sub + k, n_rows, stride=sub), :]

    def gather_rows(buf, slot0):
        for j in range(MOE_CHUNK):
            tile(xs_scr, buf * MOE_CHUNK + j)[...] = h_ref[order_ref[slot0 + j]]

    def ffn(buf, base):
        a = jnp.zeros((MOE_CHUNK, D_EXPERT), F32)
        b = jnp.zeros((MOE_CHUNK, D_EXPERT), F32)
        for p in range(sub // 2):
            lhs = jnp.concatenate([slab(xs_scr, buf * MOE_CHUNK, MOE_CHUNK, 2 * p)[...],
                                   slab(xs_scr, buf * MOE_CHUNK, MOE_CHUNK, 2 * p + 1)[...]],
                                  axis=1).astype(BF16)
            rows = slice(2 * p * LANES, (2 * p + 2) * LANES)
            a = a + jnp.dot(lhs, wg_ref[rows, :].astype(BF16), preferred_element_type=F32)
            b = b + jnp.dot(lhs, wu_ref[rows, :].astype(BF16), preferred_element_type=F32)
        hid = (a * jax.nn.sigmoid(a)) * b
        out = jnp.dot(hid.astype(BF16), wd_ref[...].astype(BF16), preferred_element_type=F32)
        for k in range(sub):
            slab(o_scr, base, MOE_CHUNK, k)[...] = out[:, k * LANES:(k + 1) * LANES]

    @pl.when(s == 0)
    def _first():
        gather_rows(0, t * MOE_SLOTS + pl.multiple_of(off_ref[t * N_EXPERTS], 8))

    @pl.when(s < N_EXPERTS)
    def _expert():
        seg = t * N_EXPERTS + s
        off = pl.multiple_of(off_ref[seg], 8)
        cur = s % 2
        nxt_seg = t * N_EXPERTS + jnp.minimum(s + 1, N_EXPERTS - 1)
        ffn(cur, off)
        gather_rows(1 - cur, t * MOE_SLOTS + pl.multiple_of(off_ref[nxt_seg], 8))

        def extra(c, carry):
            base = pl.multiple_of(off + c * MOE_CHUNK, 8)

            def gather8(i, carry2):
                slot = t * MOE_SLOTS + base + i * 8
                for k in range(8):
                    tile(xs_scr, 2 * MOE_CHUNK + i * 8 + k)[...] = h_ref[order_ref[slot + k]]
                return carry2

            lax.fori_loop(0, groups, gather8, 0)
            ffn(2, base)
            return carry

        lax.fori_loop(1, (cnt_ref[seg] + MOE_CHUNK - 1) // MOE_CHUNK, extra, 0)

    @pl.when(s >= N_EXPERTS)
    def _combine():
        tok0 = t * MOE_TILE + (s - N_EXPERTS) * MOE_OUT_TILE

        def body8(i, carry):
            for k in range(8):
                p0 = pos0_ref[tok0 + i * 8 + k]
                p1 = pos1_ref[tok0 + i * 8 + k]
                tile(comb0_scr, i * 8 + k)[...] = tile(o_scr, p0)[...]
                tile(comb1_scr, i * 8 + k)[...] = tile(o_scr, p1)[...]
            return carry

        lax.fori_loop(0, MOE_OUT_TILE // 8, body8, 0)
        wcol = wcol_ref[...]
        rows_of = lambda ref: jnp.concatenate(
            [slab(ref, 0, MOE_OUT_TILE, k)[...] for k in range(sub)], axis=1)
        comb = wcol[:, 2:3] * rows_of(comb0_scr) + wcol[:, 3:4] * rows_of(comb1_scr)
        y = x1_ref[...] + gt_ref[...] * comb
        chunk_ix = t * (MOE_TILE // MOE_OUT_TILE) + s - N_EXPERTS

        @pl.when(chunk_ix < n_ctx_tiles)
        def _ctx():
            ya_ref[...] = y

        @pl.when(chunk_ix >= n_ctx_tiles)
        def _lat():
            yb_ref[...] = y


def _route_tables(logits, b_router):
    n = logits.shape[1]
    nt = n // MOE_TILE
    tri = np.arange(SEQ_BLOCK)
    strict_upper = jnp.asarray(tri[:, None] < tri[None, :], BF16)
    tok = np.arange(MOE_TILE)
    digits = np.zeros((MOE_TILE, LANES), np.float32)
    digits[:, 0] = tok % 256
    digits[:, 1] = tok // 256
    pos, order, meta, wcol = pl.pallas_call(
        _route_kernel, grid=(nt,),
        in_specs=[pl.BlockSpec((N_EXPERTS, MOE_TILE), lambda t: (0, t)),
                  pl.BlockSpec((N_EXPERTS, 1), lambda t: (0, 0)),
                  pl.BlockSpec((SEQ_BLOCK, SEQ_BLOCK), lambda t: (0, 0)),
                  pl.BlockSpec((SEQ_BLOCK, SEQ_BLOCK), lambda t: (0, 0)),
                  pl.BlockSpec((MOE_TILE, LANES), lambda t: (0, 0))],
        out_specs=(pl.BlockSpec((8, MOE_TILE), lambda t: (0, t)),
                   pl.BlockSpec((None, MOE_SLOTS, 1), lambda t: (t, 0, 0)),
                   pl.BlockSpec((None, N_EXPERTS, 8), lambda t: (t, 0, 0)),
                   pl.BlockSpec((MOE_TILE, 8), lambda t: (t, 0))),
        out_shape=(jax.ShapeDtypeStruct((8, n), jnp.int32),
                   jax.ShapeDtypeStruct((nt, MOE_SLOTS, 1), jnp.int32),
                   jax.ShapeDtypeStruct((nt, N_EXPERTS, 8), jnp.int32),
                   jax.ShapeDtypeStruct((n, 8), F32)),
        scratch_shapes=[pltpu.VMEM((8, MOE_TILE), F32)],
        compiler_params=_params(("parallel",)), name="route_tables",
    )(logits, b_router.reshape(-1, 1), strict_upper, jnp.asarray(np.eye(SEQ_BLOCK), BF16),
      jnp.asarray(digits, BF16))
    return (order.reshape(-1), pos[0], pos[1], meta[:, :, 0].reshape(-1),
            meta[:, :, 1].reshape(-1), wcol)


def _route_kernel(logits_ref, br_ref, su_ref, eye_ref, digits_ref,
                  pos_ref, order_ref, meta_ref, wcol_ref, wrow_scr):
    e1, e2, w1, w2 = _select_experts(logits_ref[...], br_ref[...])
    wrow_scr[...] = jnp.zeros_like(wrow_scr)
    wrow_scr[2:3, :] = w1
    wrow_scr[3:4, :] = w2
    eye = eye_ref[...]
    for b in range(MOE_TILE // SEQ_BLOCK):
        cols = slice(b * SEQ_BLOCK, (b + 1) * SEQ_BLOCK)
        wcol_ref[cols, :] = functools.reduce(jnp.add, [
            lax.dot_general(eye, p, NT_DIMS, preferred_element_type=F32)
            for p in _split3(wrow_scr[:, cols])])
    eid = lax.broadcasted_iota(jnp.int32, (N_EXPERTS, MOE_TILE), 0)
    oh1, oh2 = eid == e1, eid == e2
    oh = jnp.where(oh1, 1.0, 0.0) + jnp.where(oh2, 1.0, 0.0)
    nblk = MOE_TILE // SEQ_BLOCK
    blocks = [oh[:, b * SEQ_BLOCK:(b + 1) * SEQ_BLOCK] for b in range(nblk)]
    inner = jnp.dot(jnp.concatenate(blocks, axis=0).astype(BF16), su_ref[...],
                    preferred_element_type=F32)
    run = jnp.zeros((N_EXPERTS, 1), F32)
    ranks = []
    for b in range(nblk):
        ranks.append(inner[b * N_EXPERTS:(b + 1) * N_EXPERTS, :] + run)
        run = run + jnp.sum(blocks[b], axis=1, keepdims=True)
    count = run
    seg = jnp.floor((count + 7.0) * 0.125) * 8.0
    sub = lax.broadcasted_iota(jnp.int32, (N_EXPERTS, 1), 0)
    off = jnp.zeros((N_EXPERTS, 1), F32)
    for e in range(N_EXPERTS - 1):
        off = off + jnp.where(sub > e, seg[e:e + 1, :], 0.0)
    slot = jnp.concatenate(ranks, axis=1) + off
    pos1 = jnp.sum(jnp.where(oh1, slot, 0.0), axis=0, keepdims=True).astype(jnp.int32)
    pos2 = jnp.sum(jnp.where(oh2, slot, 0.0), axis=0, keepdims=True).astype(jnp.int32)
    pos_ref[...] = jnp.zeros_like(pos_ref)
    pos_ref[0:1, :] = pos1
    pos_ref[1:2, :] = pos2
    meta_ref[...] = jnp.zeros_like(meta_ref)
    meta_ref[:, 0:1] = off.astype(jnp.int32)
    meta_ref[:, 1:2] = count.astype(jnp.int32)
    digits = digits_ref[...]
    rows = MOE_SLOTS // ROUTE_SLOT_BLOCKS
    for sb in range(ROUTE_SLOT_BLOCKS):
        j = lax.broadcasted_iota(jnp.int32, (rows, MOE_TILE), 0) + sb * rows
        hit = jnp.where(j == pos1, 1.0, 0.0) + jnp.where(j == pos2, 1.0, 0.0)
        d = jnp.dot(hit.astype(BF16), digits, preferred_element_type=F32)
        order_ref[sb * rows:(sb + 1) * rows, :] = (d[:, 0:1] + 256.0 * d[:, 1:2]).astype(jnp.int32)


def _moe(hp, logits, b_router, x1, mod3, layer, n_ctx_tiles, tiles_per_lat_seq, ctx_row,
         wg, wu, wd):
    n = hp.shape[0] // ROW_TILE_SUBLANES
    order, pos0, pos1, off, count, wcol = _route_tables(logits, b_router)
    chunks_per_tile = MOE_TILE // MOE_OUT_TILE
    n_steps = N_EXPERTS + chunks_per_tile

    def chunk_ix(t, s):
        return t * chunks_per_tile + jnp.maximum(s - N_EXPERTS, 0)

    def mod_row(g):
        return jnp.where(g < n_ctx_tiles, ctx_row, (g - n_ctx_tiles) // tiles_per_lat_seq)

    wspec = lambda r, c: pl.BlockSpec(
        (None, None, r, c), lambda t, s, *_: (layer, jnp.minimum(s, N_EXPERTS - 1), 0, 0))
    chunk_spec = pl.BlockSpec((MOE_OUT_TILE, D_MODEL), lambda t, s, *_: (chunk_ix(t, s), 0))
    grid_spec = pltpu.PrefetchScalarGridSpec(
        num_scalar_prefetch=5,
        grid=(n // MOE_TILE, n_steps),
        in_specs=[
            pl.BlockSpec((MOE_TILE, ROW_TILE_SUBLANES, LANES), lambda t, s, *_: (t, 0, 0),
                         pipeline_mode=pl.Buffered(1)),
            wspec(D_MODEL, D_EXPERT), wspec(D_MODEL, D_EXPERT), wspec(D_EXPERT, D_MODEL),
            chunk_spec,
            pl.BlockSpec((None, 1, D_MODEL), lambda t, s, *_: (mod_row(chunk_ix(t, s)), 0, 5)),
            pl.BlockSpec((MOE_OUT_TILE, 8), lambda t, s, *_: (chunk_ix(t, s), 0)),
        ],
        out_specs=(
            pl.BlockSpec((MOE_OUT_TILE, D_MODEL),
                         lambda t, s, *_: (jnp.minimum(chunk_ix(t, s), n_ctx_tiles - 1), 0)),
            pl.BlockSpec((MOE_OUT_TILE, D_MODEL),
                         lambda t, s, *_: (jnp.maximum(chunk_ix(t, s) - n_ctx_tiles, 0), 0))),
        scratch_shapes=[pltpu.VMEM((MOE_SLOTS * ROW_TILE_SUBLANES, LANES), F32),
                        pltpu.VMEM((3 * MOE_CHUNK * ROW_TILE_SUBLANES, LANES), F32),
                        pltpu.VMEM((MOE_OUT_TILE * ROW_TILE_SUBLANES, LANES), F32),
                        pltpu.VMEM((MOE_OUT_TILE * ROW_TILE_SUBLANES, LANES), F32)],
    )
    n_ctx = n_ctx_tiles * MOE_OUT_TILE
    return pl.pallas_call(
        functools.partial(_moe_kernel, n_ctx_tiles=n_ctx_tiles), grid_spec=grid_spec,
        out_shape=(jax.ShapeDtypeStruct((n_ctx, D_MODEL), F32),
                   jax.ShapeDtypeStruct((n - n_ctx, D_MODEL), F32)),
        compiler_params=_params(("arbitrary", "arbitrary")), name="experts",
    )(order, pos0, pos1, off, count, hp.reshape(n, ROW_TILE_SUBLANES, LANES), wg, wu, wd, x1,
      mod3, wcol)


def _rope_tables(seq_len):
    half = HEAD_DIM // 2
    freqs = ROPE_BASE ** (-np.arange(0, half, 2, dtype=np.float64) / half)
    pos = np.arange(seq_len)
    row, col = pos // GRID_W, pos % GRID_W
    d = np.arange(HEAD_DIM)
    position = np.where(d[None, :] < half, row[:, None], col[:, None]).astype(np.float64)
    ang = (position.astype(np.float32) * freqs.astype(np.float32)[d % (half // 2)][None, :]).astype(np.float32)
    cos, sin = np.cos(ang), np.sin(ang)
    first = (d % half) < half // 2
    sa = np.where(first[None, :], -sin, 0.0)
    sb = np.where(first[None, :], 0.0, sin)
    ident = lambda v: np.full((TOKEN_TILE, HEAD_DIM), v, np.float32)
    stack = lambda ctx, lat: jnp.asarray(
        np.tile(np.concatenate([ctx, lat.astype(np.float32)], axis=0), (1, 2)), F32)
    return stack(ident(1.0), cos), stack(ident(0.0), sa), stack(ident(0.0), sb)


def kernel(x_prompt, x_sample, c, cache_k, cache_v, state_C, state_n, state_m, c_ctx, w_mod, b_mod,
           g_mix, g_ffn, w_in, b_igate, b_fgate, g_q, g_k, g_mh, w_out, w_router, b_router,
           w_e_gate, w_e_up, w_e_down):
    n_ctx_seq, ctx_len, _ = x_prompt.shape
    n_lat_seq, lat_len, _ = x_sample.shape
    n_layers = w_mod.shape[0]
    n_ctx = n_ctx_seq * ctx_len
    assert ctx_len == SEQ_BLOCK and lat_len % TOKEN_TILE == 0
    assert n_ctx % MOE_TILE == 0 and (n_lat_seq * lat_len) % MOE_TILE == 0
    assert n_lat_seq < 16 and n_ctx % lat_len == 0
    n_ctx_tiles = n_ctx // TOKEN_TILE
    tiles_per_lat_seq = lat_len // TOKEN_TILE
    ctx_row = n_lat_seq

    x = (x_prompt.reshape(n_ctx, D_MODEL), x_sample.reshape(-1, D_MODEL))
    cond = jnp.zeros((16, D_MODEL), F32).at[:n_lat_seq].set(c).at[ctx_row].set(c_ctx)
    mod = _modulation(cond, w_mod, b_mod)
    rope = _rope_tables(lat_len)

    gate_perm = np.array([(q % 2) * N_ML_HEADS + hd + 2 * N_ML_HEADS * (q // 2)
                          for hd in range(N_ML_HEADS) for q in range(4)])

    ks, vs, cs, ns, ms = [], [], [], [], []
    for l in range(n_layers):
        mod3 = mod[l].reshape(16, 1, -1)
        precise_ctx = l < n_layers - 1

        def weight_pair(w):
            hi = w.astype(BF16)
            return hi, ((w - hi.astype(F32)).astype(BF16) if precise_ctx else None)

        w_main = weight_pair(w_in[l, :, :MAIN_WIDTH])
        w_gate = w_in[l, :, MAIN_WIDTH:][:, gate_perm]
        b_gate = jnp.concatenate([b_igate[l].reshape(-1), b_fgate[l].reshape(-1)])[gate_perm]
        q, k, v, mq, mk, mv, og, gcol, grow = _inproj(
            *x, mod3, n_ctx_tiles, tiles_per_lat_seq, ctx_row, g_mix[l], w_main, w_gate, b_gate,
            g_q[l], g_k[l], rope)

        att_ctx = _attention(q[0], k, v, 0, n_ctx_seq, ctx_len)
        past = cache_k.shape[2]
        ck = cache_k[:, l].reshape(n_lat_seq * past, LANES)
        cv = cache_v[:, l].reshape(n_lat_seq * past, LANES)
        att_lat = _attention(q[1], k, v, n_ctx, n_lat_seq, lat_len, cache=(ck, cv))

        ml_ctx, c_fin, nm_fin = _mlstm(mq[0], mk[0], mv[0], og, g_mh[l], gcol, grow, 0,
                                       n_ctx_seq, ctx_len, emit_state=True)
        n0 = state_n[:, l].transpose(0, 2, 1, 3)
        m0 = jnp.broadcast_to(state_m[:, l].transpose(0, 2, 1)[..., None], n0.shape)
        nm0 = jnp.concatenate([n0, m0, jnp.zeros_like(n0), jnp.zeros_like(n0)], axis=2)
        (ml_lat,) = _mlstm(mq[1], mk[1], mv[1], og, g_mh[l], gcol, grow, n_ctx, n_lat_seq,
                           lat_len, init=(state_C, nm0, l))

        x1, hp, logits = _outproj(
            (att_ctx, att_lat), (ml_ctx, ml_lat), x, mod3, n_ctx_tiles, tiles_per_lat_seq, ctx_row,
            weight_pair(w_out[l]), g_ffn[l], w_router)
        x = _moe(hp, logits, b_router, x1, mod3, l, n_ctx // MOE_OUT_TILE,
                 lat_len // MOE_OUT_TILE, ctx_row, w_e_gate, w_e_up, w_e_down)

        ks.append(k[:n_ctx].reshape(n_ctx_seq, ctx_len, N_KV_HEADS, HEAD_DIM))
        vs.append(v[:n_ctx].reshape(n_ctx_seq, ctx_len, N_KV_HEADS, HEAD_DIM))
        cs.append(c_fin)
        ns.append(nm_fin[:, :, 0:2, :].transpose(0, 2, 1, 3))
        ms.append(nm_fin[:, :, 2:4, 0].transpose(0, 2, 1))

    y_prompt = x[0].reshape(x_prompt.shape)
    y_sample = x[1].reshape(x_sample.shape)
    return (y_prompt, y_sample, jnp.stack(ks, axis=1), jnp.stack(vs, axis=1),
            jnp.stack(cs, axis=1), jnp.stack(ns, axis=1), jnp.stack(ms, axis=1))
```

```python
import functools

import numpy as np
import jax
import jax.numpy as jnp
from jax import lax
from jax.experimental import pallas as pl
from jax.experimental.pallas import tpu as pltpu

F32 = jnp.float32
BF16 = jnp.bfloat16

D_MODEL = 1024
HEAD_DIM = 64
ATT_WIDTH = 512
N_KV_HEADS = 2
ML_WIDTH = 512
N_ML_HEADS = 4
ML_HEAD_DIM = 128
GRID_W = 64
ROPE_BASE = 10000.0
N_EXPERTS = 16
N_GROUPS = 4
GROUP_SIZE = 4
D_EXPERT = 512
EPS = 1e-6
MAIN_WIDTH = 2816
LANES = 128
TOKEN_TILE = 512
SEQ_BLOCK = 256
MOE_TILE = 2048
MOE_CHUNK = 320
ROW_TILE_SUBLANES = D_MODEL // LANES
MLSTM_HEADS_PER_STEP = 4
MOE_OUT_TILE = 256
ROUTE_SLOT_BLOCKS = 9
MOE_SLOTS = -(-(2 * MOE_TILE + 8 * N_EXPERTS + MOE_CHUNK) // (512 * ROUTE_SLOT_BLOCKS)) * 512 * ROUTE_SLOT_BLOCKS
VMEM_LIMIT = 56 * 1024 * 1024
NEG_INF = float("-inf")
NT_DIMS = (((1,), (1,)), ((), ()))
TN_DIMS = (((0,), (0,)), ((), ()))


def _params(semantics):
    return pltpu.CompilerParams(dimension_semantics=semantics, vmem_limit_bytes=VMEM_LIMIT)


def _log_sigmoid(z):
    return jnp.minimum(z, 0.0) - jnp.log1p(jnp.exp(-jnp.abs(z)))


def _split3(x):
    h1 = x.astype(BF16)
    r1 = x - h1.astype(F32)
    h2 = r1.astype(BF16)
    h3 = (r1 - h2.astype(F32)).astype(BF16)
    return h1, h2, h3


def _mod_kernel(cond_ref, w_ref, b_ref, o_ref):
    c = cond_ref[...]
    s = c * jax.nn.sigmoid(c)
    o_ref[...] = _dot_x3(_hi_lo(s), _hi_lo(w_ref[...])) + b_ref[...]


def _modulation(cond, w_mod, b_mod):
    n_layers = w_mod.shape[0]
    n_chunks = w_mod.shape[2] // D_MODEL
    return pl.pallas_call(
        _mod_kernel,
        grid=(n_layers, n_chunks),
        in_specs=[
            pl.BlockSpec((16, D_MODEL), lambda l, j: (0, 0)),
            pl.BlockSpec((None, D_MODEL, D_MODEL), lambda l, j: (l, 0, j)),
            pl.BlockSpec((None, 1, D_MODEL), lambda l, j: (l, 0, j)),
        ],
        out_specs=pl.BlockSpec((None, 16, D_MODEL), lambda l, j: (l, 0, j)),
        out_shape=jax.ShapeDtypeStruct((n_layers, 16, w_mod.shape[2]), F32),
        compiler_params=_params(("parallel", "parallel")),
        name="modulation",
    )(cond, w_mod, b_mod.reshape(n_layers, 1, -1))


def _two_part(i, n_first, a_ref, b_ref):
    return jnp.where(i < n_first, a_ref[...], b_ref[...])


def _hi_lo(x):
    hi = x.astype(BF16)
    return hi, (x - hi.astype(F32)).astype(BF16)


def _dot_x3(a, b, dims=None):
    (ah, al), (bh, bl) = a, b
    if dims is None:
        d = lambda x, y: jnp.dot(x, y, preferred_element_type=F32)
    else:
        d = lambda x, y: lax.dot_general(x, y, dims, preferred_element_type=F32)
    return d(ah, bh) + (d(al, bh) + d(ah, bl))


def _two_part_out_specs(width, n_first, tile=TOKEN_TILE):
    return _two_part_specs(width, n_first, tile)


def _two_part_specs(width, n_first, tile=TOKEN_TILE):
    return [pl.BlockSpec((tile, width), lambda i: (jnp.minimum(i, n_first - 1), 0)),
            pl.BlockSpec((tile, width), lambda i: (jnp.maximum(i - n_first, 0), 0))]


def _inproj_kernel(*refs, n_ctx_tiles, precise_ctx):
    it = iter(refs)
    xa_ref, xb_ref, sh_ref, sc_ref, g_ref, w_ref = [next(it) for _ in range(6)]
    wl_ref = next(it) if precise_ctx else None
    (wgt_ref, wgtl_ref, brow_ref, gq_ref, gk_ref, eye_ref, cos_ref, sa_ref, sb_ref, gsum_ref,
     qa_ref, qb_ref, k_ref, v_ref, mqa_ref, mqb_ref, mka_ref, mkb_ref, mva_ref, mvb_ref,
     og_ref, gc_ref, gr_ref) = it
    tile = pl.program_id(0)
    x = _two_part(tile, n_ctx_tiles, xa_ref, xb_ref)
    ms = jnp.mean(x * x, axis=-1, keepdims=True)
    h = x * lax.rsqrt(ms + EPS) * g_ref[...]
    h = h * (1.0 + sc_ref[...]) + sh_ref[...]
    hb = h.astype(BF16)

    zr = _dot_x3((wgt_ref[...], wgtl_ref[...]), _hi_lo(h), NT_DIMS) + brow_ref[...]
    sub = lax.broadcasted_iota(jnp.int32, zr.shape, 0)
    gr = jnp.where(sub % 4 < 2, zr, _log_sigmoid(zr))
    eye = eye_ref[...]
    gc = functools.reduce(jnp.add, [lax.dot_general(eye, p, NT_DIMS, preferred_element_type=F32)
                                    for p in _split3(gr)])
    gr_ref[...] = jnp.zeros_like(gr_ref)
    for hd in range(N_ML_HEADS):
        gc_ref[hd] = gc[:, 4 * hd:4 * hd + 4]
        gr_ref[hd, 0:4, :] = gr[4 * hd:4 * hd + 4, :]

    cos = cos_ref[...]
    sa = sa_ref[...]
    sb = sb_ref[...]
    gsum = gsum_ref[...]

    def project(precise, q_ref, mq_ref, mk_ref, mv_ref):
        if precise:
            z = _dot_x3(_hi_lo(h), (w_ref[...], wl_ref[...]))
        else:
            z = jnp.dot(hb, w_ref[...], preferred_element_type=F32)
        proj = lambda c0, width: z[:, c0:c0 + width]
        act = F32 if precise else BF16

        n_qk = ATT_WIDTH // LANES + 1
        t_rows = hb.shape[0]
        zqk = proj(0, n_qk * LANES)
        zs = [zqk[:, c * LANES:(c + 1) * LANES] for c in range(n_qk)]
        sq = jnp.concatenate([z * z for z in zs], axis=0)
        ss = jnp.dot(jnp.concatenate(_hi_lo(sq), axis=0), gsum, preferred_element_type=F32)
        ss = ss[:n_qk * t_rows] + ss[n_qk * t_rows:]

        def headnorm_rope(c, gain):
            zn = zs[c] * lax.rsqrt(ss[c * t_rows:(c + 1) * t_rows] * (1.0 / HEAD_DIM) + EPS) * gain
            return zn * cos + pltpu.roll(zn, LANES - 16, 1) * sa + pltpu.roll(zn, 16, 1) * sb

        for c in range(n_qk - 1):
            q_ref[:, c * LANES:(c + 1) * LANES] = (headnorm_rope(c, gq_ref[...]) * 0.125).astype(act)
        k_ref[...] = headnorm_rope(n_qk - 1, gk_ref[...])
        v_ref[...] = proj(640, LANES)
        mq_ref[...] = proj(768, ML_WIDTH).astype(act)
        mk_ref[...] = (proj(1280, ML_WIDTH) * (ML_HEAD_DIM ** -0.5)).astype(act)
        mv_ref[...] = proj(1792, ML_WIDTH).astype(act)
        og_ref[...] = jax.nn.sigmoid(proj(2304, ML_WIDTH))

    @pl.when(tile < n_ctx_tiles)
    def _ctx():
        project(precise_ctx, qa_ref, mqa_ref, mka_ref, mva_ref)

    @pl.when(tile >= n_ctx_tiles)
    def _lat():
        project(False, qb_ref, mqb_ref, mkb_ref, mvb_ref)


def _inproj(xa, xb, mod3, n_ctx_tiles, tiles_per_lat_seq, ctx_row, g_mix, w_main, w_gate, b_gate,
            g_q, g_k, rope):
    w_hi, w_lo = w_main
    precise_ctx = w_lo is not None
    n_ctx, n_lat = xa.shape[0], xb.shape[0]
    n = n_ctx + n_lat
    n_tiles = n // TOKEN_TILE

    def mod_row(i):
        return jnp.where(i < n_ctx_tiles, ctx_row, (i - n_ctx_tiles) // tiles_per_lat_seq)

    def rope_blk(i):
        return jnp.where(i < n_ctx_tiles, 0, 1 + (i - n_ctx_tiles) % tiles_per_lat_seq)

    cos_t, sa_t, sb_t = rope
    lane = np.arange(LANES)
    gsum = jnp.asarray((lane[:, None] // HEAD_DIM) == (lane[None, :] // HEAD_DIM), BF16)
    tok = lambda w: pl.BlockSpec((TOKEN_TILE, w), lambda i: (i, 0))
    full = lambda a: pl.BlockSpec(a.shape, lambda i: (0,) * a.ndim)
    modspec = lambda j: pl.BlockSpec((None, 1, D_MODEL), lambda i: (mod_row(i), 0, j))
    ropespec = pl.BlockSpec((TOKEN_TILE, LANES), lambda i: (rope_blk(i), 0))
    consts = (g_mix.reshape(1, -1), w_hi) + ((w_lo,) if precise_ctx else ()) + (
        *_hi_lo(w_gate.T), b_gate.reshape(-1, 1),
        jnp.tile(g_q, 2).reshape(1, -1), jnp.tile(g_k, 2).reshape(1, -1),
        jnp.asarray(np.eye(TOKEN_TILE), BF16))
    args = (xa, xb, mod3, mod3) + consts + (cos_t, sa_t, sb_t, gsum)
    in_specs = _two_part_specs(D_MODEL, n_ctx_tiles) + [modspec(0), modspec(1)] \
        + [full(a) for a in consts] + [ropespec, ropespec, ropespec, full(gsum)]
    ctx_act = F32 if precise_ctx else BF16
    pair_shape = lambda w: [jax.ShapeDtypeStruct((n_ctx, w), ctx_act),
                            jax.ShapeDtypeStruct((n_lat, w), BF16)]
    pair_spec = lambda w: _two_part_out_specs(w, n_ctx_tiles)
    out_shape = (
        pair_shape(ATT_WIDTH)
        + [jax.ShapeDtypeStruct((n, LANES), F32),
           jax.ShapeDtypeStruct((n, LANES), F32)]
        + pair_shape(ML_WIDTH) + pair_shape(ML_WIDTH) + pair_shape(ML_WIDTH)
        + [jax.ShapeDtypeStruct((n, ML_WIDTH), F32),
           jax.ShapeDtypeStruct((N_ML_HEADS, n, 4), F32),
           jax.ShapeDtypeStruct((N_ML_HEADS, 8, n), F32)])
    out_specs = (pair_spec(ATT_WIDTH) + [tok(LANES), tok(LANES)]
                 + pair_spec(ML_WIDTH) + pair_spec(ML_WIDTH) + pair_spec(ML_WIDTH)
                 + [tok(ML_WIDTH),
                    pl.BlockSpec((N_ML_HEADS, TOKEN_TILE, 4), lambda i: (0, i, 0)),
                    pl.BlockSpec((N_ML_HEADS, 8, TOKEN_TILE), lambda i: (0, 0, i))])
    outs = pl.pallas_call(
        functools.partial(_inproj_kernel, n_ctx_tiles=n_ctx_tiles, precise_ctx=precise_ctx),
        grid=(n_tiles,), in_specs=in_specs, out_specs=tuple(out_specs),
        out_shape=tuple(out_shape), compiler_params=_params(("arbitrary",)), name="inproj",
    )(*args)
    qa, qb, k, v, mqa, mqb, mka, mkb, mva, mvb, og, gcol, grow = outs
    return (qa, qb), k, v, (mqa, mqb), (mka, mkb), (mva, mvb), og, gcol, grow


def _attn_kernel(*refs, n_kv, precise, cached):
    q_ref = refs[0]
    kv_refs = refs[1:1 + 2 * n_kv]
    o_ref = refs[1 + 2 * n_kv]
    dup_scr = refs[2 + 2 * n_kv:]
    tq = q_ref.shape[0]
    lo_q = lax.broadcasted_iota(jnp.int32, (tq, LANES), 1) < HEAD_DIM
    operand = _hi_lo if precise else (lambda a: a.astype(BF16))
    if precise:
        qk = lambda a, b: _dot_x3(a, b, NT_DIMS)
        pv = _dot_x3
    else:
        qk = lambda a, b: lax.dot_general(a, b, NT_DIMS, preferred_element_type=F32)
        pv = lambda a, b: jnp.dot(a, b, preferred_element_type=F32)

    def dup_half(ref, g):
        a = ref[...]
        r = pltpu.roll(a, HEAD_DIM, 1)
        lo = lax.broadcasted_iota(jnp.int32, a.shape, 1) < HEAD_DIM
        return operand(jnp.where(lo, a, r) if g == 0 else jnp.where(lo, r, a))

    if cached:
        @pl.when(pl.program_id(1) == 0)
        def _fill():
            for g in range(N_KV_HEADS):
                for j in range(2 * n_kv):
                    dup_scr[g * 2 * n_kv + j][...] = dup_half(kv_refs[j], g)

        dup = lambda j, g: dup_scr[g * 2 * n_kv + j][...]
    else:
        dup = lambda j, g: dup_half(kv_refs[j], g)

    for g in range(N_KV_HEADS):
        ks = [dup(2 * p, g) for p in range(n_kv)]
        vs = [dup(2 * p + 1, g) for p in range(n_kv)]
        for hb in range(2):
            c0 = (2 * g + hb) * LANES
            qb = q_ref[:, c0:c0 + LANES]
            outs = []
            for half in range(2):
                keep = lo_q if half == 0 else jnp.logical_not(lo_q)
                qm = operand(jnp.where(keep, qb, jnp.zeros_like(qb)))
                ss = [qk(qm, kd) for kd in ks]
                m = functools.reduce(jnp.maximum, [jnp.max(s, axis=1, keepdims=True) for s in ss])
                ps = [jnp.exp(s - m) for s in ss]
                den = functools.reduce(jnp.add, [jnp.sum(p, axis=1, keepdims=True) for p in ps])
                o = functools.reduce(jnp.add, [pv(operand(p), vd) for p, vd in zip(ps, vs)])
                outs.append(o / den)
            o_ref[:, c0:c0 + LANES] = jnp.where(lo_q, outs[0], outs[1]).astype(o_ref.dtype)


def _attention(q, k, v, kv_row0, n_seq, seq_len, cache=None):
    precise = q.dtype == F32
    nq = seq_len // SEQ_BLOCK
    sb0 = kv_row0 // seq_len
    in_specs = [
        pl.BlockSpec((SEQ_BLOCK, ATT_WIDTH), lambda b, i: (b * nq + i, 0)),
        pl.BlockSpec((seq_len, LANES), lambda b, i: (sb0 + b, 0)),
        pl.BlockSpec((seq_len, LANES), lambda b, i: (sb0 + b, 0)),
    ]
    args = [q, k, v]
    n_kv = 1
    if cache is not None:
        ck, cv = cache
        past = ck.shape[0] // n_seq
        in_specs += [pl.BlockSpec((past, LANES), lambda b, i: (b, 0))] * 2
        args += [ck, cv]
        n_kv = 2
    cached = nq > 1 and not precise
    scratch = [pltpu.VMEM((a.shape[0] // n_seq if j >= 2 else seq_len, LANES), BF16)
               for _ in range(N_KV_HEADS) for j, a in enumerate(args[1:])] if cached else []
    return pl.pallas_call(
        functools.partial(_attn_kernel, n_kv=n_kv, precise=precise, cached=cached),
        grid=(n_seq, nq), in_specs=in_specs,
        out_specs=pl.BlockSpec((SEQ_BLOCK, ATT_WIDTH), lambda b, i: (b * nq + i, 0)),
        out_shape=jax.ShapeDtypeStruct((n_seq * seq_len, ATT_WIDTH), q.dtype),
        scratch_shapes=scratch,
        compiler_params=_params(("parallel", "arbitrary" if cached else "parallel")),
        name="attention",
    )(*args)


def _mlstm_kernel(*refs, heads, **static):
    it = iter(refs)
    q_ref, k_ref, v_ref, og_ref, gmh_ref, gcol_ref, grow_ref, u_ref, l_ref = [next(it) for _ in range(9)]
    init_refs = (next(it), next(it)) if static["has_init"] else ()
    ml_ref = next(it)
    state_refs = (next(it), next(it)) if static["emit_state"] else ()
    for hd in range(heads):
        lanes = pl.ds(hd * ML_HEAD_DIM, ML_HEAD_DIM)
        head_refs = [q_ref.at[:, lanes], k_ref.at[:, lanes], v_ref.at[:, lanes], og_ref.at[:, lanes],
                     gmh_ref.at[:, lanes], gcol_ref.at[hd], grow_ref.at[hd], u_ref, l_ref]
        if init_refs:
            head_refs += [init_refs[0].at[:, hd], init_refs[1].at[hd]]
        head_refs.append(ml_ref.at[:, lanes])
        if state_refs:
            head_refs += [state_refs[0].at[:, hd], state_refs[1].at[hd]]
        _mlstm_head(*head_refs, **static)


def _mlstm_head(*refs, seq_len, has_init, emit_state, precise):
    it = iter(refs)
    q_ref, k_ref, v_ref, og_ref, gmh_ref, gcol_ref, grow_ref, u_ref, l_ref = [next(it) for _ in range(9)]
    if has_init:
        c0_ref, nm0_ref = next(it), next(it)
    ml_ref = next(it)
    if emit_state:
        cf_ref, nmf_ref = next(it), next(it)

    operand = _hi_lo if precise else (lambda a: a.astype(BF16))
    if precise:
        qk = lambda a, b: _dot_x3(a, b, NT_DIMS)
        pv = _dot_x3
    else:
        qk = lambda a, b: lax.dot_general(a, b, NT_DIMS, preferred_element_type=F32)
        pv = lambda a, b: jnp.dot(a, b, preferred_element_type=F32)

    bq = SEQ_BLOCK
    nb = seq_len // bq
    blk = lambda j: slice(j * bq, (j + 1) * bq)
    upper_incl = u_ref[...]
    lower_incl = l_ref[...]

    def tri_dot(x, tri):
        return functools.reduce(jnp.add, [jnp.dot(p, tri, preferred_element_type=F32)
                                          for p in _split3(x)])

    ig_row = [[None] * nb for _ in range(2)]
    lf_row = [[None] * nb for _ in range(2)]
    within = [[None] * nb for _ in range(2)]
    bsum = [[None] * nb for _ in range(2)]
    for j in range(nb):
        g8 = grow_ref[:, blk(j)]
        cum_f = tri_dot(g8, upper_incl)
        cum_b = tri_dot(g8, lower_incl)
        for d in range(2):
            ig_row[d][j] = g8[d:d + 1, :]
            lf_row[d][j] = g8[2 + d:3 + d, :]
            within[d][j] = (cum_f if d == 0 else cum_b)[2 + d:3 + d, :]
            bsum[d][j] = jnp.sum(lf_row[d][j], axis=1, keepdims=True)
    zero11 = jnp.zeros((1, 1), F32)
    r_i = lax.broadcasted_iota(jnp.int32, (bq, bq), 0)
    c_i = lax.broadcasted_iota(jnp.int32, (bq, bq), 1)
    causal = [c_i <= r_i, c_i >= r_i]
    gmh = gmh_ref[...]

    q_blocks = [q_ref[blk(i), :] for i in range(nb)]
    scores = [qk(operand(q_blocks[i]), operand(k_ref[blk(i), :])) for i in range(nb)]

    h_dir = [[None] * nb for _ in range(2)]
    final = [None, None]
    for d in range(2):
        if has_init:
            state = (c0_ref[d], nm0_ref[d:d + 1, :])
            m = nm0_ref[2 + d:3 + d, 0:1]
        else:
            state, m = None, zero11
        scan = range(nb) if d == 0 else range(nb - 1, -1, -1)
        for step, i in enumerate(scan):
            q_i = q_blocks[i]
            a_loc = ig_row[d][i] - within[d][i]
            m_col = jnp.maximum(m, jnp.max(jnp.where(causal[d], a_loc, NEG_INF),
                                           axis=1, keepdims=True))
            if nb > 1:
                widen = lambda col: jnp.broadcast_to(col, (bq, LANES))
                tri = lower_incl if d == 0 else upper_incl
                b_rep = functools.reduce(jnp.add, [
                    lax.dot_general(tri, jnp.broadcast_to(piece, (LANES, bq)), NT_DIMS,
                                    preferred_element_type=F32)
                    for piece in _hi_lo(lf_row[d][i])])
                m_rep = widen(m_col)
                m_wide = jnp.concatenate([m_rep] * (bq // LANES), axis=1)
            else:
                widen = lambda col: col
                b_rep = jnp.sum(jnp.where(causal[d], lf_row[d][i], 0.0), axis=1, keepdims=True)
                m_rep = m_wide = m_col
            p = jnp.exp(jnp.where(causal[d], a_loc - m_wide, NEG_INF)) * scores[i]
            den = widen(jnp.sum(p, axis=1, keepdims=True))
            num = pv(operand(p), operand(v_ref[blk(i), :]))
            if state is not None:
                c_prev, n_prev = state
                w_inter = jnp.exp(m - m_rep)
                q_b = q_i.astype(BF16)
                qc = jnp.dot(q_b, c_prev.astype(BF16), preferred_element_type=F32)
                n_rows = jnp.broadcast_to(n_prev, (LANES, ML_HEAD_DIM)).astype(BF16)
                qn = lax.dot_general(q_b, n_rows, NT_DIMS, preferred_element_type=F32)
                num = num + w_inter * qc
                den = den + w_inter * qn
            nrm = jnp.maximum(jnp.abs(den), jnp.exp(-(b_rep + m_rep)))
            h_dir[d][i] = num / nrm
            if step == nb - 1 and not emit_state:
                break
            m_last = jnp.maximum(m, jnp.max(a_loc, axis=1, keepdims=True))
            a_col = widen(gcol_ref[blk(i), d:d + 1]) - b_rep
            kw = k_ref[blk(i), :].astype(F32) * jnp.exp(a_col - m_last)
            c_new = lax.dot_general(kw.astype(BF16), v_ref[blk(i), :].astype(BF16), TN_DIMS,
                                    preferred_element_type=F32)
            n_new = jnp.sum(kw, axis=0, keepdims=True)
            if state is not None:
                decay = jnp.exp(m - m_last)
                c_new = c_new + decay * state[0]
                n_new = n_new + decay * state[1]
            state = (c_new, n_new)
            m = bsum[d][i] + m_last
        final[d] = (state, m)

    for i in range(nb):
        h = h_dir[0][i] + h_dir[1][i]
        hn = h * lax.rsqrt(jnp.mean(h * h, axis=-1, keepdims=True) + EPS) * gmh
        ml_ref[blk(i), :] = (og_ref[blk(i), :] * hn).astype(ml_ref.dtype)

    if emit_state:
        nmf_ref[...] = jnp.zeros_like(nmf_ref)
        for d in range(2):
            (c_fin, n_fin), m_fin = final[d]
            cf_ref[d] = c_fin
            nmf_ref[d:d + 1, :] = n_fin
            nmf_ref[2 + d:3 + d, :] = jnp.broadcast_to(m_fin, (1, ML_HEAD_DIM))


def _mlstm(mq, mk, mv, og, g_mh, gcol, grow, row0, n_seq, seq_len, init=None, emit_state=False):
    precise = mq.dtype == F32
    assert not (precise and init is not None)
    sb0 = row0 // seq_len
    tri = np.arange(SEQ_BLOCK)
    upper_incl = jnp.asarray(tri[:, None] <= tri[None, :], BF16)
    lower_incl = jnp.asarray(tri[:, None] >= tri[None, :], BF16)
    heads = MLSTM_HEADS_PER_STEP
    width = heads * ML_HEAD_DIM
    ownblk = lambda: pl.BlockSpec((seq_len, width), lambda b, h: (b, h))
    headblk = lambda: pl.BlockSpec((seq_len, width), lambda b, h: (sb0 + b, h))
    const = lambda a: pl.BlockSpec(a.shape, lambda b, h: (0,) * a.ndim)
    in_specs = [ownblk(), ownblk(), ownblk(), headblk(),
                pl.BlockSpec((1, width), lambda b, h: (0, h)),
                pl.BlockSpec((heads, seq_len, 4), lambda b, h: (h, sb0 + b, 0)),
                pl.BlockSpec((heads, 8, seq_len), lambda b, h: (h, 0, sb0 + b)),
                const(upper_incl), const(lower_incl)]
    args = [mq, mk, mv, og, g_mh.reshape(1, -1), gcol, grow, upper_incl, lower_incl]
    if init is not None:
        c0, nm0, layer = init
        in_specs += [
            pl.BlockSpec((None, None, 2, heads, ML_HEAD_DIM, ML_HEAD_DIM),
                         lambda b, h: (b, layer, 0, h, 0, 0)),
            pl.BlockSpec((None, heads, 8, ML_HEAD_DIM), lambda b, h: (b, h, 0, 0))]
        args += [c0, nm0]
    out_shape = [jax.ShapeDtypeStruct((n_seq * seq_len, ML_WIDTH), mq.dtype)]
    out_specs = [pl.BlockSpec((seq_len, width), lambda b, h: (b, h))]
    if emit_state:
        out_shape += [jax.ShapeDtypeStruct((n_seq, 2, N_ML_HEADS, ML_HEAD_DIM, ML_HEAD_DIM), F32),
                      jax.ShapeDtypeStruct((n_seq, N_ML_HEADS, 8, ML_HEAD_DIM), F32)]
        out_specs += [pl.BlockSpec((None, 2, heads, ML_HEAD_DIM, ML_HEAD_DIM),
                                   lambda b, h: (b, 0, h, 0, 0)),
                      pl.BlockSpec((None, heads, 8, ML_HEAD_DIM), lambda b, h: (b, h, 0, 0))]
    return pl.pallas_call(
        functools.partial(_mlstm_kernel, heads=heads, seq_len=seq_len, has_init=init is not None,
                          emit_state=emit_state, precise=precise),
        grid=(n_seq, N_ML_HEADS // heads), in_specs=in_specs, out_specs=tuple(out_specs),
        out_shape=tuple(out_shape),
        compiler_params=_params(("parallel", "parallel")), name="mlstm",
    )(*args)


def _outproj_kernel(*refs, n_ctx_tiles, precise_ctx):
    it = iter(refs)
    atta_ref, attb_ref, mla_ref, mlb_ref, xa_ref, xb_ref, w_ref = [next(it) for _ in range(7)]
    wl_ref = next(it) if precise_ctx else None
    gt_ref, sh_ref, sc_ref, g_ref, wrt_ref, wrtl_ref, x1_ref, hp_ref, logits_ref, y_scr = it
    i = pl.program_id(0)

    def mix(att, ml):
        return (jnp.dot(att, w_ref[:ATT_WIDTH, :], preferred_element_type=F32)
                + jnp.dot(ml, w_ref[ATT_WIDTH:, :], preferred_element_type=F32))

    @pl.when(i < n_ctx_tiles)
    def _ctx():
        if precise_ctx:
            y_scr[...] = (
                _dot_x3(_hi_lo(atta_ref[...]), (w_ref[:ATT_WIDTH, :], wl_ref[:ATT_WIDTH, :]))
                + _dot_x3(_hi_lo(mla_ref[...]), (w_ref[ATT_WIDTH:, :], wl_ref[ATT_WIDTH:, :])))
        else:
            y_scr[...] = mix(atta_ref[...], mla_ref[...])

    @pl.when(i >= n_ctx_tiles)
    def _lat():
        y_scr[...] = mix(attb_ref[...], mlb_ref[...])

    x1 = _two_part(i, n_ctx_tiles, xa_ref, xb_ref) + gt_ref[...] * y_scr[...]
    x1_ref[...] = x1
    ms = jnp.mean(x1 * x1, axis=-1, keepdims=True)
    h2 = x1 * lax.rsqrt(ms + EPS) * g_ref[...]
    h2 = h2 * (1.0 + sc_ref[...]) + sh_ref[...]
    for k in range(ROW_TILE_SUBLANES):
        hp_ref[pl.ds(k, TOKEN_TILE, stride=ROW_TILE_SUBLANES), :] = h2[:, k * LANES:(k + 1) * LANES]

    logits_ref[...] = _dot_x3((wrt_ref[...], wrtl_ref[...]), _hi_lo(h2), NT_DIMS)


def _select_experts(logits, b_col):
    ex = jnp.exp(logits - jnp.max(logits, axis=0, keepdims=True))
    scores = ex / jnp.sum(ex, axis=0, keepdims=True)
    sel = scores + b_col
    row = lambda a, e: a[e:e + 1, :]
    grp_score = []
    for g in range(N_GROUPS):
        xs = [row(sel, g * GROUP_SIZE + j) for j in range(GROUP_SIZE)]
        pairs = [xs[a] + xs[b] for a in range(GROUP_SIZE) for b in range(a + 1, GROUP_SIZE)]
        grp_score.append(functools.reduce(jnp.maximum, pairs))
    best = grp_score[0]
    grp = jnp.zeros_like(best, dtype=jnp.int32)
    for g in range(1, N_GROUPS):
        better = grp_score[g] > best
        grp = jnp.where(better, g, grp)
        best = jnp.where(better, grp_score[g], best)
    pick = lambda a, j: functools.reduce(
        lambda acc, g: jnp.where(grp == g, row(a, g * GROUP_SIZE + j), acc),
        range(1, N_GROUPS), row(a, j))
    xs = [pick(sel, j) for j in range(GROUP_SIZE)]
    ws = [pick(scores, j) for j in range(GROUP_SIZE)]

    def argmax4(vals):
        bv, bi = vals[0], jnp.zeros_like(grp)
        for j in range(1, GROUP_SIZE):
            better = vals[j] > bv
            bi = jnp.where(better, j, bi)
            bv = jnp.where(better, vals[j], bv)
        return bi

    i1 = argmax4(xs)
    i2 = argmax4([jnp.where(i1 == j, NEG_INF, xs[j]) for j in range(GROUP_SIZE)])
    take = lambda vals, idx: functools.reduce(
        lambda acc, j: jnp.where(idx == j, vals[j], acc), range(1, GROUP_SIZE), vals[0])
    w1, w2 = take(ws, i1), take(ws, i2)
    wsum = w1 + w2
    w1, w2 = w1 / wsum, w2 / wsum
    return grp * GROUP_SIZE + i1, grp * GROUP_SIZE + i2, w1, w2


def _outproj(att, ml, x, mod3, n_ctx_tiles, tiles_per_lat_seq, ctx_row, w_out, g_ffn, w_router):
    w_hi, w_lo = w_out
    precise_ctx = w_lo is not None
    n = x[0].shape[0] + x[1].shape[0]

    def mod_row(i):
        return jnp.where(i < n_ctx_tiles, ctx_row, (i - n_ctx_tiles) // tiles_per_lat_seq)

    tok = lambda w: pl.BlockSpec((TOKEN_TILE, w), lambda i: (i, 0))
    full = lambda a: pl.BlockSpec(a.shape, lambda i: (0,) * a.ndim)
    modspec = lambda j: pl.BlockSpec((None, 1, D_MODEL), lambda i: (mod_row(i), 0, j))
    weights = (w_hi, w_lo) if precise_ctx else (w_hi,)
    consts = (g_ffn.reshape(1, -1), *_hi_lo(w_router.T))
    args = (*att, *ml, *x, *weights, mod3, mod3, mod3) + consts
    in_specs = (_two_part_specs(ATT_WIDTH, n_ctx_tiles) + _two_part_specs(ML_WIDTH, n_ctx_tiles)
                + _two_part_specs(D_MODEL, n_ctx_tiles) + [full(w) for w in weights]
                + [modspec(2), modspec(3), modspec(4)] + [full(a) for a in consts])
    return pl.pallas_call(
        functools.partial(_outproj_kernel, n_ctx_tiles=n_ctx_tiles, precise_ctx=precise_ctx),
        grid=(n // TOKEN_TILE,), in_specs=in_specs,
        scratch_shapes=[pltpu.VMEM((TOKEN_TILE, D_MODEL), F32)],
        out_specs=(tok(D_MODEL),
                   pl.BlockSpec((TOKEN_TILE * ROW_TILE_SUBLANES, LANES), lambda i: (i, 0)),
                   pl.BlockSpec((N_EXPERTS, TOKEN_TILE), lambda i: (0, i))),
        out_shape=(jax.ShapeDtypeStruct((n, D_MODEL), F32),
                   jax.ShapeDtypeStruct((n * ROW_TILE_SUBLANES, LANES), F32),
                   jax.ShapeDtypeStruct((N_EXPERTS, n), F32)),
        compiler_params=_params(("parallel",)), name="outproj_router",
    )(*args)


def _moe_kernel(order_ref, pos0_ref, pos1_ref, off_ref, cnt_ref,
                h_ref, wg_ref, wu_ref, wd_ref, x1_ref, gt_ref, wcol_ref, ya_ref, yb_ref,
                o_scr, xs_scr, comb0_scr, comb1_scr, *, n_ctx_tiles):
    t = pl.program_id(0)
    s = pl.program_id(1)
    sub = ROW_TILE_SUBLANES
    groups = MOE_CHUNK // 8

    def tile(ref, row):
        return ref.at[pl.ds(pl.multiple_of(row * sub, sub), sub), :]

    def slab(ref, row0, n_rows, k):
        return ref.at[pl.ds(row0 * sub + k, n_rows, stride=sub), :]

    def gather_rows(buf, slot0):
        for j in range(MOE_CHUNK):
            tile(xs_scr, buf * MOE_CHUNK + j)[...] = h_ref[order_ref[slot0 + j]]

    def ffn(buf, base):
        a = jnp.zeros((MOE_CHUNK, D_EXPERT), F32)
        b = jnp.zeros((MOE_CHUNK, D_EXPERT), F32)
        for p in range(sub // 2):
            lhs = jnp.concatenate([slab(xs_scr, buf * MOE_CHUNK, MOE_CHUNK, 2 * p)[...],
                                   slab(xs_scr, buf * MOE_CHUNK, MOE_CHUNK, 2 * p + 1)[...]],
                                  axis=1).astype(BF16)
            rows = slice(2 * p * LANES, (2 * p + 2) * LANES)
            a = a + jnp.dot(lhs, wg_ref[rows, :], preferred_element_type=F32)
            b = b + jnp.dot(lhs, wu_ref[rows, :], preferred_element_type=F32)
        hid = (a * jax.nn.sigmoid(a)) * b
        out = jnp.dot(hid.astype(BF16), wd_ref[...], preferred_element_type=F32)
        for k in range(sub):
            slab(o_scr, base, MOE_CHUNK, k)[...] = out[:, k * LANES:(k + 1) * LANES]

    @pl.when(s == 0)
    def _first():
        gather_rows(0, t * MOE_SLOTS + pl.multiple_of(off_ref[t * N_EXPERTS], 8))

    @pl.when(s < N_EXPERTS)
    def _expert():
        seg = t * N_EXPERTS + s
        off = pl.multiple_of(off_ref[seg], 8)
        cur = s % 2
        nxt_seg = t * N_EXPERTS + jnp.minimum(s + 1, N_EXPERTS - 1)
        ffn(cur, off)
        gather_rows(1 - cur, t * MOE_SLOTS + pl.multiple_of(off_ref[nxt_seg], 8))

        def extra(c, carry):
            base = pl.multiple_of(off + c * MOE_CHUNK, 8)

            def gather8(i, carry2):
                slot = t * MOE_SLOTS + base + i * 8
                for k in range(8):
                    tile(xs_scr, 2 * MOE_CHUNK + i * 8 + k)[...] = h_ref[order_ref[slot + k]]
                return carry2

            lax.fori_loop(0, groups, gather8, 0)
            ffn(2, base)
            return carry

        lax.fori_loop(1, (cnt_ref[seg] + MOE_CHUNK - 1) // MOE_CHUNK, extra, 0)

    @pl.when(s >= N_EXPERTS)
    def _combine():
        tok0 = t * MOE_TILE + (s - N_EXPERTS) * MOE_OUT_TILE

        def body8(i, carry):
            for k in range(8):
                p0 = pos0_ref[tok0 + i * 8 + k]
                p1 = pos1_ref[tok0 + i * 8 + k]
                tile(comb0_scr, i * 8 + k)[...] = tile(o_scr, p0)[...]
                tile(comb1_scr, i * 8 + k)[...] = tile(o_scr, p1)[...]
            return carry

        lax.fori_loop(0, MOE_OUT_TILE // 8, body8, 0)
        wcol = wcol_ref[...]
        rows_of = lambda ref: jnp.concatenate(
            [slab(ref, 0, MOE_OUT_TILE, k)[...] for k in range(sub)], axis=1)
        comb = wcol[:, 2:3] * rows_of(comb0_scr) + wcol[:, 3:4] * rows_of(comb1_scr)
        y = x1_ref[...] + gt_ref[...] * comb
        chunk_ix = t * (MOE_TILE // MOE_OUT_TILE) + s - N_EXPERTS

        @pl.when(chunk_ix < n_ctx_tiles)
        def _ctx():
            ya_ref[...] = y

        @pl.when(chunk_ix >= n_ctx_tiles)
        def _lat():
            yb_ref[...] = y


def _route_tables(logits, b_router):
    n = logits.shape[1]
    nt = n // MOE_TILE
    tri = np.arange(SEQ_BLOCK)
    strict_upper = jnp.asarray(tri[:, None] < tri[None, :], BF16)
    tok = np.arange(MOE_TILE)
    digits = np.zeros((MOE_TILE, LANES), np.float32)
    digits[:, 0] = tok % 256
    digits[:, 1] = tok // 256
    pos, order, meta, wcol = pl.pallas_call(
        _route_kernel, grid=(nt,),
        in_specs=[pl.BlockSpec((N_EXPERTS, MOE_TILE), lambda t: (0, t)),
                  pl.BlockSpec((N_EXPERTS, 1), lambda t: (0, 0)),
                  pl.BlockSpec((SEQ_BLOCK, SEQ_BLOCK), lambda t: (0, 0)),
                  pl.BlockSpec((SEQ_BLOCK, SEQ_BLOCK), lambda t: (0, 0)),
                  pl.BlockSpec((MOE_TILE, LANES), lambda t: (0, 0))],
        out_specs=(pl.BlockSpec((8, MOE_TILE), lambda t: (0, t)),
                   pl.BlockSpec((None, MOE_SLOTS, 1), lambda t: (t, 0, 0)),
                   pl.BlockSpec((None, N_EXPERTS, 8), lambda t: (t, 0, 0)),
                   pl.BlockSpec((MOE_TILE, 8), lambda t: (t, 0))),
        out_shape=(jax.ShapeDtypeStruct((8, n), jnp.int32),
                   jax.ShapeDtypeStruct((nt, MOE_SLOTS, 1), jnp.int32),
                   jax.ShapeDtypeStruct((nt, N_EXPERTS, 8), jnp.int32),
                   jax.ShapeDtypeStruct((n, 8), F32)),
        scratch_shapes=[pltpu.VMEM((8, MOE_TILE), F32)],
        compiler_params=_params(("parallel",)), name="route_tables",
    )(logits, b_router.reshape(-1, 1), strict_upper, jnp.asarray(np.eye(SEQ_BLOCK), BF16),
      jnp.asarray(digits, BF16))
    return (order.reshape(-1), pos[0], pos[1], meta[:, :, 0].reshape(-1),
            meta[:, :, 1].reshape(-1), wcol)


def _route_kernel(logits_ref, br_ref, su_ref, eye_ref, digits_ref,
                  pos_ref, order_ref, meta_ref, wcol_ref, wrow_scr):
    e1, e2, w1, w2 = _select_experts(logits_ref[...], br_ref[...])
    wrow_scr[...] = jnp.zeros_like(wrow_scr)
    wrow_scr[2:3, :] = w1
    wrow_scr[3:4, :] = w2
    eye = eye_ref[...]
    for b in range(MOE_TILE // SEQ_BLOCK):
        cols = slice(b * SEQ_BLOCK, (b + 1) * SEQ_BLOCK)
        wcol_ref[cols, :] = functools.reduce(jnp.add, [
            lax.dot_general(eye, p, NT_DIMS, preferred_element_type=F32)
            for p in _split3(wrow_scr[:, cols])])
    eid = lax.broadcasted_iota(jnp.int32, (N_EXPERTS, MOE_TILE), 0)
    oh1, oh2 = eid == e1, eid == e2
    oh = jnp.where(oh1, 1.0, 0.0) + jnp.where(oh2, 1.0, 0.0)
    nblk = MOE_TILE // SEQ_BLOCK
    blocks = [oh[:, b * SEQ_BLOCK:(b + 1) * SEQ_BLOCK] for b in range(nblk)]
    inner = jnp.dot(jnp.concatenate(blocks, axis=0).astype(BF16), su_ref[...],
                    preferred_element_type=F32)
    run = jnp.zeros((N_EXPERTS, 1), F32)
    ranks = []
    for b in range(nblk):
        ranks.append(inner[b * N_EXPERTS:(b + 1) * N_EXPERTS, :] + run)
        run = run + jnp.sum(blocks[b], axis=1, keepdims=True)
    count = run
    seg = jnp.floor((count + 7.0) * 0.125) * 8.0
    sub = lax.broadcasted_iota(jnp.int32, (N_EXPERTS, 1), 0)
    off = jnp.zeros((N_EXPERTS, 1), F32)
    for e in range(N_EXPERTS - 1):
        off = off + jnp.where(sub > e, seg[e:e + 1, :], 0.0)
    slot = jnp.concatenate(ranks, axis=1) + off
    pos1 = jnp.sum(jnp.where(oh1, slot, 0.0), axis=0, keepdims=True).astype(jnp.int32)
    pos2 = jnp.sum(jnp.where(oh2, slot, 0.0), axis=0, keepdims=True).astype(jnp.int32)
    pos_ref[...] = jnp.zeros_like(pos_ref)
    pos_ref[0:1, :] = pos1
    pos_ref[1:2, :] = pos2
    meta_ref[...] = jnp.zeros_like(meta_ref)
    meta_ref[:, 0:1] = off.astype(jnp.int32)
    meta_ref[:, 1:2] = count.astype(jnp.int32)
    digits = digits_ref[...]
    rows = MOE_SLOTS // ROUTE_SLOT_BLOCKS
    for sb in range(ROUTE_SLOT_BLOCKS):
        j = lax.broadcasted_iota(jnp.int32, (rows, MOE_TILE), 0) + sb * rows
        hit = jnp.where(j == pos1, 1.0, 0.0) + jnp.where(j == pos2, 1.0, 0.0)
        d = jnp.dot(hit.astype(BF16), digits, preferred_element_type=F32)
        order_ref[sb * rows:(sb + 1) * rows, :] = (d[:, 0:1] + 256.0 * d[:, 1:2]).astype(jnp.int32)


def _moe(hp, logits, b_router, x1, mod3, layer, n_ctx_tiles, tiles_per_lat_seq, ctx_row,
         wg, wu, wd):
    n = hp.shape[0] // ROW_TILE_SUBLANES
    order, pos0, pos1, off, count, wcol = _route_tables(logits, b_router)
    chunks_per_tile = MOE_TILE // MOE_OUT_TILE
    n_steps = N_EXPERTS + chunks_per_tile

    def chunk_ix(t, s):
        return t * chunks_per_tile + jnp.maximum(s - N_EXPERTS, 0)

    def mod_row(g):
        return jnp.where(g < n_ctx_tiles, ctx_row, (g - n_ctx_tiles) // tiles_per_lat_seq)

    wspec = lambda r, c: pl.BlockSpec(
        (None, None, r, c), lambda t, s, *_: (layer, jnp.minimum(s, N_EXPERTS - 1), 0, 0))
    chunk_spec = pl.BlockSpec((MOE_OUT_TILE, D_MODEL), lambda t, s, *_: (chunk_ix(t, s), 0))
    grid_spec = pltpu.PrefetchScalarGridSpec(
        num_scalar_prefetch=5,
        grid=(n // MOE_TILE, n_steps),
        in_specs=[
            pl.BlockSpec((MOE_TILE, ROW_TILE_SUBLANES, LANES), lambda t, s, *_: (t, 0, 0)),
            wspec(D_MODEL, D_EXPERT), wspec(D_MODEL, D_EXPERT), wspec(D_EXPERT, D_MODEL),
            chunk_spec,
            pl.BlockSpec((None, 1, D_MODEL), lambda t, s, *_: (mod_row(chunk_ix(t, s)), 0, 5)),
            pl.BlockSpec((MOE_OUT_TILE, 8), lambda t, s, *_: (chunk_ix(t, s), 0)),
        ],
        out_specs=(
            pl.BlockSpec((MOE_OUT_TILE, D_MODEL),
                         lambda t, s, *_: (jnp.minimum(chunk_ix(t, s), n_ctx_tiles - 1), 0)),
            pl.BlockSpec((MOE_OUT_TILE, D_MODEL),
                         lambda t, s, *_: (jnp.maximum(chunk_ix(t, s) - n_ctx_tiles, 0), 0))),
        scratch_shapes=[pltpu.VMEM((MOE_SLOTS * ROW_TILE_SUBLANES, LANES), F32),
                        pltpu.VMEM((3 * MOE_CHUNK * ROW_TILE_SUBLANES, LANES), F32),
                        pltpu.VMEM((MOE_OUT_TILE * ROW_TILE_SUBLANES, LANES), F32),
                        pltpu.VMEM((MOE_OUT_TILE * ROW_TILE_SUBLANES, LANES), F32)],
    )
    n_ctx = n_ctx_tiles * MOE_OUT_TILE
    return pl.pallas_call(
        functools.partial(_moe_kernel, n_ctx_tiles=n_ctx_tiles), grid_spec=grid_spec,
        out_shape=(jax.ShapeDtypeStruct((n_ctx, D_MODEL), F32),
                   jax.ShapeDtypeStruct((n - n_ctx, D_MODEL), F32)),
        compiler_params=_params(("arbitrary", "arbitrary")), name="experts",
    )(order, pos0, pos1, off, count, hp.reshape(n, ROW_TILE_SUBLANES, LANES), wg, wu, wd, x1,
      mod3, wcol)


def _rope_tables(seq_len):
    half = HEAD_DIM // 2
    freqs = ROPE_BASE ** (-np.arange(0, half, 2, dtype=np.float64) / half)
    pos = np.arange(seq_len)
    row, col = pos // GRID_W, pos % GRID_W
    d = np.arange(HEAD_DIM)
    position = np.where(d[None, :] < half, row[:, None], col[:, None]).astype(np.float64)
    ang = (position.astype(np.float32) * freqs.astype(np.float32)[d % (half // 2)][None, :]).astype(np.float32)
    cos, sin = np.cos(ang), np.sin(ang)
    first = (d % half) < half // 2
    sa = np.where(first[None, :], -sin, 0.0)
    sb = np.where(first[None, :], 0.0, sin)
    ident = lambda v: np.full((TOKEN_TILE, HEAD_DIM), v, np.float32)
    stack = lambda ctx, lat: jnp.asarray(
        np.tile(np.concatenate([ctx, lat.astype(np.float32)], axis=0), (1, 2)), F32)
    return stack(ident(1.0), cos), stack(ident(0.0), sa), stack(ident(0.0), sb)


def kernel(x_prompt, x_sample, c, cache_k, cache_v, state_C, state_n, state_m, c_ctx, w_mod, b_mod,
           g_mix, g_ffn, w_in, b_igate, b_fgate, g_q, g_k, g_mh, w_out, w_router, b_router,
           w_e_gate, w_e_up, w_e_down):
    n_ctx_seq, ctx_len, _ = x_prompt.shape
    n_lat_seq, lat_len, _ = x_sample.shape
    n_layers = w_mod.shape[0]
    n_ctx = n_ctx_seq * ctx_len
    assert ctx_len == SEQ_BLOCK and lat_len % TOKEN_TILE == 0
    assert n_ctx % MOE_TILE == 0 and (n_lat_seq * lat_len) % MOE_TILE == 0
    assert n_lat_seq < 16 and n_ctx % lat_len == 0
    n_ctx_tiles = n_ctx // TOKEN_TILE
    tiles_per_lat_seq = lat_len // TOKEN_TILE
    ctx_row = n_lat_seq

    x = (x_prompt.reshape(n_ctx, D_MODEL), x_sample.reshape(-1, D_MODEL))
    cond = jnp.zeros((16, D_MODEL), F32).at[:n_lat_seq].set(c).at[ctx_row].set(c_ctx)
    mod = _modulation(cond, w_mod, b_mod)
    rope = _rope_tables(lat_len)

    wg_b, wu_b, wd_b = w_e_gate.astype(BF16), w_e_up.astype(BF16), w_e_down.astype(BF16)
    gate_perm = np.array([(q % 2) * N_ML_HEADS + hd + 2 * N_ML_HEADS * (q // 2)
                          for hd in range(N_ML_HEADS) for q in range(4)])

    ks, vs, cs, ns, ms = [], [], [], [], []
    for l in range(n_layers):
        mod3 = mod[l].reshape(16, 1, -1)
        precise_ctx = l < n_layers - 1

        def weight_pair(w):
            hi = w.astype(BF16)
            return hi, ((w - hi.astype(F32)).astype(BF16) if precise_ctx else None)

        w_main = weight_pair(w_in[l, :, :MAIN_WIDTH])
        w_gate = w_in[l, :, MAIN_WIDTH:][:, gate_perm]
        b_gate = jnp.concatenate([b_igate[l].reshape(-1), b_fgate[l].reshape(-1)])[gate_perm]
        q, k, v, mq, mk, mv, og, gcol, grow = _inproj(
            *x, mod3, n_ctx_tiles, tiles_per_lat_seq, ctx_row, g_mix[l], w_main, w_gate, b_gate,
            g_q[l], g_k[l], rope)

        att_ctx = _attention(q[0], k, v, 0, n_ctx_seq, ctx_len)
        past = cache_k.shape[2]
        ck = cache_k[:, l].reshape(n_lat_seq * past, LANES)
        cv = cache_v[:, l].reshape(n_lat_seq * past, LANES)
        att_lat = _attention(q[1], k, v, n_ctx, n_lat_seq, lat_len, cache=(ck, cv))

        ml_ctx, c_fin, nm_fin = _mlstm(mq[0], mk[0], mv[0], og, g_mh[l], gcol, grow, 0,
                                       n_ctx_seq, ctx_len, emit_state=True)
        n0 = state_n[:, l].transpose(0, 2, 1, 3)
        m0 = jnp.broadcast_to(state_m[:, l].transpose(0, 2, 1)[..., None], n0.shape)
        nm0 = jnp.concatenate([n0, m0, jnp.zeros_like(n0), jnp.zeros_like(n0)], axis=2)
        (ml_lat,) = _mlstm(mq[1], mk[1], mv[1], og, g_mh[l], gcol, grow, n_ctx, n_lat_seq,
                           lat_len, init=(state_C, nm0, l))

        x1, hp, logits = _outproj(
            (att_ctx, att_lat), (ml_ctx, ml_lat), x, mod3, n_ctx_tiles, tiles_per_lat_seq, ctx_row,
            weight_pair(w_out[l]), g_ffn[l], w_router)
        x = _moe(hp, logits, b_router, x1, mod3, l, n_ctx // MOE_OUT_TILE,
                 lat_len // MOE_OUT_TILE, ctx_row, wg_b, wu_b, wd_b)

        ks.append(k[:n_ctx].reshape(n_ctx_seq, ctx_len, N_KV_HEADS, HEAD_DIM))
        vs.append(v[:n_ctx].reshape(n_ctx_seq, ctx_len, N_KV_HEADS, HEAD_DIM))
        cs.append(c_fin)
        ns.append(nm_fin[:, :, 0:2, :].transpose(0, 2, 1, 3))
        ms.append(nm_fin[:, :, 2:4, 0].transpose(0, 2, 1))

    y_prompt = x[0].reshape(x_prompt.shape)
    y_sample = x[1].reshape(x_sample.shape)
    return (y_prompt, y_sample, jnp.stack(ks, axis=1), jnp.stack(vs, axis=1),
            jnp.stack(cs, axis=1), jnp.stack(ns, axis=1), jnp.stack(ms, axis=1))
```

```python
import functools

import numpy as np
import jax
import jax.numpy as jnp
from jax import lax
from jax.experimental import pallas as pl
from jax.experimental.pallas import tpu as pltpu

F32 = jnp.float32
BF16 = jnp.bfloat16

D_MODEL = 1024
HEAD_DIM = 64
ATT_WIDTH = 512
N_KV_HEADS = 2
ML_WIDTH = 512
N_ML_HEADS = 4
ML_HEAD_DIM = 128
GRID_W = 64
ROPE_BASE = 10000.0
N_EXPERTS = 16
N_GROUPS = 4
GROUP_SIZE = 4
D_EXPERT = 512
EPS = 1e-6
MAIN_WIDTH = 2816
LANES = 128
TOKEN_TILE = 512
SEQ_BLOCK = 256
ATT_Q_BLOCK = 1024
MOE_TILE = 2048
MOE_CHUNK = 320
ROW_TILE_SUBLANES = D_MODEL // LANES
MLSTM_HEADS_PER_STEP = 4
MOE_OUT_TILE = 256
ROUTE_SLOT_BLOCKS = 9
MOE_SLOTS = -(-(2 * MOE_TILE + 8 * N_EXPERTS + MOE_CHUNK) // (512 * ROUTE_SLOT_BLOCKS)) * 512 * ROUTE_SLOT_BLOCKS
VMEM_LIMIT = 56 * 1024 * 1024
NEG_INF = float("-inf")
NT_DIMS = (((1,), (1,)), ((), ()))
TN_DIMS = (((0,), (0,)), ((), ()))


def _params(semantics):
    return pltpu.CompilerParams(dimension_semantics=semantics, vmem_limit_bytes=VMEM_LIMIT)


def _log_sigmoid(z):
    return jnp.minimum(z, 0.0) - jnp.log1p(jnp.exp(-jnp.abs(z)))


def _split3(x):
    h1 = x.astype(BF16)
    r1 = x - h1.astype(F32)
    h2 = r1.astype(BF16)
    h3 = (r1 - h2.astype(F32)).astype(BF16)
    return h1, h2, h3


def _mod_kernel(cond_ref, w_ref, b_ref, o_ref):
    c = cond_ref[...]
    s = c * jax.nn.sigmoid(c)
    o_ref[...] = _dot_x3(_hi_lo(s), _hi_lo(w_ref[...])) + b_ref[...]


def _modulation(cond, w_mod, b_mod):
    n_layers = w_mod.shape[0]
    n_chunks = w_mod.shape[2] // D_MODEL
    return pl.pallas_call(
        _mod_kernel,
        grid=(n_layers, n_chunks),
        in_specs=[
            pl.BlockSpec((16, D_MODEL), lambda l, j: (0, 0)),
            pl.BlockSpec((None, D_MODEL, D_MODEL), lambda l, j: (l, 0, j)),
            pl.BlockSpec((None, 1, D_MODEL), lambda l, j: (l, 0, j)),
        ],
        out_specs=pl.BlockSpec((None, 16, D_MODEL), lambda l, j: (l, 0, j)),
        out_shape=jax.ShapeDtypeStruct((n_layers, 16, w_mod.shape[2]), F32),
        compiler_params=_params(("parallel", "parallel")),
        name="modulation",
    )(cond, w_mod, b_mod.reshape(n_layers, 1, -1))


def _two_part(i, n_first, a_ref, b_ref):
    return jnp.where(i < n_first, a_ref[...], b_ref[...])


def _hi_lo(x):
    hi = x.astype(BF16)
    return hi, (x - hi.astype(F32)).astype(BF16)


def _dot_x3(a, b, dims=None):
    (ah, al), (bh, bl) = a, b
    if dims is None:
        d = lambda x, y: jnp.dot(x, y, preferred_element_type=F32)
    else:
        d = lambda x, y: lax.dot_general(x, y, dims, preferred_element_type=F32)
    return d(ah, bh) + (d(al, bh) + d(ah, bl))


def _two_part_out_specs(width, n_first, tile=TOKEN_TILE):
    return _two_part_specs(width, n_first, tile)


def _two_part_specs(width, n_first, tile=TOKEN_TILE):
    return [pl.BlockSpec((tile, width), lambda i: (jnp.minimum(i, n_first - 1), 0)),
            pl.BlockSpec((tile, width), lambda i: (jnp.maximum(i - n_first, 0), 0))]


def _inproj_kernel(*refs, n_ctx_tiles, precise_ctx):
    it = iter(refs)
    xa_ref, xb_ref, sh_ref, sc_ref, g_ref, w_ref = [next(it) for _ in range(6)]
    wl_ref = next(it) if precise_ctx else None
    (wgt_ref, wgtl_ref, brow_ref, gq_ref, gk_ref, eye_ref, cos_ref, sa_ref, sb_ref, gsum_ref,
     qa_ref, qb_ref, k_ref, v_ref, mqa_ref, mqb_ref, mka_ref, mkb_ref, mva_ref, mvb_ref,
     og_ref, gc_ref, gr_ref) = it
    tile = pl.program_id(0)
    x = _two_part(tile, n_ctx_tiles, xa_ref, xb_ref)
    ms = jnp.mean(x * x, axis=-1, keepdims=True)
    h = x * lax.rsqrt(ms + EPS) * g_ref[...]
    h = h * (1.0 + sc_ref[...]) + sh_ref[...]
    hb = h.astype(BF16)

    zr = _dot_x3((wgt_ref[...], wgtl_ref[...]), _hi_lo(h), NT_DIMS) + brow_ref[...]
    sub = lax.broadcasted_iota(jnp.int32, zr.shape, 0)
    gr = jnp.where(sub % 4 < 2, zr, _log_sigmoid(zr))
    eye = eye_ref[...]
    gc = functools.reduce(jnp.add, [lax.dot_general(eye, p, NT_DIMS, preferred_element_type=F32)
                                    for p in _split3(gr)])
    gr_ref[...] = jnp.zeros_like(gr_ref)
    for hd in range(N_ML_HEADS):
        gc_ref[hd] = gc[:, 4 * hd:4 * hd + 4]
        gr_ref[hd, 0:4, :] = gr[4 * hd:4 * hd + 4, :]

    cos = cos_ref[...]
    sa = sa_ref[...]
    sb = sb_ref[...]
    gsum = gsum_ref[...]

    def project(precise, q_ref, mq_ref, mk_ref, mv_ref):
        if precise:
            z = _dot_x3(_hi_lo(h), (w_ref[...], wl_ref[...]))
        else:
            z = jnp.dot(hb, w_ref[...], preferred_element_type=F32)
        proj = lambda c0, width: z[:, c0:c0 + width]
        act = F32 if precise else BF16

        n_qk = ATT_WIDTH // LANES + 1
        t_rows = hb.shape[0]
        zqk = proj(0, n_qk * LANES)
        zs = [zqk[:, c * LANES:(c + 1) * LANES] for c in range(n_qk)]
        sq = jnp.concatenate([z * z for z in zs], axis=0)
        ss = jnp.dot(jnp.concatenate(_hi_lo(sq), axis=0), gsum, preferred_element_type=F32)
        ss = ss[:n_qk * t_rows] + ss[n_qk * t_rows:]

        def headnorm_rope(c, gain):
            zn = zs[c] * lax.rsqrt(ss[c * t_rows:(c + 1) * t_rows] * (1.0 / HEAD_DIM) + EPS) * gain
            return zn * cos + pltpu.roll(zn, LANES - 16, 1) * sa + pltpu.roll(zn, 16, 1) * sb

        for c in range(n_qk - 1):
            q_ref[:, c * LANES:(c + 1) * LANES] = (headnorm_rope(c, gq_ref[...]) * 0.125).astype(act)
        k_ref[...] = headnorm_rope(n_qk - 1, gk_ref[...])
        v_ref[...] = proj(640, LANES)
        mq_ref[...] = proj(768, ML_WIDTH).astype(act)
        mk_ref[...] = (proj(1280, ML_WIDTH) * (ML_HEAD_DIM ** -0.5)).astype(act)
        mv_ref[...] = proj(1792, ML_WIDTH).astype(act)
        og_ref[...] = jax.nn.sigmoid(proj(2304, ML_WIDTH))

    @pl.when(tile < n_ctx_tiles)
    def _ctx():
        project(precise_ctx, qa_ref, mqa_ref, mka_ref, mva_ref)

    @pl.when(tile >= n_ctx_tiles)
    def _lat():
        project(False, qb_ref, mqb_ref, mkb_ref, mvb_ref)


def _inproj(xa, xb, mod3, n_ctx_tiles, tiles_per_lat_seq, ctx_row, g_mix, w_main, w_gate, b_gate,
            g_q, g_k, rope):
    w_hi, w_lo = w_main
    precise_ctx = w_lo is not None
    n_ctx, n_lat = xa.shape[0], xb.shape[0]
    n = n_ctx + n_lat
    n_tiles = n // TOKEN_TILE

    def mod_row(i):
        return jnp.where(i < n_ctx_tiles, ctx_row, (i - n_ctx_tiles) // tiles_per_lat_seq)

    def rope_blk(i):
        return jnp.where(i < n_ctx_tiles, 0, 1 + (i - n_ctx_tiles) % tiles_per_lat_seq)

    cos_t, sa_t, sb_t = rope
    lane = np.arange(LANES)
    gsum = jnp.asarray((lane[:, None] // HEAD_DIM) == (lane[None, :] // HEAD_DIM), BF16)
    tok = lambda w: pl.BlockSpec((TOKEN_TILE, w), lambda i: (i, 0))
    full = lambda a: pl.BlockSpec(a.shape, lambda i: (0,) * a.ndim)
    modspec = lambda j: pl.BlockSpec((None, 1, D_MODEL), lambda i: (mod_row(i), 0, j))
    ropespec = pl.BlockSpec((TOKEN_TILE, LANES), lambda i: (rope_blk(i), 0))
    consts = (g_mix.reshape(1, -1), w_hi) + ((w_lo,) if precise_ctx else ()) + (
        *_hi_lo(w_gate.T), b_gate.reshape(-1, 1),
        jnp.tile(g_q, 2).reshape(1, -1), jnp.tile(g_k, 2).reshape(1, -1),
        jnp.asarray(np.eye(TOKEN_TILE), BF16))
    args = (xa, xb, mod3, mod3) + consts + (cos_t, sa_t, sb_t, gsum)
    in_specs = _two_part_specs(D_MODEL, n_ctx_tiles) + [modspec(0), modspec(1)] \
        + [full(a) for a in consts] + [ropespec, ropespec, ropespec, full(gsum)]
    ctx_act = F32 if precise_ctx else BF16
    pair_shape = lambda w: [jax.ShapeDtypeStruct((n_ctx, w), ctx_act),
                            jax.ShapeDtypeStruct((n_lat, w), BF16)]
    pair_spec = lambda w: _two_part_out_specs(w, n_ctx_tiles)
    out_shape = (
        pair_shape(ATT_WIDTH)
        + [jax.ShapeDtypeStruct((n, LANES), F32),
           jax.ShapeDtypeStruct((n, LANES), F32)]
        + pair_shape(ML_WIDTH) + pair_shape(ML_WIDTH) + pair_shape(ML_WIDTH)
        + [jax.ShapeDtypeStruct((n, ML_WIDTH), F32),
           jax.ShapeDtypeStruct((N_ML_HEADS, n, 4), F32),
           jax.ShapeDtypeStruct((N_ML_HEADS, 8, n), F32)])
    out_specs = (pair_spec(ATT_WIDTH) + [tok(LANES), tok(LANES)]
                 + pair_spec(ML_WIDTH) + pair_spec(ML_WIDTH) + pair_spec(ML_WIDTH)
                 + [tok(ML_WIDTH),
                    pl.BlockSpec((N_ML_HEADS, TOKEN_TILE, 4), lambda i: (0, i, 0)),
                    pl.BlockSpec((N_ML_HEADS, 8, TOKEN_TILE), lambda i: (0, 0, i))])
    outs = pl.pallas_call(
        functools.partial(_inproj_kernel, n_ctx_tiles=n_ctx_tiles, precise_ctx=precise_ctx),
        grid=(n_tiles,), in_specs=in_specs, out_specs=tuple(out_specs),
        out_shape=tuple(out_shape), compiler_params=_params(("arbitrary",)), name="inproj",
    )(*args)
    qa, qb, k, v, mqa, mqb, mka, mkb, mva, mvb, og, gcol, grow = outs
    return (qa, qb), k, v, (mqa, mqb), (mka, mkb), (mva, mvb), og, gcol, grow


def _attn_kernel(*refs, n_kv, precise, cached):
    q_ref = refs[0]
    kv_refs = refs[1:1 + 2 * n_kv]
    o_ref = refs[1 + 2 * n_kv]
    dup_scr = refs[2 + 2 * n_kv:]
    tq = q_ref.shape[0]
    lo_q = lax.broadcasted_iota(jnp.int32, (tq, LANES), 1) < HEAD_DIM
    operand = _hi_lo if precise else (lambda a: a.astype(BF16))
    if precise:
        qk = lambda a, b: _dot_x3(a, b, NT_DIMS)
        pv = _dot_x3
    else:
        qk = lambda a, b: lax.dot_general(a, b, NT_DIMS, preferred_element_type=F32)
        pv = lambda a, b: jnp.dot(a, b, preferred_element_type=F32)

    def dup_half(ref, g):
        a = ref[...]
        r = pltpu.roll(a, HEAD_DIM, 1)
        lo = lax.broadcasted_iota(jnp.int32, a.shape, 1) < HEAD_DIM
        return operand(jnp.where(lo, a, r) if g == 0 else jnp.where(lo, r, a))

    if cached:
        @pl.when(pl.program_id(1) == 0)
        def _fill():
            for g in range(N_KV_HEADS):
                for j in range(2 * n_kv):
                    dup_scr[g * 2 * n_kv + j][...] = dup_half(kv_refs[j], g)

        dup = lambda j, g: dup_scr[g * 2 * n_kv + j][...]
    else:
        dup = lambda j, g: dup_half(kv_refs[j], g)

    for g in range(N_KV_HEADS):
        ks = [dup(2 * p, g) for p in range(n_kv)]
        vs = [dup(2 * p + 1, g) for p in range(n_kv)]
        for hb in range(2):
            c0 = (2 * g + hb) * LANES
            qb = q_ref[:, c0:c0 + LANES]
            outs = []
            for half in range(2):
                keep = lo_q if half == 0 else jnp.logical_not(lo_q)
                qm = operand(jnp.where(keep, qb, jnp.zeros_like(qb)))
                ss = [qk(qm, kd) for kd in ks]
                m = functools.reduce(jnp.maximum, [jnp.max(s, axis=1, keepdims=True) for s in ss])
                ps = [jnp.exp(s - m) for s in ss]
                den = functools.reduce(jnp.add, [jnp.sum(p, axis=1, keepdims=True) for p in ps])
                o = functools.reduce(jnp.add, [pv(operand(p), vd) for p, vd in zip(ps, vs)])
                outs.append(o / den)
            o_ref[:, c0:c0 + LANES] = jnp.where(lo_q, outs[0], outs[1]).astype(o_ref.dtype)


def _attention(q, k, v, kv_row0, n_seq, seq_len, cache=None):
    precise = q.dtype == F32
    q_block = min(ATT_Q_BLOCK, seq_len)
    nq = seq_len // q_block
    sb0 = kv_row0 // seq_len
    in_specs = [
        pl.BlockSpec((q_block, ATT_WIDTH), lambda b, i: (b * nq + i, 0)),
        pl.BlockSpec((seq_len, LANES), lambda b, i: (sb0 + b, 0)),
        pl.BlockSpec((seq_len, LANES), lambda b, i: (sb0 + b, 0)),
    ]
    args = [q, k, v]
    n_kv = 1
    if cache is not None:
        ck, cv = cache
        past = ck.shape[0] // n_seq
        in_specs += [pl.BlockSpec((past, LANES), lambda b, i: (b, 0))] * 2
        args += [ck, cv]
        n_kv = 2
    cached = nq > 1 and not precise
    scratch = [pltpu.VMEM((a.shape[0] // n_seq if j >= 2 else seq_len, LANES), BF16)
               for _ in range(N_KV_HEADS) for j, a in enumerate(args[1:])] if cached else []
    return pl.pallas_call(
        functools.partial(_attn_kernel, n_kv=n_kv, precise=precise, cached=cached),
        grid=(n_seq, nq), in_specs=in_specs,
        out_specs=pl.BlockSpec((q_block, ATT_WIDTH), lambda b, i: (b * nq + i, 0)),
        out_shape=jax.ShapeDtypeStruct((n_seq * seq_len, ATT_WIDTH), q.dtype),
        scratch_shapes=scratch,
        compiler_params=_params(("parallel", "arbitrary" if cached else "parallel")),
        name="attention",
    )(*args)


def _mlstm_kernel(*refs, heads, **static):
    it = iter(refs)
    q_ref, k_ref, v_ref, og_ref, gmh_ref, gcol_ref, grow_ref, u_ref, l_ref = [next(it) for _ in range(9)]
    init_refs = (next(it), next(it)) if static["has_init"] else ()
    ml_ref = next(it)
    state_refs = (next(it), next(it)) if static["emit_state"] else ()
    for hd in range(heads):
        lanes = pl.ds(hd * ML_HEAD_DIM, ML_HEAD_DIM)
        head_refs = [q_ref.at[:, lanes], k_ref.at[:, lanes], v_ref.at[:, lanes], og_ref.at[:, lanes],
                     gmh_ref.at[:, lanes], gcol_ref.at[hd], grow_ref.at[hd], u_ref, l_ref]
        if init_refs:
            head_refs += [init_refs[0].at[:, hd], init_refs[1].at[hd]]
        head_refs.append(ml_ref.at[:, lanes])
        if state_refs:
            head_refs += [state_refs[0].at[:, hd], state_refs[1].at[hd]]
        _mlstm_head(*head_refs, **static)


def _mlstm_head(*refs, seq_len, has_init, emit_state, precise):
    it = iter(refs)
    q_ref, k_ref, v_ref, og_ref, gmh_ref, gcol_ref, grow_ref, u_ref, l_ref = [next(it) for _ in range(9)]
    if has_init:
        c0_ref, nm0_ref = next(it), next(it)
    ml_ref = next(it)
    if emit_state:
        cf_ref, nmf_ref = next(it), next(it)

    operand = _hi_lo if precise else (lambda a: a.astype(BF16))
    if precise:
        qk = lambda a, b: _dot_x3(a, b, NT_DIMS)
        pv = _dot_x3
    else:
        qk = lambda a, b: lax.dot_general(a, b, NT_DIMS, preferred_element_type=F32)
        pv = lambda a, b: jnp.dot(a, b, preferred_element_type=F32)

    bq = SEQ_BLOCK
    nb = seq_len // bq
    blk = lambda j: slice(j * bq, (j + 1) * bq)
    upper_incl = u_ref[...]
    lower_incl = l_ref[...]

    def tri_dot(x, tri):
        return functools.reduce(jnp.add, [jnp.dot(p, tri, preferred_element_type=F32)
                                          for p in _split3(x)])

    ig_row = [[None] * nb for _ in range(2)]
    lf_row = [[None] * nb for _ in range(2)]
    within = [[None] * nb for _ in range(2)]
    bsum = [[None] * nb for _ in range(2)]
    for j in range(nb):
        g8 = grow_ref[:, blk(j)]
        cum_f = tri_dot(g8, upper_incl)
        cum_b = tri_dot(g8, lower_incl)
        for d in range(2):
            ig_row[d][j] = g8[d:d + 1, :]
            lf_row[d][j] = g8[2 + d:3 + d, :]
            within[d][j] = (cum_f if d == 0 else cum_b)[2 + d:3 + d, :]
            bsum[d][j] = jnp.sum(lf_row[d][j], axis=1, keepdims=True)
    zero11 = jnp.zeros((1, 1), F32)
    r_i = lax.broadcasted_iota(jnp.int32, (bq, bq), 0)
    c_i = lax.broadcasted_iota(jnp.int32, (bq, bq), 1)
    causal = [c_i <= r_i, c_i >= r_i]
    gmh = gmh_ref[...]

    q_blocks = [q_ref[blk(i), :] for i in range(nb)]
    scores = [qk(operand(q_blocks[i]), operand(k_ref[blk(i), :])) for i in range(nb)]

    h_dir = [[None] * nb for _ in range(2)]
    final = [None, None]
    for d in range(2):
        if has_init:
            state = (c0_ref[d], nm0_ref[d:d + 1, :])
            m = nm0_ref[2 + d:3 + d, 0:1]
        else:
            state, m = None, zero11
        scan = range(nb) if d == 0 else range(nb - 1, -1, -1)
        for step, i in enumerate(scan):
            q_i = q_blocks[i]
            a_loc = ig_row[d][i] - within[d][i]
            m_col = jnp.maximum(m, jnp.max(jnp.where(causal[d], a_loc, NEG_INF),
                                           axis=1, keepdims=True))
            if nb > 1:
                widen = lambda col: jnp.broadcast_to(col, (bq, LANES))
                tri = lower_incl if d == 0 else upper_incl
                b_rep = functools.reduce(jnp.add, [
                    lax.dot_general(tri, jnp.broadcast_to(piece, (LANES, bq)), NT_DIMS,
                                    preferred_element_type=F32)
                    for piece in _hi_lo(lf_row[d][i])])
                m_rep = widen(m_col)
                m_wide = jnp.concatenate([m_rep] * (bq // LANES), axis=1)
            else:
                widen = lambda col: col
                b_rep = jnp.sum(jnp.where(causal[d], lf_row[d][i], 0.0), axis=1, keepdims=True)
                m_rep = m_wide = m_col
            p = jnp.exp(jnp.where(causal[d], a_loc - m_wide, NEG_INF)) * scores[i]
            den = widen(jnp.sum(p, axis=1, keepdims=True))
            num = pv(operand(p), operand(v_ref[blk(i), :]))
            if state is not None:
                c_prev, n_prev = state
                w_inter = jnp.exp(m - m_rep)
                q_b = q_i.astype(BF16)
                qc = jnp.dot(q_b, c_prev.astype(BF16), preferred_element_type=F32)
                n_rows = jnp.broadcast_to(n_prev, (LANES, ML_HEAD_DIM)).astype(BF16)
                qn = lax.dot_general(q_b, n_rows, NT_DIMS, preferred_element_type=F32)
                num = num + w_inter * qc
                den = den + w_inter * qn
            nrm = jnp.maximum(jnp.abs(den), jnp.exp(-(b_rep + m_rep)))
            h_dir[d][i] = num / nrm
            if step == nb - 1 and not emit_state:
                break
            m_last = jnp.maximum(m, jnp.max(a_loc, axis=1, keepdims=True))
            a_col = widen(gcol_ref[blk(i), d:d + 1]) - b_rep
            kw = k_ref[blk(i), :].astype(F32) * jnp.exp(a_col - m_last)
            c_new = lax.dot_general(kw.astype(BF16), v_ref[blk(i), :].astype(BF16), TN_DIMS,
                                    preferred_element_type=F32)
            n_new = jnp.sum(kw, axis=0, keepdims=True)
            if state is not None:
                decay = jnp.exp(m - m_last)
                c_new = c_new + decay * state[0]
                n_new = n_new + decay * state[1]
            state = (c_new, n_new)
            m = bsum[d][i] + m_last
        final[d] = (state, m)

    for i in range(nb):
        h = h_dir[0][i] + h_dir[1][i]
        hn = h * lax.rsqrt(jnp.mean(h * h, axis=-1, keepdims=True) + EPS) * gmh
        ml_ref[blk(i), :] = (og_ref[blk(i), :] * hn).astype(ml_ref.dtype)

    if emit_state:
        nmf_ref[...] = jnp.zeros_like(nmf_ref)
        for d in range(2):
            (c_fin, n_fin), m_fin = final[d]
            cf_ref[d] = c_fin
            nmf_ref[d:d + 1, :] = n_fin
            nmf_ref[2 + d:3 + d, :] = jnp.broadcast_to(m_fin, (1, ML_HEAD_DIM))


def _mlstm(mq, mk, mv, og, g_mh, gcol, grow, row0, n_seq, seq_len, init=None, emit_state=False):
    precise = mq.dtype == F32
    assert not (precise and init is not None)
    sb0 = row0 // seq_len
    tri = np.arange(SEQ_BLOCK)
    upper_incl = jnp.asarray(tri[:, None] <= tri[None, :], BF16)
    lower_incl = jnp.asarray(tri[:, None] >= tri[None, :], BF16)
    heads = MLSTM_HEADS_PER_STEP
    width = heads * ML_HEAD_DIM
    ownblk = lambda: pl.BlockSpec((seq_len, width), lambda b, h: (b, h))
    headblk = lambda: pl.BlockSpec((seq_len, width), lambda b, h: (sb0 + b, h))
    const = lambda a: pl.BlockSpec(a.shape, lambda b, h: (0,) * a.ndim)
    in_specs = [ownblk(), ownblk(), ownblk(), headblk(),
                pl.BlockSpec((1, width), lambda b, h: (0, h)),
                pl.BlockSpec((heads, seq_len, 4), lambda b, h: (h, sb0 + b, 0)),
                pl.BlockSpec((heads, 8, seq_len), lambda b, h: (h, 0, sb0 + b)),
                const(upper_incl), const(lower_incl)]
    args = [mq, mk, mv, og, g_mh.reshape(1, -1), gcol, grow, upper_incl, lower_incl]
    if init is not None:
        c0, nm0, layer = init
        in_specs += [
            pl.BlockSpec((None, None, 2, heads, ML_HEAD_DIM, ML_HEAD_DIM),
                         lambda b, h: (b, layer, 0, h, 0, 0)),
            pl.BlockSpec((None, heads, 8, ML_HEAD_DIM), lambda b, h: (b, h, 0, 0))]
        args += [c0, nm0]
    out_shape = [jax.ShapeDtypeStruct((n_seq * seq_len, ML_WIDTH), mq.dtype)]
    out_specs = [pl.BlockSpec((seq_len, width), lambda b, h: (b, h))]
    if emit_state:
        out_shape += [jax.ShapeDtypeStruct((n_seq, 2, N_ML_HEADS, ML_HEAD_DIM, ML_HEAD_DIM), F32),
                      jax.ShapeDtypeStruct((n_seq, N_ML_HEADS, 8, ML_HEAD_DIM), F32)]
        out_specs += [pl.BlockSpec((None, 2, heads, ML_HEAD_DIM, ML_HEAD_DIM),
                                   lambda b, h: (b, 0, h, 0, 0)),
                      pl.BlockSpec((None, heads, 8, ML_HEAD_DIM), lambda b, h: (b, h, 0, 0))]
    return pl.pallas_call(
        functools.partial(_mlstm_kernel, heads=heads, seq_len=seq_len, has_init=init is not None,
                          emit_state=emit_state, precise=precise),
        grid=(n_seq, N_ML_HEADS // heads), in_specs=in_specs, out_specs=tuple(out_specs),
        out_shape=tuple(out_shape),
        compiler_params=_params(("parallel", "parallel")), name="mlstm",
    )(*args)


def _outproj_kernel(*refs, n_ctx_tiles, precise_ctx):
    it = iter(refs)
    atta_ref, attb_ref, mla_ref, mlb_ref, xa_ref, xb_ref, w_ref = [next(it) for _ in range(7)]
    wl_ref = next(it) if precise_ctx else None
    gt_ref, sh_ref, sc_ref, g_ref, wrt_ref, wrtl_ref, x1_ref, hp_ref, logits_ref, y_scr = it
    i = pl.program_id(0)

    def mix(att, ml):
        return (jnp.dot(att, w_ref[:ATT_WIDTH, :], preferred_element_type=F32)
                + jnp.dot(ml, w_ref[ATT_WIDTH:, :], preferred_element_type=F32))

    @pl.when(i < n_ctx_tiles)
    def _ctx():
        if precise_ctx:
            y_scr[...] = (
                _dot_x3(_hi_lo(atta_ref[...]), (w_ref[:ATT_WIDTH, :], wl_ref[:ATT_WIDTH, :]))
                + _dot_x3(_hi_lo(mla_ref[...]), (w_ref[ATT_WIDTH:, :], wl_ref[ATT_WIDTH:, :])))
        else:
            y_scr[...] = mix(atta_ref[...], mla_ref[...])

    @pl.when(i >= n_ctx_tiles)
    def _lat():
        y_scr[...] = mix(attb_ref[...], mlb_ref[...])

    x1 = _two_part(i, n_ctx_tiles, xa_ref, xb_ref) + gt_ref[...] * y_scr[...]
    x1_ref[...] = x1
    ms = jnp.mean(x1 * x1, axis=-1, keepdims=True)
    h2 = x1 * lax.rsqrt(ms + EPS) * g_ref[...]
    h2 = h2 * (1.0 + sc_ref[...]) + sh_ref[...]
    for k in range(ROW_TILE_SUBLANES):
        hp_ref[pl.ds(k, TOKEN_TILE, stride=ROW_TILE_SUBLANES), :] = h2[:, k * LANES:(k + 1) * LANES]

    logits_ref[...] = _dot_x3((wrt_ref[...], wrtl_ref[...]), _hi_lo(h2), NT_DIMS)


def _select_experts(logits, b_col):
    ex = jnp.exp(logits - jnp.max(logits, axis=0, keepdims=True))
    scores = ex / jnp.sum(ex, axis=0, keepdims=True)
    sel = scores + b_col
    row = lambda a, e: a[e:e + 1, :]
    grp_score = []
    for g in range(N_GROUPS):
        xs = [row(sel, g * GROUP_SIZE + j) for j in range(GROUP_SIZE)]
        pairs = [xs[a] + xs[b] for a in range(GROUP_SIZE) for b in range(a + 1, GROUP_SIZE)]
        grp_score.append(functools.reduce(jnp.maximum, pairs))
    best = grp_score[0]
    grp = jnp.zeros_like(best, dtype=jnp.int32)
    for g in range(1, N_GROUPS):
        better = grp_score[g] > best
        grp = jnp.where(better, g, grp)
        best = jnp.where(better, grp_score[g], best)
    pick = lambda a, j: functools.reduce(
        lambda acc, g: jnp.where(grp == g, row(a, g * GROUP_SIZE + j), acc),
        range(1, N_GROUPS), row(a, j))
    xs = [pick(sel, j) for j in range(GROUP_SIZE)]
    ws = [pick(scores, j) for j in range(GROUP_SIZE)]

    def argmax4(vals):
        bv, bi = vals[0], jnp.zeros_like(grp)
        for j in range(1, GROUP_SIZE):
            better = vals[j] > bv
            bi = jnp.where(better, j, bi)
            bv = jnp.where(better, vals[j], bv)
        return bi

    i1 = argmax4(xs)
    i2 = argmax4([jnp.where(i1 == j, NEG_INF, xs[j]) for j in range(GROUP_SIZE)])
    take = lambda vals, idx: functools.reduce(
        lambda acc, j: jnp.where(idx == j, vals[j], acc), range(1, GROUP_SIZE), vals[0])
    w1, w2 = take(ws, i1), take(ws, i2)
    wsum = w1 + w2
    w1, w2 = w1 / wsum, w2 / wsum
    return grp * GROUP_SIZE + i1, grp * GROUP_SIZE + i2, w1, w2


def _outproj(att, ml, x, mod3, n_ctx_tiles, tiles_per_lat_seq, ctx_row, w_out, g_ffn, w_router):
    w_hi, w_lo = w_out
    precise_ctx = w_lo is not None
    n = x[0].shape[0] + x[1].shape[0]

    def mod_row(i):
        return jnp.where(i < n_ctx_tiles, ctx_row, (i - n_ctx_tiles) // tiles_per_lat_seq)

    tok = lambda w: pl.BlockSpec((TOKEN_TILE, w), lambda i: (i, 0))
    full = lambda a: pl.BlockSpec(a.shape, lambda i: (0,) * a.ndim)
    modspec = lambda j: pl.BlockSpec((None, 1, D_MODEL), lambda i: (mod_row(i), 0, j))
    weights = (w_hi, w_lo) if precise_ctx else (w_hi,)
    consts = (g_ffn.reshape(1, -1), *_hi_lo(w_router.T))
    args = (*att, *ml, *x, *weights, mod3, mod3, mod3) + consts
    in_specs = (_two_part_specs(ATT_WIDTH, n_ctx_tiles) + _two_part_specs(ML_WIDTH, n_ctx_tiles)
                + _two_part_specs(D_MODEL, n_ctx_tiles) + [full(w) for w in weights]
                + [modspec(2), modspec(3), modspec(4)] + [full(a) for a in consts])
    return pl.pallas_call(
        functools.partial(_outproj_kernel, n_ctx_tiles=n_ctx_tiles, precise_ctx=precise_ctx),
        grid=(n // TOKEN_TILE,), in_specs=in_specs,
        scratch_shapes=[pltpu.VMEM((TOKEN_TILE, D_MODEL), F32)],
        out_specs=(tok(D_MODEL),
                   pl.BlockSpec((TOKEN_TILE * ROW_TILE_SUBLANES, LANES), lambda i: (i, 0)),
                   pl.BlockSpec((N_EXPERTS, TOKEN_TILE), lambda i: (0, i))),
        out_shape=(jax.ShapeDtypeStruct((n, D_MODEL), F32),
                   jax.ShapeDtypeStruct((n * ROW_TILE_SUBLANES, LANES), F32),
                   jax.ShapeDtypeStruct((N_EXPERTS, n), F32)),
        compiler_params=_params(("parallel",)), name="outproj_router",
    )(*args)


def _moe_kernel(order_ref, pos0_ref, pos1_ref, off_ref, cnt_ref,
                h_ref, wg_ref, wu_ref, wd_ref, x1_ref, gt_ref, wcol_ref, ya_ref, yb_ref,
                o_scr, xs_scr, comb0_scr, comb1_scr, *, n_ctx_tiles):
    t = pl.program_id(0)
    s = pl.program_id(1)
    sub = ROW_TILE_SUBLANES
    groups = MOE_CHUNK // 8

    def tile(ref, row):
        return ref.at[pl.ds(pl.multiple_of(row * sub, sub), sub), :]

    def slab(ref, row0, n_rows, k):
        return ref.at[pl.ds(row0 * sub + k, n_rows, stride=sub), :]

    def gather_rows(buf, slot0):
        for j in range(MOE_CHUNK):
            tile(xs_scr, buf * MOE_CHUNK + j)[...] = h_ref[order_ref[slot0 + j]]

    def ffn(buf, base):
        a = jnp.zeros((MOE_CHUNK, D_EXPERT), F32)
        b = jnp.zeros((MOE_CHUNK, D_EXPERT), F32)
        for p in range(sub // 2):
            lhs = jnp.concatenate([slab(xs_scr, buf * MOE_CHUNK, MOE_CHUNK, 2 * p)[...],
                                   slab(xs_scr, buf * MOE_CHUNK, MOE_CHUNK, 2 * p + 1)[...]],
                                  axis=1).astype(BF16)
            rows = slice(2 * p * LANES, (2 * p + 2) * LANES)
            a = a + jnp.dot(lhs, wg_ref[rows, :], preferred_element_type=F32)
            b = b + jnp.dot(lhs, wu_ref[rows, :], preferred_element_type=F32)
        hid = (a * jax.nn.sigmoid(a)) * b
        out = jnp.dot(hid.astype(BF16), wd_ref[...], preferred_element_type=F32)
        for k in range(sub):
            slab(o_scr, base, MOE_CHUNK, k)[...] = out[:, k * LANES:(k + 1) * LANES]

    @pl.when(s == 0)
    def _first():
        gather_rows(0, t * MOE_SLOTS + pl.multiple_of(off_ref[t * N_EXPERTS], 8))

    @pl.when(s < N_EXPERTS)
    def _expert():
        seg = t * N_EXPERTS + s
        off = pl.multiple_of(off_ref[seg], 8)
        cur = s % 2
        nxt_seg = t * N_EXPERTS + jnp.minimum(s + 1, N_EXPERTS - 1)
        ffn(cur, off)
        gather_rows(1 - cur, t * MOE_SLOTS + pl.multiple_of(off_ref[nxt_seg], 8))

        def extra(c, carry):
            base = pl.multiple_of(off + c * MOE_CHUNK, 8)

            def gather8(i, carry2):
                slot = t * MOE_SLOTS + base + i * 8
                for k in range(8):
                    tile(xs_scr, 2 * MOE_CHUNK + i * 8 + k)[...] = h_ref[order_ref[slot + k]]
                return carry2

            lax.fori_loop(0, groups, gather8, 0)
            ffn(2, base)
            return carry

        lax.fori_loop(1, (cnt_ref[seg] + MOE_CHUNK - 1) // MOE_CHUNK, extra, 0)

    @pl.when(s >= N_EXPERTS)
    def _combine():
        tok0 = t * MOE_TILE + (s - N_EXPERTS) * MOE_OUT_TILE

        def body8(i, carry):
            for k in range(8):
                p0 = pos0_ref[tok0 + i * 8 + k]
                p1 = pos1_ref[tok0 + i * 8 + k]
                tile(comb0_scr, i * 8 + k)[...] = tile(o_scr, p0)[...]
                tile(comb1_scr, i * 8 + k)[...] = tile(o_scr, p1)[...]
            return carry

        lax.fori_loop(0, MOE_OUT_TILE // 8, body8, 0)
        wcol = wcol_ref[...]
        rows_of = lambda ref: jnp.concatenate(
            [slab(ref, 0, MOE_OUT_TILE, k)[...] for k in range(sub)], axis=1)
        comb = wcol[:, 2:3] * rows_of(comb0_scr) + wcol[:, 3:4] * rows_of(comb1_scr)
        y = x1_ref[...] + gt_ref[...] * comb
        chunk_ix = t * (MOE_TILE // MOE_OUT_TILE) + s - N_EXPERTS

        @pl.when(chunk_ix < n_ctx_tiles)
        def _ctx():
            ya_ref[...] = y

        @pl.when(chunk_ix >= n_ctx_tiles)
        def _lat():
            yb_ref[...] = y


def _route_tables(logits, b_router):
    n = logits.shape[1]
    nt = n // MOE_TILE
    tri = np.arange(SEQ_BLOCK)
    strict_upper = jnp.asarray(tri[:, None] < tri[None, :], BF16)
    tok = np.arange(MOE_TILE)
    digits = np.zeros((MOE_TILE, LANES), np.float32)
    digits[:, 0] = tok % 256
    digits[:, 1] = tok // 256
    pos, order, meta, wcol = pl.pallas_call(
        _route_kernel, grid=(nt,),
        in_specs=[pl.BlockSpec((N_EXPERTS, MOE_TILE), lambda t: (0, t)),
                  pl.BlockSpec((N_EXPERTS, 1), lambda t: (0, 0)),
                  pl.BlockSpec((SEQ_BLOCK, SEQ_BLOCK), lambda t: (0, 0)),
                  pl.BlockSpec((SEQ_BLOCK, SEQ_BLOCK), lambda t: (0, 0)),
                  pl.BlockSpec((MOE_TILE, LANES), lambda t: (0, 0))],
        out_specs=(pl.BlockSpec((8, MOE_TILE), lambda t: (0, t)),
                   pl.BlockSpec((None, MOE_SLOTS, 1), lambda t: (t, 0, 0)),
                   pl.BlockSpec((None, N_EXPERTS, 8), lambda t: (t, 0, 0)),
                   pl.BlockSpec((MOE_TILE, 8), lambda t: (t, 0))),
        out_shape=(jax.ShapeDtypeStruct((8, n), jnp.int32),
                   jax.ShapeDtypeStruct((nt, MOE_SLOTS, 1), jnp.int32),
                   jax.ShapeDtypeStruct((nt, N_EXPERTS, 8), jnp.int32),
                   jax.ShapeDtypeStruct((n, 8), F32)),
        scratch_shapes=[pltpu.VMEM((8, MOE_TILE), F32)],
        compiler_params=_params(("parallel",)), name="route_tables",
    )(logits, b_router.reshape(-1, 1), strict_upper, jnp.asarray(np.eye(SEQ_BLOCK), BF16),
      jnp.asarray(digits, BF16))
    return (order.reshape(-1), pos[0], pos[1], meta[:, :, 0].reshape(-1),
            meta[:, :, 1].reshape(-1), wcol)


def _route_kernel(logits_ref, br_ref, su_ref, eye_ref, digits_ref,
                  pos_ref, order_ref, meta_ref, wcol_ref, wrow_scr):
    e1, e2, w1, w2 = _select_experts(logits_ref[...], br_ref[...])
    wrow_scr[...] = jnp.zeros_like(wrow_scr)
    wrow_scr[2:3, :] = w1
    wrow_scr[3:4, :] = w2
    eye = eye_ref[...]
    for b in range(MOE_TILE // SEQ_BLOCK):
        cols = slice(b * SEQ_BLOCK, (b + 1) * SEQ_BLOCK)
        wcol_ref[cols, :] = functools.reduce(jnp.add, [
            lax.dot_general(eye, p, NT_DIMS, preferred_element_type=F32)
            for p in _split3(wrow_scr[:, cols])])
    eid = lax.broadcasted_iota(jnp.int32, (N_EXPERTS, MOE_TILE), 0)
    oh1, oh2 = eid == e1, eid == e2
    oh = jnp.where(oh1, 1.0, 0.0) + jnp.where(oh2, 1.0, 0.0)
    nblk = MOE_TILE // SEQ_BLOCK
    blocks = [oh[:, b * SEQ_BLOCK:(b + 1) * SEQ_BLOCK] for b in range(nblk)]
    inner = jnp.dot(jnp.concatenate(blocks, axis=0).astype(BF16), su_ref[...],
                    preferred_element_type=F32)
    run = jnp.zeros((N_EXPERTS, 1), F32)
    ranks = []
    for b in range(nblk):
        ranks.append(inner[b * N_EXPERTS:(b + 1) * N_EXPERTS, :] + run)
        run = run + jnp.sum(blocks[b], axis=1, keepdims=True)
    count = run
    seg = jnp.floor((count + 7.0) * 0.125) * 8.0
    sub = lax.broadcasted_iota(jnp.int32, (N_EXPERTS, 1), 0)
    off = jnp.zeros((N_EXPERTS, 1), F32)
    for e in range(N_EXPERTS - 1):
        off = off + jnp.where(sub > e, seg[e:e + 1, :], 0.0)
    slot = jnp.concatenate(ranks, axis=1) + off
    pos1 = jnp.sum(jnp.where(oh1, slot, 0.0), axis=0, keepdims=True).astype(jnp.int32)
    pos2 = jnp.sum(jnp.where(oh2, slot, 0.0), axis=0, keepdims=True).astype(jnp.int32)
    pos_ref[...] = jnp.zeros_like(pos_ref)
    pos_ref[0:1, :] = pos1
    pos_ref[1:2, :] = pos2
    meta_ref[...] = jnp.zeros_like(meta_ref)
    meta_ref[:, 0:1] = off.astype(jnp.int32)
    meta_ref[:, 1:2] = count.astype(jnp.int32)
    digits = digits_ref[...]
    rows = MOE_SLOTS // ROUTE_SLOT_BLOCKS
    for sb in range(ROUTE_SLOT_BLOCKS):
        j = lax.broadcasted_iota(jnp.int32, (rows, MOE_TILE), 0) + sb * rows
        hit = jnp.where(j == pos1, 1.0, 0.0) + jnp.where(j == pos2, 1.0, 0.0)
        d = jnp.dot(hit.astype(BF16), digits, preferred_element_type=F32)
        order_ref[sb * rows:(sb + 1) * rows, :] = (d[:, 0:1] + 256.0 * d[:, 1:2]).astype(jnp.int32)


def _moe(hp, logits, b_router, x1, mod3, layer, n_ctx_tiles, tiles_per_lat_seq, ctx_row,
         wg, wu, wd):
    n = hp.shape[0] // ROW_TILE_SUBLANES
    order, pos0, pos1, off, count, wcol = _route_tables(logits, b_router)
    chunks_per_tile = MOE_TILE // MOE_OUT_TILE
    n_steps = N_EXPERTS + chunks_per_tile

    def chunk_ix(t, s):
        return t * chunks_per_tile + jnp.maximum(s - N_EXPERTS, 0)

    def mod_row(g):
        return jnp.where(g < n_ctx_tiles, ctx_row, (g - n_ctx_tiles) // tiles_per_lat_seq)

    wspec = lambda r, c: pl.BlockSpec(
        (None, None, r, c), lambda t, s, *_: (layer, jnp.minimum(s, N_EXPERTS - 1), 0, 0))
    chunk_spec = pl.BlockSpec((MOE_OUT_TILE, D_MODEL), lambda t, s, *_: (chunk_ix(t, s), 0))
    grid_spec = pltpu.PrefetchScalarGridSpec(
        num_scalar_prefetch=5,
        grid=(n // MOE_TILE, n_steps),
        in_specs=[
            pl.BlockSpec((MOE_TILE, ROW_TILE_SUBLANES, LANES), lambda t, s, *_: (t, 0, 0)),
            wspec(D_MODEL, D_EXPERT), wspec(D_MODEL, D_EXPERT), wspec(D_EXPERT, D_MODEL),
            chunk_spec,
            pl.BlockSpec((None, 1, D_MODEL), lambda t, s, *_: (mod_row(chunk_ix(t, s)), 0, 5)),
            pl.BlockSpec((MOE_OUT_TILE, 8), lambda t, s, *_: (chunk_ix(t, s), 0)),
        ],
        out_specs=(
            pl.BlockSpec((MOE_OUT_TILE, D_MODEL),
                         lambda t, s, *_: (jnp.minimum(chunk_ix(t, s), n_ctx_tiles - 1), 0)),
            pl.BlockSpec((MOE_OUT_TILE, D_MODEL),
                         lambda t, s, *_: (jnp.maximum(chunk_ix(t, s) - n_ctx_tiles, 0), 0))),
        scratch_shapes=[pltpu.VMEM((MOE_SLOTS * ROW_TILE_SUBLANES, LANES), F32),
                        pltpu.VMEM((3 * MOE_CHUNK * ROW_TILE_SUBLANES, LANES), F32),
                        pltpu.VMEM((MOE_OUT_TILE * ROW_TILE_SUBLANES, LANES), F32),
                        pltpu.VMEM((MOE_OUT_TILE * ROW_TILE_SUBLANES, LANES), F32)],
    )
    n_ctx = n_ctx_tiles * MOE_OUT_TILE
    return pl.pallas_call(
        functools.partial(_moe_kernel, n_ctx_tiles=n_ctx_tiles), grid_spec=grid_spec,
        out_shape=(jax.ShapeDtypeStruct((n_ctx, D_MODEL), F32),
                   jax.ShapeDtypeStruct((n - n_ctx, D_MODEL), F32)),
        compiler_params=_params(("arbitrary", "arbitrary")), name="experts",
    )(order, pos0, pos1, off, count, hp.reshape(n, ROW_TILE_SUBLANES, LANES), wg, wu, wd, x1,
      mod3, wcol)


def _rope_tables(seq_len):
    half = HEAD_DIM // 2
    freqs = ROPE_BASE ** (-np.arange(0, half, 2, dtype=np.float64) / half)
    pos = np.arange(seq_len)
    row, col = pos // GRID_W, pos % GRID_W
    d = np.arange(HEAD_DIM)
    position = np.where(d[None, :] < half, row[:, None], col[:, None]).astype(np.float64)
    ang = (position.astype(np.float32) * freqs.astype(np.float32)[d % (half // 2)][None, :]).astype(np.float32)
    cos, sin = np.cos(ang), np.sin(ang)
    first = (d % half) < half // 2
    sa = np.where(first[None, :], -sin, 0.0)
    sb = np.where(first[None, :], 0.0, sin)
    ident = lambda v: np.full((TOKEN_TILE, HEAD_DIM), v, np.float32)
    stack = lambda ctx, lat: jnp.asarray(
        np.tile(np.concatenate([ctx, lat.astype(np.float32)], axis=0), (1, 2)), F32)
    return stack(ident(1.0), cos), stack(ident(0.0), sa), stack(ident(0.0), sb)


def kernel(x_prompt, x_sample, c, cache_k, cache_v, state_C, state_n, state_m, c_ctx, w_mod, b_mod,
           g_mix, g_ffn, w_in, b_igate, b_fgate, g_q, g_k, g_mh, w_out, w_router, b_router,
           w_e_gate, w_e_up, w_e_down):
    n_ctx_seq, ctx_len, _ = x_prompt.shape
    n_lat_seq, lat_len, _ = x_sample.shape
    n_layers = w_mod.shape[0]
    n_ctx = n_ctx_seq * ctx_len
    assert ctx_len == SEQ_BLOCK and lat_len % TOKEN_TILE == 0
    assert n_ctx % MOE_TILE == 0 and (n_lat_seq * lat_len) % MOE_TILE == 0
    assert n_lat_seq < 16 and n_ctx % lat_len == 0
    n_ctx_tiles = n_ctx // TOKEN_TILE
    tiles_per_lat_seq = lat_len // TOKEN_TILE
    ctx_row = n_lat_seq

    x = (x_prompt.reshape(n_ctx, D_MODEL), x_sample.reshape(-1, D_MODEL))
    cond = jnp.zeros((16, D_MODEL), F32).at[:n_lat_seq].set(c).at[ctx_row].set(c_ctx)
    mod = _modulation(cond, w_mod, b_mod)
    rope = _rope_tables(lat_len)

    wg_b, wu_b, wd_b = w_e_gate.astype(BF16), w_e_up.astype(BF16), w_e_down.astype(BF16)
    gate_perm = np.array([(q % 2) * N_ML_HEADS + hd + 2 * N_ML_HEADS * (q // 2)
                          for hd in range(N_ML_HEADS) for q in range(4)])

    ks, vs, cs, ns, ms = [], [], [], [], []
    for l in range(n_layers):
        mod3 = mod[l].reshape(16, 1, -1)
        precise_ctx = l < n_layers - 1

        def weight_pair(w):
            hi = w.astype(BF16)
            return hi, ((w - hi.astype(F32)).astype(BF16) if precise_ctx else None)

        w_main = weight_pair(w_in[l, :, :MAIN_WIDTH])
        w_gate = w_in[l, :, MAIN_WIDTH:][:, gate_perm]
        b_gate = jnp.concatenate([b_igate[l].reshape(-1), b_fgate[l].reshape(-1)])[gate_perm]
        q, k, v, mq, mk, mv, og, gcol, grow = _inproj(
            *x, mod3, n_ctx_tiles, tiles_per_lat_seq, ctx_row, g_mix[l], w_main, w_gate, b_gate,
            g_q[l], g_k[l], rope)

        att_ctx = _attention(q[0], k, v, 0, n_ctx_seq, ctx_len)
        past = cache_k.shape[2]
        ck = cache_k[:, l].reshape(n_lat_seq * past, LANES)
        cv = cache_v[:, l].reshape(n_lat_seq * past, LANES)
        att_lat = _attention(q[1], k, v, n_ctx, n_lat_seq, lat_len, cache=(ck, cv))

        ml_ctx, c_fin, nm_fin = _mlstm(mq[0], mk[0], mv[0], og, g_mh[l], gcol, grow, 0,
                                       n_ctx_seq, ctx_len, emit_state=True)
        n0 = state_n[:, l].transpose(0, 2, 1, 3)
        m0 = jnp.broadcast_to(state_m[:, l].transpose(0, 2, 1)[..., None], n0.shape)
        nm0 = jnp.concatenate([n0, m0, jnp.zeros_like(n0), jnp.zeros_like(n0)], axis=2)
        (ml_lat,) = _mlstm(mq[1], mk[1], mv[1], og, g_mh[l], gcol, grow, n_ctx, n_lat_seq,
                           lat_len, init=(state_C, nm0, l))

        x1, hp, logits = _outproj(
            (att_ctx, att_lat), (ml_ctx, ml_lat), x, mod3, n_ctx_tiles, tiles_per_lat_seq, ctx_row,
            weight_pair(w_out[l]), g_ffn[l], w_router)
        x = _moe(hp, logits, b_router, x1, mod3, l, n_ctx // MOE_OUT_TILE,
                 lat_len // MOE_OUT_TILE, ctx_row, wg_b, wu_b, wd_b)

        ks.append(k[:n_ctx].reshape(n_ctx_seq, ctx_len, N_KV_HEADS, HEAD_DIM))
        vs.append(v[:n_ctx].reshape(n_ctx_seq, ctx_len, N_KV_HEADS, HEAD_DIM))
        cs.append(c_fin)
        ns.append(nm_fin[:, :, 0:2, :].transpose(0, 2, 1, 3))
        ms.append(nm_fin[:, :, 2:4, 0].transpose(0, 2, 1))

    y_prompt = x[0].reshape(x_prompt.shape)
    y_sample = x[1].reshape(x_sample.shape)
    return (y_prompt, y_sample, jnp.stack(ks, axis=1), jnp.stack(vs, axis=1),
            jnp.stack(cs, axis=1), jnp.stack(ns, axis=1), jnp.stack(ms, axis=1))
```

```python
import functools

import numpy as np
import jax
import jax.numpy as jnp
from jax import lax
from jax.experimental import pallas as pl
from jax.experimental.pallas import tpu as pltpu

F32 = jnp.float32
BF16 = jnp.bfloat16

D_MODEL = 1024
HEAD_DIM = 64
ATT_WIDTH = 512
N_KV_HEADS = 2
ML_WIDTH = 512
N_ML_HEADS = 4
ML_HEAD_DIM = 128
GRID_W = 64
ROPE_BASE = 10000.0
N_EXPERTS = 16
N_GROUPS = 4
GROUP_SIZE = 4
D_EXPERT = 512
EPS = 1e-6
MAIN_WIDTH = 2816
LANES = 128
TOKEN_TILE = 512
SEQ_BLOCK = 256
ATT_Q_BLOCK = 512
MOE_TILE = 2048
MOE_CHUNK = 320
ROW_TILE_SUBLANES = D_MODEL // LANES
MLSTM_HEADS_PER_STEP = 4
MOE_OUT_TILE = 256
ROUTE_SLOT_BLOCKS = 9
MOE_SLOTS = -(-(2 * MOE_TILE + 8 * N_EXPERTS + MOE_CHUNK) // (512 * ROUTE_SLOT_BLOCKS)) * 512 * ROUTE_SLOT_BLOCKS
VMEM_LIMIT = 56 * 1024 * 1024
NEG_INF = float("-inf")
NT_DIMS = (((1,), (1,)), ((), ()))
TN_DIMS = (((0,), (0,)), ((), ()))


def _params(semantics):
    return pltpu.CompilerParams(dimension_semantics=semantics, vmem_limit_bytes=VMEM_LIMIT)


def _log_sigmoid(z):
    return jnp.minimum(z, 0.0) - jnp.log1p(jnp.exp(-jnp.abs(z)))


def _split3(x):
    h1 = x.astype(BF16)
    r1 = x - h1.astype(F32)
    h2 = r1.astype(BF16)
    h3 = (r1 - h2.astype(F32)).astype(BF16)
    return h1, h2, h3


def _mod_kernel(cond_ref, w_ref, b_ref, o_ref):
    c = cond_ref[...]
    s = c * jax.nn.sigmoid(c)
    o_ref[...] = _dot_x3(_hi_lo(s), _hi_lo(w_ref[...])) + b_ref[...]


def _modulation(cond, w_mod, b_mod):
    n_layers = w_mod.shape[0]
    n_chunks = w_mod.shape[2] // D_MODEL
    return pl.pallas_call(
        _mod_kernel,
        grid=(n_layers, n_chunks),
        in_specs=[
            pl.BlockSpec((16, D_MODEL), lambda l, j: (0, 0)),
            pl.BlockSpec((None, D_MODEL, D_MODEL), lambda l, j: (l, 0, j)),
            pl.BlockSpec((None, 1, D_MODEL), lambda l, j: (l, 0, j)),
        ],
        out_specs=pl.BlockSpec((None, 16, D_MODEL), lambda l, j: (l, 0, j)),
        out_shape=jax.ShapeDtypeStruct((n_layers, 16, w_mod.shape[2]), F32),
        compiler_params=_params(("parallel", "parallel")),
        name="modulation",
    )(cond, w_mod, b_mod.reshape(n_layers, 1, -1))


def _two_part(i, n_first, a_ref, b_ref):
    return jnp.where(i < n_first, a_ref[...], b_ref[...])


def _hi_lo(x):
    hi = x.astype(BF16)
    return hi, (x - hi.astype(F32)).astype(BF16)


def _dot_x3(a, b, dims=None):
    (ah, al), (bh, bl) = a, b
    if dims is None:
        d = lambda x, y: jnp.dot(x, y, preferred_element_type=F32)
    else:
        d = lambda x, y: lax.dot_general(x, y, dims, preferred_element_type=F32)
    return d(ah, bh) + (d(al, bh) + d(ah, bl))


def _two_part_out_specs(width, n_first, tile=TOKEN_TILE):
    return _two_part_specs(width, n_first, tile)


def _two_part_specs(width, n_first, tile=TOKEN_TILE):
    return [pl.BlockSpec((tile, width), lambda i: (jnp.minimum(i, n_first - 1), 0)),
            pl.BlockSpec((tile, width), lambda i: (jnp.maximum(i - n_first, 0), 0))]


def _inproj_kernel(*refs, n_ctx_tiles, precise_ctx):
    it = iter(refs)
    xa_ref, xb_ref, sh_ref, sc_ref, g_ref, w_ref = [next(it) for _ in range(6)]
    wl_ref = next(it) if precise_ctx else None
    (wgt_ref, wgtl_ref, brow_ref, gq_ref, gk_ref, eye_ref, cos_ref, sa_ref, sb_ref, gsum_ref,
     qa_ref, qb_ref, k_ref, v_ref, mqa_ref, mqb_ref, mka_ref, mkb_ref, mva_ref, mvb_ref,
     og_ref, gc_ref, gr_ref) = it
    tile = pl.program_id(0)
    x = _two_part(tile, n_ctx_tiles, xa_ref, xb_ref)
    ms = jnp.mean(x * x, axis=-1, keepdims=True)
    h = x * lax.rsqrt(ms + EPS) * g_ref[...]
    h = h * (1.0 + sc_ref[...]) + sh_ref[...]
    hb = h.astype(BF16)

    zr = _dot_x3((wgt_ref[...], wgtl_ref[...]), _hi_lo(h), NT_DIMS) + brow_ref[...]
    sub = lax.broadcasted_iota(jnp.int32, zr.shape, 0)
    gr = jnp.where(sub % 4 < 2, zr, _log_sigmoid(zr))
    eye = eye_ref[...]
    gc = functools.reduce(jnp.add, [lax.dot_general(eye, p, NT_DIMS, preferred_element_type=F32)
                                    for p in _split3(gr)])
    gr_ref[...] = jnp.zeros_like(gr_ref)
    for hd in range(N_ML_HEADS):
        gc_ref[hd] = gc[:, 4 * hd:4 * hd + 4]
        gr_ref[hd, 0:4, :] = gr[4 * hd:4 * hd + 4, :]

    cos = cos_ref[...]
    sa = sa_ref[...]
    sb = sb_ref[...]
    gsum = gsum_ref[...]

    def project(precise, q_ref, mq_ref, mk_ref, mv_ref):
        if precise:
            z = _dot_x3(_hi_lo(h), (w_ref[...], wl_ref[...]))
        else:
            z = jnp.dot(hb, w_ref[...], preferred_element_type=F32)
        proj = lambda c0, width: z[:, c0:c0 + width]
        act = F32 if precise else BF16

        n_qk = ATT_WIDTH // LANES + 1
        t_rows = hb.shape[0]
        zqk = proj(0, n_qk * LANES)
        zs = [zqk[:, c * LANES:(c + 1) * LANES] for c in range(n_qk)]
        sq = jnp.concatenate([z * z for z in zs], axis=0)
        ss = jnp.dot(jnp.concatenate(_hi_lo(sq), axis=0), gsum, preferred_element_type=F32)
        ss = ss[:n_qk * t_rows] + ss[n_qk * t_rows:]

        def headnorm_rope(c, gain):
            zn = zs[c] * lax.rsqrt(ss[c * t_rows:(c + 1) * t_rows] * (1.0 / HEAD_DIM) + EPS) * gain
            return zn * cos + pltpu.roll(zn, LANES - 16, 1) * sa + pltpu.roll(zn, 16, 1) * sb

        for c in range(n_qk - 1):
            q_ref[:, c * LANES:(c + 1) * LANES] = (headnorm_rope(c, gq_ref[...]) * 0.125).astype(act)
        k_ref[...] = headnorm_rope(n_qk - 1, gk_ref[...])
        v_ref[...] = proj(640, LANES)
        mq_ref[...] = proj(768, ML_WIDTH).astype(act)
        mk_ref[...] = (proj(1280, ML_WIDTH) * (ML_HEAD_DIM ** -0.5)).astype(act)
        mv_ref[...] = proj(1792, ML_WIDTH).astype(act)
        og_ref[...] = jax.nn.sigmoid(proj(2304, ML_WIDTH))

    @pl.when(tile < n_ctx_tiles)
    def _ctx():
        project(precise_ctx, qa_ref, mqa_ref, mka_ref, mva_ref)

    @pl.when(tile >= n_ctx_tiles)
    def _lat():
        project(False, qb_ref, mqb_ref, mkb_ref, mvb_ref)


def _inproj(xa, xb, mod3, n_ctx_tiles, tiles_per_lat_seq, ctx_row, g_mix, w_main, w_gate, b_gate,
            g_q, g_k, rope):
    w_hi, w_lo = w_main
    precise_ctx = w_lo is not None
    n_ctx, n_lat = xa.shape[0], xb.shape[0]
    n = n_ctx + n_lat
    n_tiles = n // TOKEN_TILE

    def mod_row(i):
        return jnp.where(i < n_ctx_tiles, ctx_row, (i - n_ctx_tiles) // tiles_per_lat_seq)

    def rope_blk(i):
        return jnp.where(i < n_ctx_tiles, 0, 1 + (i - n_ctx_tiles) % tiles_per_lat_seq)

    cos_t, sa_t, sb_t = rope
    lane = np.arange(LANES)
    gsum = jnp.asarray((lane[:, None] // HEAD_DIM) == (lane[None, :] // HEAD_DIM), BF16)
    tok = lambda w: pl.BlockSpec((TOKEN_TILE, w), lambda i: (i, 0))
    full = lambda a: pl.BlockSpec(a.shape, lambda i: (0,) * a.ndim)
    modspec = lambda j: pl.BlockSpec((None, 1, D_MODEL), lambda i: (mod_row(i), 0, j))
    ropespec = pl.BlockSpec((TOKEN_TILE, LANES), lambda i: (rope_blk(i), 0))
    consts = (g_mix.reshape(1, -1), w_hi) + ((w_lo,) if precise_ctx else ()) + (
        *_hi_lo(w_gate.T), b_gate.reshape(-1, 1),
        jnp.tile(g_q, 2).reshape(1, -1), jnp.tile(g_k, 2).reshape(1, -1),
        jnp.asarray(np.eye(TOKEN_TILE), BF16))
    args = (xa, xb, mod3, mod3) + consts + (cos_t, sa_t, sb_t, gsum)
    in_specs = _two_part_specs(D_MODEL, n_ctx_tiles) + [modspec(0), modspec(1)] \
        + [full(a) for a in consts] + [ropespec, ropespec, ropespec, full(gsum)]
    ctx_act = F32 if precise_ctx else BF16
    pair_shape = lambda w: [jax.ShapeDtypeStruct((n_ctx, w), ctx_act),
                            jax.ShapeDtypeStruct((n_lat, w), BF16)]
    pair_spec = lambda w: _two_part_out_specs(w, n_ctx_tiles)
    out_shape = (
        pair_shape(ATT_WIDTH)
        + [jax.ShapeDtypeStruct((n, LANES), F32),
           jax.ShapeDtypeStruct((n, LANES), F32)]
        + pair_shape(ML_WIDTH) + pair_shape(ML_WIDTH) + pair_shape(ML_WIDTH)
        + [jax.ShapeDtypeStruct((n, ML_WIDTH), F32),
           jax.ShapeDtypeStruct((N_ML_HEADS, n, 4), F32),
           jax.ShapeDtypeStruct((N_ML_HEADS, 8, n), F32)])
    out_specs = (pair_spec(ATT_WIDTH) + [tok(LANES), tok(LANES)]
                 + pair_spec(ML_WIDTH) + pair_spec(ML_WIDTH) + pair_spec(ML_WIDTH)
                 + [tok(ML_WIDTH),
                    pl.BlockSpec((N_ML_HEADS, TOKEN_TILE, 4), lambda i: (0, i, 0)),
                    pl.BlockSpec((N_ML_HEADS, 8, TOKEN_TILE), lambda i: (0, 0, i))])
    outs = pl.pallas_call(
        functools.partial(_inproj_kernel, n_ctx_tiles=n_ctx_tiles, precise_ctx=precise_ctx),
        grid=(n_tiles,), in_specs=in_specs, out_specs=tuple(out_specs),
        out_shape=tuple(out_shape), compiler_params=_params(("arbitrary",)), name="inproj",
    )(*args)
    qa, qb, k, v, mqa, mqb, mka, mkb, mva, mvb, og, gcol, grow = outs
    return (qa, qb), k, v, (mqa, mqb), (mka, mkb), (mva, mvb), og, gcol, grow


def _attn_kernel(*refs, n_kv, precise, cached):
    q_ref = refs[0]
    kv_refs = refs[1:1 + 2 * n_kv]
    o_ref = refs[1 + 2 * n_kv]
    dup_scr = refs[2 + 2 * n_kv:]
    tq = q_ref.shape[0]
    lo_q = lax.broadcasted_iota(jnp.int32, (tq, LANES), 1) < HEAD_DIM
    operand = _hi_lo if precise else (lambda a: a.astype(BF16))
    if precise:
        qk = lambda a, b: _dot_x3(a, b, NT_DIMS)
        pv = _dot_x3
    else:
        qk = lambda a, b: lax.dot_general(a, b, NT_DIMS, preferred_element_type=F32)
        pv = lambda a, b: jnp.dot(a, b, preferred_element_type=F32)

    def dup_half(ref, g, transposed):
        a = ref[...]
        r = pltpu.roll(a, HEAD_DIM, 1)
        lo = lax.broadcasted_iota(jnp.int32, a.shape, 1) < HEAD_DIM
        d = jnp.where(lo, a, r) if g == 0 else jnp.where(lo, r, a)
        return operand(d.T if transposed else d)

    if cached:
        @pl.when(pl.program_id(1) == 0)
        def _fill():
            for g in range(N_KV_HEADS):
                for j in range(2 * n_kv):
                    dup_scr[g * 2 * n_kv + j][...] = dup_half(kv_refs[j], g, j % 2 == 1)

        dup = lambda j, g: dup_scr[g * 2 * n_kv + j][...]
    else:
        dup = lambda j, g: dup_half(kv_refs[j], g, j % 2 == 1)

    for g in range(N_KV_HEADS):
        ks = [dup(2 * p, g) for p in range(n_kv)]
        vs = [dup(2 * p + 1, g) for p in range(n_kv)]
        for hb in range(2):
            c0 = (2 * g + hb) * LANES
            qb = q_ref[:, c0:c0 + LANES]
            outs = []
            for half in range(2):
                keep = lo_q if half == 0 else jnp.logical_not(lo_q)
                qm = operand(jnp.where(keep, qb, jnp.zeros_like(qb)))
                ss = [qk(kd, qm) for kd in ks]
                m = functools.reduce(jnp.maximum, [jnp.max(s, axis=0, keepdims=True) for s in ss])
                ps = [jnp.exp(s - m) for s in ss]
                den = functools.reduce(jnp.add, [jnp.sum(p, axis=0, keepdims=True) for p in ps])
                o_t = functools.reduce(jnp.add, [pv(vt, operand(p)) for p, vt in zip(ps, vs)])
                outs.append((o_t / den).T)
            o_ref[:, c0:c0 + LANES] = jnp.where(lo_q, outs[0], outs[1]).astype(o_ref.dtype)


def _attention(q, k, v, kv_row0, n_seq, seq_len, cache=None):
    precise = q.dtype == F32
    q_block = min(ATT_Q_BLOCK, seq_len)
    nq = seq_len // q_block
    sb0 = kv_row0 // seq_len
    in_specs = [
        pl.BlockSpec((q_block, ATT_WIDTH), lambda b, i: (b * nq + i, 0)),
        pl.BlockSpec((seq_len, LANES), lambda b, i: (sb0 + b, 0)),
        pl.BlockSpec((seq_len, LANES), lambda b, i: (sb0 + b, 0)),
    ]
    args = [q, k, v]
    n_kv = 1
    if cache is not None:
        ck, cv = cache
        past = ck.shape[0] // n_seq
        in_specs += [pl.BlockSpec((past, LANES), lambda b, i: (b, 0))] * 2
        args += [ck, cv]
        n_kv = 2
    cached = nq > 1 and not precise
    kv_rows = lambda j, a: a.shape[0] // n_seq if j >= 2 else seq_len
    scratch = [pltpu.VMEM((LANES, kv_rows(j, a)) if j % 2 else (kv_rows(j, a), LANES), BF16)
               for _ in range(N_KV_HEADS) for j, a in enumerate(args[1:])] if cached else []
    return pl.pallas_call(
        functools.partial(_attn_kernel, n_kv=n_kv, precise=precise, cached=cached),
        grid=(n_seq, nq), in_specs=in_specs,
        out_specs=pl.BlockSpec((q_block, ATT_WIDTH), lambda b, i: (b * nq + i, 0)),
        out_shape=jax.ShapeDtypeStruct((n_seq * seq_len, ATT_WIDTH), q.dtype),
        scratch_shapes=scratch,
        compiler_params=_params(("parallel", "arbitrary" if cached else "parallel")),
        name="attention",
    )(*args)


def _mlstm_kernel(*refs, heads, **static):
    it = iter(refs)
    q_ref, k_ref, v_ref, og_ref, gmh_ref, gcol_ref, grow_ref, u_ref, l_ref = [next(it) for _ in range(9)]
    init_refs = (next(it), next(it)) if static["has_init"] else ()
    ml_ref = next(it)
    state_refs = (next(it), next(it)) if static["emit_state"] else ()
    for hd in range(heads):
        lanes = pl.ds(hd * ML_HEAD_DIM, ML_HEAD_DIM)
        head_refs = [q_ref.at[:, lanes], k_ref.at[:, lanes], v_ref.at[:, lanes], og_ref.at[:, lanes],
                     gmh_ref.at[:, lanes], gcol_ref.at[hd], grow_ref.at[hd], u_ref, l_ref]
        if init_refs:
            head_refs += [init_refs[0].at[:, hd], init_refs[1].at[hd]]
        head_refs.append(ml_ref.at[:, lanes])
        if state_refs:
            head_refs += [state_refs[0].at[:, hd], state_refs[1].at[hd]]
        _mlstm_head(*head_refs, **static)


def _mlstm_head(*refs, seq_len, has_init, emit_state, precise):
    it = iter(refs)
    q_ref, k_ref, v_ref, og_ref, gmh_ref, gcol_ref, grow_ref, u_ref, l_ref = [next(it) for _ in range(9)]
    if has_init:
        c0_ref, nm0_ref = next(it), next(it)
    ml_ref = next(it)
    if emit_state:
        cf_ref, nmf_ref = next(it), next(it)

    operand = _hi_lo if precise else (lambda a: a.astype(BF16))
    if precise:
        qk = lambda a, b: _dot_x3(a, b, NT_DIMS)
        pv = _dot_x3
    else:
        qk = lambda a, b: lax.dot_general(a, b, NT_DIMS, preferred_element_type=F32)
        pv = lambda a, b: jnp.dot(a, b, preferred_element_type=F32)

    bq = SEQ_BLOCK
    nb = seq_len // bq
    blk = lambda j: slice(j * bq, (j + 1) * bq)
    upper_incl = u_ref[...]
    lower_incl = l_ref[...]

    def tri_dot(x, tri):
        return functools.reduce(jnp.add, [jnp.dot(p, tri, preferred_element_type=F32)
                                          for p in _split3(x)])

    ig_row = [[None] * nb for _ in range(2)]
    lf_row = [[None] * nb for _ in range(2)]
    within = [[None] * nb for _ in range(2)]
    bsum = [[None] * nb for _ in range(2)]
    for j in range(nb):
        g8 = grow_ref[:, blk(j)]
        cum_f = tri_dot(g8, upper_incl)
        cum_b = tri_dot(g8, lower_incl)
        for d in range(2):
            ig_row[d][j] = g8[d:d + 1, :]
            lf_row[d][j] = g8[2 + d:3 + d, :]
            within[d][j] = (cum_f if d == 0 else cum_b)[2 + d:3 + d, :]
            bsum[d][j] = jnp.sum(lf_row[d][j], axis=1, keepdims=True)
    zero11 = jnp.zeros((1, 1), F32)
    r_i = lax.broadcasted_iota(jnp.int32, (bq, bq), 0)
    c_i = lax.broadcasted_iota(jnp.int32, (bq, bq), 1)
    causal = [c_i <= r_i, c_i >= r_i]
    gmh = gmh_ref[...]

    q_blocks = [q_ref[blk(i), :] for i in range(nb)]
    scores = [qk(operand(q_blocks[i]), operand(k_ref[blk(i), :])) for i in range(nb)]

    h_dir = [[None] * nb for _ in range(2)]
    final = [None, None]
    for d in range(2):
        if has_init:
            state = (c0_ref[d], nm0_ref[d:d + 1, :])
            m = nm0_ref[2 + d:3 + d, 0:1]
        else:
            state, m = None, zero11
        scan = range(nb) if d == 0 else range(nb - 1, -1, -1)
        for step, i in enumerate(scan):
            q_i = q_blocks[i]
            a_loc = ig_row[d][i] - within[d][i]
            m_col = jnp.maximum(m, jnp.max(jnp.where(causal[d], a_loc, NEG_INF),
                                           axis=1, keepdims=True))
            if nb > 1:
                widen = lambda col: jnp.broadcast_to(col, (bq, LANES))
                tri = lower_incl if d == 0 else upper_incl
                b_rep = functools.reduce(jnp.add, [
                    lax.dot_general(tri, jnp.broadcast_to(piece, (LANES, bq)), NT_DIMS,
                                    preferred_element_type=F32)
                    for piece in _hi_lo(lf_row[d][i])])
                m_rep = widen(m_col)
                m_wide = jnp.concatenate([m_rep] * (bq // LANES), axis=1)
            else:
                widen = lambda col: col
                b_rep = jnp.sum(jnp.where(causal[d], lf_row[d][i], 0.0), axis=1, keepdims=True)
                m_rep = m_wide = m_col
            p = jnp.exp(jnp.where(causal[d], a_loc - m_wide, NEG_INF)) * scores[i]
            den = widen(jnp.sum(p, axis=1, keepdims=True))
            num = pv(operand(p), operand(v_ref[blk(i), :]))
            if state is not None:
                c_prev, n_prev = state
                w_inter = jnp.exp(m - m_rep)
                q_b = q_i.astype(BF16)
                qc = jnp.dot(q_b, c_prev.astype(BF16), preferred_element_type=F32)
                n_rows = jnp.broadcast_to(n_prev, (LANES, ML_HEAD_DIM)).astype(BF16)
                qn = lax.dot_general(q_b, n_rows, NT_DIMS, preferred_element_type=F32)
                num = num + w_inter * qc
                den = den + w_inter * qn
            nrm = jnp.maximum(jnp.abs(den), jnp.exp(-(b_rep + m_rep)))
            h_dir[d][i] = num / nrm
            if step == nb - 1 and not emit_state:
                break
            m_last = jnp.maximum(m, jnp.max(a_loc, axis=1, keepdims=True))
            a_col = widen(gcol_ref[blk(i), d:d + 1]) - b_rep
            kw = k_ref[blk(i), :].astype(F32) * jnp.exp(a_col - m_last)
            c_new = lax.dot_general(kw.astype(BF16), v_ref[blk(i), :].astype(BF16), TN_DIMS,
                                    preferred_element_type=F32)
            n_new = jnp.sum(kw, axis=0, keepdims=True)
            if state is not None:
                decay = jnp.exp(m - m_last)
                c_new = c_new + decay * state[0]
                n_new = n_new + decay * state[1]
            state = (c_new, n_new)
            m = bsum[d][i] + m_last
        final[d] = (state, m)

    for i in range(nb):
        h = h_dir[0][i] + h_dir[1][i]
        hn = h * lax.rsqrt(jnp.mean(h * h, axis=-1, keepdims=True) + EPS) * gmh
        ml_ref[blk(i), :] = (og_ref[blk(i), :] * hn).astype(ml_ref.dtype)

    if emit_state:
        nmf_ref[...] = jnp.zeros_like(nmf_ref)
        for d in range(2):
            (c_fin, n_fin), m_fin = final[d]
            cf_ref[d] = c_fin
            nmf_ref[d:d + 1, :] = n_fin
            nmf_ref[2 + d:3 + d, :] = jnp.broadcast_to(m_fin, (1, ML_HEAD_DIM))


def _mlstm(mq, mk, mv, og, g_mh, gcol, grow, row0, n_seq, seq_len, init=None, emit_state=False):
    precise = mq.dtype == F32
    assert not (precise and init is not None)
    sb0 = row0 // seq_len
    tri = np.arange(SEQ_BLOCK)
    upper_incl = jnp.asarray(tri[:, None] <= tri[None, :], BF16)
    lower_incl = jnp.asarray(tri[:, None] >= tri[None, :], BF16)
    heads = MLSTM_HEADS_PER_STEP
    width = heads * ML_HEAD_DIM
    ownblk = lambda: pl.BlockSpec((seq_len, width), lambda b, h: (b, h))
    headblk = lambda: pl.BlockSpec((seq_len, width), lambda b, h: (sb0 + b, h))
    const = lambda a: pl.BlockSpec(a.shape, lambda b, h: (0,) * a.ndim)
    in_specs = [ownblk(), ownblk(), ownblk(), headblk(),
                pl.BlockSpec((1, width), lambda b, h: (0, h)),
                pl.BlockSpec((heads, seq_len, 4), lambda b, h: (h, sb0 + b, 0)),
                pl.BlockSpec((heads, 8, seq_len), lambda b, h: (h, 0, sb0 + b)),
                const(upper_incl), const(lower_incl)]
    args = [mq, mk, mv, og, g_mh.reshape(1, -1), gcol, grow, upper_incl, lower_incl]
    if init is not None:
        c0, nm0, layer = init
        in_specs += [
            pl.BlockSpec((None, None, 2, heads, ML_HEAD_DIM, ML_HEAD_DIM),
                         lambda b, h: (b, layer, 0, h, 0, 0)),
            pl.BlockSpec((None, heads, 8, ML_HEAD_DIM), lambda b, h: (b, h, 0, 0))]
        args += [c0, nm0]
    out_shape = [jax.ShapeDtypeStruct((n_seq * seq_len, ML_WIDTH), mq.dtype)]
    out_specs = [pl.BlockSpec((seq_len, width), lambda b, h: (b, h))]
    if emit_state:
        out_shape += [jax.ShapeDtypeStruct((n_seq, 2, N_ML_HEADS, ML_HEAD_DIM, ML_HEAD_DIM), F32),
                      jax.ShapeDtypeStruct((n_seq, N_ML_HEADS, 8, ML_HEAD_DIM), F32)]
        out_specs += [pl.BlockSpec((None, 2, heads, ML_HEAD_DIM, ML_HEAD_DIM),
                                   lambda b, h: (b, 0, h, 0, 0)),
                      pl.BlockSpec((None, heads, 8, ML_HEAD_DIM), lambda b, h: (b, h, 0, 0))]
    return pl.pallas_call(
        functools.partial(_mlstm_kernel, heads=heads, seq_len=seq_len, has_init=init is not None,
                          emit_state=emit_state, precise=precise),
        grid=(n_seq, N_ML_HEADS // heads), in_specs=in_specs, out_specs=tuple(out_specs),
        out_shape=tuple(out_shape),
        compiler_params=_params(("parallel", "parallel")), name="mlstm",
    )(*args)


def _outproj_kernel(*refs, n_ctx_tiles, precise_ctx):
    it = iter(refs)
    atta_ref, attb_ref, mla_ref, mlb_ref, xa_ref, xb_ref, w_ref = [next(it) for _ in range(7)]
    wl_ref = next(it) if precise_ctx else None
    gt_ref, sh_ref, sc_ref, g_ref, wrt_ref, wrtl_ref, x1_ref, hp_ref, logits_ref, y_scr = it
    i = pl.program_id(0)

    def mix(att, ml):
        return (jnp.dot(att, w_ref[:ATT_WIDTH, :], preferred_element_type=F32)
                + jnp.dot(ml, w_ref[ATT_WIDTH:, :], preferred_element_type=F32))

    @pl.when(i < n_ctx_tiles)
    def _ctx():
        if precise_ctx:
            y_scr[...] = (
                _dot_x3(_hi_lo(atta_ref[...]), (w_ref[:ATT_WIDTH, :], wl_ref[:ATT_WIDTH, :]))
                + _dot_x3(_hi_lo(mla_ref[...]), (w_ref[ATT_WIDTH:, :], wl_ref[ATT_WIDTH:, :])))
        else:
            y_scr[...] = mix(atta_ref[...], mla_ref[...])

    @pl.when(i >= n_ctx_tiles)
    def _lat():
        y_scr[...] = mix(attb_ref[...], mlb_ref[...])

    x1 = _two_part(i, n_ctx_tiles, xa_ref, xb_ref) + gt_ref[...] * y_scr[...]
    x1_ref[...] = x1
    ms = jnp.mean(x1 * x1, axis=-1, keepdims=True)
    h2 = x1 * lax.rsqrt(ms + EPS) * g_ref[...]
    h2 = h2 * (1.0 + sc_ref[...]) + sh_ref[...]
    for k in range(ROW_TILE_SUBLANES):
        hp_ref[pl.ds(k, TOKEN_TILE, stride=ROW_TILE_SUBLANES), :] = h2[:, k * LANES:(k + 1) * LANES]

    logits_ref[...] = _dot_x3((wrt_ref[...], wrtl_ref[...]), _hi_lo(h2), NT_DIMS)


def _select_experts(logits, b_col):
    ex = jnp.exp(logits - jnp.max(logits, axis=0, keepdims=True))
    scores = ex / jnp.sum(ex, axis=0, keepdims=True)
    sel = scores + b_col
    row = lambda a, e: a[e:e + 1, :]
    grp_score = []
    for g in range(N_GROUPS):
        xs = [row(sel, g * GROUP_SIZE + j) for j in range(GROUP_SIZE)]
        pairs = [xs[a] + xs[b] for a in range(GROUP_SIZE) for b in range(a + 1, GROUP_SIZE)]
        grp_score.append(functools.reduce(jnp.maximum, pairs))
    best = grp_score[0]
    grp = jnp.zeros_like(best, dtype=jnp.int32)
    for g in range(1, N_GROUPS):
        better = grp_score[g] > best
        grp = jnp.where(better, g, grp)
        best = jnp.where(better, grp_score[g], best)
    pick = lambda a, j: functools.reduce(
        lambda acc, g: jnp.where(grp == g, row(a, g * GROUP_SIZE + j), acc),
        range(1, N_GROUPS), row(a, j))
    xs = [pick(sel, j) for j in range(GROUP_SIZE)]
    ws = [pick(scores, j) for j in range(GROUP_SIZE)]

    def argmax4(vals):
        bv, bi = vals[0], jnp.zeros_like(grp)
        for j in range(1, GROUP_SIZE):
            better = vals[j] > bv
            bi = jnp.where(better, j, bi)
            bv = jnp.where(better, vals[j], bv)
        return bi

    i1 = argmax4(xs)
    i2 = argmax4([jnp.where(i1 == j, NEG_INF, xs[j]) for j in range(GROUP_SIZE)])
    take = lambda vals, idx: functools.reduce(
        lambda acc, j: jnp.where(idx == j, vals[j], acc), range(1, GROUP_SIZE), vals[0])
    w1, w2 = take(ws, i1), take(ws, i2)
    wsum = w1 + w2
    w1, w2 = w1 / wsum, w2 / wsum
    return grp * GROUP_SIZE + i1, grp * GROUP_SIZE + i2, w1, w2


def _outproj(att, ml, x, mod3, n_ctx_tiles, tiles_per_lat_seq, ctx_row, w_out, g_ffn, w_router):
    w_hi, w_lo = w_out
    precise_ctx = w_lo is not None
    n = x[0].shape[0] + x[1].shape[0]

    def mod_row(i):
        return jnp.where(i < n_ctx_tiles, ctx_row, (i - n_ctx_tiles) // tiles_per_lat_seq)

    tok = lambda w: pl.BlockSpec((TOKEN_TILE, w), lambda i: (i, 0))
    full = lambda a: pl.BlockSpec(a.shape, lambda i: (0,) * a.ndim)
    modspec = lambda j: pl.BlockSpec((None, 1, D_MODEL), lambda i: (mod_row(i), 0, j))
    weights = (w_hi, w_lo) if precise_ctx else (w_hi,)
    consts = (g_ffn.reshape(1, -1), *_hi_lo(w_router.T))
    args = (*att, *ml, *x, *weights, mod3, mod3, mod3) + consts
    in_specs = (_two_part_specs(ATT_WIDTH, n_ctx_tiles) + _two_part_specs(ML_WIDTH, n_ctx_tiles)
                + _two_part_specs(D_MODEL, n_ctx_tiles) + [full(w) for w in weights]
                + [modspec(2), modspec(3), modspec(4)] + [full(a) for a in consts])
    return pl.pallas_call(
        functools.partial(_outproj_kernel, n_ctx_tiles=n_ctx_tiles, precise_ctx=precise_ctx),
        grid=(n // TOKEN_TILE,), in_specs=in_specs,
        scratch_shapes=[pltpu.VMEM((TOKEN_TILE, D_MODEL), F32)],
        out_specs=(tok(D_MODEL),
                   pl.BlockSpec((TOKEN_TILE * ROW_TILE_SUBLANES, LANES), lambda i: (i, 0)),
                   pl.BlockSpec((N_EXPERTS, TOKEN_TILE), lambda i: (0, i))),
        out_shape=(jax.ShapeDtypeStruct((n, D_MODEL), F32),
                   jax.ShapeDtypeStruct((n * ROW_TILE_SUBLANES, LANES), F32),
                   jax.ShapeDtypeStruct((N_EXPERTS, n), F32)),
        compiler_params=_params(("parallel",)), name="outproj_router",
    )(*args)


def _moe_kernel(order_ref, pos0_ref, pos1_ref, off_ref, cnt_ref,
                h_ref, wg_ref, wu_ref, wd_ref, x1_ref, gt_ref, wcol_ref, ya_ref, yb_ref,
                o_scr, xs_scr, comb0_scr, comb1_scr, *, n_ctx_tiles):
    t = pl.program_id(0)
    s = pl.program_id(1)
    sub = ROW_TILE_SUBLANES
    groups = MOE_CHUNK // 8

    def tile(ref, row):
        return ref.at[pl.ds(pl.multiple_of(row * sub, sub), sub), :]

    def slab(ref, row0, n_rows, k):
        return ref.at[pl.ds(row0 * sub + k, n_rows, stride=sub), :]

    def gather_rows(buf, slot0):
        for j in range(MOE_CHUNK):
            tile(xs_scr, buf * MOE_CHUNK + j)[...] = h_ref[order_ref[slot0 + j]]

    def ffn(buf, base):
        a = jnp.zeros((MOE_CHUNK, D_EXPERT), F32)
        b = jnp.zeros((MOE_CHUNK, D_EXPERT), F32)
        for p in range(sub // 2):
            lhs = jnp.concatenate([slab(xs_scr, buf * MOE_CHUNK, MOE_CHUNK, 2 * p)[...],
                                   slab(xs_scr, buf * MOE_CHUNK, MOE_CHUNK, 2 * p + 1)[...]],
                                  axis=1).astype(BF16)
            rows = slice(2 * p * LANES, (2 * p + 2) * LANES)
            a = a + jnp.dot(lhs, wg_ref[rows, :], preferred_element_type=F32)
            b = b + jnp.dot(lhs, wu_ref[rows, :], preferred_element_type=F32)
        hid = (a * jax.nn.sigmoid(a)) * b
        out = jnp.dot(hid.astype(BF16), wd_ref[...], preferred_element_type=F32)
        for k in range(sub):
            slab(o_scr, base, MOE_CHUNK, k)[...] = out[:, k * LANES:(k + 1) * LANES]

    @pl.when(s == 0)
    def _first():
        gather_rows(0, t * MOE_SLOTS + pl.multiple_of(off_ref[t * N_EXPERTS], 8))

    @pl.when(s < N_EXPERTS)
    def _expert():
        seg = t * N_EXPERTS + s
        off = pl.multiple_of(off_ref[seg], 8)
        cur = s % 2
        nxt_seg = t * N_EXPERTS + jnp.minimum(s + 1, N_EXPERTS - 1)
        ffn(cur, off)
        gather_rows(1 - cur, t * MOE_SLOTS + pl.multiple_of(off_ref[nxt_seg], 8))

        def extra(c, carry):
            base = pl.multiple_of(off + c * MOE_CHUNK, 8)

            def gather8(i, carry2):
                slot = t * MOE_SLOTS + base + i * 8
                for k in range(8):
                    tile(xs_scr, 2 * MOE_CHUNK + i * 8 + k)[...] = h_ref[order_ref[slot + k]]
                return carry2

            lax.fori_loop(0, groups, gather8, 0)
            ffn(2, base)
            return carry

        lax.fori_loop(1, (cnt_ref[seg] + MOE_CHUNK - 1) // MOE_CHUNK, extra, 0)

    @pl.when(s >= N_EXPERTS)
    def _combine():
        tok0 = t * MOE_TILE + (s - N_EXPERTS) * MOE_OUT_TILE

        def body8(i, carry):
            for k in range(8):
                p0 = pos0_ref[tok0 + i * 8 + k]
                p1 = pos1_ref[tok0 + i * 8 + k]
                tile(comb0_scr, i * 8 + k)[...] = tile(o_scr, p0)[...]
                tile(comb1_scr, i * 8 + k)[...] = tile(o_scr, p1)[...]
            return carry

        lax.fori_loop(0, MOE_OUT_TILE // 8, body8, 0)
        wcol = wcol_ref[...]
        rows_of = lambda ref: jnp.concatenate(
            [slab(ref, 0, MOE_OUT_TILE, k)[...] for k in range(sub)], axis=1)
        comb = wcol[:, 2:3] * rows_of(comb0_scr) + wcol[:, 3:4] * rows_of(comb1_scr)
        y = x1_ref[...] + gt_ref[...] * comb
        chunk_ix = t * (MOE_TILE // MOE_OUT_TILE) + s - N_EXPERTS

        @pl.when(chunk_ix < n_ctx_tiles)
        def _ctx():
            ya_ref[...] = y

        @pl.when(chunk_ix >= n_ctx_tiles)
        def _lat():
            yb_ref[...] = y


def _route_tables(logits, b_router):
    n = logits.shape[1]
    nt = n // MOE_TILE
    tri = np.arange(SEQ_BLOCK)
    strict_upper = jnp.asarray(tri[:, None] < tri[None, :], BF16)
    tok = np.arange(MOE_TILE)
    digits = np.zeros((MOE_TILE, LANES), np.float32)
    digits[:, 0] = tok % 256
    digits[:, 1] = tok // 256
    pos, order, meta, wcol = pl.pallas_call(
        _route_kernel, grid=(nt,),
        in_specs=[pl.BlockSpec((N_EXPERTS, MOE_TILE), lambda t: (0, t)),
                  pl.BlockSpec((N_EXPERTS, 1), lambda t: (0, 0)),
                  pl.BlockSpec((SEQ_BLOCK, SEQ_BLOCK), lambda t: (0, 0)),
                  pl.BlockSpec((SEQ_BLOCK, SEQ_BLOCK), lambda t: (0, 0)),
                  pl.BlockSpec((MOE_TILE, LANES), lambda t: (0, 0))],
        out_specs=(pl.BlockSpec((8, MOE_TILE), lambda t: (0, t)),
                   pl.BlockSpec((None, MOE_SLOTS, 1), lambda t: (t, 0, 0)),
                   pl.BlockSpec((None, N_EXPERTS, 8), lambda t: (t, 0, 0)),
                   pl.BlockSpec((MOE_TILE, 8), lambda t: (t, 0))),
        out_shape=(jax.ShapeDtypeStruct((8, n), jnp.int32),
                   jax.ShapeDtypeStruct((nt, MOE_SLOTS, 1), jnp.int32),
                   jax.ShapeDtypeStruct((nt, N_EXPERTS, 8), jnp.int32),
                   jax.ShapeDtypeStruct((n, 8), F32)),
        scratch_shapes=[pltpu.VMEM((8, MOE_TILE), F32)],
        compiler_params=_params(("parallel",)), name="route_tables",
    )(logits, b_router.reshape(-1, 1), strict_upper, jnp.asarray(np.eye(SEQ_BLOCK), BF16),
      jnp.asarray(digits, BF16))
    return (order.reshape(-1), pos[0], pos[1], meta[:, :, 0].reshape(-1),
            meta[:, :, 1].reshape(-1), wcol)


def _route_kernel(logits_ref, br_ref, su_ref, eye_ref, digits_ref,
                  pos_ref, order_ref, meta_ref, wcol_ref, wrow_scr):
    e1, e2, w1, w2 = _select_experts(logits_ref[...], br_ref[...])
    wrow_scr[...] = jnp.zeros_like(wrow_scr)
    wrow_scr[2:3, :] = w1
    wrow_scr[3:4, :] = w2
    eye = eye_ref[...]
    for b in range(MOE_TILE // SEQ_BLOCK):
        cols = slice(b * SEQ_BLOCK, (b + 1) * SEQ_BLOCK)
        wcol_ref[cols, :] = functools.reduce(jnp.add, [
            lax.dot_general(eye, p, NT_DIMS, preferred_element_type=F32)
            for p in _split3(wrow_scr[:, cols])])
    eid = lax.broadcasted_iota(jnp.int32, (N_EXPERTS, MOE_TILE), 0)
    oh1, oh2 = eid == e1, eid == e2
    oh = jnp.where(oh1, 1.0, 0.0) + jnp.where(oh2, 1.0, 0.0)
    nblk = MOE_TILE // SEQ_BLOCK
    blocks = [oh[:, b * SEQ_BLOCK:(b + 1) * SEQ_BLOCK] for b in range(nblk)]
    inner = jnp.dot(jnp.concatenate(blocks, axis=0).astype(BF16), su_ref[...],
                    preferred_element_type=F32)
    run = jnp.zeros((N_EXPERTS, 1), F32)
    ranks = []
    for b in range(nblk):
        ranks.append(inner[b * N_EXPERTS:(b + 1) * N_EXPERTS, :] + run)
        run = run + jnp.sum(blocks[b], axis=1, keepdims=True)
    count = run
    seg = jnp.floor((count + 7.0) * 0.125) * 8.0
    sub = lax.broadcasted_iota(jnp.int32, (N_EXPERTS, 1), 0)
    off = jnp.zeros((N_EXPERTS, 1), F32)
    for e in range(N_EXPERTS - 1):
        off = off + jnp.where(sub > e, seg[e:e + 1, :], 0.0)
    slot = jnp.concatenate(ranks, axis=1) + off
    pos1 = jnp.sum(jnp.where(oh1, slot, 0.0), axis=0, keepdims=True).astype(jnp.int32)
    pos2 = jnp.sum(jnp.where(oh2, slot, 0.0), axis=0, keepdims=True).astype(jnp.int32)
    pos_ref[...] = jnp.zeros_like(pos_ref)
    pos_ref[0:1, :] = pos1
    pos_ref[1:2, :] = pos2
    meta_ref[...] = jnp.zeros_like(meta_ref)
    meta_ref[:, 0:1] = off.astype(jnp.int32)
    meta_ref[:, 1:2] = count.astype(jnp.int32)
    digits = digits_ref[...]
    rows = MOE_SLOTS // ROUTE_SLOT_BLOCKS
    for sb in range(ROUTE_SLOT_BLOCKS):
        j = lax.broadcasted_iota(jnp.int32, (rows, MOE_TILE), 0) + sb * rows
        hit = jnp.where(j == pos1, 1.0, 0.0) + jnp.where(j == pos2, 1.0, 0.0)
        d = jnp.dot(hit.astype(BF16), digits, preferred_element_type=F32)
        order_ref[sb * rows:(sb + 1) * rows, :] = (d[:, 0:1] + 256.0 * d[:, 1:2]).astype(jnp.int32)


def _moe(hp, logits, b_router, x1, mod3, layer, n_ctx_tiles, tiles_per_lat_seq, ctx_row,
         wg, wu, wd):
    n = hp.shape[0] // ROW_TILE_SUBLANES
    order, pos0, pos1, off, count, wcol = _route_tables(logits, b_router)
    chunks_per_tile = MOE_TILE // MOE_OUT_TILE
    n_steps = N_EXPERTS + chunks_per_tile

    def chunk_ix(t, s):
        return t * chunks_per_tile + jnp.maximum(s - N_EXPERTS, 0)

    def mod_row(g):
        return jnp.where(g < n_ctx_tiles, ctx_row, (g - n_ctx_tiles) // tiles_per_lat_seq)

    wspec = lambda r, c: pl.BlockSpec(
        (None, None, r, c), lambda t, s, *_: (layer, jnp.minimum(s, N_EXPERTS - 1), 0, 0))
    chunk_spec = pl.BlockSpec((MOE_OUT_TILE, D_MODEL), lambda t, s, *_: (chunk_ix(t, s), 0))
    grid_spec = pltpu.PrefetchScalarGridSpec(
        num_scalar_prefetch=5,
        grid=(n // MOE_TILE, n_steps),
        in_specs=[
            pl.BlockSpec((MOE_TILE, ROW_TILE_SUBLANES, LANES), lambda t, s, *_: (t, 0, 0)),
            wspec(D_MODEL, D_EXPERT), wspec(D_MODEL, D_EXPERT), wspec(D_EXPERT, D_MODEL),
            chunk_spec,
            pl.BlockSpec((None, 1, D_MODEL), lambda t, s, *_: (mod_row(chunk_ix(t, s)), 0, 5)),
            pl.BlockSpec((MOE_OUT_TILE, 8), lambda t, s, *_: (chunk_ix(t, s), 0)),
        ],
        out_specs=(
            pl.BlockSpec((MOE_OUT_TILE, D_MODEL),
                         lambda t, s, *_: (jnp.minimum(chunk_ix(t, s), n_ctx_tiles - 1), 0)),
            pl.BlockSpec((MOE_OUT_TILE, D_MODEL),
                         lambda t, s, *_: (jnp.maximum(chunk_ix(t, s) - n_ctx_tiles, 0), 0))),
        scratch_shapes=[pltpu.VMEM((MOE_SLOTS * ROW_TILE_SUBLANES, LANES), F32),
                        pltpu.VMEM((3 * MOE_CHUNK * ROW_TILE_SUBLANES, LANES), F32),
                        pltpu.VMEM((MOE_OUT_TILE * ROW_TILE_SUBLANES, LANES), F32),
                        pltpu.VMEM((MOE_OUT_TILE * ROW_TILE_SUBLANES, LANES), F32)],
    )
    n_ctx = n_ctx_tiles * MOE_OUT_TILE
    return pl.pallas_call(
        functools.partial(_moe_kernel, n_ctx_tiles=n_ctx_tiles), grid_spec=grid_spec,
        out_shape=(jax.ShapeDtypeStruct((n_ctx, D_MODEL), F32),
                   jax.ShapeDtypeStruct((n - n_ctx, D_MODEL), F32)),
        compiler_params=_params(("arbitrary", "arbitrary")), name="experts",
    )(order, pos0, pos1, off, count, hp.reshape(n, ROW_TILE_SUBLANES, LANES), wg, wu, wd, x1,
      mod3, wcol)


def _rope_tables(seq_len):
    half = HEAD_DIM // 2
    freqs = ROPE_BASE ** (-np.arange(0, half, 2, dtype=np.float64) / half)
    pos = np.arange(seq_len)
    row, col = pos // GRID_W, pos % GRID_W
    d = np.arange(HEAD_DIM)
    position = np.where(d[None, :] < half, row[:, None], col[:, None]).astype(np.float64)
    ang = (position.astype(np.float32) * freqs.astype(np.float32)[d % (half // 2)][None, :]).astype(np.float32)
    cos, sin = np.cos(ang), np.sin(ang)
    first = (d % half) < half // 2
    sa = np.where(first[None, :], -sin, 0.0)
    sb = np.where(first[None, :], 0.0, sin)
    ident = lambda v: np.full((TOKEN_TILE, HEAD_DIM), v, np.float32)
    stack = lambda ctx, lat: jnp.asarray(
        np.tile(np.concatenate([ctx, lat.astype(np.float32)], axis=0), (1, 2)), F32)
    return stack(ident(1.0), cos), stack(ident(0.0), sa), stack(ident(0.0), sb)


def kernel(x_prompt, x_sample, c, cache_k, cache_v, state_C, state_n, state_m, c_ctx, w_mod, b_mod,
           g_mix, g_ffn, w_in, b_igate, b_fgate, g_q, g_k, g_mh, w_out, w_router, b_router,
           w_e_gate, w_e_up, w_e_down):
    n_ctx_seq, ctx_len, _ = x_prompt.shape
    n_lat_seq, lat_len, _ = x_sample.shape
    n_layers = w_mod.shape[0]
    n_ctx = n_ctx_seq * ctx_len
    assert ctx_len == SEQ_BLOCK and lat_len % TOKEN_TILE == 0
    assert n_ctx % MOE_TILE == 0 and (n_lat_seq * lat_len) % MOE_TILE == 0
    assert n_lat_seq < 16 and n_ctx % lat_len == 0
    n_ctx_tiles = n_ctx // TOKEN_TILE
    tiles_per_lat_seq = lat_len // TOKEN_TILE
    ctx_row = n_lat_seq

    x = (x_prompt.reshape(n_ctx, D_MODEL), x_sample.reshape(-1, D_MODEL))
    cond = jnp.zeros((16, D_MODEL), F32).at[:n_lat_seq].set(c).at[ctx_row].set(c_ctx)
    mod = _modulation(cond, w_mod, b_mod)
    rope = _rope_tables(lat_len)

    wg_b, wu_b, wd_b = w_e_gate.astype(BF16), w_e_up.astype(BF16), w_e_down.astype(BF16)
    gate_perm = np.array([(q % 2) * N_ML_HEADS + hd + 2 * N_ML_HEADS * (q // 2)
                          for hd in range(N_ML_HEADS) for q in range(4)])

    ks, vs, cs, ns, ms = [], [], [], [], []
    for l in range(n_layers):
        mod3 = mod[l].reshape(16, 1, -1)
        precise_ctx = l < n_layers - 1

        def weight_pair(w):
            hi = w.astype(BF16)
            return hi, ((w - hi.astype(F32)).astype(BF16) if precise_ctx else None)

        w_main = weight_pair(w_in[l, :, :MAIN_WIDTH])
        w_gate = w_in[l, :, MAIN_WIDTH:][:, gate_perm]
        b_gate = jnp.concatenate([b_igate[l].reshape(-1), b_fgate[l].reshape(-1)])[gate_perm]
        q, k, v, mq, mk, mv, og, gcol, grow = _inproj(
            *x, mod3, n_ctx_tiles, tiles_per_lat_seq, ctx_row, g_mix[l], w_main, w_gate, b_gate,
            g_q[l], g_k[l], rope)

        att_ctx = _attention(q[0], k, v, 0, n_ctx_seq, ctx_len)
        past = cache_k.shape[2]
        ck = cache_k[:, l].reshape(n_lat_seq * past, LANES)
        cv = cache_v[:, l].reshape(n_lat_seq * past, LANES)
        att_lat = _attention(q[1], k, v, n_ctx, n_lat_seq, lat_len, cache=(ck, cv))

        ml_ctx, c_fin, nm_fin = _mlstm(mq[0], mk[0], mv[0], og, g_mh[l], gcol, grow, 0,
                                       n_ctx_seq, ctx_len, emit_state=True)
        n0 = state_n[:, l].transpose(0, 2, 1, 3)
        m0 = jnp.broadcast_to(state_m[:, l].transpose(0, 2, 1)[..., None], n0.shape)
        nm0 = jnp.concatenate([n0, m0, jnp.zeros_like(n0), jnp.zeros_like(n0)], axis=2)
        (ml_lat,) = _mlstm(mq[1], mk[1], mv[1], og, g_mh[l], gcol, grow, n_ctx, n_lat_seq,
                           lat_len, init=(state_C, nm0, l))

        x1, hp, logits = _outproj(
            (att_ctx, att_lat), (ml_ctx, ml_lat), x, mod3, n_ctx_tiles, tiles_per_lat_seq, ctx_row,
            weight_pair(w_out[l]), g_ffn[l], w_router)
        x = _moe(hp, logits, b_router, x1, mod3, l, n_ctx // MOE_OUT_TILE,
                 lat_len // MOE_OUT_TILE, ctx_row, wg_b, wu_b, wd_b)

        ks.append(k[:n_ctx].reshape(n_ctx_seq, ctx_len, N_KV_HEADS, HEAD_DIM))
        vs.append(v[:n_ctx].reshape(n_ctx_seq, ctx_len, N_KV_HEADS, HEAD_DIM))
        cs.append(c_fin)
        ns.append(nm_fin[:, :, 0:2, :].transpose(0, 2, 1, 3))
        ms.append(nm_fin[:, :, 2:4, 0].transpose(0, 2, 1))

    y_prompt = x[0].reshape(x_prompt.shape)
    y_sample = x[1].reshape(x_sample.shape)
    return (y_prompt, y_sample, jnp.stack(ks, axis=1), jnp.stack(vs, axis=1),
            jnp.stack(cs, axis=1), jnp.stack(ns, axis=1), jnp.stack(ms, axis=1))
```
